```python
import math
import jax, jax.numpy as jnp
from jax import lax
import numpy as np

D_MODEL = 2048
BATCH = 2
SEQ = 4096
DEPTH = 2
DEC_BATCH = 128
DEC_SEQ = 4
PAST_LEN = 2048
PAGE_SIZE = 128

D_MIX = D_MODEL
D_NSA = D_MIX // 2
D_CONV = D_MIX - D_NSA
HEAD_DIM = 64
N_HEADS = D_NSA // HEAD_DIM
N_KV = 2
HPG = N_HEADS // N_KV
N_BRANCH = 3
CMP_BLOCK = 32
CMP_STRIDE = 16
SEL_BLOCK = 64
TOPK = 16
WINDOW = 512
Q_BLOCK = 128
FORCE_BONUS = 1e4
CONV_W = 3
D_FF = 5632
MEM_LEN = 256
MEM_HEADS = 4
MEM_HEAD_DIM = 128
D_MEM = MEM_HEADS * MEM_HEAD_DIM
SPLITS = (D_NSA, 2 * N_KV * HEAD_DIM, 2 * N_KV * HEAD_DIM, 2 * N_KV * HEAD_DIM,
          N_BRANCH * N_HEADS, D_CONV, D_CONV, D_CONV)
D_IN = D_NSA + 6 * N_KV * HEAD_DIM + N_BRANCH * N_HEADS + 3 * D_CONV
EPS = 1e-6
NEG = -1e30

kernel_name = 'nsa_shortconv_hybrid_step'


def rmsnorm(x, g):
    xf = x.astype(jnp.float32)
    y = xf * lax.rsqrt(jnp.mean(xf * xf, axis=-1, keepdims=True) + EPS)
    return (y * g.astype(jnp.float32)).astype(x.dtype)


def masked_softmax(s, mask):
    s = jnp.where(mask, s.astype(jnp.float32), NEG)
    m = jnp.max(s, axis=-1, keepdims=True)
    e = jnp.exp(s - m) * mask
    return e / jnp.maximum(jnp.sum(e, axis=-1, keepdims=True), 1e-30)


def causal_dwconv(u, prev, w):
    T = u.shape[1]
    ucat = jnp.concatenate([prev.astype(u.dtype), u], axis=1)
    y = w[0] * ucat[:, 0:T]
    for k in range(1, CONV_W):
        y = y + w[k] * ucat[:, k:k + T]
    return y, ucat[:, -(CONV_W - 1):]


def cmp_to_sel_overlap(n_cmp, n_sel):
    c0 = np.arange(n_cmp)[:, None] * CMP_STRIDE
    s0 = np.arange(n_sel)[None, :] * SEL_BLOCK
    ov = np.minimum(c0 + CMP_BLOCK, s0 + SEL_BLOCK) - np.maximum(c0, s0)
    return np.clip(ov, 0, None).astype(np.float32) / CMP_BLOCK


def compress_blocks(kv, w_cmp, pe_cmp):
    L = kv.shape[1]
    n_cmp = (L - CMP_BLOCK) // CMP_STRIDE + 1
    idx = np.arange(n_cmp)[:, None] * CMP_STRIDE + np.arange(CMP_BLOCK)[None, :]
    blocks = kv[:, idx] + pe_cmp.transpose(1, 0, 2)[:, :, None, :]
    return jnp.einsum('bnscgd,csde->bncge', blocks, w_cmp)


def nsa_attend(q, gates, cmp_kv, sel_kv, win_kv, q_start, win_start):
    B, T, H, D = q.shape
    qb = math.gcd(T, Q_BLOCK)
    nb = T // qb
    scale = D ** -0.5
    n_cmp = cmp_kv.shape[1]
    cmp_end = jnp.asarray(np.arange(n_cmp) * CMP_STRIDE + CMP_BLOCK - 1, jnp.int32)
    L = sel_kv.shape[1]
    n_sel = -(-L // SEL_BLOCK)
    k_top = min(TOPK, n_sel)
    overlap = jnp.asarray(cmp_to_sel_overlap(n_cmp, n_sel))
    sel_pad = jnp.pad(sel_kv, ((0, 0), (0, n_sel * SEL_BLOCK - L), (0, 0), (0, 0), (0, 0)))
    sel_blocks = sel_pad.reshape(B, n_sel, SEL_BLOCK, 2, N_KV, D).transpose(0, 4, 1, 2, 3, 5)
    win_pad = jnp.pad(win_kv, ((0, 0), (WINDOW, 0), (0, 0), (0, 0), (0, 0)))
    qs = q.reshape(B, nb, qb, N_KV, HPG, D).transpose(1, 0, 2, 3, 4, 5)
    gs = gates.reshape(B, nb, qb, N_KV, HPG, N_BRANCH).transpose(1, 0, 2, 3, 4, 5)
    b_ix = jnp.arange(B)[:, None, None, None]
    g_ix = jnp.arange(N_KV)[None, :, None, None]
    j = jnp.arange(n_sel)

    def block(args):
        i, q_blk, g_blk = args
        q0 = q_start + i * qb
        t = q0 + jnp.arange(qb)
        s_c = jnp.einsum('bqghd,bngd->bgqhn', q_blk, cmp_kv[:, :, 0]) * scale
        m_c = cmp_end[None, :] <= t[:, None]
        p_c = masked_softmax(s_c, m_c[None, None, :, None, :])
        o_c = jnp.einsum('bgqhn,bngd->bqghd', p_c.astype(q.dtype), cmp_kv[:, :, 1])
        imp = jnp.einsum('bgqhn,nj->bgqj', p_c, overlap)
        valid = (j[None, :] * SEL_BLOCK) <= t[:, None]
        cur = t[:, None] // SEL_BLOCK
        forced = (j[None, :] == 0) | (j[None, :] == cur) | (j[None, :] == cur - 1)
        imp = jnp.where(valid, imp + jnp.where(forced, FORCE_BONUS, 0.0), NEG)
        _, idx = lax.top_k(imp, k_top)
        sel = sel_blocks[b_ix, g_ix, idx]
        sel = sel.reshape(B, N_KV, qb, k_top * SEL_BLOCK, 2, D)
        kpos = (idx[..., None] * SEL_BLOCK + jnp.arange(SEL_BLOCK)).reshape(B, N_KV, qb, k_top * SEL_BLOCK)
        m_s = (kpos <= t[None, None, :, None])[:, :, :, None, :]
        s_s = jnp.einsum('bqghd,bgqmd->bgqhm', q_blk, sel[..., 0, :]) * scale
        p_s = masked_softmax(s_s, m_s)
        o_s = jnp.einsum('bgqhm,bgqmd->bqghd', p_s.astype(q.dtype), sel[..., 1, :])
        w = lax.dynamic_slice_in_dim(win_pad, q0 - win_start, WINDOW + qb, axis=1)
        wpos = q0 - WINDOW + jnp.arange(WINDOW + qb)
        dlt = t[:, None] - wpos[None, :]
        m_w = (dlt >= 0) & (dlt < WINDOW) & (wpos[None, :] >= win_start)
        s_w = jnp.einsum('bqghd,bsgd->bgqhs', q_blk, w[:, :, 0]) * scale
        p_w = masked_softmax(s_w, m_w[None, None, :, None, :])
        o_w = jnp.einsum('bgqhs,bsgd->bqghd', p_w.astype(q.dtype), w[:, :, 1])
        return o_c * g_blk[..., 0:1] + o_s * g_blk[..., 1:2] + o_w * g_blk[..., 2:3]

    out = lax.map(block, (jnp.arange(nb), qs, gs))
    return out.transpose(1, 0, 2, 3, 4, 5).reshape(B, T, H * D)


def trunk_layer(x, mem_kv, cmp_past, sel_past, win_past, conv_prev, ffn_prev, q_start, win_keep,
                g_mix, w_in, w_cmp, pe_cmp, conv_w, g_out_nsa, g_out_conv, w_out,
                g_mem, w_mem_q, w_mem_o, g_ffn, w_ff_gate, w_ff_up, ffn_conv_w, w_ff_down):
    B, T, _ = x.shape
    h = rmsnorm(x, g_mix)
    offs = [int(o) for o in np.cumsum(SPLITS)[:-1]]
    q, kv_c, kv_s, kv_w, g_lin, c_b, c_c, c_h = jnp.split(h @ w_in, offs, axis=-1)
    q = q.reshape(B, T, N_HEADS, HEAD_DIM)
    kv_c = kv_c.reshape(B, T, 2, N_KV, HEAD_DIM)
    kv_s = kv_s.reshape(B, T, 2, N_KV, HEAD_DIM)
    kv_w = kv_w.reshape(B, T, 2, N_KV, HEAD_DIM)
    if cmp_past is None:
        cmp_full, sel_full, win_full, win_start = kv_c, kv_s, kv_w, 0
    else:
        cmp_full = jnp.concatenate([cmp_past.astype(x.dtype), kv_c], axis=1)
        sel_full = jnp.concatenate([sel_past.astype(x.dtype), kv_s], axis=1)
        win_full = jnp.concatenate([win_past.astype(x.dtype), kv_w], axis=1)
        win_start = q_start - win_past.shape[1]
    cmp_blk = compress_blocks(cmp_full, w_cmp, pe_cmp)
    gates = jax.nn.sigmoid(g_lin).reshape(B, T, N_HEADS, N_BRANCH)
    o_nsa = nsa_attend(q, gates, cmp_blk, sel_full, win_full, q_start, win_start)
    v, conv_state = causal_dwconv(c_c * c_h, conv_prev, conv_w)
    o_conv = c_b * v
    mix = jnp.concatenate([rmsnorm(o_nsa, g_out_nsa), rmsnorm(o_conv, g_out_conv)], axis=-1)
    x = x + mix @ w_out
    hm = rmsnorm(x, g_mem)
    qm = (hm @ w_mem_q).reshape(B, T, MEM_HEADS, MEM_HEAD_DIM)
    s = jnp.einsum('bthd,bmhd->bhtm', qm, mem_kv[:, :, 0].astype(x.dtype)) * (MEM_HEAD_DIM ** -0.5)
    p = jax.nn.softmax(s.astype(jnp.float32), axis=-1).astype(x.dtype)
    om = jnp.einsum('bhtm,bmhd->bthd', p, mem_kv[:, :, 1].astype(x.dtype)).reshape(B, T, D_MEM)
    x = x + om @ w_mem_o
    hf = rmsnorm(x, g_ffn)
    a, ffn_state = causal_dwconv(hf @ w_ff_gate, ffn_prev, ffn_conv_w)
    x = x + (jax.nn.silu(a) * (hf @ w_ff_up)) @ w_ff_down
    return x, kv_c, kv_s, win_full[:, -win_keep:], conv_state, ffn_state


def setup_inputs(seed: int = 0) -> dict:
    key = jax.random.key(seed)
    ks = iter(jax.random.split(key, 40))

    def nrm(shape, scale):
        return jax.random.normal(next(ks), shape, jnp.float32) * scale

    def gain(shape):
        return jnp.ones(shape, jnp.float32) + nrm(shape, 0.02)

    n_pages = PAST_LEN // PAGE_SIZE
    n_used = DEC_BATCH * n_pages
    n_phys = n_used + max(1, n_used // 4)
    win_buf = min(WINDOW, PAST_LEN)
    page_table = jax.random.permutation(next(ks), n_phys)[:n_used].reshape(DEC_BATCH, n_pages).astype(jnp.int32)
    return {
        'x_prompt': nrm((BATCH, SEQ, D_MODEL), 1.0),
        'x_sample': nrm((DEC_BATCH, DEC_SEQ, D_MODEL), 1.0),
        'cache_cmp_kv': nrm((DEPTH, n_phys, PAGE_SIZE, 2, N_KV, HEAD_DIM), 1.0),
        'cache_sel_kv': nrm((DEPTH, n_phys, PAGE_SIZE, 2, N_KV, HEAD_DIM), 1.0),
        'cache_win_kv': nrm((DEPTH, DEC_BATCH, win_buf, 2, N_KV, HEAD_DIM), 1.0),
        'cache_mem_kv': nrm((DEPTH, DEC_BATCH, MEM_LEN, 2, MEM_HEADS, MEM_HEAD_DIM), 1.0),
        'state_conv': nrm((DEPTH, DEC_BATCH, CONV_W - 1, D_CONV), 1.0),
        'state_ffn_conv': nrm((DEPTH, DEC_BATCH, CONV_W - 1, D_FF), 1.0),
        'page_table': page_table,
        'mem_prompt': nrm((BATCH, MEM_LEN, D_MODEL), 1.0),
        'g_mix': gain((DEPTH, D_MODEL)),
        'w_in': nrm((DEPTH, D_MODEL, D_IN), D_MODEL ** -0.5),
        'w_cmp': nrm((DEPTH, 2, CMP_BLOCK, HEAD_DIM, HEAD_DIM), (CMP_BLOCK * HEAD_DIM) ** -0.5),
        'pe_cmp': nrm((DEPTH, 2, CMP_BLOCK, HEAD_DIM), 0.1),
        'conv_w': nrm((DEPTH, CONV_W, D_CONV), CONV_W ** -0.5),
        'g_out_nsa': gain((DEPTH, D_NSA)),
        'g_out_conv': gain((DEPTH, D_CONV)),
        'w_out': nrm((DEPTH, D_MIX, D_MODEL), D_MIX ** -0.5),
        'g_mem_src': gain((DEPTH, D_MODEL)),
        'w_mem_kv': nrm((DEPTH, D_MODEL, 2 * D_MEM), D_MODEL ** -0.5),
        'g_mem': gain((DEPTH, D_MODEL)),
        'w_mem_q': nrm((DEPTH, D_MODEL, D_MEM), D_MODEL ** -0.5),
        'w_mem_o': nrm((DEPTH, D_MEM, D_MODEL), D_MEM ** -0.5),
        'g_ffn': gain((DEPTH, D_MODEL)),
        'w_ff_gate': nrm((DEPTH, D_MODEL, D_FF), D_MODEL ** -0.5),
        'w_ff_up': nrm((DEPTH, D_MODEL, D_FF), D_MODEL ** -0.5),
        'ffn_conv_w': nrm((DEPTH, CONV_W, D_FF), CONV_W ** -0.5),
        'w_ff_down': nrm((DEPTH, D_FF, D_MODEL), D_FF ** -0.5),
        'g_final': gain((D_MODEL,)),
    }


def reference(x_prompt, x_sample, cache_cmp_kv, cache_sel_kv, cache_win_kv, cache_mem_kv,
              state_conv, state_ffn_conv, page_table, mem_prompt,
              g_mix, w_in, w_cmp, pe_cmp, conv_w, g_out_nsa, g_out_conv, w_out,
              g_mem_src, w_mem_kv, g_mem, w_mem_q, w_mem_o, g_ffn, w_ff_gate, w_ff_up,
              ffn_conv_w, w_ff_down, g_final):
    bp, seq = x_prompt.shape[0], x_prompt.shape[1]
    bs = x_sample.shape[0]
    n_mem = mem_prompt.shape[1]
    n_pages = page_table.shape[1]
    past_len = n_pages * PAGE_SIZE
    win_buf = cache_win_kv.shape[2]
    xp, xs = x_prompt, x_sample
    p_cmp, p_sel, p_win, p_mem, p_conv, p_ffn = [], [], [], [], [], []
    s_cmp, s_sel, s_win, s_conv, s_ffn = [], [], [], [], []
    for l in range(DEPTH):
        weights = (g_mix[l], w_in[l], w_cmp[l], pe_cmp[l], conv_w[l], g_out_nsa[l], g_out_conv[l],
                   w_out[l], g_mem[l], w_mem_q[l], w_mem_o[l], g_ffn[l], w_ff_gate[l], w_ff_up[l],
                   ffn_conv_w[l], w_ff_down[l])
        mem_kv_p = (rmsnorm(mem_prompt, g_mem_src[l]) @ w_mem_kv[l]).reshape(bp, n_mem, 2, MEM_HEADS, MEM_HEAD_DIM)
        xp, kc, kse, kw, cst, fst = trunk_layer(
            xp, mem_kv_p, None, None, None,
            jnp.zeros((bp, CONV_W - 1, D_CONV), xp.dtype), jnp.zeros((bp, CONV_W - 1, D_FF), xp.dtype),
            0, min(WINDOW, seq), *weights)
        p_cmp.append(kc)
        p_sel.append(kse)
        p_win.append(kw)
        p_mem.append(mem_kv_p)
        p_conv.append(cst)
        p_ffn.append(fst)
        cmp_past = cache_cmp_kv[l][page_table].reshape(bs, past_len, 2, N_KV, HEAD_DIM)
        sel_past = cache_sel_kv[l][page_table].reshape(bs, past_len, 2, N_KV, HEAD_DIM)
        xs, kc2, ks2, kw2, cst2, fst2 = trunk_layer(
            xs, cache_mem_kv[l], cmp_past, sel_past, cache_win_kv[l], state_conv[l], state_ffn_conv[l],
            past_len, win_buf, *weights)
        s_cmp.append(kc2)
        s_sel.append(ks2)
        s_win.append(kw2)
        s_conv.append(cst2)
        s_ffn.append(fst2)
    y_prompt = rmsnorm(xp, g_final)
    y_sample = rmsnorm(xs, g_final)
    return (y_prompt, y_sample,
            jnp.stack(p_cmp), jnp.stack(p_sel), jnp.stack(p_win), jnp.stack(p_mem),
            jnp.stack(p_conv), jnp.stack(p_ffn),
            jnp.stack(s_cmp), jnp.stack(s_sel), jnp.stack(s_win), jnp.stack(s_conv), jnp.stack(s_ffn))
```

```python
import functools

import numpy as np
import jax
import jax.numpy as jnp
from jax import lax
from jax.experimental import pallas as pl
from jax.experimental.pallas import tpu as pltpu

F32 = jnp.float32
BF16 = jnp.bfloat16

D_MODEL = 2048
D_NSA = 1024
D_CONV = 1024
HEAD_DIM = 64
N_HEADS = 16
N_KV = 2
HPG = 8
N_BRANCH = 3
KV_W = 2 * N_KV * HEAD_DIM
CMP_BLOCK = 32
CMP_STRIDE = 16
SEL_BLOCK = 64
TOPK = 16
WINDOW = 512
Q_BLOCK = 128
FORCE_BONUS = 1e4
D_FF = 5632
MEM_HEADS = 4
MEM_HEAD_DIM = 128
D_MEM = 512
PAGE = 128
EPS = 1e-6
NEG = -1e30

COL_Q = 0
COL_CB = 1024
COL_CC = 2048
COL_CH = 3072
COL_KC = 4096
COL_KS = 4352
COL_KW = 4608
COL_G = 4864
D_IN_PAD = 4992

VMEM_LIMIT = 56 * 1024 * 1024


def _cparams(sem):
    return pltpu.CompilerParams(dimension_semantics=sem, vmem_limit_bytes=VMEM_LIMIT)


def _rms(x, g):
    return x * lax.rsqrt(jnp.mean(x * x, axis=-1, keepdims=True) + EPS) * g


def _dot(a, b):
    return jnp.dot(a, b, preferred_element_type=F32)


def _dot_nt(a, b):
    return lax.dot_general(a, b, (((1,), (1,)), ((), ())), preferred_element_type=F32)


def _masked_softmax(s, mask):
    s = jnp.where(mask, s, NEG)
    m = jnp.max(s, axis=-1, keepdims=True)
    e = jnp.where(mask, jnp.exp(s - m), 0.0)
    return e / jnp.maximum(jnp.sum(e, axis=-1, keepdims=True), 1e-30)


def _sigmoid(x):
    return 1.0 / (1.0 + jnp.exp(-x))


def _rms_matmul_kernel(x_ref, g_ref, w_ref, o_ref, h_ref):
    @pl.when(pl.program_id(1) == 0)
    def _():
        h_ref[...] = _rms(x_ref[...], g_ref[...]).astype(BF16)

    o_ref[...] = _dot(h_ref[...], w_ref[...]).astype(o_ref.dtype)


def rms_matmul(x, g, w, tm, tn):
    m, d = x.shape
    n = w.shape[1]
    return pl.pallas_call(
        _rms_matmul_kernel,
        grid=(m // tm, n // tn),
        in_specs=[pl.BlockSpec((tm, d), lambda i, j: (i, 0)),
                  pl.BlockSpec((1, d), lambda i, j: (0, 0)),
                  pl.BlockSpec((d, tn), lambda i, j: (0, j))],
        out_specs=pl.BlockSpec((tm, tn), lambda i, j: (i, j)),
        out_shape=jax.ShapeDtypeStruct((m, n), F32),
        scratch_shapes=[pltpu.VMEM((tm, d), BF16)],
        compiler_params=_cparams(("arbitrary", "arbitrary")),
        name="rms_matmul",
    )(x, g, w)


def _matmul_res_kernel(x_ref, a_ref, w_ref, o_ref):
    o_ref[...] = x_ref[...] + _dot(a_ref[...].astype(BF16), w_ref[...])


def matmul_residual(x, a, w, tm):
    m, n = x.shape
    k = a.shape[1]
    return pl.pallas_call(
        _matmul_res_kernel,
        grid=(m // tm,),
        in_specs=[pl.BlockSpec((tm, n), lambda i: (i, 0)),
                  pl.BlockSpec((tm, k), lambda i: (i, 0)),
                  pl.BlockSpec((k, n), lambda i: (0, 0))],
        out_specs=pl.BlockSpec((tm, n), lambda i: (i, 0)),
        out_shape=jax.ShapeDtypeStruct((m, n), F32),
        compiler_params=_cparams(("arbitrary",)),
        name="matmul_residual",
    )(x, a, w)


def _compress_accumulate(load_rows, pe_ref, w_ref, nrow):
    acc0 = jnp.zeros((nrow, KV_W), F32)
    acc1 = jnp.zeros((nrow, KV_W), F32)
    for r in range(CMP_STRIDE):
        rows = load_rows(r)
        y0 = (rows + pe_ref[r:r + 1, :]).astype(BF16)
        y1 = (rows + pe_ref[CMP_STRIDE + r:CMP_STRIDE + r + 1, :]).astype(BF16)
        acc0 = acc0 + _dot(y0, w_ref[r])
        acc1 = acc1 + _dot(y1, w_ref[CMP_STRIDE + r])
    return acc0, acc1


def _compress_prompt_kernel(k_ref, v_ref, pe_ref, w_ref, o_ref, *, nblk):
    def load_rows(r):
        rows = pl.ds(r, nblk, stride=CMP_STRIDE)
        return jnp.concatenate([k_ref[rows, :], v_ref[rows, :]], axis=-1)

    acc0, acc1 = _compress_accumulate(load_rows, pe_ref, w_ref, nblk)
    o_ref[...] = (acc0 + pltpu.roll(acc1, nblk - 1, 0)).astype(BF16)


def compress_prompt(p_all, pe2, wbd, batch, seq):
    nblk = seq // CMP_STRIDE
    half = KV_W // 2
    return pl.pallas_call(
        functools.partial(_compress_prompt_kernel, nblk=nblk),
        grid=(batch,),
        in_specs=[pl.BlockSpec((seq, half), lambda b: (b, COL_KC // half)),
                  pl.BlockSpec((seq, half), lambda b: (b, COL_KC // half + 1)),
                  pl.BlockSpec((CMP_BLOCK, KV_W), lambda b: (0, 0)),
                  pl.BlockSpec((CMP_BLOCK, KV_W, KV_W), lambda b: (0, 0, 0))],
        out_specs=pl.BlockSpec((None, nblk, KV_W), lambda b: (b, 0, 0)),
        out_shape=jax.ShapeDtypeStruct((batch, nblk, KV_W), BF16),
        compiler_params=_cparams(("arbitrary",)),
        name="compress_prompt",
    )(p_all, p_all, pe2, wbd)


def _compress_paged_kernel(pt_ref, *refs, bt, npages):
    page_refs = refs[:bt * npages]
    pe_ref, w_ref, o_ref, kbuf, vbuf = refs[bt * npages:]
    nblk = npages * PAGE // CMP_STRIDE
    half = KV_W // 2
    for bb in range(bt):
        for p in range(npages):
            page = page_refs[bb * npages + p]
            kbuf[bb, p * PAGE:(p + 1) * PAGE, :] = page[:, 0:half]
            vbuf[bb, p * PAGE:(p + 1) * PAGE, :] = page[:, half:KV_W]

    def load_rows(r):
        rows = pl.ds(r, nblk, stride=CMP_STRIDE)
        both = jnp.concatenate([kbuf[:, rows, :], vbuf[:, rows, :]], axis=-1)
        return both.reshape(bt * nblk, KV_W)

    acc0, acc1 = _compress_accumulate(load_rows, pe_ref, w_ref, bt * nblk)
    out = (acc0 + pltpu.roll(acc1, bt * nblk - 1, 0)).astype(BF16)
    o_ref[...] = out.reshape(bt, nblk, KV_W)


def compress_paged(cache, page_idx, pe2, wbd, bt):
    nb, npages = page_idx.shape
    nblk = npages * PAGE // CMP_STRIDE
    page_specs = [
        pl.BlockSpec((None, PAGE, KV_W), functools.partial(
            lambda i, pt, bb, p: (pt[i * bt + bb, p], 0, 0), bb=bb, p=p))
        for bb in range(bt) for p in range(npages)]
    grid_spec = pltpu.PrefetchScalarGridSpec(
        num_scalar_prefetch=1,
        grid=(nb // bt,),
        in_specs=page_specs + [pl.BlockSpec((CMP_BLOCK, KV_W), lambda i, pt: (0, 0)),
                               pl.BlockSpec((CMP_BLOCK, KV_W, KV_W), lambda i, pt: (0, 0, 0))],
        out_specs=pl.BlockSpec((bt, nblk, KV_W), lambda i, pt: (i, 0, 0)),
        scratch_shapes=[pltpu.VMEM((bt, npages * PAGE, KV_W // 2), F32)] * 2)
    return pl.pallas_call(
        functools.partial(_compress_paged_kernel, bt=bt, npages=npages),
        grid_spec=grid_spec,
        out_shape=jax.ShapeDtypeStruct((nb, nblk, KV_W), BF16),
        compiler_params=_cparams(("arbitrary",)),
        name="compress_paged",
    )(page_idx, *([cache] * (bt * npages)), pe2, wbd)


def _topk_mask(imp, idx, n_cand, axis):
    rank = jnp.zeros(imp.shape, jnp.int32)
    for jp in range(n_cand):
        c = imp[jp:jp + 1, :] if axis == 0 else imp[:, jp:jp + 1]
        beats = (c > imp) | ((c == imp) & (idx > jp))
        rank = rank + beats.astype(jnp.int32)
    return (rank < TOPK).astype(F32)


def _importance(imp, idx, tpos, n_sel):
    valid = (idx * SEL_BLOCK <= tpos) & (idx < n_sel)
    cur = tpos // SEL_BLOCK
    forced = (idx == 0) | (idx == cur) | (idx == cur - 1)
    return jnp.where(valid, imp + jnp.where(forced, FORCE_BONUS, 0.0), NEG)


KEY_TILE = 512


def _nsa_prompt_kernel(q_ref, gate_ref, cmp_ref, ks_ref, kw_ref, ovt_ref, o_ref,
                       ksb, vsb, kwb, vwb, *, seq):
    g = pl.program_id(1)
    i = pl.program_id(2)
    n_rows = HPG * Q_BLOCK
    n_cmp_pad = seq // CMP_STRIDE
    n_sel = seq // SEL_BLOCK

    @pl.when(i == 0)
    def _():
        ks = ks_ref[...]
        kw = kw_ref[...]
        g0 = g == 0
        ksb[...] = jnp.where(g0, ks[:, 0:64], ks[:, 64:128]).astype(BF16)
        vsb[...] = jnp.where(g0, ks[:, 128:192], ks[:, 192:256]).astype(BF16)
        kwb[...] = jnp.where(g0, kw[:, 0:64], kw[:, 64:128]).astype(BF16)
        vwb[...] = jnp.where(g0, kw[:, 128:192], kw[:, 192:256]).astype(BF16)

    t0 = i * Q_BLOCK
    qb = q_ref[...]
    qs = jnp.concatenate([qb[:, h * HEAD_DIM:(h + 1) * HEAD_DIM] for h in range(HPG)],
                         axis=0).astype(BF16)

    kc = cmp_ref[0]
    vc = cmp_ref[1]
    s_c = _dot_nt(qs, kc)
    row = lax.broadcasted_iota(jnp.int32, (n_rows, n_cmp_pad), 0)
    ncol = lax.broadcasted_iota(jnp.int32, (n_rows, n_cmp_pad), 1)
    tq = (row & (Q_BLOCK - 1)) + t0
    m_c = (ncol * CMP_STRIDE + (CMP_BLOCK - 1) <= tq) & (ncol < n_cmp_pad - 1)
    p_c = _masked_softmax(s_c, m_c).astype(BF16)
    o_c = _dot(p_c, vc)

    po = _dot_nt(ovt_ref[...], p_c)
    imp_t = po[:, 0:Q_BLOCK]
    for h in range(1, HPG):
        imp_t = imp_t + po[:, h * Q_BLOCK:(h + 1) * Q_BLOCK]
    jj = lax.broadcasted_iota(jnp.int32, (128, Q_BLOCK), 0)
    tt = lax.broadcasted_iota(jnp.int32, (128, Q_BLOCK), 1) + t0
    imp2 = _importance(imp_t, jj, tt, n_sel)
    sel = _topk_mask(imp2, jj, n_sel, 0).T.astype(BF16)

    def sel_tile(c, carry):
        m, l, acc = carry
        k0 = pl.multiple_of(c * KEY_TILE, KEY_TILE)
        kt = ksb[pl.ds(k0, KEY_TILE), :]
        vt = vsb[pl.ds(k0, KEY_TILE), :]
        s = _dot_nt(qs, kt)
        jm = lax.broadcasted_iota(jnp.int32, (128, KEY_TILE), 0)
        kk = lax.broadcasted_iota(jnp.int32, (128, KEY_TILE), 1)
        expand = (jm == c * (KEY_TILE // SEL_BLOCK) + kk // SEL_BLOCK).astype(BF16)
        msel = _dot(sel, expand)
        tq1 = lax.broadcasted_iota(jnp.int32, (Q_BLOCK, KEY_TILE), 0) + t0
        ok = jnp.where((msel > 0.5) & (kk + k0 <= tq1), 1.0, 0.0)
        okf = jnp.concatenate([ok] * HPG, axis=0) > 0.5
        s = jnp.where(okf, s, NEG)
        m_new = jnp.maximum(m, jnp.max(s, axis=-1, keepdims=True))
        alpha = jnp.exp(m - m_new)
        p = jnp.where(okf, jnp.exp(s - m_new), 0.0)
        l = alpha * l + jnp.sum(p, axis=-1, keepdims=True)
        acc = alpha * acc + _dot(p.astype(BF16), vt)
        return m_new, l, acc

    m0 = jnp.full((n_rows, 1), NEG, F32)
    l0 = jnp.zeros((n_rows, 1), F32)
    a0 = jnp.zeros((n_rows, HEAD_DIM), F32)
    n_tiles = (t0 + Q_BLOCK + KEY_TILE - 1) // KEY_TILE
    _, l_s, acc_s = lax.fori_loop(0, n_tiles, sel_tile, (m0, l0, a0))
    o_s = acc_s / jnp.maximum(l_s, 1e-30)

    span = WINDOW + Q_BLOCK
    ws = pl.multiple_of(jnp.maximum(t0 - WINDOW, 0), Q_BLOCK)
    s_w = _dot_nt(qs, kwb[pl.ds(ws, span), :])
    roww = lax.broadcasted_iota(jnp.int32, (n_rows, span), 0)
    colw = lax.broadcasted_iota(jnp.int32, (n_rows, span), 1)
    dlt = (roww & (Q_BLOCK - 1)) + t0 - (colw + ws)
    p_w = _masked_softmax(s_w, (dlt >= 0) & (dlt < WINDOW)).astype(BF16)
    o_w = _dot(p_w, vwb[pl.ds(ws, span), :])

    gs = _sigmoid(gate_ref[...])
    gsel = jnp.where(g == 0, gs[:, 0:HPG * N_BRANCH], gs[:, HPG * N_BRANCH:2 * HPG * N_BRANCH])
    for h in range(HPG):
        r0, r1 = h * Q_BLOCK, (h + 1) * Q_BLOCK
        c0 = h * N_BRANCH
        o_ref[:, h * HEAD_DIM:(h + 1) * HEAD_DIM] = (
            o_c[r0:r1] * gsel[:, c0:c0 + 1] + o_s[r0:r1] * gsel[:, c0 + 1:c0 + 2]
            + o_w[r0:r1] * gsel[:, c0 + 2:c0 + 3])


def nsa_prompt(p_all, cmp_t, ovt, batch, seq):
    nqb = seq // Q_BLOCK
    gw = HPG * HEAD_DIM
    n_cmp_pad = seq // CMP_STRIDE
    return pl.pallas_call(
        functools.partial(_nsa_prompt_kernel, seq=seq),
        grid=(batch, N_KV, nqb),
        in_specs=[
            pl.BlockSpec((Q_BLOCK, gw), lambda b, g, i: (b * nqb + i, g)),
            pl.BlockSpec((Q_BLOCK, 128), lambda b, g, i: (b * nqb + i, COL_G // 128)),
            pl.BlockSpec((None, 2, None, n_cmp_pad, HEAD_DIM), lambda b, g, i: (b, 0, g, 0, 0)),
            pl.BlockSpec((seq, KV_W), lambda b, g, i: (b, COL_KS // KV_W)),
            pl.BlockSpec((seq, KV_W), lambda b, g, i: (b, COL_KW // KV_W)),
            pl.BlockSpec((128, n_cmp_pad), lambda b, g, i: (0, 0)),
        ],
        out_specs=pl.BlockSpec((Q_BLOCK, gw), lambda b, g, i: (b * nqb + i, g)),
        out_shape=jax.ShapeDtypeStruct((batch * seq, D_NSA), F32),
        scratch_shapes=[pltpu.VMEM((seq, HEAD_DIM), BF16)] * 4,
        compiler_params=_cparams(("arbitrary", "arbitrary", "arbitrary")),
        name="nsa_prompt",
    )(p_all, p_all, cmp_t, p_all, p_all, ovt)


def _pick_group(o2, rg):
    return jnp.where(rg == 0, o2[:, 0:HEAD_DIM], o2[:, HEAD_DIM:2 * HEAD_DIM])


def _nsa_sample_kernel(pt_ref, *refs, npages, dec_seq):
    page_refs = refs[:npages]
    (q_ref, gate_ref, cmp_ref, ksn_ref, win_ref, kwn_ref, ov_ref, exp_ref,
     o_ref, kvb, wkv) = refs[npages:]
    past = npages * PAGE
    n_cmp = (past + dec_seq - CMP_BLOCK) // CMP_STRIDE + 1
    n_sel = -(-(past + dec_seq) // SEL_BLOCK)
    n_rows = HPG * N_KV * dec_seq
    n_keys = past + PAGE
    win_buf = win_ref.shape[0]
    w_keys = win_buf + PAGE

    q = q_ref[...].astype(BF16)
    rowi = lax.broadcasted_iota(jnp.int32, (n_rows, 1), 0)
    rg = (rowi // dec_seq) % N_KV
    tpos = past + rowi % dec_seq

    cm = cmp_ref[...]
    n_cmp_pad = cm.shape[0]
    s_c = _dot_nt(q, cm[:, 0:128])
    ncol = lax.broadcasted_iota(jnp.int32, (n_rows, n_cmp_pad), 1)
    m_c = (ncol * CMP_STRIDE + (CMP_BLOCK - 1) <= tpos) & (ncol < n_cmp)
    p_c = _masked_softmax(s_c, m_c).astype(BF16)
    o_c = _pick_group(_dot(p_c, cm[:, 128:256]), rg)

    po = _dot(p_c, ov_ref[...])
    gt = N_KV * dec_seq
    imp = po[0:gt]
    for h in range(1, HPG):
        imp = imp + po[h * gt:(h + 1) * gt]
    jj = lax.broadcasted_iota(jnp.int32, (gt, 128), 1)
    t8 = past + lax.broadcasted_iota(jnp.int32, (gt, 128), 0) % dec_seq
    imp2 = _importance(imp, jj, t8, n_sel)
    sel8 = _topk_mask(imp2, jj, n_sel, 1)
    sel = jnp.concatenate([sel8] * HPG, axis=0).astype(BF16)
    msel = _dot(sel, exp_ref[...])

    pad = jnp.zeros((PAGE - dec_seq, KV_W), F32)
    for p in range(npages):
        kvb[p * PAGE:(p + 1) * PAGE, :] = page_refs[p][...].astype(BF16)
    kvb[past:past + PAGE, :] = jnp.concatenate([ksn_ref[...], pad], axis=0).astype(BF16)
    kcol = lax.broadcasted_iota(jnp.int32, (n_rows, n_keys), 1)
    s_s = _dot_nt(q, kvb[:, 0:128])
    p_s = _masked_softmax(s_s, (msel > 0.5) & (kcol <= tpos)).astype(BF16)
    o_s = _pick_group(_dot(p_s, kvb[:, 128:256]), rg)

    wkv[0:win_buf, :] = win_ref[...].astype(BF16)
    wkv[win_buf:w_keys, :] = jnp.concatenate([kwn_ref[...], pad], axis=0).astype(BF16)
    wcol = lax.broadcasted_iota(jnp.int32, (n_rows, w_keys), 1)
    dlt = tpos - (wcol + (past - win_buf))
    s_w = _dot_nt(q, wkv[:, 0:128])
    p_w = _masked_softmax(s_w, (dlt >= 0) & (dlt < WINDOW)).astype(BF16)
    o_w = _pick_group(_dot(p_w, wkv[:, 128:256]), rg)

    gs = _sigmoid(gate_ref[...])
    o_ref[...] = o_c * gs[:, 0:1] + o_s * gs[:, 1:2] + o_w * gs[:, 2:3]


def nsa_sample(cache_sel, page_idx, qbd, gates, cmp_s, ks_new, win_past, kw_new, ov_s, expand):
    nb, npages = page_idx.shape
    dec_seq = ks_new.shape[1]
    n_rows = qbd.shape[1]
    win_buf = win_past.shape[1]
    n_cmp_pad = cmp_s.shape[1]
    n_keys = npages * PAGE + PAGE
    page_specs = [
        pl.BlockSpec((None, PAGE, KV_W), functools.partial(lambda b, pt, p: (pt[b, p], 0, 0), p=p))
        for p in range(npages)]
    per_b = lambda b, pt: (b, 0, 0)
    const2 = lambda b, pt: (0, 0)
    grid_spec = pltpu.PrefetchScalarGridSpec(
        num_scalar_prefetch=1,
        grid=(nb,),
        in_specs=page_specs + [
            pl.BlockSpec((None, n_rows, 128), per_b),
            pl.BlockSpec((None, n_rows, N_BRANCH), per_b),
            pl.BlockSpec((None, n_cmp_pad, KV_W), per_b),
            pl.BlockSpec((None, dec_seq, KV_W), per_b),
            pl.BlockSpec((None, win_buf, KV_W), per_b),
            pl.BlockSpec((None, dec_seq, KV_W), per_b),
            pl.BlockSpec((n_cmp_pad, 128), const2),
            pl.BlockSpec((128, n_keys), const2),
        ],
        out_specs=pl.BlockSpec((None, n_rows, HEAD_DIM), per_b),
        scratch_shapes=[pltpu.VMEM((n_keys, KV_W), BF16),
                        pltpu.VMEM((win_buf + PAGE, KV_W), BF16)])
    return pl.pallas_call(
        functools.partial(_nsa_sample_kernel, npages=npages, dec_seq=dec_seq),
        grid_spec=grid_spec,
        out_shape=jax.ShapeDtypeStruct((nb, n_rows, HEAD_DIM), F32),
        compiler_params=_cparams(("arbitrary",)),
        name="nsa_sample",
    )(page_idx, *([cache_sel] * npages), qbd, gates, cmp_s, ks_new, win_past, kw_new, ov_s, expand)


def _shift_rows(u, p1, p2):
    r = lax.broadcasted_iota(jnp.int32, (u.shape[0], 1), 0)
    u1 = jnp.where(r >= 1, pltpu.roll(u, 1, 0), p1)
    u2 = jnp.where(r >= 2, pltpu.roll(u, 2, 0), jnp.where(r == 1, p1, p2))
    return u1, u2


def _mix_project(o_nsa, o_conv, gn, gc, w, x):
    mix = jnp.concatenate([_rms(o_nsa, gn), _rms(o_conv, gc)], axis=-1).astype(BF16)
    return x + _dot(mix, w)


def _mixout_prompt_kernel(on_ref, cb_ref, cc_ref, ch_ref, cch_ref, chh_ref, cw_ref, gn_ref, gc_ref,
                          w_ref, x_ref, o_ref, ut_ref, *, tiles_per_seq):
    first = (pl.program_id(0) % tiles_per_seq) == 0
    u = cc_ref[...] * ch_ref[...]
    uh = jnp.where(first, 0.0, cch_ref[...] * chh_ref[...])
    u1, u2 = _shift_rows(u, uh[7:8], uh[6:7])
    cw = cw_ref[...]
    v = cw[0:1] * u2 + cw[1:2] * u1 + cw[2:3] * u
    o_ref[...] = _mix_project(on_ref[...], cb_ref[...] * v, gn_ref[...], gc_ref[...],
                              w_ref[...], x_ref[...])
    ut_ref[...] = u[u.shape[0] - 8:, :]


def mixout_prompt(o_nsa, p_all, cw, gn, gc, w, x, tm, seq):
    m = x.shape[0]
    nt = m // tm
    cblk = lambda c: (lambda i: (i, c // D_CONV))
    halo = lambda c: (lambda i: (jnp.maximum(i * (tm // 8) - 1, 0), c // D_CONV))
    const = lambda i: (0, 0)
    return pl.pallas_call(
        functools.partial(_mixout_prompt_kernel, tiles_per_seq=seq // tm),
        grid=(nt,),
        in_specs=[pl.BlockSpec((tm, D_NSA), lambda i: (i, 0)),
                  pl.BlockSpec((tm, D_CONV), cblk(COL_CB)),
                  pl.BlockSpec((tm, D_CONV), cblk(COL_CC)),
                  pl.BlockSpec((tm, D_CONV), cblk(COL_CH)),
                  pl.BlockSpec((8, D_CONV), halo(COL_CC)),
                  pl.BlockSpec((8, D_CONV), halo(COL_CH)),
                  pl.BlockSpec((8, D_CONV), const),
                  pl.BlockSpec((1, D_NSA), const),
                  pl.BlockSpec((1, D_CONV), const),
                  pl.BlockSpec((D_MODEL, D_MODEL), const),
                  pl.BlockSpec((tm, D_MODEL), lambda i: (i, 0))],
        out_specs=[pl.BlockSpec((tm, D_MODEL), lambda i: (i, 0)),
                   pl.BlockSpec((None, 8, D_CONV), lambda i: (i, 0, 0))],
        out_shape=[jax.ShapeDtypeStruct((m, D_MODEL), F32),
                   jax.ShapeDtypeStruct((nt, 8, D_CONV), F32)],
        compiler_params=_cparams(("arbitrary",)),
        name="mixout_prompt",
    )(o_nsa, p_all, p_all, p_all, p_all, p_all, cw, gn, gc, w, x)


def _mixout_sample_kernel(on_ref, cb_ref, cc_ref, ch_ref, st_ref, cw_ref, gn_ref, gc_ref,
                          w_ref, x_ref, o_ref, nst_ref, *, dec_seq):
    ucat = [st_ref[0], st_ref[1]] + [cc_ref[t] * ch_ref[t] for t in range(dec_seq)]
    cw = cw_ref[...]
    for t in range(dec_seq):
        v = cw[0:1] * ucat[t] + cw[1:2] * ucat[t + 1] + cw[2:3] * ucat[t + 2]
        o_ref[t] = _mix_project(on_ref[t], cb_ref[t] * v, gn_ref[...], gc_ref[...],
                                w_ref[...], x_ref[t])
    nst_ref[0] = ucat[dec_seq]
    nst_ref[1] = ucat[dec_seq + 1]


def mixout_sample(o_nsa, p_s, state, cw, gn, gc, w, x):
    dec_seq, nb, _ = x.shape
    cblk = lambda c: (lambda i: (0, 0, c // D_CONV))
    full3 = lambda i: (0, 0, 0)
    const = lambda i: (0, 0)
    return pl.pallas_call(
        functools.partial(_mixout_sample_kernel, dec_seq=dec_seq),
        grid=(1,),
        in_specs=[pl.BlockSpec((dec_seq, nb, D_NSA), full3),
                  pl.BlockSpec((dec_seq, nb, D_CONV), cblk(COL_CB)),
                  pl.BlockSpec((dec_seq, nb, D_CONV), cblk(COL_CC)),
                  pl.BlockSpec((dec_seq, nb, D_CONV), cblk(COL_CH)),
                  pl.BlockSpec((2, nb, D_CONV), full3),
                  pl.BlockSpec((8, D_CONV), const),
                  pl.BlockSpec((1, D_NSA), const),
                  pl.BlockSpec((1, D_CONV), const),
                  pl.BlockSpec((D_MODEL, D_MODEL), const),
                  pl.BlockSpec((dec_seq, nb, D_MODEL), full3)],
        out_specs=[pl.BlockSpec((dec_seq, nb, D_MODEL), full3),
                   pl.BlockSpec((2, nb, D_CONV), full3)],
        out_shape=[jax.ShapeDtypeStruct((dec_seq, nb, D_MODEL), F32),
                   jax.ShapeDtypeStruct((2, nb, D_CONV), F32)],
        compiler_params=_cparams(("arbitrary",)),
        name="mixout_sample",
    )(o_nsa, p_s, p_s, p_s, state, cw, gn, gc, w, x)


MEM_SCALE = MEM_HEAD_DIM ** -0.5


def _mem_heads(qm, kv):
    outs = []
    for h in range(MEM_HEADS):
        c0, c1 = h * MEM_HEAD_DIM, (h + 1) * MEM_HEAD_DIM
        s = _dot_nt(qm[:, c0:c1], kv[:, c0:c1]) * MEM_SCALE
        e = jnp.exp(s - jnp.max(s, axis=-1, keepdims=True))
        p = (e / jnp.sum(e, axis=-1, keepdims=True)).astype(BF16)
        outs.append(_dot(p, kv[:, D_MEM + c0:D_MEM + c1]))
    return jnp.concatenate(outs, axis=-1).astype(BF16)


def _mem_prompt_kernel(x_ref, g_ref, wq_ref, kv_ref, wo_ref, o_ref):
    x = x_ref[...]
    qm = _dot(_rms(x, g_ref[...]).astype(BF16), wq_ref[...]).astype(BF16)
    om = _mem_heads(qm, kv_ref[...].astype(BF16))
    o_ref[...] = x + _dot(om, wo_ref[...])


def mem_prompt_attn(x, g, wq, mem_kv, wo, tm, seq):
    m = x.shape[0]
    mem_len = mem_kv.shape[1]
    tps = seq // tm
    const = lambda i: (0, 0)
    return pl.pallas_call(
        _mem_prompt_kernel,
        grid=(m // tm,),
        in_specs=[pl.BlockSpec((tm, D_MODEL), lambda i: (i, 0)),
                  pl.BlockSpec((1, D_MODEL), const),
                  pl.BlockSpec((D_MODEL, D_MEM), const),
                  pl.BlockSpec((None, mem_len, 2 * D_MEM), lambda i: (i // tps, 0, 0)),
                  pl.BlockSpec((D_MEM, D_MODEL), const)],
        out_specs=pl.BlockSpec((tm, D_MODEL), lambda i: (i, 0)),
        out_shape=jax.ShapeDtypeStruct((m, D_MODEL), F32),
        compiler_params=_cparams(("arbitrary",)),
        name="mem_prompt_attn",
    )(x, g, wq, mem_kv, wo)


def _mem_sample_kernel(q_ref, kv_ref, o_ref, *, bt):
    for bb in range(bt):
        o_ref[bb] = _mem_heads(q_ref[bb].astype(BF16), kv_ref[bb].astype(BF16)).astype(F32)


def mem_sample_attn(qm, mem_kv, bt):
    nb, dec_seq, _ = qm.shape
    mem_len = mem_kv.shape[1]
    return pl.pallas_call(
        functools.partial(_mem_sample_kernel, bt=bt),
        grid=(nb // bt,),
        in_specs=[pl.BlockSpec((bt, dec_seq, D_MEM), lambda i: (i, 0, 0)),
                  pl.BlockSpec((bt, mem_len, 2 * D_MEM), lambda i: (i, 0, 0))],
        out_specs=pl.BlockSpec((bt, dec_seq, D_MEM), lambda i: (i, 0, 0)),
        out_shape=jax.ShapeDtypeStruct((nb, dec_seq, D_MEM), F32),
        compiler_params=_cparams(("arbitrary",)),
        name="mem_sample_attn",
    )(qm, mem_kv)


def _silu(a):
    return a * _sigmoid(a)


def _ffn_prompt_kernel(x_ref, xh_ref, g_ref, wg_ref, wu_ref, cw_ref, wd_ref, gf_ref,
                       o_ref, gt_ref, h_s, hh_s, acc_s, *, tiles_per_seq, final):
    j = pl.program_id(1)
    tm = x_ref.shape[0]

    @pl.when(j == 0)
    def _():
        first = (pl.program_id(0) % tiles_per_seq) == 0
        h_s[...] = _rms(x_ref[...], g_ref[...]).astype(BF16)
        hh = jnp.where(first, 0.0, _rms(xh_ref[...], g_ref[...]))
        hh_s[...] = jnp.concatenate([hh, jnp.zeros_like(hh)], axis=0).astype(BF16)
        acc_s[...] = jnp.zeros_like(acc_s)

    h = h_s[...]
    gate = _dot(h, wg_ref[...])
    gate_h = _dot(hh_s[...], wg_ref[...])
    g1, g2 = _shift_rows(gate, gate_h[7:8], gate_h[6:7])
    cw = cw_ref[...]
    a = cw[0:1] * g2 + cw[1:2] * g1 + cw[2:3] * gate
    z = (_silu(a) * _dot(h, wu_ref[...])).astype(BF16)
    acc_s[...] += _dot(z, wd_ref[...])
    gt_ref[...] = gate[tm - 8:, :]

    @pl.when(j == pl.num_programs(1) - 1)
    def _():
        y = x_ref[...] + acc_s[...]
        o_ref[...] = _rms(y, gf_ref[...]) if final else y


def ffn_prompt(x, g, wg, wu, cw, wd, gf, tm, tn, seq, final):
    m = x.shape[0]
    nt = m // tm
    nj = D_FF // tn
    const = lambda i, j: (0, 0)
    return pl.pallas_call(
        functools.partial(_ffn_prompt_kernel, tiles_per_seq=seq // tm, final=final),
        grid=(nt, nj),
        in_specs=[pl.BlockSpec((tm, D_MODEL), lambda i, j: (i, 0)),
                  pl.BlockSpec((8, D_MODEL), lambda i, j: (jnp.maximum(i * (tm // 8) - 1, 0), 0)),
                  pl.BlockSpec((1, D_MODEL), const),
                  pl.BlockSpec((D_MODEL, tn), lambda i, j: (0, j)),
                  pl.BlockSpec((D_MODEL, tn), lambda i, j: (0, j)),
                  pl.BlockSpec((8, tn), lambda i, j: (0, j)),
                  pl.BlockSpec((tn, D_MODEL), lambda i, j: (j, 0)),
                  pl.BlockSpec((1, D_MODEL), const)],
        out_specs=[pl.BlockSpec((tm, D_MODEL), lambda i, j: (i, 0)),
                   pl.BlockSpec((None, 8, tn), lambda i, j: (i, 0, j))],
        out_shape=[jax.ShapeDtypeStruct((m, D_MODEL), F32),
                   jax.ShapeDtypeStruct((nt, 8, D_FF), F32)],
        scratch_shapes=[pltpu.VMEM((tm, D_MODEL), BF16),
                        pltpu.VMEM((16, D_MODEL), BF16),
                        pltpu.VMEM((tm, D_MODEL), F32)],
        compiler_params=_cparams(("arbitrary", "arbitrary")),
        name="ffn_prompt",
    )(x, x, g, wg, wu, cw, wd, gf)


def _ffn_sample_kernel(x_ref, st_ref, g_ref, wg_ref, wu_ref, cw_ref, wd_ref, gf_ref,
                       o_ref, nst_ref, h_s, acc_s, *, dec_seq, final):
    j = pl.program_id(0)

    @pl.when(j == 0)
    def _():
        for t in range(dec_seq):
            h_s[t] = _rms(x_ref[t], g_ref[...]).astype(BF16)
        acc_s[...] = jnp.zeros_like(acc_s)

    gcat = [st_ref[0], st_ref[1]] + [_dot(h_s[t], wg_ref[...]) for t in range(dec_seq)]
    cw = cw_ref[...]
    for t in range(dec_seq):
        a = cw[0:1] * gcat[t] + cw[1:2] * gcat[t + 1] + cw[2:3] * gcat[t + 2]
        z = (_silu(a) * _dot(h_s[t], wu_ref[...])).astype(BF16)
        acc_s[t] += _dot(z, wd_ref[...])
    nst_ref[0] = gcat[dec_seq]
    nst_ref[1] = gcat[dec_seq + 1]

    @pl.when(j == pl.num_programs(0) - 1)
    def _():
        for t in range(dec_seq):
            y = x_ref[t] + acc_s[t]
            o_ref[t] = _rms(y, gf_ref[...]) if final else y


def ffn_sample(x, state, g, wg, wu, cw, wd, gf, tn, final):
    dec_seq, nb, _ = x.shape
    full3 = lambda j: (0, 0, 0)
    const = lambda j: (0, 0)
    return pl.pallas_call(
        functools.partial(_ffn_sample_kernel, dec_seq=dec_seq, final=final),
        grid=(D_FF // tn,),
        in_specs=[pl.BlockSpec((dec_seq, nb, D_MODEL), full3),
                  pl.BlockSpec((2, nb, tn), lambda j: (0, 0, j)),
                  pl.BlockSpec((1, D_MODEL), const),
                  pl.BlockSpec((D_MODEL, tn), lambda j: (0, j)),
                  pl.BlockSpec((D_MODEL, tn), lambda j: (0, j)),
                  pl.BlockSpec((8, tn), lambda j: (0, j)),
                  pl.BlockSpec((tn, D_MODEL), lambda j: (j, 0)),
                  pl.BlockSpec((1, D_MODEL), const)],
        out_specs=[pl.BlockSpec((dec_seq, nb, D_MODEL), full3),
                   pl.BlockSpec((2, nb, tn), lambda j: (0, 0, j))],
        out_shape=[jax.ShapeDtypeStruct((dec_seq, nb, D_MODEL), F32),
                   jax.ShapeDtypeStruct((2, nb, D_FF), F32)],
        scratch_shapes=[pltpu.VMEM((dec_seq, nb, D_MODEL), BF16),
                        pltpu.VMEM((dec_seq, nb, D_MODEL), F32)],
        compiler_params=_cparams(("arbitrary",)),
        name="ffn_sample",
    )(x, state, g, wg, wu, cw, wd, gf)


def _overlap(n_cmp, n_sel, rows, cols):
    c0 = np.arange(n_cmp)[:, None] * CMP_STRIDE
    s0 = np.arange(n_sel)[None, :] * SEL_BLOCK
    ov = np.minimum(c0 + CMP_BLOCK, s0 + SEL_BLOCK) - np.maximum(c0, s0)
    out = np.zeros((rows, cols), np.float32)
    out[:n_cmp, :n_sel] = np.clip(ov, 0, None).astype(np.float32) / CMP_BLOCK
    return out


def _pad_rows(a, rows):
    return jnp.concatenate([a, jnp.zeros((rows - a.shape[0],) + a.shape[1:], a.dtype)], axis=0)


def _prep_w_in(w_in_l):
    offs = np.cumsum((D_NSA, KV_W, KV_W, KV_W, N_BRANCH * N_HEADS, D_CONV, D_CONV, D_CONV))
    q, kc, ks, kw, gl, cb, cc, ch = jnp.split(w_in_l, [int(o) for o in offs[:-1]], axis=-1)
    glp = jnp.concatenate([gl, jnp.zeros((D_MODEL, 128 - N_BRANCH * N_HEADS), F32)], axis=-1)
    return jnp.concatenate([q * (HEAD_DIM ** -0.5), cb, cc, ch, kc, ks, kw, glp], axis=-1).astype(BF16)


def _prep_cmp(w_cmp_l, pe_cmp_l):
    eye = jnp.eye(2 * N_KV, dtype=F32).reshape(2, N_KV, 2, N_KV)
    wbd = jnp.einsum('csde,cgkh->scgdkhe', w_cmp_l, eye).reshape(CMP_BLOCK, KV_W, KV_W).astype(BF16)
    pe2 = jnp.broadcast_to(pe_cmp_l.transpose(1, 0, 2)[:, :, None, :],
                           (CMP_BLOCK, 2, N_KV, HEAD_DIM)).reshape(CMP_BLOCK, KV_W)
    return wbd, pe2


TM_PROJ = 512
TN_PROJ = 1664
TM_MIX = 256
TM_MEM = 256
TM_FFN = 512
TN_FFN = 512
BT_CMP = 4
BT_MEM = 8


def _layer_prompt(x, mem_kv, lw, batch, seq, final):
    p_all = rms_matmul(x, lw['g_mix'], lw['w_in'], TM_PROJ, TN_PROJ)
    cmp = compress_prompt(p_all, lw['pe2'], lw['wbd'], batch, seq)
    n_cmp_pad = seq // CMP_STRIDE
    cmp_t = cmp.reshape(batch, n_cmp_pad, 2, N_KV, HEAD_DIM).transpose(0, 2, 3, 1, 4)
    n_cmp = (seq - CMP_BLOCK) // CMP_STRIDE + 1
    ovt = jnp.asarray(_overlap(n_cmp, seq // SEL_BLOCK, n_cmp_pad, 128).T, BF16)
    o_nsa = nsa_prompt(p_all, cmp_t, ovt, batch, seq)
    x1, u_tail = mixout_prompt(o_nsa, p_all, lw['conv_w'], lw['g_out_nsa'], lw['g_out_conv'],
                               lw['w_out'], x, TM_MIX, seq)
    x2 = mem_prompt_attn(x1, lw['g_mem'], lw['w_mem_q'], mem_kv, lw['w_mem_o'], TM_MEM, seq)
    x3, g_tail = ffn_prompt(x2, lw['g_ffn'], lw['w_ff_gate'], lw['w_ff_up'], lw['ffn_conv_w'],
                            lw['w_ff_down'], lw['g_final'], TM_FFN, TN_FFN, seq, final)
    p3 = p_all.reshape(batch, seq, D_IN_PAD)
    kv_c = p3[:, :, COL_KC:COL_KC + KV_W]
    kv_s = p3[:, :, COL_KS:COL_KS + KV_W]
    kv_w = p3[:, seq - min(WINDOW, seq):, COL_KW:COL_KW + KV_W]
    conv_state = u_tail.reshape(batch, seq // TM_MIX, 8, D_CONV)[:, -1, 6:8]
    ffn_state = g_tail.reshape(batch, seq // TM_FFN, 8, D_FF)[:, -1, 6:8]
    return x3, kv_c, kv_s, kv_w, conv_state, ffn_state


def _layer_sample(x, lw, cache_cmp, cache_sel, page_idx, win_past, mem_kv, st_conv, st_ffn,
                  ov_s, expand, final):
    dec_seq, nb, _ = x.shape
    rows = dec_seq * nb
    p_s = rms_matmul(x.reshape(rows, D_MODEL), lw['g_mix'], lw['w_in'], rows, TN_PROJ)
    p3 = p_s.reshape(dec_seq, nb, D_IN_PAD)
    cmp_s = compress_paged(cache_cmp, page_idx, lw['pe2'], lw['wbd'], BT_CMP)
    q = p3[:, :, COL_Q:COL_Q + D_NSA].reshape(dec_seq, nb, N_KV, HPG, HEAD_DIM).transpose(1, 3, 2, 0, 4)
    gsel = (jnp.arange(N_KV)[:, None] == jnp.arange(N_KV)[None, :]).astype(F32)
    qbd = (q[:, :, :, :, None, :] * gsel[None, None, :, None, :, None]).reshape(
        nb, HPG * N_KV * dec_seq, N_KV * HEAD_DIM)
    gates = p3[:, :, COL_G:COL_G + N_BRANCH * N_HEADS].reshape(
        dec_seq, nb, N_KV, HPG, N_BRANCH).transpose(1, 3, 2, 0, 4).reshape(nb, HPG * N_KV * dec_seq, N_BRANCH)
    kv_c = p3[:, :, COL_KC:COL_KC + KV_W].transpose(1, 0, 2)
    kv_s = p3[:, :, COL_KS:COL_KS + KV_W].transpose(1, 0, 2)
    kv_w = p3[:, :, COL_KW:COL_KW + KV_W].transpose(1, 0, 2)
    o = nsa_sample(cache_sel, page_idx, qbd, gates, cmp_s, kv_s, win_past, kv_w, ov_s, expand)
    o_nsa = o.reshape(nb, HPG, N_KV, dec_seq, HEAD_DIM).transpose(3, 0, 2, 1, 4).reshape(dec_seq, nb, D_NSA)
    x1, conv_state = mixout_sample(o_nsa, p3, st_conv, lw['conv_w'], lw['g_out_nsa'],
                                   lw['g_out_conv'], lw['w_out'], x)
    qm = rms_matmul(x1.reshape(rows, D_MODEL), lw['g_mem'], lw['w_mem_q'], rows, D_MEM)
    om = mem_sample_attn(qm.reshape(dec_seq, nb, D_MEM).transpose(1, 0, 2), mem_kv, BT_MEM)
    x2 = matmul_residual(x1.reshape(rows, D_MODEL), om.transpose(1, 0, 2).reshape(rows, D_MEM),
                         lw['w_mem_o'], rows).reshape(dec_seq, nb, D_MODEL)
    x3, ffn_state = ffn_sample(x2, st_ffn, lw['g_ffn'], lw['w_ff_gate'], lw['w_ff_up'],
                               lw['ffn_conv_w'], lw['w_ff_down'], lw['g_final'], TN_FFN, final)
    win_new = jnp.concatenate([win_past[:, dec_seq:], kv_w], axis=1)
    return (x3, kv_c, kv_s, win_new, conv_state.transpose(1, 0, 2), ffn_state.transpose(1, 0, 2))


def kernel(x_prompt, x_sample, cache_cmp_kv, cache_sel_kv, cache_win_kv, cache_mem_kv,
           state_conv, state_ffn_conv, page_table, mem_prompt,
           g_mix, w_in, w_cmp, pe_cmp, conv_w, g_out_nsa, g_out_conv, w_out,
           g_mem_src, w_mem_kv, g_mem, w_mem_q, w_mem_o, g_ffn, w_ff_gate, w_ff_up,
           ffn_conv_w, w_ff_down, g_final):
    batch, seq, _ = x_prompt.shape
    nb, dec_seq, _ = x_sample.shape
    depth = w_in.shape[0]
    n_phys = cache_cmp_kv.shape[1]
    npages = page_table.shape[1]
    past = npages * PAGE
    win_buf = cache_win_kv.shape[2]
    mem_len = mem_prompt.shape[1]
    kv_shape = (2, N_KV, HEAD_DIM)

    n_cmp_s = (past + dec_seq - CMP_BLOCK) // CMP_STRIDE + 1
    n_sel_s = -(-(past + dec_seq) // SEL_BLOCK)
    ov_s = jnp.asarray(_overlap(n_cmp_s, n_sel_s, past // CMP_STRIDE, 128), BF16)
    n_keys = past + PAGE
    expand = jnp.asarray(np.arange(128)[:, None] == (np.arange(n_keys)[None, :] // SEL_BLOCK), BF16)

    cache_cmp = cache_cmp_kv.reshape(depth * n_phys, PAGE, KV_W)
    cache_sel = cache_sel_kv.reshape(depth * n_phys, PAGE, KV_W)
    cache_win = cache_win_kv.reshape(depth, nb, win_buf, KV_W)
    cache_mem = cache_mem_kv.reshape(depth, nb, mem_len, 2 * D_MEM)

    xp = x_prompt.reshape(batch * seq, D_MODEL)
    xs = x_sample.transpose(1, 0, 2)
    mem_rows = mem_prompt.reshape(batch * mem_len, D_MODEL)
    outs = [[] for _ in range(11)]
    for l in range(depth):
        wbd, pe2 = _prep_cmp(w_cmp[l], pe_cmp[l])
        lw = {
            'g_mix': g_mix[l][None], 'w_in': _prep_w_in(w_in[l]), 'wbd': wbd, 'pe2': pe2,
            'conv_w': _pad_rows(conv_w[l], 8), 'g_out_nsa': g_out_nsa[l][None],
            'g_out_conv': g_out_conv[l][None], 'w_out': w_out[l].astype(BF16),
            'g_mem': g_mem[l][None], 'w_mem_q': w_mem_q[l].astype(BF16),
            'w_mem_o': w_mem_o[l].astype(BF16), 'g_ffn': g_ffn[l][None],
            'w_ff_gate': w_ff_gate[l].astype(BF16), 'w_ff_up': w_ff_up[l].astype(BF16),
            'ffn_conv_w': _pad_rows(ffn_conv_w[l], 8), 'w_ff_down': w_ff_down[l].astype(BF16),
            'g_final': g_final[None],
        }
        final = l == depth - 1
        mem_kv_p = rms_matmul(mem_rows, g_mem_src[l][None], w_mem_kv[l].astype(BF16),
                              batch * mem_len, 2 * D_MEM).reshape(batch, mem_len, 2 * D_MEM)
        xp, kc, ks, kw, cst, fst = _layer_prompt(xp, mem_kv_p, lw, batch, seq, final)
        page_idx = page_table + l * n_phys
        xs, kc2, ks2, kw2, cst2, fst2 = _layer_sample(
            xs, lw, cache_cmp, cache_sel, page_idx, cache_win[l], cache_mem[l],
            state_conv[l].transpose(1, 0, 2), state_ffn_conv[l].transpose(1, 0, 2),
            ov_s, expand, final)
        for lst, val in zip(outs, (kc, ks, kw, mem_kv_p, cst, fst, kc2, ks2, kw2, cst2, fst2)):
            lst.append(val)

    st = [jnp.stack(o) for o in outs]
    return (xp.reshape(batch, seq, D_MODEL), xs.transpose(1, 0, 2),
            st[0].reshape((depth, batch, seq) + kv_shape),
            st[1].reshape((depth, batch, seq) + kv_shape),
            st[2].reshape((depth, batch, min(WINDOW, seq)) + kv_shape),
            st[3].reshape(depth, batch, mem_len, 2, MEM_HEADS, MEM_HEAD_DIM),
            st[4], st[5],
            st[6].reshape((depth, nb, dec_seq) + kv_shape),
            st[7].reshape((depth, nb, dec_seq) + kv_shape),
            st[8].reshape((depth, nb, win_buf) + kv_shape),
            st[9], st[10])
```

```python
import functools

import numpy as np
import jax
import jax.numpy as jnp
from jax import lax
from jax.experimental import pallas as pl
from jax.experimental.pallas import tpu as pltpu

F32 = jnp.float32
BF16 = jnp.bfloat16

D_MODEL = 2048
D_NSA = 1024
D_CONV = 1024
HEAD_DIM = 64
N_HEADS = 16
N_KV = 2
HPG = 8
N_BRANCH = 3
KV_W = 2 * N_KV * HEAD_DIM
KV_HALF = N_KV * HEAD_DIM
CMP_BLOCK = 32
CMP_STRIDE = 16
SEL_BLOCK = 64
TOPK = 16
WINDOW = 512
Q_BLOCK = 128
FORCE_BONUS = 1e4
D_FF = 5632
MEM_HEADS = 4
MEM_HEAD_DIM = 128
D_MEM = 512
PAGE = 128
EPS = 1e-6
NEG = -1e30

COL_Q = 0
COL_CB = 1024
COL_CC = 2048
COL_CH = 3072
COL_G = 4096
D_MAIN = 4224
ROW_KC = 0
ROW_KS = 256
ROW_KW = 512
D_KVT = 768

VMEM_LIMIT = 56 * 1024 * 1024


def _cparams(sem):
    return pltpu.CompilerParams(dimension_semantics=sem, vmem_limit_bytes=VMEM_LIMIT)


def _rms(x, g):
    return x * lax.rsqrt(jnp.mean(x * x, axis=-1, keepdims=True) + EPS) * g


def _dot(a, b):
    return jnp.dot(a, b, preferred_element_type=F32)


def _dot_nt(a, b):
    return lax.dot_general(a, b, (((1,), (1,)), ((), ())), preferred_element_type=F32)


def _softmax_bias(s, bias):
    s = s + bias
    e = jnp.exp(s - jnp.max(s, axis=-1, keepdims=True))
    return e, jnp.sum(e, axis=-1, keepdims=True)


def _sigmoid(x):
    return 1.0 / (1.0 + jnp.exp(-x))


def _rms_matmul_kernel(x_ref, g_ref, w_ref, o_ref, h_ref):
    @pl.when(pl.program_id(1) == 0)
    def _():
        h_ref[...] = _rms(x_ref[...], g_ref[...]).astype(BF16)

    o_ref[...] = _dot(h_ref[...], w_ref[...]).astype(o_ref.dtype)


def rms_matmul(x, g, w, tm, tn):
    m, d = x.shape
    n = w.shape[1]
    return pl.pallas_call(
        _rms_matmul_kernel,
        grid=(m // tm, n // tn),
        in_specs=[pl.BlockSpec((tm, d), lambda i, j: (i, 0)),
                  pl.BlockSpec((1, d), lambda i, j: (0, 0)),
                  pl.BlockSpec((d, tn), lambda i, j: (0, j))],
        out_specs=pl.BlockSpec((tm, tn), lambda i, j: (i, j)),
        out_shape=jax.ShapeDtypeStruct((m, n), F32),
        scratch_shapes=[pltpu.VMEM((tm, d), BF16)],
        compiler_params=_cparams(("arbitrary", "arbitrary")),
        name="rms_matmul",
    )(x, g, w)


def _proj_in_kernel(x_ref, g_ref, w_ref, wt_ref, o_ref, ot_ref, h_ref):
    @pl.when(pl.program_id(1) == 0)
    def _():
        h = _rms(x_ref[...], g_ref[...]).astype(BF16)
        h_ref[...] = h
        ot_ref[...] = _dot_nt(wt_ref[...], h)

    o_ref[...] = _dot(h_ref[...], w_ref[...])


def proj_in(x, g, w_main, wt_kv, tm, tn, seq):
    m, d = x.shape
    tps = seq // tm
    return pl.pallas_call(
        _proj_in_kernel,
        grid=(m // tm, D_MAIN // tn),
        in_specs=[pl.BlockSpec((tm, d), lambda i, j: (i, 0)),
                  pl.BlockSpec((1, d), lambda i, j: (0, 0)),
                  pl.BlockSpec((d, tn), lambda i, j: (0, j)),
                  pl.BlockSpec((D_KVT, d), lambda i, j: (0, 0))],
        out_specs=[pl.BlockSpec((tm, tn), lambda i, j: (i, j)),
                   pl.BlockSpec((None, D_KVT, tm), lambda i, j: (i // tps, 0, i % tps))],
        out_shape=[jax.ShapeDtypeStruct((m, D_MAIN), F32),
                   jax.ShapeDtypeStruct((m // seq, D_KVT, seq), F32)],
        scratch_shapes=[pltpu.VMEM((tm, d), BF16)],
        compiler_params=_cparams(("arbitrary", "arbitrary")),
        name="proj_in",
    )(x, g, w_main, wt_kv)


def _matmul_res_kernel(x_ref, a_ref, w_ref, o_ref):
    o_ref[...] = x_ref[...] + _dot(a_ref[...].astype(BF16), w_ref[...])


def matmul_residual(x, a, w, tm):
    m, n = x.shape
    k = a.shape[1]
    return pl.pallas_call(
        _matmul_res_kernel,
        grid=(m // tm,),
        in_specs=[pl.BlockSpec((tm, n), lambda i: (i, 0)),
                  pl.BlockSpec((tm, k), lambda i: (i, 0)),
                  pl.BlockSpec((k, n), lambda i: (0, 0))],
        out_specs=pl.BlockSpec((tm, n), lambda i: (i, 0)),
        out_shape=jax.ShapeDtypeStruct((m, n), F32),
        compiler_params=_cparams(("arbitrary",)),
        name="matmul_residual",
    )(x, a, w)


RPC = PAGE // CMP_STRIDE


def _compress_chunks(chunk_at, n_chunks, pet_ref, perm_ref, w_ref, xs_ref):
    perm = perm_ref[...]
    for ci in range(n_chunks):
        chunk = chunk_at(ci)
        for a in range(2):
            xs_ref[a, ci] = _dot_nt(perm, (chunk + pet_ref[a]).astype(BF16))
    rows = n_chunks * RPC
    acc0 = jnp.zeros((rows, KV_W), F32)
    acc1 = jnp.zeros((rows, KV_W), F32)
    for r in range(CMP_STRIDE):
        x0 = xs_ref[0, :, r * RPC:(r + 1) * RPC, :].reshape(rows, KV_W).astype(BF16)
        x1 = xs_ref[1, :, r * RPC:(r + 1) * RPC, :].reshape(rows, KV_W).astype(BF16)
        acc0 = acc0 + _dot(x0, w_ref[r])
        acc1 = acc1 + _dot(x1, w_ref[CMP_STRIDE + r])
    return acc0 + pltpu.roll(acc1, rows - 1, 0)


def _compress_prompt_kernel(kvt_ref, pet_ref, perm_ref, w_ref, o_ref, xs_ref, *, n_chunks):
    out = _compress_chunks(lambda ci: kvt_ref[:, ci * PAGE:(ci + 1) * PAGE], n_chunks,
                           pet_ref, perm_ref, w_ref, xs_ref)
    o_ref[...] = out.astype(BF16)


def compress_prompt(kvt, pet, perm, wbd):
    batch, _, seq = kvt.shape
    n_chunks = seq // PAGE
    nblk = seq // CMP_STRIDE
    return pl.pallas_call(
        functools.partial(_compress_prompt_kernel, n_chunks=n_chunks),
        grid=(batch,),
        in_specs=[pl.BlockSpec((None, KV_W, seq), lambda b: (b, ROW_KC // KV_W, 0)),
                  pl.BlockSpec((2, KV_W, PAGE), lambda b: (0, 0, 0)),
                  pl.BlockSpec((PAGE, PAGE), lambda b: (0, 0)),
                  pl.BlockSpec((CMP_BLOCK, KV_W, KV_W), lambda b: (0, 0, 0))],
        out_specs=pl.BlockSpec((None, nblk, KV_W), lambda b: (b, 0, 0)),
        out_shape=jax.ShapeDtypeStruct((batch, nblk, KV_W), BF16),
        scratch_shapes=[pltpu.VMEM((2, n_chunks, PAGE, KV_W), F32)],
        compiler_params=_cparams(("arbitrary",)),
        name="compress_prompt",
    )(kvt, pet, perm, wbd)


def _compress_paged_kernel(pt_ref, *refs, bt, npages):
    page_refs = refs[:bt * npages]
    pet_ref, perm_ref, w_ref, o_ref, xs_ref = refs[bt * npages:]
    out = _compress_chunks(lambda ci: page_refs[ci][...].reshape(KV_W, PAGE), bt * npages,
                           pet_ref, perm_ref, w_ref, xs_ref)
    o_ref[...] = out.astype(BF16).reshape(bt, npages * RPC, KV_W)


def compress_paged(cache, page_idx, pet, perm, wbd, bt):
    nb, npages = page_idx.shape
    nblk = npages * RPC
    page_specs = [
        pl.BlockSpec((None, 2, N_KV, HEAD_DIM, PAGE), functools.partial(
            lambda i, pt, bb, p: (pt[i * bt + bb, p], 0, 0, 0, 0), bb=bb, p=p))
        for bb in range(bt) for p in range(npages)]
    grid_spec = pltpu.PrefetchScalarGridSpec(
        num_scalar_prefetch=1,
        grid=(nb // bt,),
        in_specs=page_specs + [pl.BlockSpec((2, KV_W, PAGE), lambda i, pt: (0, 0, 0)),
                               pl.BlockSpec((PAGE, PAGE), lambda i, pt: (0, 0)),
                               pl.BlockSpec((CMP_BLOCK, KV_W, KV_W), lambda i, pt: (0, 0, 0))],
        out_specs=pl.BlockSpec((bt, nblk, KV_W), lambda i, pt: (i, 0, 0)),
        scratch_shapes=[pltpu.VMEM((2, bt * npages, PAGE, KV_W), F32)])
    return pl.pallas_call(
        functools.partial(_compress_paged_kernel, bt=bt, npages=npages),
        grid_spec=grid_spec,
        out_shape=jax.ShapeDtypeStruct((nb, nblk, KV_W), BF16),
        compiler_params=_cparams(("arbitrary",)),
        name="compress_paged",
    )(page_idx, *([cache] * (bt * npages)), pet, perm, wbd)


def _importance(imp, idx, tpos, n_sel):
    valid = (idx * SEL_BLOCK <= tpos) & (idx < n_sel)
    cur = tpos // SEL_BLOCK
    forced = (idx == 0) | (idx == cur) | (idx == cur - 1)
    return jnp.where(valid, imp + jnp.where(forced, FORCE_BONUS, 0.0), NEG)


def _topk_rows(imp, n_cand):
    groups = [imp[v * 8:(v + 1) * 8] for v in range(n_cand // 8)]
    sub = lax.broadcasted_iota(jnp.int32, groups[0].shape, 0)
    ranks = [jnp.zeros(g.shape, F32) for g in groups]
    for jp in range(n_cand):
        c = imp[jp:jp + 1, :]
        for v, g in enumerate(groups):
            ge = jnp.where(c >= g, 1.0, 0.0)
            gt = jnp.where(c > g, 1.0, 0.0)
            if v * 8 > jp:
                beats = ge
            elif v * 8 + 7 < jp:
                beats = gt
            else:
                beats = jnp.where(sub > jp - v * 8, ge, gt)
            ranks[v] = ranks[v] + beats
    return jnp.concatenate([jnp.where(r < TOPK, 1.0, 0.0) for r in ranks], axis=0)


def _topk_lanes(imp, idx, n_cand):
    rank = jnp.zeros(imp.shape, F32)
    for jp in range(n_cand):
        c = imp[:, jp:jp + 1]
        rank = rank + jnp.where(idx > jp, jnp.where(c >= imp, 1.0, 0.0), jnp.where(c > imp, 1.0, 0.0))
    return jnp.where(rank < TOPK, 1.0, 0.0)


KEY_TILE = 512


def _tile_heads(x):
    return jnp.concatenate([x] * HPG, axis=0)


def _nsa_prompt_kernel(q_ref, gate_ref, cmp_ref, kst_ref, vst_ref, kwt_ref, vwt_ref, ovt_ref, exp_ref,
                       o_ref, ksb, vsb, kwb, vwb, *, seq):
    g = pl.program_id(1)
    i = pl.program_id(2)
    n_cmp_pad = seq // CMP_STRIDE
    n_sel = seq // SEL_BLOCK

    @pl.when(i == 0)
    def _():
        ksb[...] = kst_ref[...].astype(BF16)
        vsb[...] = vst_ref[...].astype(BF16)
        kwb[...] = kwt_ref[...].astype(BF16)
        vwb[...] = vwt_ref[...].astype(BF16)

    t0 = i * Q_BLOCK
    qb = q_ref[...]
    qs = jnp.concatenate([qb[:, h * HEAD_DIM:(h + 1) * HEAD_DIM] for h in range(HPG)],
                         axis=0).astype(BF16)
    tcol = lax.broadcasted_iota(jnp.int32, (Q_BLOCK, 1), 0) + t0

    ncol = lax.broadcasted_iota(jnp.int32, (Q_BLOCK, n_cmp_pad), 1)
    vis = jnp.where(ncol < n_cmp_pad - 1, ncol * CMP_STRIDE + (CMP_BLOCK - 1), seq) <= tcol
    e_c, l_c = _softmax_bias(_dot_nt(qs, cmp_ref[0]), _tile_heads(jnp.where(vis, 0.0, NEG)))
    any_vis = _tile_heads(jnp.where(tcol >= CMP_BLOCK - 1, 1.0, 0.0))
    p_c = (e_c * (any_vis / jnp.maximum(l_c, 1e-30))).astype(BF16)
    o_c = _dot(p_c, cmp_ref[1])

    po = _dot_nt(ovt_ref[...], p_c)
    imp_t = po[:, 0:Q_BLOCK]
    for h in range(1, HPG):
        imp_t = imp_t + po[:, h * Q_BLOCK:(h + 1) * Q_BLOCK]
    jj = lax.broadcasted_iota(jnp.int32, (n_sel, Q_BLOCK), 0)
    tt = lax.broadcasted_iota(jnp.int32, (n_sel, Q_BLOCK), 1) + t0
    sel_t = _topk_rows(_importance(imp_t, jj, tt, n_sel), n_sel)
    sel = jnp.concatenate([sel_t, jnp.zeros((128 - n_sel, Q_BLOCK), F32)], axis=0).T.astype(BF16)

    kk = lax.broadcasted_iota(jnp.int32, (Q_BLOCK, KEY_TILE), 1)

    def sel_tile(c, carry):
        m, l, acc = carry
        k0 = pl.multiple_of(c * KEY_TILE, KEY_TILE)
        msel = _dot(sel, exp_ref[:, pl.ds(k0, KEY_TILE)])
        bias = jnp.where(msel > 0.5, jnp.where(kk + k0 <= tcol, 0.0, NEG), NEG)
        s = _dot(qs, ksb[:, pl.ds(k0, KEY_TILE)]) + _tile_heads(bias)
        m_new = jnp.maximum(m, jnp.max(s, axis=-1, keepdims=True))
        alpha = jnp.exp(m - m_new)
        p = jnp.exp(s - m_new)
        l = alpha * l + jnp.sum(p, axis=-1, keepdims=True)
        acc = alpha * acc + _dot_nt(p.astype(BF16), vsb[:, pl.ds(k0, KEY_TILE)])
        return m_new, l, acc

    n_rows = HPG * Q_BLOCK
    m0 = jnp.full((n_rows, 1), NEG, F32)
    l0 = jnp.zeros((n_rows, 1), F32)
    a0 = jnp.zeros((n_rows, HEAD_DIM), F32)
    n_tiles = (t0 + Q_BLOCK + KEY_TILE - 1) // KEY_TILE
    _, l_s, acc_s = lax.fori_loop(0, n_tiles, sel_tile, (m0, l0, a0))
    o_s = acc_s / l_s

    span = WINDOW + Q_BLOCK
    ws = pl.multiple_of(jnp.maximum(t0 - WINDOW, 0), Q_BLOCK)
    dlt = tcol - (lax.broadcasted_iota(jnp.int32, (Q_BLOCK, span), 1) + ws)
    bias_w = jnp.where(dlt >= 0, jnp.where(dlt < WINDOW, 0.0, NEG), NEG)
    e_w, l_w = _softmax_bias(_dot(qs, kwb[:, pl.ds(ws, span)]), _tile_heads(bias_w))
    o_w = _dot_nt((e_w * (1.0 / l_w)).astype(BF16), vwb[:, pl.ds(ws, span)])

    gs = _sigmoid(gate_ref[...])
    gsel = jnp.where(g == 0, gs[:, 0:HPG * N_BRANCH], gs[:, HPG * N_BRANCH:2 * HPG * N_BRANCH])
    for h in range(HPG):
        r0, r1 = h * Q_BLOCK, (h + 1) * Q_BLOCK
        c0 = h * N_BRANCH
        o_ref[:, h * HEAD_DIM:(h + 1) * HEAD_DIM] = (
            o_c[r0:r1] * gsel[:, c0:c0 + 1] + o_s[r0:r1] * gsel[:, c0 + 1:c0 + 2]
            + o_w[r0:r1] * gsel[:, c0 + 2:c0 + 3])


def nsa_prompt(p_main, kvt, cmp_t, ovt, expand):
    batch, _, seq = kvt.shape
    nqb = seq // Q_BLOCK
    gw = HPG * HEAD_DIM
    n_cmp_pad = seq // CMP_STRIDE
    n_sel = seq // SEL_BLOCK
    kvt_spec = lambda row: pl.BlockSpec((None, HEAD_DIM, seq),
                                        lambda b, g, i: (b, row // HEAD_DIM + g, 0))
    return pl.pallas_call(
        functools.partial(_nsa_prompt_kernel, seq=seq),
        grid=(batch, N_KV, nqb),
        in_specs=[
            pl.BlockSpec((Q_BLOCK, gw), lambda b, g, i: (b * nqb + i, g)),
            pl.BlockSpec((Q_BLOCK, 128), lambda b, g, i: (b * nqb + i, COL_G // 128)),
            pl.BlockSpec((None, 2, None, n_cmp_pad, HEAD_DIM), lambda b, g, i: (b, 0, g, 0, 0)),
            kvt_spec(ROW_KS), kvt_spec(ROW_KS + KV_HALF),
            kvt_spec(ROW_KW), kvt_spec(ROW_KW + KV_HALF),
            pl.BlockSpec((n_sel, n_cmp_pad), lambda b, g, i: (0, 0)),
            pl.BlockSpec((128, seq), lambda b, g, i: (0, 0)),
        ],
        out_specs=pl.BlockSpec((Q_BLOCK, gw), lambda b, g, i: (b * nqb + i, g)),
        out_shape=jax.ShapeDtypeStruct((batch * seq, D_NSA), F32),
        scratch_shapes=[pltpu.VMEM((HEAD_DIM, seq), BF16)] * 4,
        compiler_params=_cparams(("arbitrary", "arbitrary", "arbitrary")),
        name="nsa_prompt",
    )(p_main, p_main, cmp_t, kvt, kvt, kvt, kvt, ovt, expand)


def _pick_group(o2, rg):
    return jnp.where(rg == 0, o2[:, 0:HEAD_DIM], o2[:, HEAD_DIM:2 * HEAD_DIM])


def _masked_softmax(s, mask):
    s = jnp.where(mask, s, NEG)
    m = jnp.max(s, axis=-1, keepdims=True)
    e = jnp.where(mask, jnp.exp(s - m), 0.0)
    return e / jnp.maximum(jnp.sum(e, axis=-1, keepdims=True), 1e-30)


def _nsa_sample_kernel(pt_ref, *refs, npages, dec_seq):
    page_refs = refs[:npages]
    (q_ref, gate_ref, cmp_ref, ksn_ref, win_ref, kwn_ref, ov_ref, exp_ref,
     o_ref, kts, vts, ktw, vtw) = refs[npages:]
    past = npages * PAGE
    n_cmp = (past + dec_seq - CMP_BLOCK) // CMP_STRIDE + 1
    n_sel = -(-(past + dec_seq) // SEL_BLOCK)
    n_rows = HPG * N_KV * dec_seq
    n_keys = past + PAGE
    win_buf = win_ref.shape[-1]
    w_keys = win_buf + PAGE

    q = q_ref[...].astype(BF16)
    rowi = lax.broadcasted_iota(jnp.int32, (n_rows, 1), 0)
    rg = (rowi // dec_seq) % N_KV
    tpos = past + rowi % dec_seq

    cm = cmp_ref[...]
    n_cmp_pad = cm.shape[0]
    s_c = _dot_nt(q, cm[:, 0:KV_HALF])
    ncol = lax.broadcasted_iota(jnp.int32, (n_rows, n_cmp_pad), 1)
    m_c = (ncol * CMP_STRIDE + (CMP_BLOCK - 1) <= tpos) & (ncol < n_cmp)
    p_c = _masked_softmax(s_c, m_c).astype(BF16)
    o_c = _pick_group(_dot(p_c, cm[:, KV_HALF:KV_W]), rg)

    po = _dot(p_c, ov_ref[...])
    gt = N_KV * dec_seq
    imp = po[0:gt]
    for h in range(1, HPG):
        imp = imp + po[h * gt:(h + 1) * gt]
    jj = lax.broadcasted_iota(jnp.int32, (gt, 128), 1)
    t8 = past + lax.broadcasted_iota(jnp.int32, (gt, 128), 0) % dec_seq
    sel8 = _topk_lanes(_importance(imp, jj, t8, n_sel), jj, n_sel)
    sel = jnp.concatenate([sel8] * HPG, axis=0).astype(BF16)
    msel = _dot(sel, exp_ref[...])

    for p in range(npages):
        kts[:, p * PAGE:(p + 1) * PAGE] = page_refs[p][0].reshape(KV_HALF, PAGE).astype(BF16)
        vts[:, p * PAGE:(p + 1) * PAGE] = page_refs[p][1].reshape(KV_HALF, PAGE).astype(BF16)
    kts[:, past:n_keys] = ksn_ref[0:KV_HALF, :].astype(BF16)
    vts[:, past:n_keys] = ksn_ref[KV_HALF:KV_W, :].astype(BF16)
    kcol = lax.broadcasted_iota(jnp.int32, (n_rows, n_keys), 1)
    p_s = _masked_softmax(_dot(q, kts[...]), (msel > 0.5) & (kcol <= tpos)).astype(BF16)
    o_s = _pick_group(_dot_nt(p_s, vts[...]), rg)

    ktw[:, 0:win_buf] = win_ref[0].reshape(KV_HALF, win_buf).astype(BF16)
    vtw[:, 0:win_buf] = win_ref[1].reshape(KV_HALF, win_buf).astype(BF16)
    ktw[:, win_buf:w_keys] = kwn_ref[0:KV_HALF, :].astype(BF16)
    vtw[:, win_buf:w_keys] = kwn_ref[KV_HALF:KV_W, :].astype(BF16)
    wcol = lax.broadcasted_iota(jnp.int32, (n_rows, w_keys), 1)
    dlt = tpos - (wcol + (past - win_buf))
    p_w = _masked_softmax(_dot(q, ktw[...]), (dlt >= 0) & (dlt < WINDOW)).astype(BF16)
    o_w = _pick_group(_dot_nt(p_w, vtw[...]), rg)

    gs = _sigmoid(gate_ref[...])
    o_ref[...] = o_c * gs[:, 0:1] + o_s * gs[:, 1:2] + o_w * gs[:, 2:3]


def nsa_sample(cache_sel, page_idx, qbd, gates, cmp_s, ks_new, cache_win, layer, kw_new, ov_s, expand):
    nb, npages = page_idx.shape
    n_rows = qbd.shape[1]
    dec_seq = n_rows // (HPG * N_KV)
    win_buf = cache_win.shape[-1]
    n_cmp_pad = cmp_s.shape[1]
    n_keys = npages * PAGE + PAGE
    page_specs = [
        pl.BlockSpec((None, 2, N_KV, HEAD_DIM, PAGE),
                     functools.partial(lambda b, pt, p: (pt[b, p], 0, 0, 0, 0), p=p))
        for p in range(npages)]
    per_b = lambda b, pt: (b, 0, 0)
    const2 = lambda b, pt: (0, 0)
    grid_spec = pltpu.PrefetchScalarGridSpec(
        num_scalar_prefetch=1,
        grid=(nb,),
        in_specs=page_specs + [
            pl.BlockSpec((None, n_rows, KV_HALF), per_b),
            pl.BlockSpec((None, n_rows, N_BRANCH), per_b),
            pl.BlockSpec((None, n_cmp_pad, KV_W), per_b),
            pl.BlockSpec((None, KV_W, PAGE), per_b),
            pl.BlockSpec((None, None, 2, N_KV, HEAD_DIM, win_buf), lambda b, pt: (layer, b, 0, 0, 0, 0)),
            pl.BlockSpec((None, KV_W, PAGE), per_b),
            pl.BlockSpec((n_cmp_pad, 128), const2),
            pl.BlockSpec((128, n_keys), const2),
        ],
        out_specs=pl.BlockSpec((None, n_rows, HEAD_DIM), per_b),
        scratch_shapes=[pltpu.VMEM((KV_HALF, n_keys), BF16)] * 2
        + [pltpu.VMEM((KV_HALF, win_buf + PAGE), BF16)] * 2)
    return pl.pallas_call(
        functools.partial(_nsa_sample_kernel, npages=npages, dec_seq=dec_seq),
        grid_spec=grid_spec,
        out_shape=jax.ShapeDtypeStruct((nb, n_rows, HEAD_DIM), F32),
        compiler_params=_cparams(("arbitrary",)),
        name="nsa_sample",
    )(page_idx, *([cache_sel] * npages), qbd, gates, cmp_s, ks_new, cache_win, kw_new, ov_s, expand)


def _shift_rows(u, p1, p2):
    r = lax.broadcasted_iota(jnp.int32, (u.shape[0], 1), 0)
    u1 = jnp.where(r >= 1, pltpu.roll(u, 1, 0), p1)
    u2 = jnp.where(r >= 2, pltpu.roll(u, 2, 0), jnp.where(r == 1, p1, p2))
    return u1, u2


def _mix_project(o_nsa, o_conv, gn, gc, w, x):
    mix = jnp.concatenate([_rms(o_nsa, gn), _rms(o_conv, gc)], axis=-1).astype(BF16)
    return x + _dot(mix, w)


def _mixout_prompt_kernel(on_ref, cb_ref, cc_ref, ch_ref, cch_ref, chh_ref, cw_ref, gn_ref, gc_ref,
                          w_ref, x_ref, o_ref, ut_ref, *, tiles_per_seq):
    first = (pl.program_id(0) % tiles_per_seq) == 0
    u = cc_ref[...] * ch_ref[...]
    uh = jnp.where(first, 0.0, cch_ref[...] * chh_ref[...])
    u1, u2 = _shift_rows(u, uh[7:8], uh[6:7])
    cw = cw_ref[...]
    v = cw[0:1] * u2 + cw[1:2] * u1 + cw[2:3] * u
    o_ref[...] = _mix_project(on_ref[...], cb_ref[...] * v, gn_ref[...], gc_ref[...],
                              w_ref[...], x_ref[...])
    ut_ref[...] = u[u.shape[0] - 8:, :]


def mixout_prompt(o_nsa, p_main, cw, gn, gc, w, x, tm, seq):
    m = x.shape[0]
    nt = m // tm
    cblk = lambda c: (lambda i: (i, c // D_CONV))
    halo = lambda c: (lambda i: (jnp.maximum(i * (tm // 8) - 1, 0), c // D_CONV))
    const = lambda i: (0, 0)
    return pl.pallas_call(
        functools.partial(_mixout_prompt_kernel, tiles_per_seq=seq // tm),
        grid=(nt,),
        in_specs=[pl.BlockSpec((tm, D_NSA), lambda i: (i, 0)),
                  pl.BlockSpec((tm, D_CONV), cblk(COL_CB)),
                  pl.BlockSpec((tm, D_CONV), cblk(COL_CC)),
                  pl.BlockSpec((tm, D_CONV), cblk(COL_CH)),
                  pl.BlockSpec((8, D_CONV), halo(COL_CC)),
                  pl.BlockSpec((8, D_CONV), halo(COL_CH)),
                  pl.BlockSpec((8, D_CONV), const),
                  pl.BlockSpec((1, D_NSA), const),
                  pl.BlockSpec((1, D_CONV), const),
                  pl.BlockSpec((D_MODEL, D_MODEL), const),
                  pl.BlockSpec((tm, D_MODEL), lambda i: (i, 0))],
        out_specs=[pl.BlockSpec((tm, D_MODEL), lambda i: (i, 0)),
                   pl.BlockSpec((None, 8, D_CONV), lambda i: (i, 0, 0))],
        out_shape=[jax.ShapeDtypeStruct((m, D_MODEL), F32),
                   jax.ShapeDtypeStruct((nt, 8, D_CONV), F32)],
        compiler_params=_cparams(("arbitrary",)),
        name="mixout_prompt",
    )(o_nsa, p_main, p_main, p_main, p_main, p_main, cw, gn, gc, w, x)


def _mixout_sample_kernel(on_ref, cb_ref, cc_ref, ch_ref, st_ref, cw_ref, gn_ref, gc_ref,
                          w_ref, x_ref, o_ref, nst_ref, *, dec_seq):
    ucat = [st_ref[0], st_ref[1]] + [cc_ref[t] * ch_ref[t] for t in range(dec_seq)]
    cw = cw_ref[...]
    for t in range(dec_seq):
        v = cw[0:1] * ucat[t] + cw[1:2] * ucat[t + 1] + cw[2:3] * ucat[t + 2]
        o_ref[t] = _mix_project(on_ref[t], cb_ref[t] * v, gn_ref[...], gc_ref[...],
                                w_ref[...], x_ref[t])
    nst_ref[0] = ucat[dec_seq]
    nst_ref[1] = ucat[dec_seq + 1]


def mixout_sample(o_nsa, p_s, state, cw, gn, gc, w, x):
    dec_seq, nb, _ = x.shape
    cblk = lambda c: (lambda i: (0, 0, c // D_CONV))
    full3 = lambda i: (0, 0, 0)
    const = lambda i: (0, 0)
    return pl.pallas_call(
        functools.partial(_mixout_sample_kernel, dec_seq=dec_seq),
        grid=(1,),
        in_specs=[pl.BlockSpec((dec_seq, nb, D_NSA), full3),
                  pl.BlockSpec((dec_seq, nb, D_CONV), cblk(COL_CB)),
                  pl.BlockSpec((dec_seq, nb, D_CONV), cblk(COL_CC)),
                  pl.BlockSpec((dec_seq, nb, D_CONV), cblk(COL_CH)),
                  pl.BlockSpec((2, nb, D_CONV), full3),
                  pl.BlockSpec((8, D_CONV), const),
                  pl.BlockSpec((1, D_NSA), const),
                  pl.BlockSpec((1, D_CONV), const),
                  pl.BlockSpec((D_MODEL, D_MODEL), const),
                  pl.BlockSpec((dec_seq, nb, D_MODEL), full3)],
        out_specs=[pl.BlockSpec((dec_seq, nb, D_MODEL), full3),
                   pl.BlockSpec((2, nb, D_CONV), full3)],
        out_shape=[jax.ShapeDtypeStruct((dec_seq, nb, D_MODEL), F32),
                   jax.ShapeDtypeStruct((2, nb, D_CONV), F32)],
        compiler_params=_cparams(("arbitrary",)),
        name="mixout_sample",
    )(o_nsa, p_s, p_s, p_s, state, cw, gn, gc, w, x)


MEM_SCALE = MEM_HEAD_DIM ** -0.5


def _softmax(s):
    e = jnp.exp(s - jnp.max(s, axis=-1, keepdims=True))
    return e / jnp.sum(e, axis=-1, keepdims=True)


def _mem_prompt_kernel(x_ref, g_ref, wq_ref, kv_ref, wo_ref, o_ref):
    x = x_ref[...]
    qm = _dot(_rms(x, g_ref[...]).astype(BF16), wq_ref[...]).astype(BF16)
    kv = kv_ref[...].astype(BF16)
    outs = []
    for h in range(MEM_HEADS):
        c0, c1 = h * MEM_HEAD_DIM, (h + 1) * MEM_HEAD_DIM
        p = _softmax(_dot_nt(qm[:, c0:c1], kv[:, c0:c1]) * MEM_SCALE).astype(BF16)
        outs.append(_dot(p, kv[:, D_MEM + c0:D_MEM + c1]))
    om = jnp.concatenate(outs, axis=-1).astype(BF16)
    o_ref[...] = x + _dot(om, wo_ref[...])


def mem_prompt_attn(x, g, wq, mem_kv, wo, tm, seq):
    m = x.shape[0]
    mem_len = mem_kv.shape[1]
    tps = seq // tm
    const = lambda i: (0, 0)
    return pl.pallas_call(
        _mem_prompt_kernel,
        grid=(m // tm,),
        in_specs=[pl.BlockSpec((tm, D_MODEL), lambda i: (i, 0)),
                  pl.BlockSpec((1, D_MODEL), const),
                  pl.BlockSpec((D_MODEL, D_MEM), const),
                  pl.BlockSpec((None, mem_len, 2 * D_MEM), lambda i: (i // tps, 0, 0)),
                  pl.BlockSpec((D_MEM, D_MODEL), const)],
        out_specs=pl.BlockSpec((tm, D_MODEL), lambda i: (i, 0)),
        out_shape=jax.ShapeDtypeStruct((m, D_MODEL), F32),
        compiler_params=_cparams(("arbitrary",)),
        name="mem_prompt_attn",
    )(x, g, wq, mem_kv, wo)


def _mem_sample_kernel(q_ref, kv_ref, o_ref, *, bt, mem_len):
    n_rows = q_ref.shape[1]
    stride = 2 * MEM_HEADS
    rowh = lax.broadcasted_iota(jnp.int32, (n_rows, 1), 0) // (n_rows // MEM_HEADS)
    for bb in range(bt):
        kmat = jnp.concatenate([kv_ref[bb, pl.ds(h, mem_len, stride=stride), :]
                                for h in range(MEM_HEADS)], axis=-1).astype(BF16)
        vmat = jnp.concatenate([kv_ref[bb, pl.ds(MEM_HEADS + h, mem_len, stride=stride), :]
                                for h in range(MEM_HEADS)], axis=-1).astype(BF16)
        p = _softmax(_dot_nt(q_ref[bb].astype(BF16), kmat) * MEM_SCALE).astype(BF16)
        o2 = _dot(p, vmat)
        out = jnp.zeros((n_rows, MEM_HEAD_DIM), F32)
        for h in range(MEM_HEADS):
            out = out + jnp.where(rowh == h, o2[:, h * MEM_HEAD_DIM:(h + 1) * MEM_HEAD_DIM], 0.0)
        o_ref[bb] = out


def mem_sample_attn(qbd, cache_mem, layer, bt):
    nb, n_rows, _ = qbd.shape
    rows = cache_mem.shape[2]
    return pl.pallas_call(
        functools.partial(_mem_sample_kernel, bt=bt, mem_len=rows // (2 * MEM_HEADS)),
        grid=(nb // bt,),
        in_specs=[pl.BlockSpec((bt, n_rows, D_MEM), lambda i: (i, 0, 0)),
                  pl.BlockSpec((None, bt, rows, MEM_HEAD_DIM), lambda i: (layer, i, 0, 0))],
        out_specs=pl.BlockSpec((bt, n_rows, MEM_HEAD_DIM), lambda i: (i, 0, 0)),
        out_shape=jax.ShapeDtypeStruct((nb, n_rows, MEM_HEAD_DIM), F32),
        compiler_params=_cparams(("arbitrary",)),
        name="mem_sample_attn",
    )(qbd, cache_mem)


def _silu(a):
    return a * _sigmoid(a)


def _ffn_prompt_kernel(x_ref, xh_ref, g_ref, wg_ref, wu_ref, cw_ref, wd_ref, gf_ref,
                       o_ref, gt_ref, h_s, hh_s, acc_s, *, tiles_per_seq, final):
    j = pl.program_id(1)
    tm = x_ref.shape[0]

    @pl.when(j == 0)
    def _():
        first = (pl.program_id(0) % tiles_per_seq) == 0
        h_s[...] = _rms(x_ref[...], g_ref[...]).astype(BF16)
        hh = jnp.where(first, 0.0, _rms(xh_ref[...], g_ref[...]))
        hh_s[...] = jnp.concatenate([hh, jnp.zeros_like(hh)], axis=0).astype(BF16)
        acc_s[...] = jnp.zeros_like(acc_s)

    h = h_s[...]
    gate = _dot(h, wg_ref[...])
    gate_h = _dot(hh_s[...], wg_ref[...])
    g1, g2 = _shift_rows(gate, gate_h[7:8], gate_h[6:7])
    cw = cw_ref[...]
    a = cw[0:1] * g2 + cw[1:2] * g1 + cw[2:3] * gate
    z = (_silu(a) * _dot(h, wu_ref[...])).astype(BF16)
    acc_s[...] += _dot(z, wd_ref[...])
    gt_ref[...] = gate[tm - 8:, :]

    @pl.when(j == pl.num_programs(1) - 1)
    def _():
        y = x_ref[...] + acc_s[...]
        o_ref[...] = _rms(y, gf_ref[...]) if final else y


def ffn_prompt(x, g, wg, wu, cw, wd, gf, tm, tn, seq, final):
    m = x.shape[0]
    nt = m // tm
    nj = D_FF // tn
    const = lambda i, j: (0, 0)
    return pl.pallas_call(
        functools.partial(_ffn_prompt_kernel, tiles_per_seq=seq // tm, final=final),
        grid=(nt, nj),
        in_specs=[pl.BlockSpec((tm, D_MODEL), lambda i, j: (i, 0)),
                  pl.BlockSpec((8, D_MODEL), lambda i, j: (jnp.maximum(i * (tm // 8) - 1, 0), 0)),
                  pl.BlockSpec((1, D_MODEL), const),
                  pl.BlockSpec((D_MODEL, tn), lambda i, j: (0, j)),
                  pl.BlockSpec((D_MODEL, tn), lambda i, j: (0, j)),
                  pl.BlockSpec((8, tn), lambda i, j: (0, j)),
                  pl.BlockSpec((tn, D_MODEL), lambda i, j: (j, 0)),
                  pl.BlockSpec((1, D_MODEL), const)],
        out_specs=[pl.BlockSpec((tm, D_MODEL), lambda i, j: (i, 0)),
                   pl.BlockSpec((None, 8, tn), lambda i, j: (i, 0, j))],
        out_shape=[jax.ShapeDtypeStruct((m, D_MODEL), F32),
                   jax.ShapeDtypeStruct((nt, 8, D_FF), F32)],
        scratch_shapes=[pltpu.VMEM((tm, D_MODEL), BF16),
                        pltpu.VMEM((16, D_MODEL), BF16),
                        pltpu.VMEM((tm, D_MODEL), F32)],
        compiler_params=_cparams(("arbitrary", "arbitrary")),
        name="ffn_prompt",
    )(x, x, g, wg, wu, cw, wd, gf)


def _ffn_sample_kernel(x_ref, st_ref, g_ref, wg_ref, wu_ref, cw_ref, wd_ref, gf_ref,
                       o_ref, nst_ref, h_s, acc_s, *, dec_seq, final):
    j = pl.program_id(0)

    @pl.when(j == 0)
    def _():
        for t in range(dec_seq):
            h_s[t] = _rms(x_ref[t], g_ref[...]).astype(BF16)
        acc_s[...] = jnp.zeros_like(acc_s)

    gcat = [st_ref[0], st_ref[1]] + [_dot(h_s[t], wg_ref[...]) for t in range(dec_seq)]
    cw = cw_ref[...]
    for t in range(dec_seq):
        a = cw[0:1] * gcat[t] + cw[1:2] * gcat[t + 1] + cw[2:3] * gcat[t + 2]
        z = (_silu(a) * _dot(h_s[t], wu_ref[...])).astype(BF16)
        acc_s[t] += _dot(z, wd_ref[...])
    nst_ref[0] = gcat[dec_seq]
    nst_ref[1] = gcat[dec_seq + 1]

    @pl.when(j == pl.num_programs(0) - 1)
    def _():
        for t in range(dec_seq):
            y = x_ref[t] + acc_s[t]
            o_ref[t] = _rms(y, gf_ref[...]) if final else y


def ffn_sample(x, state, g, wg, wu, cw, wd, gf, tn, final):
    dec_seq, nb, _ = x.shape
    full3 = lambda j: (0, 0, 0)
    const = lambda j: (0, 0)
    return pl.pallas_call(
        functools.partial(_ffn_sample_kernel, dec_seq=dec_seq, final=final),
        grid=(D_FF // tn,),
        in_specs=[pl.BlockSpec((dec_seq, nb, D_MODEL), full3),
                  pl.BlockSpec((2, nb, tn), lambda j: (0, 0, j)),
                  pl.BlockSpec((1, D_MODEL), const),
                  pl.BlockSpec((D_MODEL, tn), lambda j: (0, j)),
                  pl.BlockSpec((D_MODEL, tn), lambda j: (0, j)),
                  pl.BlockSpec((8, tn), lambda j: (0, j)),
                  pl.BlockSpec((tn, D_MODEL), lambda j: (j, 0)),
                  pl.BlockSpec((1, D_MODEL), const)],
        out_specs=[pl.BlockSpec((dec_seq, nb, D_MODEL), full3),
                   pl.BlockSpec((2, nb, tn), lambda j: (0, 0, j))],
        out_shape=[jax.ShapeDtypeStruct((dec_seq, nb, D_MODEL), F32),
                   jax.ShapeDtypeStruct((2, nb, D_FF), F32)],
        scratch_shapes=[pltpu.VMEM((dec_seq, nb, D_MODEL), BF16),
                        pltpu.VMEM((dec_seq, nb, D_MODEL), F32)],
        compiler_params=_cparams(("arbitrary",)),
        name="ffn_sample",
    )(x, state, g, wg, wu, cw, wd, gf)


def _overlap(n_cmp, n_sel, rows, cols):
    c0 = np.arange(n_cmp)[:, None] * CMP_STRIDE
    s0 = np.arange(n_sel)[None, :] * SEL_BLOCK
    ov = np.minimum(c0 + CMP_BLOCK, s0 + SEL_BLOCK) - np.maximum(c0, s0)
    out = np.zeros((rows, cols), np.float32)
    out[:n_cmp, :n_sel] = np.clip(ov, 0, None).astype(np.float32) / CMP_BLOCK
    return out


def _block_expand(n_keys):
    return jnp.asarray(np.arange(128)[:, None] == (np.arange(n_keys)[None, :] // SEL_BLOCK), BF16)


def _chunk_perm():
    row = np.arange(PAGE)
    return jnp.asarray((row[:, None] % RPC) * CMP_STRIDE + row[:, None] // RPC == row[None, :], BF16)


def _pad_rows(a, rows):
    return jnp.concatenate([a, jnp.zeros((rows - a.shape[0],) + a.shape[1:], a.dtype)], axis=0)


def _prep_w_in(w_in_l):
    offs = np.cumsum((D_NSA, KV_W, KV_W, KV_W, N_BRANCH * N_HEADS, D_CONV, D_CONV, D_CONV))
    q, kc, ks, kw, gl, cb, cc, ch = jnp.split(w_in_l, [int(o) for o in offs[:-1]], axis=-1)
    glp = jnp.concatenate([gl, jnp.zeros((D_MODEL, 128 - N_BRANCH * N_HEADS), F32)], axis=-1)
    w_main = jnp.concatenate([q * (HEAD_DIM ** -0.5), cb, cc, ch, glp], axis=-1).astype(BF16)
    wt_kv = jnp.concatenate([kc, ks, kw], axis=-1).T.astype(BF16)
    return w_main, wt_kv


def _prep_cmp(w_cmp_l, pe_cmp_l):
    eye = jnp.eye(2 * N_KV, dtype=F32).reshape(2, N_KV, 2, N_KV)
    wbd = jnp.einsum('csde,cgkh->scgdkhe', w_cmp_l, eye).reshape(CMP_BLOCK, KV_W, KV_W).astype(BF16)
    pe = pe_cmp_l.reshape(2, 2, CMP_STRIDE, HEAD_DIM).transpose(1, 0, 3, 2)
    pet = jnp.broadcast_to(pe[:, :, None, :, None, :], (2, 2, N_KV, HEAD_DIM, RPC, CMP_STRIDE))
    return wbd, pet.reshape(2, KV_W, PAGE)


TM_PROJ = 512
TN_PROJ = 1408
TM_MIX = 256
TM_MEM = 256
TM_FFN = 512
TN_FFN = 512
BT_CMP = 4
BT_MEM = 8


def _kv_rows(kvt, row, start=0):
    slab = kvt[:, row:row + KV_W, start:]
    b, _, t = slab.shape
    return slab.reshape(b, 2, N_KV, HEAD_DIM, t).transpose(0, 4, 1, 2, 3)


def _layer_prompt(x, mem_kv, lw, consts, batch, seq, final):
    p_main, kvt = proj_in(x, lw['g_mix'], lw['w_main'], lw['wt_kv'], TM_PROJ, TN_PROJ, seq)
    cmp = compress_prompt(kvt, lw['pet'], consts['perm'], lw['wbd'])
    n_cmp_pad = seq // CMP_STRIDE
    cmp_t = cmp.reshape(batch, n_cmp_pad, 2, N_KV, HEAD_DIM).transpose(0, 2, 3, 1, 4)
    o_nsa = nsa_prompt(p_main, kvt, cmp_t, consts['ovt_p'], consts['expand_p'])
    x1, u_tail = mixout_prompt(o_nsa, p_main, lw['conv_w'], lw['g_out_nsa'], lw['g_out_conv'],
                               lw['w_out'], x, TM_MIX, seq)
    x2 = mem_prompt_attn(x1, lw['g_mem'], lw['w_mem_q'], mem_kv, lw['w_mem_o'], TM_MEM, seq)
    x3, g_tail = ffn_prompt(x2, lw['g_ffn'], lw['w_ff_gate'], lw['w_ff_up'], lw['ffn_conv_w'],
                            lw['w_ff_down'], lw['g_final'], TM_FFN, TN_FFN, seq, final)
    conv_state = u_tail.reshape(batch, seq // TM_MIX, 8, D_CONV)[:, -1, 6:8]
    ffn_state = g_tail.reshape(batch, seq // TM_FFN, 8, D_FF)[:, -1, 6:8]
    return (x3, _kv_rows(kvt, ROW_KC), _kv_rows(kvt, ROW_KS),
            _kv_rows(kvt, ROW_KW, seq - min(WINDOW, seq)), conv_state, ffn_state)


def _layer_sample(x, lw, consts, layer, cache_cmp, cache_sel, page_idx, cache_win, cache_mem,
                  st_conv, st_ffn, final):
    dec_seq, nb, _ = x.shape
    rows = dec_seq * nb
    p_s, kvt = proj_in(x.reshape(rows, D_MODEL), lw['g_mix'], lw['w_main'], lw['wt_kv'],
                       rows, TN_PROJ, rows)
    p3 = p_s.reshape(dec_seq, nb, D_MAIN)
    kvn = kvt.reshape(D_KVT, dec_seq, nb).transpose(2, 0, 1)
    kvn_pad = jnp.concatenate([kvn, jnp.zeros((nb, D_KVT, PAGE - dec_seq), F32)], axis=-1)
    cmp_s = compress_paged(cache_cmp, page_idx, lw['pet'], consts['perm'], lw['wbd'], BT_CMP)
    q = p3[:, :, COL_Q:COL_Q + D_NSA].reshape(dec_seq, nb, N_KV, HPG, HEAD_DIM).transpose(1, 3, 2, 0, 4)
    gsel = jnp.eye(N_KV, dtype=F32)
    qbd = (q[:, :, :, :, None, :] * gsel[None, None, :, None, :, None]).reshape(
        nb, HPG * N_KV * dec_seq, KV_HALF)
    gates = p3[:, :, COL_G:COL_G + N_BRANCH * N_HEADS].reshape(
        dec_seq, nb, N_KV, HPG, N_BRANCH).transpose(1, 3, 2, 0, 4).reshape(nb, HPG * N_KV * dec_seq, N_BRANCH)
    o = nsa_sample(cache_sel, page_idx, qbd, gates, cmp_s, kvn_pad[:, ROW_KS:ROW_KS + KV_W],
                   cache_win, layer, kvn_pad[:, ROW_KW:ROW_KW + KV_W], consts['ov_s'], consts['expand_s'])
    o_nsa = o.reshape(nb, HPG, N_KV, dec_seq, HEAD_DIM).transpose(3, 0, 2, 1, 4).reshape(dec_seq, nb, D_NSA)
    x1, conv_state = mixout_sample(o_nsa, p3, st_conv, lw['conv_w'], lw['g_out_nsa'],
                                   lw['g_out_conv'], lw['w_out'], x)
    qm = rms_matmul(x1.reshape(rows, D_MODEL), lw['g_mem'], lw['w_mem_q'], rows, D_MEM)
    qh = qm.reshape(dec_seq, nb, MEM_HEADS, MEM_HEAD_DIM).transpose(1, 2, 0, 3)
    qmbd = (qh[:, :, :, None, :] * jnp.eye(MEM_HEADS, dtype=F32)[None, :, None, :, None]).reshape(
        nb, MEM_HEADS * dec_seq, D_MEM)
    om = mem_sample_attn(qmbd, cache_mem, layer, BT_MEM)
    om = om.reshape(nb, MEM_HEADS, dec_seq, MEM_HEAD_DIM).transpose(2, 0, 1, 3).reshape(rows, D_MEM)
    x2 = matmul_residual(x1.reshape(rows, D_MODEL), om, lw['w_mem_o'], rows).reshape(dec_seq, nb, D_MODEL)
    x3, ffn_state = ffn_sample(x2, st_ffn, lw['g_ffn'], lw['w_ff_gate'], lw['w_ff_up'],
                               lw['ffn_conv_w'], lw['w_ff_down'], lw['g_final'], TN_FFN, final)
    kv5 = kvn.reshape(nb, N_BRANCH, 2, N_KV, HEAD_DIM, dec_seq).transpose(1, 0, 5, 2, 3, 4)
    return (x3, kv5[0], kv5[1], kv5[2], conv_state.transpose(1, 0, 2), ffn_state.transpose(1, 0, 2))


def kernel(x_prompt, x_sample, cache_cmp_kv, cache_sel_kv, cache_win_kv, cache_mem_kv,
           state_conv, state_ffn_conv, page_table, mem_prompt,
           g_mix, w_in, w_cmp, pe_cmp, conv_w, g_out_nsa, g_out_conv, w_out,
           g_mem_src, w_mem_kv, g_mem, w_mem_q, w_mem_o, g_ffn, w_ff_gate, w_ff_up,
           ffn_conv_w, w_ff_down, g_final):
    batch, seq, _ = x_prompt.shape
    nb, dec_seq, _ = x_sample.shape
    depth = w_in.shape[0]
    n_phys = cache_cmp_kv.shape[1]
    npages = page_table.shape[1]
    past = npages * PAGE
    win_buf = cache_win_kv.shape[2]
    mem_len = mem_prompt.shape[1]

    n_cmp_s = (past + dec_seq - CMP_BLOCK) // CMP_STRIDE + 1
    n_sel_s = -(-(past + dec_seq) // SEL_BLOCK)
    n_cmp_p = (seq - CMP_BLOCK) // CMP_STRIDE + 1
    n_sel_p = seq // SEL_BLOCK
    consts = {
        'ov_s': jnp.asarray(_overlap(n_cmp_s, n_sel_s, past // CMP_STRIDE, 128), BF16),
        'expand_s': _block_expand(past + PAGE),
        'ovt_p': jnp.asarray(_overlap(n_cmp_p, n_sel_p, seq // CMP_STRIDE, n_sel_p).T, BF16),
        'expand_p': _block_expand(seq),
        'perm': _chunk_perm(),
    }

    cache_cmp = cache_cmp_kv.transpose(0, 1, 3, 4, 5, 2).reshape(depth * n_phys, 2, N_KV, HEAD_DIM, PAGE)
    cache_sel = cache_sel_kv.transpose(0, 1, 3, 4, 5, 2).reshape(depth * n_phys, 2, N_KV, HEAD_DIM, PAGE)
    cache_win = cache_win_kv.transpose(0, 1, 3, 4, 5, 2)
    cache_mem = cache_mem_kv.reshape(depth, nb, mem_len * 2 * MEM_HEADS, MEM_HEAD_DIM)

    xp = x_prompt.reshape(batch * seq, D_MODEL)
    xs = x_sample.transpose(1, 0, 2)
    mem_rows = mem_prompt.reshape(batch * mem_len, D_MODEL)
    outs = [[] for _ in range(11)]
    for l in range(depth):
        wbd, pet = _prep_cmp(w_cmp[l], pe_cmp[l])
        w_main, wt_kv = _prep_w_in(w_in[l])
        lw = {
            'g_mix': g_mix[l][None], 'w_main': w_main, 'wt_kv': wt_kv, 'wbd': wbd, 'pet': pet,
            'conv_w': _pad_rows(conv_w[l], 8), 'g_out_nsa': g_out_nsa[l][None],
            'g_out_conv': g_out_conv[l][None], 'w_out': w_out[l].astype(BF16),
            'g_mem': g_mem[l][None], 'w_mem_q': w_mem_q[l].astype(BF16),
            'w_mem_o': w_mem_o[l].astype(BF16), 'g_ffn': g_ffn[l][None],
            'w_ff_gate': w_ff_gate[l].astype(BF16), 'w_ff_up': w_ff_up[l].astype(BF16),
            'ffn_conv_w': _pad_rows(ffn_conv_w[l], 8), 'w_ff_down': w_ff_down[l].astype(BF16),
            'g_final': g_final[None],
        }
        final = l == depth - 1
        mem_kv_p = rms_matmul(mem_rows, g_mem_src[l][None], w_mem_kv[l].astype(BF16),
                              batch * mem_len, 2 * D_MEM).reshape(batch, mem_len, 2 * D_MEM)
        xp, kc, ks, kw, cst, fst = _layer_prompt(xp, mem_kv_p, lw, consts, batch, seq, final)
        page_idx = page_table + l * n_phys
        xs, kc2, ks2, kw2, cst2, fst2 = _layer_sample(
            xs, lw, consts, l, cache_cmp, cache_sel, page_idx, cache_win, cache_mem,
            state_conv[l].transpose(1, 0, 2), state_ffn_conv[l].transpose(1, 0, 2), final)
        win_new = jnp.concatenate([cache_win_kv[l][:, dec_seq:], kw2], axis=1)
        for lst, val in zip(outs, (kc, ks, kw, mem_kv_p, cst, fst, kc2, ks2, win_new, cst2, fst2)):
            lst.append(val)

    st = [jnp.stack(o) for o in outs]
    return (xp.reshape(batch, seq, D_MODEL), xs.transpose(1, 0, 2),
            st[0], st[1], st[2],
            st[3].reshape(depth, batch, mem_len, 2, MEM_HEADS, MEM_HEAD_DIM),
            st[4], st[5], st[6], st[7], st[8], st[9], st[10])
```

```python
import functools

import numpy as np
import jax
import jax.numpy as jnp
from jax import lax
from jax.experimental import pallas as pl
from jax.experimental.pallas import tpu as pltpu

F32 = jnp.float32
BF16 = jnp.bfloat16

D_MODEL = 2048
D_NSA = 1024
D_CONV = 1024
HEAD_DIM = 64
N_HEADS = 16
N_KV = 2
HPG = 8
N_BRANCH = 3
KV_W = 2 * N_KV * HEAD_DIM
KV_HALF = N_KV * HEAD_DIM
CMP_BLOCK = 32
CMP_STRIDE = 16
SEL_BLOCK = 64
TOPK = 16
WINDOW = 512
Q_BLOCK = 128
FORCE_BONUS = 1e4
D_FF = 5632
MEM_HEADS = 4
MEM_HEAD_DIM = 128
D_MEM = 512
PAGE = 128
EPS = 1e-6
NEG = -1e30

COL_Q = 0
COL_CB = 1024
COL_CC = 2048
COL_CH = 3072
COL_G = 4096
D_MAIN = 4224
ROW_KC = 0
ROW_KS = 256
ROW_KW = 512
D_KVT = 768

VMEM_LIMIT = 56 * 1024 * 1024


def _cparams(sem):
    return pltpu.CompilerParams(dimension_semantics=sem, vmem_limit_bytes=VMEM_LIMIT)


def _rms(x, g):
    return x * lax.rsqrt(jnp.mean(x * x, axis=-1, keepdims=True) + EPS) * g


def _dot(a, b):
    return jnp.dot(a, b, preferred_element_type=F32)


def _dot_nt(a, b):
    return lax.dot_general(a, b, (((1,), (1,)), ((), ())), preferred_element_type=F32)


def _softmax_bias(s, bias):
    s = s + bias
    e = jnp.exp(s - jnp.max(s, axis=-1, keepdims=True))
    return e, jnp.sum(e, axis=-1, keepdims=True)


def _sigmoid(x):
    return 1.0 / (1.0 + jnp.exp(-x))


def _rms_matmul_kernel(x_ref, g_ref, w_ref, o_ref, h_ref):
    @pl.when(pl.program_id(1) == 0)
    def _():
        h_ref[...] = _rms(x_ref[...], g_ref[...]).astype(BF16)

    o_ref[...] = _dot(h_ref[...], w_ref[...]).astype(o_ref.dtype)


def rms_matmul(x, g, w, tm, tn):
    m, d = x.shape
    n = w.shape[1]
    return pl.pallas_call(
        _rms_matmul_kernel,
        grid=(m // tm, n // tn),
        in_specs=[pl.BlockSpec((tm, d), lambda i, j: (i, 0)),
                  pl.BlockSpec((1, d), lambda i, j: (0, 0)),
                  pl.BlockSpec((d, tn), lambda i, j: (0, j))],
        out_specs=pl.BlockSpec((tm, tn), lambda i, j: (i, j)),
        out_shape=jax.ShapeDtypeStruct((m, n), F32),
        scratch_shapes=[pltpu.VMEM((tm, d), BF16)],
        compiler_params=_cparams(("arbitrary", "arbitrary")),
        name="rms_matmul",
    )(x, g, w)


def _proj_in_kernel(x_ref, g_ref, w_ref, wt_ref, o_ref, ot_ref, h_ref):
    @pl.when(pl.program_id(1) == 0)
    def _():
        h = _rms(x_ref[...], g_ref[...]).astype(BF16)
        h_ref[...] = h
        ot_ref[...] = _dot_nt(wt_ref[...], h)

    o_ref[...] = _dot(h_ref[...], w_ref[...])


def proj_in(x, g, w_main, wt_kv, tm, tn, seq):
    m, d = x.shape
    tps = seq // tm
    return pl.pallas_call(
        _proj_in_kernel,
        grid=(m // tm, D_MAIN // tn),
        in_specs=[pl.BlockSpec((tm, d), lambda i, j: (i, 0)),
                  pl.BlockSpec((1, d), lambda i, j: (0, 0)),
                  pl.BlockSpec((d, tn), lambda i, j: (0, j)),
                  pl.BlockSpec((D_KVT, d), lambda i, j: (0, 0))],
        out_specs=[pl.BlockSpec((tm, tn), lambda i, j: (i, j)),
                   pl.BlockSpec((None, D_KVT, tm), lambda i, j: (i // tps, 0, i % tps))],
        out_shape=[jax.ShapeDtypeStruct((m, D_MAIN), F32),
                   jax.ShapeDtypeStruct((m // seq, D_KVT, seq), F32)],
        scratch_shapes=[pltpu.VMEM((tm, d), BF16)],
        compiler_params=_cparams(("arbitrary", "arbitrary")),
        name="proj_in",
    )(x, g, w_main, wt_kv)


def _matmul_res_kernel(x_ref, a_ref, w_ref, o_ref):
    o_ref[...] = x_ref[...] + _dot(a_ref[...].astype(BF16), w_ref[...])


def matmul_residual(x, a, w, tm):
    m, n = x.shape
    k = a.shape[1]
    return pl.pallas_call(
        _matmul_res_kernel,
        grid=(m // tm,),
        in_specs=[pl.BlockSpec((tm, n), lambda i: (i, 0)),
                  pl.BlockSpec((tm, k), lambda i: (i, 0)),
                  pl.BlockSpec((k, n), lambda i: (0, 0))],
        out_specs=pl.BlockSpec((tm, n), lambda i: (i, 0)),
        out_shape=jax.ShapeDtypeStruct((m, n), F32),
        compiler_params=_cparams(("arbitrary",)),
        name="matmul_residual",
    )(x, a, w)


RPC = PAGE // CMP_STRIDE


def _compress_chunks(chunk_at, n_chunks, pet_ref, perm_ref, w_ref, xs_ref):
    perm = perm_ref[...]
    for ci in range(n_chunks):
        chunk = chunk_at(ci)
        for a in range(2):
            xs_ref[a, ci] = _dot_nt(perm, (chunk + pet_ref[a]).astype(BF16))
    rows = n_chunks * RPC
    acc0 = jnp.zeros((rows, KV_W), F32)
    acc1 = jnp.zeros((rows, KV_W), F32)
    for r in range(CMP_STRIDE):
        x0 = xs_ref[0, :, r * RPC:(r + 1) * RPC, :].reshape(rows, KV_W).astype(BF16)
        x1 = xs_ref[1, :, r * RPC:(r + 1) * RPC, :].reshape(rows, KV_W).astype(BF16)
        acc0 = acc0 + _dot(x0, w_ref[r])
        acc1 = acc1 + _dot(x1, w_ref[CMP_STRIDE + r])
    return acc0 + pltpu.roll(acc1, rows - 1, 0)


def _compress_prompt_kernel(kvt_ref, pet_ref, perm_ref, w_ref, o_ref, xs_ref, *, n_chunks):
    out = _compress_chunks(lambda ci: kvt_ref[:, ci * PAGE:(ci + 1) * PAGE], n_chunks,
                           pet_ref, perm_ref, w_ref, xs_ref)
    o_ref[...] = out.astype(BF16)


def compress_prompt(kvt, pet, perm, wbd):
    batch, _, seq = kvt.shape
    n_chunks = seq // PAGE
    nblk = seq // CMP_STRIDE
    return pl.pallas_call(
        functools.partial(_compress_prompt_kernel, n_chunks=n_chunks),
        grid=(batch,),
        in_specs=[pl.BlockSpec((None, KV_W, seq), lambda b: (b, ROW_KC // KV_W, 0)),
                  pl.BlockSpec((2, KV_W, PAGE), lambda b: (0, 0, 0)),
                  pl.BlockSpec((PAGE, PAGE), lambda b: (0, 0)),
                  pl.BlockSpec((CMP_BLOCK, KV_W, KV_W), lambda b: (0, 0, 0))],
        out_specs=pl.BlockSpec((None, nblk, KV_W), lambda b: (b, 0, 0)),
        out_shape=jax.ShapeDtypeStruct((batch, nblk, KV_W), BF16),
        scratch_shapes=[pltpu.VMEM((2, n_chunks, PAGE, KV_W), F32)],
        compiler_params=_cparams(("arbitrary",)),
        name="compress_prompt",
    )(kvt, pet, perm, wbd)


def _compress_paged_kernel(pt_ref, *refs, bt, npages):
    page_refs = refs[:bt * npages]
    pet_ref, perm_ref, w_ref, o_ref, xs_ref = refs[bt * npages:]
    out = _compress_chunks(lambda ci: page_refs[ci][...].reshape(KV_W, PAGE), bt * npages,
                           pet_ref, perm_ref, w_ref, xs_ref)
    o_ref[...] = out.astype(BF16).reshape(bt, npages * RPC, KV_W)


def compress_paged(cache, page_idx, pet, perm, wbd, bt):
    nb, npages = page_idx.shape
    nblk = npages * RPC
    page_specs = [
        pl.BlockSpec((None, 2, N_KV, HEAD_DIM, PAGE), functools.partial(
            lambda i, pt, bb, p: (pt[i * bt + bb, p], 0, 0, 0, 0), bb=bb, p=p))
        for bb in range(bt) for p in range(npages)]
    grid_spec = pltpu.PrefetchScalarGridSpec(
        num_scalar_prefetch=1,
        grid=(nb // bt,),
        in_specs=page_specs + [pl.BlockSpec((2, KV_W, PAGE), lambda i, pt: (0, 0, 0)),
                               pl.BlockSpec((PAGE, PAGE), lambda i, pt: (0, 0)),
                               pl.BlockSpec((CMP_BLOCK, KV_W, KV_W), lambda i, pt: (0, 0, 0))],
        out_specs=pl.BlockSpec((bt, nblk, KV_W), lambda i, pt: (i, 0, 0)),
        scratch_shapes=[pltpu.VMEM((2, bt * npages, PAGE, KV_W), F32)])
    return pl.pallas_call(
        functools.partial(_compress_paged_kernel, bt=bt, npages=npages),
        grid_spec=grid_spec,
        out_shape=jax.ShapeDtypeStruct((nb, nblk, KV_W), BF16),
        compiler_params=_cparams(("arbitrary",)),
        name="compress_paged",
    )(page_idx, *([cache] * (bt * npages)), pet, perm, wbd)


def _importance(imp, idx, tpos, n_sel):
    valid = (idx * SEL_BLOCK <= tpos) & (idx < n_sel)
    cur = tpos // SEL_BLOCK
    forced = (idx == 0) | (idx == cur) | (idx == cur - 1)
    return jnp.where(valid, imp + jnp.where(forced, FORCE_BONUS, 0.0), NEG)


def _topk_rows(imp, n_cand):
    groups = [imp[v * 8:(v + 1) * 8] for v in range(n_cand // 8)]
    sub = lax.broadcasted_iota(jnp.int32, groups[0].shape, 0)
    ranks = [jnp.zeros(g.shape, F32) for g in groups]
    for jp in range(n_cand):
        c = imp[jp:jp + 1, :]
        for v, g in enumerate(groups):
            ge = jnp.where(c >= g, 1.0, 0.0)
            gt = jnp.where(c > g, 1.0, 0.0)
            if v * 8 > jp:
                beats = ge
            elif v * 8 + 7 < jp:
                beats = gt
            else:
                beats = jnp.where(sub > jp - v * 8, ge, gt)
            ranks[v] = ranks[v] + beats
    return jnp.concatenate([jnp.where(r < TOPK, 1.0, 0.0) for r in ranks], axis=0)


def _topk_lanes(imp, idx, n_cand):
    rank = jnp.zeros(imp.shape, F32)
    for jp in range(n_cand):
        c = imp[:, jp:jp + 1]
        rank = rank + jnp.where(idx > jp, jnp.where(c >= imp, 1.0, 0.0), jnp.where(c > imp, 1.0, 0.0))
    return jnp.where(rank < TOPK, 1.0, 0.0)


KEY_TILE = 512
V_PAD = 16


def _tile_heads(x):
    return jnp.concatenate([x] * HPG, axis=0)


def _nsa_prompt_kernel(q_ref, gate_ref, cmp_ref, kst_ref, vst_ref, kwt_ref, vwt_ref, ovt_ref, exp_ref,
                       o_ref, ksb, vsb, kwb, vwb, *, seq):
    g = pl.program_id(1)
    i = pl.program_id(2)
    n_cmp_pad = seq // CMP_STRIDE
    n_sel = seq // SEL_BLOCK

    @pl.when(i == 0)
    def _():
        ones_row = jnp.where(lax.broadcasted_iota(jnp.int32, (V_PAD, seq), 0) == 0, 1.0, 0.0).astype(BF16)
        ksb[...] = kst_ref[...].astype(BF16)
        vsb[0:HEAD_DIM, :] = vst_ref[...].astype(BF16)
        vsb[HEAD_DIM:HEAD_DIM + V_PAD, :] = ones_row
        kwb[...] = kwt_ref[...].astype(BF16)
        vwb[0:HEAD_DIM, :] = vwt_ref[...].astype(BF16)
        vwb[HEAD_DIM:HEAD_DIM + V_PAD, :] = ones_row

    t0 = i * Q_BLOCK
    qb = q_ref[...]
    qs = jnp.concatenate([qb[:, h * HEAD_DIM:(h + 1) * HEAD_DIM] for h in range(HPG)],
                         axis=0).astype(BF16)
    tcol = lax.broadcasted_iota(jnp.int32, (Q_BLOCK, 1), 0) + t0

    ncol = lax.broadcasted_iota(jnp.int32, (Q_BLOCK, n_cmp_pad), 1)
    vis = jnp.where(ncol < n_cmp_pad - 1, ncol * CMP_STRIDE + (CMP_BLOCK - 1), seq) <= tcol
    e_c, l_c = _softmax_bias(_dot_nt(qs, cmp_ref[0]), _tile_heads(jnp.where(vis, 0.0, NEG)))
    any_vis = _tile_heads(jnp.where(tcol >= CMP_BLOCK - 1, 1.0, 0.0))
    p_c = (e_c * (any_vis / jnp.maximum(l_c, 1e-30))).astype(BF16)
    o_c = _dot(p_c, cmp_ref[1])

    po = _dot_nt(ovt_ref[...], p_c)
    imp_t = po[:, 0:Q_BLOCK]
    for h in range(1, HPG):
        imp_t = imp_t + po[:, h * Q_BLOCK:(h + 1) * Q_BLOCK]
    jj = lax.broadcasted_iota(jnp.int32, (n_sel, Q_BLOCK), 0)
    tt = lax.broadcasted_iota(jnp.int32, (n_sel, Q_BLOCK), 1) + t0
    sel_t = _topk_rows(_importance(imp_t, jj, tt, n_sel), n_sel)
    sel = jnp.concatenate([sel_t, jnp.zeros((128 - n_sel, Q_BLOCK), F32)], axis=0).T.astype(BF16)

    kk = lax.broadcasted_iota(jnp.int32, (Q_BLOCK, KEY_TILE), 1)

    def scores(c):
        k0 = pl.multiple_of(c * KEY_TILE, KEY_TILE)
        msel = _dot(sel, exp_ref[:, pl.ds(k0, KEY_TILE)])
        bias = jnp.where(msel > 0.5, jnp.where(kk + k0 <= tcol, 0.0, NEG), NEG)
        return _dot(qs, ksb[:, pl.ds(k0, KEY_TILE)]) + _tile_heads(bias)

    def accumulate(c, s, carry):
        m, acc = carry
        k0 = pl.multiple_of(c * KEY_TILE, KEY_TILE)
        m_new = jnp.maximum(m, jnp.max(s, axis=-1, keepdims=True))
        p = jnp.exp(s - m_new).astype(BF16)
        return m_new, jnp.exp(m - m_new) * acc + _dot_nt(p, vsb[:, pl.ds(k0, KEY_TILE)])

    def tile_pair(c2, carry):
        s_a = scores(2 * c2)
        s_b = scores(2 * c2 + 1)
        return accumulate(2 * c2 + 1, s_b, accumulate(2 * c2, s_a, carry))

    n_rows = HPG * Q_BLOCK
    m0 = jnp.full((n_rows, 1), NEG, F32)
    a0 = jnp.zeros((n_rows, HEAD_DIM + V_PAD), F32)
    n_tiles = (t0 + Q_BLOCK + KEY_TILE - 1) // KEY_TILE
    carry = lax.fori_loop(0, n_tiles // 2, tile_pair, (m0, a0))
    _, acc_s = lax.cond(n_tiles % 2 == 1,
                        lambda cr: accumulate(n_tiles - 1, scores(n_tiles - 1), cr),
                        lambda cr: cr, carry)
    o_s = acc_s[:, 0:HEAD_DIM] / acc_s[:, HEAD_DIM:HEAD_DIM + 1]

    span = WINDOW + Q_BLOCK
    ws = pl.multiple_of(jnp.maximum(t0 - WINDOW, 0), Q_BLOCK)
    dlt = tcol - (lax.broadcasted_iota(jnp.int32, (Q_BLOCK, span), 1) + ws)
    bias_w = jnp.where(dlt >= 0, jnp.where(dlt < WINDOW, 0.0, NEG), NEG)
    s_w = _dot(qs, kwb[:, pl.ds(ws, span)]) + _tile_heads(bias_w)
    e_w = jnp.exp(s_w - jnp.max(s_w, axis=-1, keepdims=True)).astype(BF16)
    acc_w = _dot_nt(e_w, vwb[:, pl.ds(ws, span)])
    o_w = acc_w[:, 0:HEAD_DIM] / acc_w[:, HEAD_DIM:HEAD_DIM + 1]

    gs = _sigmoid(gate_ref[...])
    gsel = jnp.where(g == 0, gs[:, 0:HPG * N_BRANCH], gs[:, HPG * N_BRANCH:2 * HPG * N_BRANCH])
    for h in range(HPG):
        r0, r1 = h * Q_BLOCK, (h + 1) * Q_BLOCK
        c0 = h * N_BRANCH
        o_ref[:, h * HEAD_DIM:(h + 1) * HEAD_DIM] = (
            o_c[r0:r1] * gsel[:, c0:c0 + 1] + o_s[r0:r1] * gsel[:, c0 + 1:c0 + 2]
            + o_w[r0:r1] * gsel[:, c0 + 2:c0 + 3])


def nsa_prompt(p_main, kvt, cmp_t, ovt, expand):
    batch, _, seq = kvt.shape
    nqb = seq // Q_BLOCK
    gw = HPG * HEAD_DIM
    n_cmp_pad = seq // CMP_STRIDE
    n_sel = seq // SEL_BLOCK
    kvt_spec = lambda row: pl.BlockSpec((None, HEAD_DIM, seq),
                                        lambda b, g, i: (b, row // HEAD_DIM + g, 0))
    return pl.pallas_call(
        functools.partial(_nsa_prompt_kernel, seq=seq),
        grid=(batch, N_KV, nqb),
        in_specs=[
            pl.BlockSpec((Q_BLOCK, gw), lambda b, g, i: (b * nqb + i, g)),
            pl.BlockSpec((Q_BLOCK, 128), lambda b, g, i: (b * nqb + i, COL_G // 128)),
            pl.BlockSpec((None, 2, None, n_cmp_pad, HEAD_DIM), lambda b, g, i: (b, 0, g, 0, 0)),
            kvt_spec(ROW_KS), kvt_spec(ROW_KS + KV_HALF),
            kvt_spec(ROW_KW), kvt_spec(ROW_KW + KV_HALF),
            pl.BlockSpec((n_sel, n_cmp_pad), lambda b, g, i: (0, 0)),
            pl.BlockSpec((128, seq), lambda b, g, i: (0, 0)),
        ],
        out_specs=pl.BlockSpec((Q_BLOCK, gw), lambda b, g, i: (b * nqb + i, g)),
        out_shape=jax.ShapeDtypeStruct((batch * seq, D_NSA), F32),
        scratch_shapes=[pltpu.VMEM((HEAD_DIM, seq), BF16), pltpu.VMEM((HEAD_DIM + V_PAD, seq), BF16)] * 2,
        compiler_params=_cparams(("arbitrary", "arbitrary", "arbitrary")),
        name="nsa_prompt",
    )(p_main, p_main, cmp_t, kvt, kvt, kvt, kvt, ovt, expand)


def _pick_group(o2, rg):
    return jnp.where(rg == 0, o2[:, 0:HEAD_DIM], o2[:, HEAD_DIM:2 * HEAD_DIM])


def _masked_softmax(s, mask):
    s = jnp.where(mask, s, NEG)
    m = jnp.max(s, axis=-1, keepdims=True)
    e = jnp.where(mask, jnp.exp(s - m), 0.0)
    return e / jnp.maximum(jnp.sum(e, axis=-1, keepdims=True), 1e-30)


def _nsa_sample_kernel(pt_ref, *refs, bt, npages, dec_seq):
    page_refs = refs[:bt * npages]
    (q_ref, gate_ref, cmp_ref, ksn_ref, win_ref, kwn_ref, ov_ref, exp_ref,
     o_ref, wout_ref, kts, vts, ktw, vtw) = refs[bt * npages:]
    past = npages * PAGE
    n_cmp = (past + dec_seq - CMP_BLOCK) // CMP_STRIDE + 1
    n_sel = -(-(past + dec_seq) // SEL_BLOCK)
    n_rows = HPG * N_KV * dec_seq
    n_keys = past + PAGE
    win_buf = win_ref.shape[-1]
    w_keys = win_buf + PAGE
    n_cmp_pad = cmp_ref.shape[1]
    gt = N_KV * dec_seq

    rowi = lax.broadcasted_iota(jnp.int32, (n_rows, 1), 0)
    rg = (rowi // dec_seq) % N_KV
    tpos = past + rowi % dec_seq
    ncol = lax.broadcasted_iota(jnp.int32, (n_rows, n_cmp_pad), 1)
    m_c = (ncol * CMP_STRIDE + (CMP_BLOCK - 1) <= tpos) & (ncol < n_cmp)
    jj = lax.broadcasted_iota(jnp.int32, (gt, 128), 1)
    t8 = past + lax.broadcasted_iota(jnp.int32, (gt, 128), 0) % dec_seq
    kcol = lax.broadcasted_iota(jnp.int32, (n_rows, n_keys), 1)
    wcol = lax.broadcasted_iota(jnp.int32, (n_rows, w_keys), 1)
    dlt = tpos - (wcol + (past - win_buf))
    m_w = (dlt >= 0) & (dlt < WINDOW)
    lane = lax.broadcasted_iota(jnp.int32, (KV_W, PAGE), 1)

    bbs = range(bt)

    for bb in bbs:
        for p in range(npages):
            page = page_refs[bb * npages + p]
            kts[bb, :, p * PAGE:(p + 1) * PAGE] = page[0].reshape(KV_HALF, PAGE).astype(BF16)
            vts[bb, :, p * PAGE:(p + 1) * PAGE] = page[1].reshape(KV_HALF, PAGE).astype(BF16)
        kts[bb, :, past:n_keys] = ksn_ref[bb, 0:KV_HALF, :].astype(BF16)
        vts[bb, :, past:n_keys] = ksn_ref[bb, KV_HALF:KV_W, :].astype(BF16)
        win = win_ref[bb].reshape(KV_W, win_buf)
        kwn = kwn_ref[bb]
        ktw[bb, :, 0:win_buf] = win[0:KV_HALF].astype(BF16)
        vtw[bb, :, 0:win_buf] = win[KV_HALF:KV_W].astype(BF16)
        ktw[bb, :, win_buf:w_keys] = kwn[0:KV_HALF, :].astype(BF16)
        vtw[bb, :, win_buf:w_keys] = kwn[KV_HALF:KV_W, :].astype(BF16)
        shifted = pltpu.roll(win, win_buf - dec_seq, 1)
        tail = jnp.where(lane >= PAGE - dec_seq, pltpu.roll(kwn, PAGE - dec_seq, 1),
                         shifted[:, win_buf - PAGE:])
        wout_ref[bb] = jnp.concatenate([shifted[:, 0:win_buf - PAGE], tail], axis=1).reshape(
            2, N_KV, HEAD_DIM, win_buf)

    qs = [q_ref[bb].astype(BF16) for bb in bbs]

    s_c = [_dot_nt(qs[bb], cmp_ref[bb, :, 0:KV_HALF]) for bb in bbs]
    s_w = [_dot(qs[bb], ktw[bb]) for bb in bbs]
    p_c = [_masked_softmax(s, m_c).astype(BF16) for s in s_c]
    p_w = [_masked_softmax(s, m_w).astype(BF16) for s in s_w]
    o_c = [_pick_group(_dot(p_c[bb], cmp_ref[bb, :, KV_HALF:KV_W]), rg) for bb in bbs]
    o_w = [_pick_group(_dot_nt(p_w[bb], vtw[bb]), rg) for bb in bbs]

    po = [_dot(p, ov_ref[...]) for p in p_c]
    imp = [functools.reduce(lambda a, h: a + x[h * gt:(h + 1) * gt], range(1, HPG), x[0:gt]) for x in po]
    sel8 = [_topk_lanes(_importance(x, jj, t8, n_sel), jj, n_sel) for x in imp]
    msel = [_dot(jnp.concatenate([x] * HPG, axis=0).astype(BF16), exp_ref[...]) for x in sel8]

    causal = kcol <= tpos
    s_s = [_dot(qs[bb], kts[bb]) for bb in bbs]
    p_s = [_masked_softmax(s_s[bb], (msel[bb] > 0.5) & causal).astype(BF16) for bb in bbs]
    o_s = [_pick_group(_dot_nt(p_s[bb], vts[bb]), rg) for bb in bbs]

    for bb in bbs:
        gs = _sigmoid(gate_ref[bb])
        o_ref[bb] = o_c[bb] * gs[:, 0:1] + o_s[bb] * gs[:, 1:2] + o_w[bb] * gs[:, 2:3]


def nsa_sample(cache_sel, page_idx, qbd, gates, cmp_s, ks_new, cache_win, layer, kw_new, ov_s, expand, bt):
    nb, npages = page_idx.shape
    n_rows = qbd.shape[1]
    dec_seq = n_rows // (HPG * N_KV)
    win_buf = cache_win.shape[-1]
    n_cmp_pad = cmp_s.shape[1]
    n_keys = npages * PAGE + PAGE
    page_specs = [
        pl.BlockSpec((None, 2, N_KV, HEAD_DIM, PAGE), functools.partial(
            lambda i, pt, bb, p: (pt[i * bt + bb, p], 0, 0, 0, 0), bb=bb, p=p))
        for bb in range(bt) for p in range(npages)]
    per_b = lambda i, pt: (i, 0, 0)
    const2 = lambda i, pt: (0, 0)
    win_shape = (2, N_KV, HEAD_DIM, win_buf)
    grid_spec = pltpu.PrefetchScalarGridSpec(
        num_scalar_prefetch=1,
        grid=(nb // bt,),
        in_specs=page_specs + [
            pl.BlockSpec((bt, n_rows, KV_HALF), per_b),
            pl.BlockSpec((bt, n_rows, N_BRANCH), per_b),
            pl.BlockSpec((bt, n_cmp_pad, KV_W), per_b),
            pl.BlockSpec((bt, KV_W, PAGE), per_b),
            pl.BlockSpec((None, bt) + win_shape, lambda i, pt: (layer, i, 0, 0, 0, 0)),
            pl.BlockSpec((bt, KV_W, PAGE), per_b),
            pl.BlockSpec((n_cmp_pad, 128), const2),
            pl.BlockSpec((128, n_keys), const2),
        ],
        out_specs=[pl.BlockSpec((bt, n_rows, HEAD_DIM), per_b),
                   pl.BlockSpec((bt,) + win_shape, lambda i, pt: (i, 0, 0, 0, 0))],
        scratch_shapes=[pltpu.VMEM((bt, KV_HALF, n_keys), BF16)] * 2
        + [pltpu.VMEM((bt, KV_HALF, win_buf + PAGE), BF16)] * 2)
    return pl.pallas_call(
        functools.partial(_nsa_sample_kernel, bt=bt, npages=npages, dec_seq=dec_seq),
        grid_spec=grid_spec,
        out_shape=[jax.ShapeDtypeStruct((nb, n_rows, HEAD_DIM), F32),
                   jax.ShapeDtypeStruct((nb,) + win_shape, F32)],
        compiler_params=_cparams(("arbitrary",)),
        name="nsa_sample",
    )(page_idx, *([cache_sel] * (bt * npages)), qbd, gates, cmp_s, ks_new, cache_win, kw_new, ov_s, expand)


def _shift_rows(u, p1, p2):
    r = lax.broadcasted_iota(jnp.int32, (u.shape[0], 1), 0)
    u1 = jnp.where(r >= 1, pltpu.roll(u, 1, 0), p1)
    u2 = jnp.where(r >= 2, pltpu.roll(u, 2, 0), jnp.where(r == 1, p1, p2))
    return u1, u2


def _mix_project(o_nsa, o_conv, gn, gc, w, x):
    mix = jnp.concatenate([_rms(o_nsa, gn), _rms(o_conv, gc)], axis=-1).astype(BF16)
    return x + _dot(mix, w)


def _mixout_prompt_kernel(on_ref, cb_ref, cc_ref, ch_ref, cch_ref, chh_ref, cw_ref, gn_ref, gc_ref,
                          w_ref, x_ref, o_ref, ut_ref, *, tiles_per_seq):
    first = (pl.program_id(0) % tiles_per_seq) == 0
    u = cc_ref[...] * ch_ref[...]
    uh = jnp.where(first, 0.0, cch_ref[...] * chh_ref[...])
    u1, u2 = _shift_rows(u, uh[7:8], uh[6:7])
    cw = cw_ref[...]
    v = cw[0:1] * u2 + cw[1:2] * u1 + cw[2:3] * u
    o_ref[...] = _mix_project(on_ref[...], cb_ref[...] * v, gn_ref[...], gc_ref[...],
                              w_ref[...], x_ref[...])
    ut_ref[...] = u[u.shape[0] - 8:, :]


def mixout_prompt(o_nsa, p_main, cw, gn, gc, w, x, tm, seq):
    m = x.shape[0]
    nt = m // tm
    cblk = lambda c: (lambda i: (i, c // D_CONV))
    halo = lambda c: (lambda i: (jnp.maximum(i * (tm // 8) - 1, 0), c // D_CONV))
    const = lambda i: (0, 0)
    return pl.pallas_call(
        functools.partial(_mixout_prompt_kernel, tiles_per_seq=seq // tm),
        grid=(nt,),
        in_specs=[pl.BlockSpec((tm, D_NSA), lambda i: (i, 0)),
                  pl.BlockSpec((tm, D_CONV), cblk(COL_CB)),
                  pl.BlockSpec((tm, D_CONV), cblk(COL_CC)),
                  pl.BlockSpec((tm, D_CONV), cblk(COL_CH)),
                  pl.BlockSpec((8, D_CONV), halo(COL_CC)),
                  pl.BlockSpec((8, D_CONV), halo(COL_CH)),
                  pl.BlockSpec((8, D_CONV), const),
                  pl.BlockSpec((1, D_NSA), const),
                  pl.BlockSpec((1, D_CONV), const),
                  pl.BlockSpec((D_MODEL, D_MODEL), const),
                  pl.BlockSpec((tm, D_MODEL), lambda i: (i, 0))],
        out_specs=[pl.BlockSpec((tm, D_MODEL), lambda i: (i, 0)),
                   pl.BlockSpec((None, 8, D_CONV), lambda i: (i, 0, 0))],
        out_shape=[jax.ShapeDtypeStruct((m, D_MODEL), F32),
                   jax.ShapeDtypeStruct((nt, 8, D_CONV), F32)],
        compiler_params=_cparams(("arbitrary",)),
        name="mixout_prompt",
    )(o_nsa, p_main, p_main, p_main, p_main, p_main, cw, gn, gc, w, x)


def _mixout_sample_kernel(on_ref, cb_ref, cc_ref, ch_ref, st_ref, cw_ref, gn_ref, gc_ref,
                          w_ref, x_ref, o_ref, nst_ref, *, dec_seq):
    ucat = [st_ref[0], st_ref[1]] + [cc_ref[t] * ch_ref[t] for t in range(dec_seq)]
    cw = cw_ref[...]
    for t in range(dec_seq):
        v = cw[0:1] * ucat[t] + cw[1:2] * ucat[t + 1] + cw[2:3] * ucat[t + 2]
        o_ref[t] = _mix_project(on_ref[t], cb_ref[t] * v, gn_ref[...], gc_ref[...],
                                w_ref[...], x_ref[t])
    nst_ref[0] = ucat[dec_seq]
    nst_ref[1] = ucat[dec_seq + 1]


def mixout_sample(o_nsa, p_s, state, cw, gn, gc, w, x):
    dec_seq, nb, _ = x.shape
    cblk = lambda c: (lambda i: (0, 0, c // D_CONV))
    full3 = lambda i: (0, 0, 0)
    const = lambda i: (0, 0)
    return pl.pallas_call(
        functools.partial(_mixout_sample_kernel, dec_seq=dec_seq),
        grid=(1,),
        in_specs=[pl.BlockSpec((dec_seq, nb, D_NSA), full3),
                  pl.BlockSpec((dec_seq, nb, D_CONV), cblk(COL_CB)),
                  pl.BlockSpec((dec_seq, nb, D_CONV), cblk(COL_CC)),
                  pl.BlockSpec((dec_seq, nb, D_CONV), cblk(COL_CH)),
                  pl.BlockSpec((2, nb, D_CONV), full3),
                  pl.BlockSpec((8, D_CONV), const),
                  pl.BlockSpec((1, D_NSA), const),
                  pl.BlockSpec((1, D_CONV), const),
                  pl.BlockSpec((D_MODEL, D_MODEL), const),
                  pl.BlockSpec((dec_seq, nb, D_MODEL), full3)],
        out_specs=[pl.BlockSpec((dec_seq, nb, D_MODEL), full3),
                   pl.BlockSpec((2, nb, D_CONV), full3)],
        out_shape=[jax.ShapeDtypeStruct((dec_seq, nb, D_MODEL), F32),
                   jax.ShapeDtypeStruct((2, nb, D_CONV), F32)],
        compiler_params=_cparams(("arbitrary",)),
        name="mixout_sample",
    )(o_nsa, p_s, p_s, p_s, state, cw, gn, gc, w, x)


MEM_SCALE = MEM_HEAD_DIM ** -0.5


def _softmax(s):
    e = jnp.exp(s - jnp.max(s, axis=-1, keepdims=True))
    return e / jnp.sum(e, axis=-1, keepdims=True)


def _mem_prompt_kernel(x_ref, g_ref, wq_ref, kv_ref, wo_ref, o_ref):
    x = x_ref[...]
    qm = _dot(_rms(x, g_ref[...]).astype(BF16), wq_ref[...]).astype(BF16)
    kv = kv_ref[...].astype(BF16)
    outs = []
    for h in range(MEM_HEADS):
        c0, c1 = h * MEM_HEAD_DIM, (h + 1) * MEM_HEAD_DIM
        p = _softmax(_dot_nt(qm[:, c0:c1], kv[:, c0:c1]) * MEM_SCALE).astype(BF16)
        outs.append(_dot(p, kv[:, D_MEM + c0:D_MEM + c1]))
    om = jnp.concatenate(outs, axis=-1).astype(BF16)
    o_ref[...] = x + _dot(om, wo_ref[...])


def mem_prompt_attn(x, g, wq, mem_kv, wo, tm, seq):
    m = x.shape[0]
    mem_len = mem_kv.shape[1]
    tps = seq // tm
    const = lambda i: (0, 0)
    return pl.pallas_call(
        _mem_prompt_kernel,
        grid=(m // tm,),
        in_specs=[pl.BlockSpec((tm, D_MODEL), lambda i: (i, 0)),
                  pl.BlockSpec((1, D_MODEL), const),
                  pl.BlockSpec((D_MODEL, D_MEM), const),
                  pl.BlockSpec((None, mem_len, 2 * D_MEM), lambda i: (i // tps, 0, 0)),
                  pl.BlockSpec((D_MEM, D_MODEL), const)],
        out_specs=pl.BlockSpec((tm, D_MODEL), lambda i: (i, 0)),
        out_shape=jax.ShapeDtypeStruct((m, D_MODEL), F32),
        compiler_params=_cparams(("arbitrary",)),
        name="mem_prompt_attn",
    )(x, g, wq, mem_kv, wo)


def _mem_sample_kernel(q_ref, kv_ref, o_ref, *, bt, mem_len):
    n_rows = q_ref.shape[1]
    stride = 2 * MEM_HEADS
    rowh = lax.broadcasted_iota(jnp.int32, (n_rows, 1), 0) // (n_rows // MEM_HEADS)
    for bb in range(bt):
        kmat = jnp.concatenate([kv_ref[bb, pl.ds(h, mem_len, stride=stride), :]
                                for h in range(MEM_HEADS)], axis=-1).astype(BF16)
        vmat = jnp.concatenate([kv_ref[bb, pl.ds(MEM_HEADS + h, mem_len, stride=stride), :]
                                for h in range(MEM_HEADS)], axis=-1).astype(BF16)
        p = _softmax(_dot_nt(q_ref[bb].astype(BF16), kmat) * MEM_SCALE).astype(BF16)
        o2 = _dot(p, vmat)
        out = jnp.zeros((n_rows, MEM_HEAD_DIM), F32)
        for h in range(MEM_HEADS):
            out = out + jnp.where(rowh == h, o2[:, h * MEM_HEAD_DIM:(h + 1) * MEM_HEAD_DIM], 0.0)
        o_ref[bb] = out


def mem_sample_attn(qbd, cache_mem, layer, bt):
    nb, n_rows, _ = qbd.shape
    rows = cache_mem.shape[2]
    return pl.pallas_call(
        functools.partial(_mem_sample_kernel, bt=bt, mem_len=rows // (2 * MEM_HEADS)),
        grid=(nb // bt,),
        in_specs=[pl.BlockSpec((bt, n_rows, D_MEM), lambda i: (i, 0, 0)),
                  pl.BlockSpec((None, bt, rows, MEM_HEAD_DIM), lambda i: (layer, i, 0, 0))],
        out_specs=pl.BlockSpec((bt, n_rows, MEM_HEAD_DIM), lambda i: (i, 0, 0)),
        out_shape=jax.ShapeDtypeStruct((nb, n_rows, MEM_HEAD_DIM), F32),
        compiler_params=_cparams(("arbitrary",)),
        name="mem_sample_attn",
    )(qbd, cache_mem)


def _silu(a):
    return a * _sigmoid(a)


def _ffn_prompt_kernel(x_ref, xh_ref, g_ref, wg_ref, wu_ref, cw_ref, wd_ref, gf_ref,
                       o_ref, gt_ref, h_s, hh_s, acc_s, *, tiles_per_seq, final):
    j = pl.program_id(1)
    tm = x_ref.shape[0]

    @pl.when(j == 0)
    def _():
        first = (pl.program_id(0) % tiles_per_seq) == 0
        h_s[...] = _rms(x_ref[...], g_ref[...]).astype(BF16)
        hh = jnp.where(first, 0.0, _rms(xh_ref[...], g_ref[...]))
        hh_s[...] = jnp.concatenate([hh, jnp.zeros_like(hh)], axis=0).astype(BF16)
        acc_s[...] = jnp.zeros_like(acc_s)

    h = h_s[...]
    gate = _dot(h, wg_ref[...])
    gate_h = _dot(hh_s[...], wg_ref[...])
    g1, g2 = _shift_rows(gate, gate_h[7:8], gate_h[6:7])
    cw = cw_ref[...]
    a = cw[0:1] * g2 + cw[1:2] * g1 + cw[2:3] * gate
    z = (_silu(a) * _dot(h, wu_ref[...])).astype(BF16)
    acc_s[...] += _dot(z, wd_ref[...])
    gt_ref[...] = gate[tm - 8:, :]

    @pl.when(j == pl.num_programs(1) - 1)
    def _():
        y = x_ref[...] + acc_s[...]
        o_ref[...] = _rms(y, gf_ref[...]) if final else y


def ffn_prompt(x, g, wg, wu, cw, wd, gf, tm, tn, seq, final):
    m = x.shape[0]
    nt = m // tm
    nj = D_FF // tn
    const = lambda i, j: (0, 0)
    return pl.pallas_call(
        functools.partial(_ffn_prompt_kernel, tiles_per_seq=seq // tm, final=final),
        grid=(nt, nj),
        in_specs=[pl.BlockSpec((tm, D_MODEL), lambda i, j: (i, 0)),
                  pl.BlockSpec((8, D_MODEL), lambda i, j: (jnp.maximum(i * (tm // 8) - 1, 0), 0)),
                  pl.BlockSpec((1, D_MODEL), const),
                  pl.BlockSpec((D_MODEL, tn), lambda i, j: (0, j)),
                  pl.BlockSpec((D_MODEL, tn), lambda i, j: (0, j)),
                  pl.BlockSpec((8, tn), lambda i, j: (0, j)),
                  pl.BlockSpec((tn, D_MODEL), lambda i, j: (j, 0)),
                  pl.BlockSpec((1, D_MODEL), const)],
        out_specs=[pl.BlockSpec((tm, D_MODEL), lambda i, j: (i, 0), pipeline_mode=pl.Buffered(1)),
                   pl.BlockSpec((None, 8, tn), lambda i, j: (i, 0, j))],
        out_shape=[jax.ShapeDtypeStruct((m, D_MODEL), F32),
                   jax.ShapeDtypeStruct((nt, 8, D_FF), F32)],
        scratch_shapes=[pltpu.VMEM((tm, D_MODEL), BF16),
                        pltpu.VMEM((16, D_MODEL), BF16),
                        pltpu.VMEM((tm, D_MODEL), F32)],
        compiler_params=_cparams(("arbitrary", "arbitrary")),
        name="ffn_prompt",
    )(x, x, g, wg, wu, cw, wd, gf)


def _ffn_sample_kernel(x_ref, st_ref, g_ref, wg_ref, wu_ref, cw_ref, wd_ref, gf_ref,
                       o_ref, nst_ref, h_s, acc_s, *, dec_seq, final):
    j = pl.program_id(0)

    @pl.when(j == 0)
    def _():
        for t in range(dec_seq):
            h_s[t] = _rms(x_ref[t], g_ref[...]).astype(BF16)
        acc_s[...] = jnp.zeros_like(acc_s)

    gcat = [st_ref[0], st_ref[1]] + [_dot(h_s[t], wg_ref[...]) for t in range(dec_seq)]
    cw = cw_ref[...]
    for t in range(dec_seq):
        a = cw[0:1] * gcat[t] + cw[1:2] * gcat[t + 1] + cw[2:3] * gcat[t + 2]
        z = (_silu(a) * _dot(h_s[t], wu_ref[...])).astype(BF16)
        acc_s[t] += _dot(z, wd_ref[...])
    nst_ref[0] = gcat[dec_seq]
    nst_ref[1] = gcat[dec_seq + 1]

    @pl.when(j == pl.num_programs(0) - 1)
    def _():
        for t in range(dec_seq):
            y = x_ref[t] + acc_s[t]
            o_ref[t] = _rms(y, gf_ref[...]) if final else y


def ffn_sample(x, state, g, wg, wu, cw, wd, gf, tn, final):
    dec_seq, nb, _ = x.shape
    full3 = lambda j: (0, 0, 0)
    const = lambda j: (0, 0)
    return pl.pallas_call(
        functools.partial(_ffn_sample_kernel, dec_seq=dec_seq, final=final),
        grid=(D_FF // tn,),
        in_specs=[pl.BlockSpec((dec_seq, nb, D_MODEL), full3),
                  pl.BlockSpec((2, nb, tn), lambda j: (0, 0, j)),
                  pl.BlockSpec((1, D_MODEL), const),
                  pl.BlockSpec((D_MODEL, tn), lambda j: (0, j)),
                  pl.BlockSpec((D_MODEL, tn), lambda j: (0, j)),
                  pl.BlockSpec((8, tn), lambda j: (0, j)),
                  pl.BlockSpec((tn, D_MODEL), lambda j: (j, 0)),
                  pl.BlockSpec((1, D_MODEL), const)],
        out_specs=[pl.BlockSpec((dec_seq, nb, D_MODEL), full3),
                   pl.BlockSpec((2, nb, tn), lambda j: (0, 0, j))],
        out_shape=[jax.ShapeDtypeStruct((dec_seq, nb, D_MODEL), F32),
                   jax.ShapeDtypeStruct((2, nb, D_FF), F32)],
        scratch_shapes=[pltpu.VMEM((dec_seq, nb, D_MODEL), BF16),
                        pltpu.VMEM((dec_seq, nb, D_MODEL), F32)],
        compiler_params=_cparams(("arbitrary",)),
        name="ffn_sample",
    )(x, state, g, wg, wu, cw, wd, gf)


def _overlap(n_cmp, n_sel, rows, cols):
    c0 = np.arange(n_cmp)[:, None] * CMP_STRIDE
    s0 = np.arange(n_sel)[None, :] * SEL_BLOCK
    ov = np.minimum(c0 + CMP_BLOCK, s0 + SEL_BLOCK) - np.maximum(c0, s0)
    out = np.zeros((rows, cols), np.float32)
    out[:n_cmp, :n_sel] = np.clip(ov, 0, None).astype(np.float32) / CMP_BLOCK
    return out


def _block_expand(n_keys):
    return jnp.asarray(np.arange(128)[:, None] == (np.arange(n_keys)[None, :] // SEL_BLOCK), BF16)


def _chunk_perm():
    row = np.arange(PAGE)
    return jnp.asarray((row[:, None] % RPC) * CMP_STRIDE + row[:, None] // RPC == row[None, :], BF16)


def _pad_rows(a, rows):
    return jnp.concatenate([a, jnp.zeros((rows - a.shape[0],) + a.shape[1:], a.dtype)], axis=0)


def _prep_w_in(w_in_l):
    offs = np.cumsum((D_NSA, KV_W, KV_W, KV_W, N_BRANCH * N_HEADS, D_CONV, D_CONV, D_CONV))
    q, kc, ks, kw, gl, cb, cc, ch = jnp.split(w_in_l, [int(o) for o in offs[:-1]], axis=-1)
    glp = jnp.concatenate([gl, jnp.zeros((D_MODEL, 128 - N_BRANCH * N_HEADS), F32)], axis=-1)
    w_main = jnp.concatenate([q * (HEAD_DIM ** -0.5), cb, cc, ch, glp], axis=-1).astype(BF16)
    wt_kv = jnp.concatenate([kc, ks, kw], axis=-1).T.astype(BF16)
    return w_main, wt_kv


def _prep_cmp(w_cmp_l, pe_cmp_l):
    eye = jnp.eye(2 * N_KV, dtype=F32).reshape(2, N_KV, 2, N_KV)
    wbd = jnp.einsum('csde,cgkh->scgdkhe', w_cmp_l, eye).reshape(CMP_BLOCK, KV_W, KV_W).astype(BF16)
    pe = pe_cmp_l.reshape(2, 2, CMP_STRIDE, HEAD_DIM).transpose(1, 0, 3, 2)
    pet = jnp.broadcast_to(pe[:, :, None, :, None, :], (2, 2, N_KV, HEAD_DIM, RPC, CMP_STRIDE))
    return wbd, pet.reshape(2, KV_W, PAGE)


TM_PROJ = 512
TN_PROJ = 1408
TM_MIX = 256
TM_MEM = 256
TM_FFN = 1024
TN_FFN = 256
BT_CMP = 4
BT_NSA = 4
BT_MEM = 8


def _kv_rows(kvt, row, start=0):
    slab = kvt[:, row:row + KV_W, start:]
    b, _, t = slab.shape
    return slab.reshape(b, 2, N_KV, HEAD_DIM, t).transpose(0, 4, 1, 2, 3)


def _layer_prompt(x, mem_kv, lw, consts, batch, seq, final):
    p_main, kvt = proj_in(x, lw['g_mix'], lw['w_main'], lw['wt_kv'], TM_PROJ, TN_PROJ, seq)
    cmp = compress_prompt(kvt, lw['pet'], consts['perm'], lw['wbd'])
    n_cmp_pad = seq // CMP_STRIDE
    cmp_t = cmp.reshape(batch, n_cmp_pad, 2, N_KV, HEAD_DIM).transpose(0, 2, 3, 1, 4)
    o_nsa = nsa_prompt(p_main, kvt, cmp_t, consts['ovt_p'], consts['expand_p'])
    x1, u_tail = mixout_prompt(o_nsa, p_main, lw['conv_w'], lw['g_out_nsa'], lw['g_out_conv'],
                               lw['w_out'], x, TM_MIX, seq)
    x2 = mem_prompt_attn(x1, lw['g_mem'], lw['w_mem_q'], mem_kv, lw['w_mem_o'], TM_MEM, seq)
    x3, g_tail = ffn_prompt(x2, lw['g_ffn'], lw['w_ff_gate'], lw['w_ff_up'], lw['ffn_conv_w'],
                            lw['w_ff_down'], lw['g_final'], TM_FFN, TN_FFN, seq, final)
    conv_state = u_tail.reshape(batch, seq // TM_MIX, 8, D_CONV)[:, -1, 6:8]
    ffn_state = g_tail.reshape(batch, seq // TM_FFN, 8, D_FF)[:, -1, 6:8]
    return (x3, _kv_rows(kvt, ROW_KC), _kv_rows(kvt, ROW_KS),
            _kv_rows(kvt, ROW_KW, seq - min(WINDOW, seq)), conv_state, ffn_state)


def _layer_sample(x, lw, consts, layer, cache_cmp, cache_sel, page_idx, cache_win, cache_mem,
                  st_conv, st_ffn, final):
    dec_seq, nb, _ = x.shape
    rows = dec_seq * nb
    p_s, kvt = proj_in(x.reshape(rows, D_MODEL), lw['g_mix'], lw['w_main'], lw['wt_kv'],
                       rows, TN_PROJ, rows)
    p3 = p_s.reshape(dec_seq, nb, D_MAIN)
    kvn = kvt.reshape(D_KVT, dec_seq, nb).transpose(2, 0, 1)
    kvn_pad = jnp.concatenate([kvn, jnp.zeros((nb, D_KVT, PAGE - dec_seq), F32)], axis=-1)
    cmp_s = compress_paged(cache_cmp, page_idx, lw['pet'], consts['perm'], lw['wbd'], BT_CMP)
    q = p3[:, :, COL_Q:COL_Q + D_NSA].reshape(dec_seq, nb, N_KV, HPG, HEAD_DIM).transpose(1, 3, 2, 0, 4)
    gsel = jnp.eye(N_KV, dtype=F32)
    qbd = (q[:, :, :, :, None, :] * gsel[None, None, :, None, :, None]).reshape(
        nb, HPG * N_KV * dec_seq, KV_HALF)
    gates = p3[:, :, COL_G:COL_G + N_BRANCH * N_HEADS].reshape(
        dec_seq, nb, N_KV, HPG, N_BRANCH).transpose(1, 3, 2, 0, 4).reshape(nb, HPG * N_KV * dec_seq, N_BRANCH)
    o, win_new = nsa_sample(cache_sel, page_idx, qbd, gates, cmp_s, kvn_pad[:, ROW_KS:ROW_KS + KV_W],
                            cache_win, layer, kvn_pad[:, ROW_KW:ROW_KW + KV_W],
                            consts['ov_s'], consts['expand_s'], BT_NSA)
    o_nsa = o.reshape(nb, HPG, N_KV, dec_seq, HEAD_DIM).transpose(3, 0, 2, 1, 4).reshape(dec_seq, nb, D_NSA)
    x1, conv_state = mixout_sample(o_nsa, p3, st_conv, lw['conv_w'], lw['g_out_nsa'],
                                   lw['g_out_conv'], lw['w_out'], x)
    qm = rms_matmul(x1.reshape(rows, D_MODEL), lw['g_mem'], lw['w_mem_q'], rows, D_MEM)
    qh = qm.reshape(dec_seq, nb, MEM_HEADS, MEM_HEAD_DIM).transpose(1, 2, 0, 3)
    qmbd = (qh[:, :, :, None, :] * jnp.eye(MEM_HEADS, dtype=F32)[None, :, None, :, None]).reshape(
        nb, MEM_HEADS * dec_seq, D_MEM)
    om = mem_sample_attn(qmbd, cache_mem, layer, BT_MEM)
    om = om.reshape(nb, MEM_HEADS, dec_seq, MEM_HEAD_DIM).transpose(2, 0, 1, 3).reshape(rows, D_MEM)
    x2 = matmul_residual(x1.reshape(rows, D_MODEL), om, lw['w_mem_o'], rows).reshape(dec_seq, nb, D_MODEL)
    x3, ffn_state = ffn_sample(x2, st_ffn, lw['g_ffn'], lw['w_ff_gate'], lw['w_ff_up'],
                               lw['ffn_conv_w'], lw['w_ff_down'], lw['g_final'], TN_FFN, final)
    kv5 = kvn.reshape(nb, N_BRANCH, 2, N_KV, HEAD_DIM, dec_seq).transpose(1, 0, 5, 2, 3, 4)
    return (x3, kv5[0], kv5[1], win_new.transpose(0, 4, 1, 2, 3),
            conv_state.transpose(1, 0, 2), ffn_state.transpose(1, 0, 2))


def kernel(x_prompt, x_sample, cache_cmp_kv, cache_sel_kv, cache_win_kv, cache_mem_kv,
           state_conv, state_ffn_conv, page_table, mem_prompt,
           g_mix, w_in, w_cmp, pe_cmp, conv_w, g_out_nsa, g_out_conv, w_out,
           g_mem_src, w_mem_kv, g_mem, w_mem_q, w_mem_o, g_ffn, w_ff_gate, w_ff_up,
           ffn_conv_w, w_ff_down, g_final):
    batch, seq, _ = x_prompt.shape
    nb, dec_seq, _ = x_sample.shape
    depth = w_in.shape[0]
    n_phys = cache_cmp_kv.shape[1]
    npages = page_table.shape[1]
    past = npages * PAGE
    win_buf = cache_win_kv.shape[2]
    mem_len = mem_prompt.shape[1]

    n_cmp_s = (past + dec_seq - CMP_BLOCK) // CMP_STRIDE + 1
    n_sel_s = -(-(past + dec_seq) // SEL_BLOCK)
    n_cmp_p = (seq - CMP_BLOCK) // CMP_STRIDE + 1
    n_sel_p = seq // SEL_BLOCK
    consts = {
        'ov_s': jnp.asarray(_overlap(n_cmp_s, n_sel_s, past // CMP_STRIDE, 128), BF16),
        'expand_s': _block_expand(past + PAGE),
        'ovt_p': jnp.asarray(_overlap(n_cmp_p, n_sel_p, seq // CMP_STRIDE, n_sel_p).T, BF16),
        'expand_p': _block_expand(seq),
        'perm': _chunk_perm(),
    }

    cache_cmp = cache_cmp_kv.transpose(0, 1, 3, 4, 5, 2).reshape(depth * n_phys, 2, N_KV, HEAD_DIM, PAGE)
    cache_sel = cache_sel_kv.transpose(0, 1, 3, 4, 5, 2).reshape(depth * n_phys, 2, N_KV, HEAD_DIM, PAGE)
    cache_win = cache_win_kv.transpose(0, 1, 3, 4, 5, 2)
    cache_mem = cache_mem_kv.reshape(depth, nb, mem_len * 2 * MEM_HEADS, MEM_HEAD_DIM)

    xp = x_prompt.reshape(batch * seq, D_MODEL)
    xs = x_sample.transpose(1, 0, 2)
    mem_rows = mem_prompt.reshape(batch * mem_len, D_MODEL)
    outs = [[] for _ in range(11)]
    for l in range(depth):
        wbd, pet = _prep_cmp(w_cmp[l], pe_cmp[l])
        w_main, wt_kv = _prep_w_in(w_in[l])
        lw = {
            'g_mix': g_mix[l][None], 'w_main': w_main, 'wt_kv': wt_kv, 'wbd': wbd, 'pet': pet,
            'conv_w': _pad_rows(conv_w[l], 8), 'g_out_nsa': g_out_nsa[l][None],
            'g_out_conv': g_out_conv[l][None], 'w_out': w_out[l].astype(BF16),
            'g_mem': g_mem[l][None], 'w_mem_q': w_mem_q[l].astype(BF16),
            'w_mem_o': w_mem_o[l].astype(BF16), 'g_ffn': g_ffn[l][None],
            'w_ff_gate': w_ff_gate[l].astype(BF16), 'w_ff_up': w_ff_up[l].astype(BF16),
            'ffn_conv_w': _pad_rows(ffn_conv_w[l], 8), 'w_ff_down': w_ff_down[l].astype(BF16),
            'g_final': g_final[None],
        }
        final = l == depth - 1
        mem_kv_p = rms_matmul(mem_rows, g_mem_src[l][None], w_mem_kv[l].astype(BF16),
                              batch * mem_len, 2 * D_MEM).reshape(batch, mem_len, 2 * D_MEM)
        xp, kc, ks, kw, cst, fst = _layer_prompt(xp, mem_kv_p, lw, consts, batch, seq, final)
        page_idx = page_table + l * n_phys
        xs, kc2, ks2, kw2, cst2, fst2 = _layer_sample(
            xs, lw, consts, l, cache_cmp, cache_sel, page_idx, cache_win, cache_mem,
            state_conv[l].transpose(1, 0, 2), state_ffn_conv[l].transpose(1, 0, 2), final)
        for lst, val in zip(outs, (kc, ks, kw, mem_kv_p, cst, fst, kc2, ks2, kw2, cst2, fst2)):
            lst.append(val)

    st = [jnp.stack(o) for o in outs]
    return (xp.reshape(batch, seq, D_MODEL), xs.transpose(1, 0, 2),
            st[0], st[1], st[2],
            st[3].reshape(depth, batch, mem_len, 2, MEM_HEADS, MEM_HEAD_DIM),
            st[4], st[5], st[6], st[7], st[8], st[9], st[10])
```

```python
import functools

import numpy as np
import jax
import jax.numpy as jnp
from jax import lax
from jax.experimental import pallas as pl
from jax.experimental.pallas import tpu as pltpu

F32 = jnp.float32
BF16 = jnp.bfloat16

D_MODEL = 2048
D_NSA = 1024
D_CONV = 1024
HEAD_DIM = 64
N_HEADS = 16
N_KV = 2
HPG = 8
N_BRANCH = 3
KV_W = 2 * N_KV * HEAD_DIM
KV_HALF = N_KV * HEAD_DIM
CMP_BLOCK = 32
CMP_STRIDE = 16
SEL_BLOCK = 64
TOPK = 16
WINDOW = 512
Q_BLOCK = 128
FORCE_BONUS = 1e4
D_FF = 5632
MEM_HEADS = 4
MEM_HEAD_DIM = 128
D_MEM = 512
PAGE = 128
EPS = 1e-6
NEG = -1e30

COL_Q = 0
COL_CB = 1024
COL_CC = 2048
COL_CH = 3072
D_MAIN = 4096
ROW_KC = 0
ROW_KS = 256
ROW_KW = 512
ROW_G = 768
D_KVT = 896

VMEM_LIMIT = 56 * 1024 * 1024


def _cparams(sem):
    return pltpu.CompilerParams(dimension_semantics=sem, vmem_limit_bytes=VMEM_LIMIT)


def _rms(x, g):
    return x * lax.rsqrt(jnp.mean(x * x, axis=-1, keepdims=True) + EPS) * g


def _dot(a, b):
    return jnp.dot(a, b, preferred_element_type=F32)


def _dot_nt(a, b):
    return lax.dot_general(a, b, (((1,), (1,)), ((), ())), preferred_element_type=F32)


def _softmax_bias(s, bias):
    s = s + bias
    e = jnp.exp(s - jnp.max(s, axis=-1, keepdims=True))
    return e, jnp.sum(e, axis=-1, keepdims=True)


def _sigmoid(x):
    return 1.0 / (1.0 + jnp.exp(-x))


def _rms_matmul_kernel(x_ref, g_ref, w_ref, o_ref, h_ref):
    @pl.when(pl.program_id(1) == 0)
    def _():
        h_ref[...] = _rms(x_ref[...], g_ref[...]).astype(BF16)

    o_ref[...] = _dot(h_ref[...], w_ref[...]).astype(o_ref.dtype)


def _wspec(block, imap, layer):
    return pl.BlockSpec((None,) + block, lambda *a: (layer,) + imap(*a))


def rms_matmul(x, g, w, tm, tn, layer):
    m, d = x.shape
    n = w.shape[2]
    return pl.pallas_call(
        _rms_matmul_kernel,
        grid=(m // tm, n // tn),
        in_specs=[pl.BlockSpec((tm, d), lambda i, j: (i, 0)),
                  pl.BlockSpec((1, d), lambda i, j: (0, 0)),
                  _wspec((d, tn), lambda i, j: (0, j), layer)],
        out_specs=pl.BlockSpec((tm, tn), lambda i, j: (i, j)),
        out_shape=jax.ShapeDtypeStruct((m, n), F32),
        scratch_shapes=[pltpu.VMEM((tm, d), BF16)],
        compiler_params=_cparams(("arbitrary", "arbitrary")),
        name="rms_matmul",
    )(x, g, w)


def _proj_in_kernel(x_ref, g_ref, w_ref, wt_ref, o_ref, ot_ref, h_ref):
    @pl.when(pl.program_id(1) == 0)
    def _():
        h = _rms(x_ref[...], g_ref[...]).astype(BF16)
        h_ref[...] = h
        ot_ref[...] = _dot_nt(wt_ref[...], h)

    o_ref[...] = _dot(h_ref[...], w_ref[...])


def proj_in(x, g, w_main, wt_kv, tm, tn, seq):
    m, d = x.shape
    tps = seq // tm
    return pl.pallas_call(
        _proj_in_kernel,
        grid=(m // tm, D_MAIN // tn),
        in_specs=[pl.BlockSpec((tm, d), lambda i, j: (i, 0)),
                  pl.BlockSpec((1, d), lambda i, j: (0, 0)),
                  pl.BlockSpec((d, tn), lambda i, j: (0, j)),
                  pl.BlockSpec((D_KVT, d), lambda i, j: (0, 0))],
        out_specs=[pl.BlockSpec((tm, tn), lambda i, j: (i, j)),
                   pl.BlockSpec((None, D_KVT, tm), lambda i, j: (i // tps, 0, i % tps))],
        out_shape=[jax.ShapeDtypeStruct((m, D_MAIN), F32),
                   jax.ShapeDtypeStruct((m // seq, D_KVT, seq), F32)],
        scratch_shapes=[pltpu.VMEM((tm, d), BF16)],
        compiler_params=_cparams(("arbitrary", "arbitrary")),
        name="proj_in",
    )(x, g, w_main, wt_kv)


def _matmul_res_kernel(x_ref, a_ref, w_ref, o_ref):
    o_ref[...] = x_ref[...] + _dot(a_ref[...].astype(BF16), w_ref[...])


def matmul_residual(x, a, w, tm, layer):
    m, n = x.shape
    k = a.shape[1]
    return pl.pallas_call(
        _matmul_res_kernel,
        grid=(m // tm,),
        in_specs=[pl.BlockSpec((tm, n), lambda i: (i, 0)),
                  pl.BlockSpec((tm, k), lambda i: (i, 0)),
                  _wspec((k, n), lambda i: (0, 0), layer)],
        out_specs=pl.BlockSpec((tm, n), lambda i: (i, 0)),
        out_shape=jax.ShapeDtypeStruct((m, n), F32),
        compiler_params=_cparams(("arbitrary",)),
        name="matmul_residual",
    )(x, a, w)


RPC = PAGE // CMP_STRIDE


def _compress_chunks(chunk_at, n_chunks, pet_ref, perm_ref, w_ref, xs_ref):
    perm = perm_ref[...]
    for ci in range(n_chunks):
        chunk = chunk_at(ci)
        for a in range(2):
            xs_ref[a, ci] = _dot_nt(perm, (chunk + pet_ref[a]).astype(BF16))
    rows = n_chunks * RPC
    acc0 = jnp.zeros((rows, KV_W), F32)
    acc1 = jnp.zeros((rows, KV_W), F32)
    for r in range(CMP_STRIDE):
        x0 = xs_ref[0, :, r * RPC:(r + 1) * RPC, :].reshape(rows, KV_W).astype(BF16)
        x1 = xs_ref[1, :, r * RPC:(r + 1) * RPC, :].reshape(rows, KV_W).astype(BF16)
        acc0 = acc0 + _dot(x0, w_ref[r])
        acc1 = acc1 + _dot(x1, w_ref[CMP_STRIDE + r])
    return acc0 + pltpu.roll(acc1, rows - 1, 0)


def _compress_prompt_kernel(kvt_ref, pet_ref, perm_ref, w_ref, o_ref, xs_ref, *, n_chunks):
    out = _compress_chunks(lambda ci: kvt_ref[:, ci * PAGE:(ci + 1) * PAGE], n_chunks,
                           pet_ref, perm_ref, w_ref, xs_ref)
    o_ref[...] = out.astype(BF16)


def compress_prompt(kvt, pet, perm, wbd):
    batch, _, seq = kvt.shape
    n_chunks = seq // PAGE
    nblk = seq // CMP_STRIDE
    return pl.pallas_call(
        functools.partial(_compress_prompt_kernel, n_chunks=n_chunks),
        grid=(batch,),
        in_specs=[pl.BlockSpec((None, KV_W, seq), lambda b: (b, ROW_KC // KV_W, 0)),
                  pl.BlockSpec((2, KV_W, PAGE), lambda b: (0, 0, 0)),
                  pl.BlockSpec((PAGE, PAGE), lambda b: (0, 0)),
                  pl.BlockSpec((CMP_BLOCK, KV_W, KV_W), lambda b: (0, 0, 0))],
        out_specs=pl.BlockSpec((None, nblk, KV_W), lambda b: (b, 0, 0)),
        out_shape=jax.ShapeDtypeStruct((batch, nblk, KV_W), BF16),
        scratch_shapes=[pltpu.VMEM((2, n_chunks, PAGE, KV_W), F32)],
        compiler_params=_cparams(("arbitrary",)),
        name="compress_prompt",
    )(kvt, pet, perm, wbd)


def _compress_paged_kernel(pt_ref, *refs, bt, npages):
    page_refs = refs[:bt * npages]
    pet_ref, perm_ref, w_ref, o_ref, xs_ref = refs[bt * npages:]
    out = _compress_chunks(lambda ci: page_refs[ci][...].reshape(KV_W, PAGE), bt * npages,
                           pet_ref, perm_ref, w_ref, xs_ref)
    o_ref[...] = out.astype(BF16).reshape(bt, npages * RPC, KV_W)


def compress_paged(cache, page_idx, pet, perm, wbd, bt):
    nb, npages = page_idx.shape
    nblk = npages * RPC
    page_specs = [
        pl.BlockSpec((None, 2, N_KV, HEAD_DIM, PAGE), functools.partial(
            lambda i, pt, bb, p: (pt[i * bt + bb, p], 0, 0, 0, 0), bb=bb, p=p))
        for bb in range(bt) for p in range(npages)]
    grid_spec = pltpu.PrefetchScalarGridSpec(
        num_scalar_prefetch=1,
        grid=(nb // bt,),
        in_specs=page_specs + [pl.BlockSpec((2, KV_W, PAGE), lambda i, pt: (0, 0, 0)),
                               pl.BlockSpec((PAGE, PAGE), lambda i, pt: (0, 0)),
                               pl.BlockSpec((CMP_BLOCK, KV_W, KV_W), lambda i, pt: (0, 0, 0))],
        out_specs=pl.BlockSpec((bt, nblk, KV_W), lambda i, pt: (i, 0, 0)),
        scratch_shapes=[pltpu.VMEM((2, bt * npages, PAGE, KV_W), F32)])
    return pl.pallas_call(
        functools.partial(_compress_paged_kernel, bt=bt, npages=npages),
        grid_spec=grid_spec,
        out_shape=jax.ShapeDtypeStruct((nb, nblk, KV_W), BF16),
        compiler_params=_cparams(("arbitrary",)),
        name="compress_paged",
    )(page_idx, *([cache] * (bt * npages)), pet, perm, wbd)


def _importance(imp, idx, tpos, n_sel):
    valid = (idx * SEL_BLOCK <= tpos) & (idx < n_sel)
    cur = tpos // SEL_BLOCK
    forced = (idx == 0) | (idx == cur) | (idx == cur - 1)
    return jnp.where(valid, imp + jnp.where(forced, FORCE_BONUS, 0.0), NEG)


def _topk_rows(imp, n_cand):
    groups = [imp[v * 8:(v + 1) * 8] for v in range(n_cand // 8)]
    sub = lax.broadcasted_iota(jnp.int32, groups[0].shape, 0)
    ranks = [jnp.zeros(g.shape, F32) for g in groups]
    for jp in range(n_cand):
        c = imp[jp:jp + 1, :]
        for v, g in enumerate(groups):
            ge = jnp.where(c >= g, 1.0, 0.0)
            gt = jnp.where(c > g, 1.0, 0.0)
            if v * 8 > jp:
                beats = ge
            elif v * 8 + 7 < jp:
                beats = gt
            else:
                beats = jnp.where(sub > jp - v * 8, ge, gt)
            ranks[v] = ranks[v] + beats
    return jnp.concatenate([jnp.where(r < TOPK, 1.0, 0.0) for r in ranks], axis=0)


def _topk_lanes(imp, idx, n_cand):
    rank = jnp.zeros(imp.shape, F32)
    for jp in range(n_cand):
        c = imp[:, jp:jp + 1]
        rank = rank + jnp.where(idx > jp, jnp.where(c >= imp, 1.0, 0.0), jnp.where(c > imp, 1.0, 0.0))
    return jnp.where(rank < TOPK, 1.0, 0.0)


KEY_TILE = 512
V_PAD = 16


def _tile_heads(x):
    return jnp.concatenate([x] * HPG, axis=0)


def _nsa_prompt_kernel(q_ref, gate_ref, cmp_ref, kst_ref, vst_ref, kwt_ref, vwt_ref, ovt_ref, exp_ref,
                       o_ref, ksb, vsb, kwb, vwb, *, seq):
    g = pl.program_id(1)
    i = pl.program_id(2)
    n_cmp_pad = seq // CMP_STRIDE
    n_sel = seq // SEL_BLOCK

    @pl.when(i == 0)
    def _():
        ones_row = jnp.where(lax.broadcasted_iota(jnp.int32, (V_PAD, seq), 0) == 0, 1.0, 0.0).astype(BF16)
        ksb[...] = kst_ref[...].astype(BF16)
        vsb[0:HEAD_DIM, :] = vst_ref[...].astype(BF16)
        vsb[HEAD_DIM:HEAD_DIM + V_PAD, :] = ones_row
        kwb[...] = kwt_ref[...].astype(BF16)
        vwb[0:HEAD_DIM, :] = vwt_ref[...].astype(BF16)
        vwb[HEAD_DIM:HEAD_DIM + V_PAD, :] = ones_row

    t0 = i * Q_BLOCK
    qb = q_ref[...]
    qs = jnp.concatenate([qb[:, h * HEAD_DIM:(h + 1) * HEAD_DIM] for h in range(HPG)],
                         axis=0).astype(BF16)
    tcol = lax.broadcasted_iota(jnp.int32, (Q_BLOCK, 1), 0) + t0

    ncol = lax.broadcasted_iota(jnp.int32, (Q_BLOCK, n_cmp_pad), 1)
    vis = jnp.where(ncol < n_cmp_pad - 1, ncol * CMP_STRIDE + (CMP_BLOCK - 1), seq) <= tcol
    e_c, l_c = _softmax_bias(_dot_nt(qs, cmp_ref[0]), _tile_heads(jnp.where(vis, 0.0, NEG)))
    any_vis = _tile_heads(jnp.where(tcol >= CMP_BLOCK - 1, 1.0, 0.0))
    p_c = (e_c * (any_vis / jnp.maximum(l_c, 1e-30))).astype(BF16)
    o_c = _dot(p_c, cmp_ref[1])

    po = _dot_nt(ovt_ref[...], p_c)
    imp_t = po[:, 0:Q_BLOCK]
    for h in range(1, HPG):
        imp_t = imp_t + po[:, h * Q_BLOCK:(h + 1) * Q_BLOCK]
    jj = lax.broadcasted_iota(jnp.int32, (n_sel, Q_BLOCK), 0)
    tt = lax.broadcasted_iota(jnp.int32, (n_sel, Q_BLOCK), 1) + t0
    sel_t = _topk_rows(_importance(imp_t, jj, tt, n_sel), n_sel)
    sel = jnp.concatenate([sel_t, jnp.zeros((128 - n_sel, Q_BLOCK), F32)], axis=0).T.astype(BF16)

    kk = lax.broadcasted_iota(jnp.int32, (Q_BLOCK, KEY_TILE), 1)

    def scores(c):
        k0 = pl.multiple_of(c * KEY_TILE, KEY_TILE)
        msel = _dot(sel, exp_ref[:, pl.ds(k0, KEY_TILE)])
        bias = jnp.where(msel > 0.5, jnp.where(kk + k0 <= tcol, 0.0, NEG), NEG)
        return _dot(qs, ksb[:, pl.ds(k0, KEY_TILE)]) + _tile_heads(bias)

    def accumulate(c, s, carry):
        m, acc = carry
        k0 = pl.multiple_of(c * KEY_TILE, KEY_TILE)
        m_new = jnp.maximum(m, jnp.max(s, axis=-1, keepdims=True))
        p = jnp.exp(s - m_new).astype(BF16)
        return m_new, jnp.exp(m - m_new) * acc + _dot_nt(p, vsb[:, pl.ds(k0, KEY_TILE)])

    def tile_pair(c2, carry):
        s_a = scores(2 * c2)
        s_b = scores(2 * c2 + 1)
        return accumulate(2 * c2 + 1, s_b, accumulate(2 * c2, s_a, carry))

    n_rows = HPG * Q_BLOCK
    m0 = jnp.full((n_rows, 1), NEG, F32)
    a0 = jnp.zeros((n_rows, HEAD_DIM + V_PAD), F32)
    n_tiles = (t0 + Q_BLOCK + KEY_TILE - 1) // KEY_TILE
    carry = lax.fori_loop(0, n_tiles // 2, tile_pair, (m0, a0))
    _, acc_s = lax.cond(n_tiles % 2 == 1,
                        lambda cr: accumulate(n_tiles - 1, scores(n_tiles - 1), cr),
                        lambda cr: cr, carry)
    o_s = acc_s[:, 0:HEAD_DIM] / acc_s[:, HEAD_DIM:HEAD_DIM + 1]

    span = WINDOW + Q_BLOCK
    ws = pl.multiple_of(jnp.maximum(t0 - WINDOW, 0), Q_BLOCK)
    dlt = tcol - (lax.broadcasted_iota(jnp.int32, (Q_BLOCK, span), 1) + ws)
    bias_w = jnp.where(dlt >= 0, jnp.where(dlt < WINDOW, 0.0, NEG), NEG)
    s_w = _dot(qs, kwb[:, pl.ds(ws, span)]) + _tile_heads(bias_w)
    e_w = jnp.exp(s_w - jnp.max(s_w, axis=-1, keepdims=True)).astype(BF16)
    acc_w = _dot_nt(e_w, vwb[:, pl.ds(ws, span)])
    o_w = acc_w[:, 0:HEAD_DIM] / acc_w[:, HEAD_DIM:HEAD_DIM + 1]

    gs = _sigmoid(gate_ref[...]).T
    gsel = jnp.where(g == 0, gs[:, 0:HPG * N_BRANCH], gs[:, HPG * N_BRANCH:2 * HPG * N_BRANCH])
    for h in range(HPG):
        r0, r1 = h * Q_BLOCK, (h + 1) * Q_BLOCK
        c0 = h * N_BRANCH
        o_ref[:, h * HEAD_DIM:(h + 1) * HEAD_DIM] = (
            o_c[r0:r1] * gsel[:, c0:c0 + 1] + o_s[r0:r1] * gsel[:, c0 + 1:c0 + 2]
            + o_w[r0:r1] * gsel[:, c0 + 2:c0 + 3])


def nsa_prompt(p_main, kvt, cmp_t, ovt, expand):
    batch, _, seq = kvt.shape
    nqb = seq // Q_BLOCK
    gw = HPG * HEAD_DIM
    n_cmp_pad = seq // CMP_STRIDE
    n_sel = seq // SEL_BLOCK
    kvt_spec = lambda row: pl.BlockSpec((None, HEAD_DIM, seq),
                                        lambda b, g, i: (b, row // HEAD_DIM + g, 0))
    return pl.pallas_call(
        functools.partial(_nsa_prompt_kernel, seq=seq),
        grid=(batch, N_KV, nqb),
        in_specs=[
            pl.BlockSpec((Q_BLOCK, gw), lambda b, g, i: (b * nqb + i, g)),
            pl.BlockSpec((None, 128, Q_BLOCK), lambda b, g, i: (b, ROW_G // 128, i)),
            pl.BlockSpec((None, 2, None, n_cmp_pad, HEAD_DIM), lambda b, g, i: (b, 0, g, 0, 0)),
            kvt_spec(ROW_KS), kvt_spec(ROW_KS + KV_HALF),
            kvt_spec(ROW_KW), kvt_spec(ROW_KW + KV_HALF),
            pl.BlockSpec((n_sel, n_cmp_pad), lambda b, g, i: (0, 0)),
            pl.BlockSpec((128, seq), lambda b, g, i: (0, 0)),
        ],
        out_specs=pl.BlockSpec((Q_BLOCK, gw), lambda b, g, i: (b * nqb + i, g)),
        out_shape=jax.ShapeDtypeStruct((batch * seq, D_NSA), F32),
        scratch_shapes=[pltpu.VMEM((HEAD_DIM, seq), BF16), pltpu.VMEM((HEAD_DIM + V_PAD, seq), BF16)] * 2,
        compiler_params=_cparams(("arbitrary", "arbitrary", "arbitrary")),
        name="nsa_prompt",
    )(p_main, kvt, cmp_t, kvt, kvt, kvt, kvt, ovt, expand)


def _pick_group(o2, rg):
    return jnp.where(rg == 0, o2[:, 0:HEAD_DIM], o2[:, HEAD_DIM:2 * HEAD_DIM])


def _masked_softmax(s, mask):
    s = jnp.where(mask, s, NEG)
    m = jnp.max(s, axis=-1, keepdims=True)
    e = jnp.where(mask, jnp.exp(s - m), 0.0)
    return e / jnp.maximum(jnp.sum(e, axis=-1, keepdims=True), 1e-30)


def _nsa_sample_kernel(pt_ref, *refs, bt, npages, dec_seq):
    page_refs = refs[:bt * npages]
    (q_ref, gate_ref, cmp_ref, ksn_ref, win_ref, kwn_ref, ov_ref, exp_ref,
     o_ref, wout_ref, kts, vts, ktw, vtw) = refs[bt * npages:]
    past = npages * PAGE
    n_cmp = (past + dec_seq - CMP_BLOCK) // CMP_STRIDE + 1
    n_sel = -(-(past + dec_seq) // SEL_BLOCK)
    n_rows = HPG * N_KV * dec_seq
    n_keys = past + PAGE
    win_buf = win_ref.shape[-1]
    w_keys = win_buf + PAGE
    n_cmp_pad = cmp_ref.shape[1]
    gt = N_KV * dec_seq

    rowi = lax.broadcasted_iota(jnp.int32, (n_rows, 1), 0)
    rg = (rowi // dec_seq) % N_KV
    tpos = past + rowi % dec_seq
    ncol = lax.broadcasted_iota(jnp.int32, (n_rows, n_cmp_pad), 1)
    m_c = (ncol * CMP_STRIDE + (CMP_BLOCK - 1) <= tpos) & (ncol < n_cmp)
    jj = lax.broadcasted_iota(jnp.int32, (gt, 128), 1)
    t8 = past + lax.broadcasted_iota(jnp.int32, (gt, 128), 0) % dec_seq
    kcol = lax.broadcasted_iota(jnp.int32, (n_rows, n_keys), 1)
    wcol = lax.broadcasted_iota(jnp.int32, (n_rows, w_keys), 1)
    dlt = tpos - (wcol + (past - win_buf))
    m_w = (dlt >= 0) & (dlt < WINDOW)
    lane = lax.broadcasted_iota(jnp.int32, (KV_W, PAGE), 1)

    bbs = range(bt)

    for bb in bbs:
        for p in range(npages):
            page = page_refs[bb * npages + p]
            kts[bb, :, p * PAGE:(p + 1) * PAGE] = page[0].reshape(KV_HALF, PAGE).astype(BF16)
            vts[bb, :, p * PAGE:(p + 1) * PAGE] = page[1].reshape(KV_HALF, PAGE).astype(BF16)
        kts[bb, :, past:n_keys] = ksn_ref[bb, 0:KV_HALF, :].astype(BF16)
        vts[bb, :, past:n_keys] = ksn_ref[bb, KV_HALF:KV_W, :].astype(BF16)
        win = win_ref[bb].reshape(KV_W, win_buf)
        kwn = kwn_ref[bb]
        ktw[bb, :, 0:win_buf] = win[0:KV_HALF].astype(BF16)
        vtw[bb, :, 0:win_buf] = win[KV_HALF:KV_W].astype(BF16)
        ktw[bb, :, win_buf:w_keys] = kwn[0:KV_HALF, :].astype(BF16)
        vtw[bb, :, win_buf:w_keys] = kwn[KV_HALF:KV_W, :].astype(BF16)
        shifted = pltpu.roll(win, win_buf - dec_seq, 1)
        tail = jnp.where(lane >= PAGE - dec_seq, pltpu.roll(kwn, PAGE - dec_seq, 1),
                         shifted[:, win_buf - PAGE:])
        wout_ref[bb] = jnp.concatenate([shifted[:, 0:win_buf - PAGE], tail], axis=1).reshape(
            2, N_KV, HEAD_DIM, win_buf)

    qs = [q_ref[bb].astype(BF16) for bb in bbs]

    s_c = [_dot_nt(qs[bb], cmp_ref[bb, :, 0:KV_HALF]) for bb in bbs]
    s_w = [_dot(qs[bb], ktw[bb]) for bb in bbs]
    p_c = [_masked_softmax(s, m_c).astype(BF16) for s in s_c]
    p_w = [_masked_softmax(s, m_w).astype(BF16) for s in s_w]
    o_c = [_pick_group(_dot(p_c[bb], cmp_ref[bb, :, KV_HALF:KV_W]), rg) for bb in bbs]
    o_w = [_pick_group(_dot_nt(p_w[bb], vtw[bb]), rg) for bb in bbs]

    po = [_dot(p, ov_ref[...]) for p in p_c]
    imp = [functools.reduce(lambda a, h: a + x[h * gt:(h + 1) * gt], range(1, HPG), x[0:gt]) for x in po]
    sel8 = [_topk_lanes(_importance(x, jj, t8, n_sel), jj, n_sel) for x in imp]
    msel = [_dot(jnp.concatenate([x] * HPG, axis=0).astype(BF16), exp_ref[...]) for x in sel8]

    causal = kcol <= tpos
    s_s = [_dot(qs[bb], kts[bb]) for bb in bbs]
    p_s = [_masked_softmax(s_s[bb], (msel[bb] > 0.5) & causal).astype(BF16) for bb in bbs]
    o_s = [_pick_group(_dot_nt(p_s[bb], vts[bb]), rg) for bb in bbs]

    for bb in bbs:
        gs = _sigmoid(gate_ref[bb])
        o_ref[bb] = o_c[bb] * gs[:, 0:1] + o_s[bb] * gs[:, 1:2] + o_w[bb] * gs[:, 2:3]


def nsa_sample(cache_sel, page_idx, qbd, gates, cmp_s, ks_new, cache_win, layer, kw_new, ov_s, expand, bt):
    nb, npages = page_idx.shape
    n_rows = qbd.shape[1]
    dec_seq = n_rows // (HPG * N_KV)
    win_buf = cache_win.shape[-1]
    n_cmp_pad = cmp_s.shape[1]
    n_keys = npages * PAGE + PAGE
    page_specs = [
        pl.BlockSpec((None, 2, N_KV, HEAD_DIM, PAGE), functools.partial(
            lambda i, pt, bb, p: (pt[i * bt + bb, p], 0, 0, 0, 0), bb=bb, p=p))
        for bb in range(bt) for p in range(npages)]
    per_b = lambda i, pt: (i, 0, 0)
    const2 = lambda i, pt: (0, 0)
    win_shape = (2, N_KV, HEAD_DIM, win_buf)
    grid_spec = pltpu.PrefetchScalarGridSpec(
        num_scalar_prefetch=1,
        grid=(nb // bt,),
        in_specs=page_specs + [
            pl.BlockSpec((bt, n_rows, KV_HALF), per_b),
            pl.BlockSpec((bt, n_rows, N_BRANCH), per_b),
            pl.BlockSpec((bt, n_cmp_pad, KV_W), per_b),
            pl.BlockSpec((bt, KV_W, PAGE), per_b),
            pl.BlockSpec((None, bt) + win_shape, lambda i, pt: (layer, i, 0, 0, 0, 0)),
            pl.BlockSpec((bt, KV_W, PAGE), per_b),
            pl.BlockSpec((n_cmp_pad, 128), const2),
            pl.BlockSpec((128, n_keys), const2),
        ],
        out_specs=[pl.BlockSpec((bt, n_rows, HEAD_DIM), per_b),
                   pl.BlockSpec((bt,) + win_shape, lambda i, pt: (i, 0, 0, 0, 0))],
        scratch_shapes=[pltpu.VMEM((bt, KV_HALF, n_keys), BF16)] * 2
        + [pltpu.VMEM((bt, KV_HALF, win_buf + PAGE), BF16)] * 2)
    return pl.pallas_call(
        functools.partial(_nsa_sample_kernel, bt=bt, npages=npages, dec_seq=dec_seq),
        grid_spec=grid_spec,
        out_shape=[jax.ShapeDtypeStruct((nb, n_rows, HEAD_DIM), F32),
                   jax.ShapeDtypeStruct((nb,) + win_shape, F32)],
        compiler_params=_cparams(("arbitrary",)),
        name="nsa_sample",
    )(page_idx, *([cache_sel] * (bt * npages)), qbd, gates, cmp_s, ks_new, cache_win, kw_new, ov_s, expand)


def _shift_rows(u, p1, p2):
    r = lax.broadcasted_iota(jnp.int32, (u.shape[0], 1), 0)
    u1 = jnp.where(r >= 1, pltpu.roll(u, 1, 0), p1)
    u2 = jnp.where(r >= 2, pltpu.roll(u, 2, 0), jnp.where(r == 1, p1, p2))
    return u1, u2


def _mix_project(o_nsa, o_conv, gn, gc, w, x):
    mix = jnp.concatenate([_rms(o_nsa, gn), _rms(o_conv, gc)], axis=-1).astype(BF16)
    return x + _dot(mix, w)


def _mixout_prompt_kernel(on_ref, cb_ref, cc_ref, ch_ref, cch_ref, chh_ref, cw_ref, gn_ref, gc_ref,
                          w_ref, x_ref, o_ref, ut_ref, *, tiles_per_seq):
    first = (pl.program_id(0) % tiles_per_seq) == 0
    u = cc_ref[...] * ch_ref[...]
    uh = jnp.where(first, 0.0, cch_ref[...] * chh_ref[...])
    u1, u2 = _shift_rows(u, uh[7:8], uh[6:7])
    cw = cw_ref[...]
    v = cw[0:1] * u2 + cw[1:2] * u1 + cw[2:3] * u
    o_ref[...] = _mix_project(on_ref[...], cb_ref[...] * v, gn_ref[...], gc_ref[...],
                              w_ref[...], x_ref[...])
    ut_ref[...] = u[u.shape[0] - 8:, :]


def mixout_prompt(o_nsa, p_main, cw, gn, gc, w, x, tm, seq, layer):
    m = x.shape[0]
    nt = m // tm
    cblk = lambda c: (lambda i: (i, c // D_CONV))
    halo = lambda c: (lambda i: (jnp.maximum(i * (tm // 8) - 1, 0), c // D_CONV))
    const = lambda i: (0, 0)
    return pl.pallas_call(
        functools.partial(_mixout_prompt_kernel, tiles_per_seq=seq // tm),
        grid=(nt,),
        in_specs=[pl.BlockSpec((tm, D_NSA), lambda i: (i, 0)),
                  pl.BlockSpec((tm, D_CONV), cblk(COL_CB)),
                  pl.BlockSpec((tm, D_CONV), cblk(COL_CC)),
                  pl.BlockSpec((tm, D_CONV), cblk(COL_CH)),
                  pl.BlockSpec((8, D_CONV), halo(COL_CC)),
                  pl.BlockSpec((8, D_CONV), halo(COL_CH)),
                  pl.BlockSpec((8, D_CONV), const),
                  pl.BlockSpec((1, D_NSA), const),
                  pl.BlockSpec((1, D_CONV), const),
                  _wspec((D_MODEL, D_MODEL), const, layer),
                  pl.BlockSpec((tm, D_MODEL), lambda i: (i, 0))],
        out_specs=[pl.BlockSpec((tm, D_MODEL), lambda i: (i, 0)),
                   pl.BlockSpec((None, 8, D_CONV), lambda i: (i, 0, 0))],
        out_shape=[jax.ShapeDtypeStruct((m, D_MODEL), F32),
                   jax.ShapeDtypeStruct((nt, 8, D_CONV), F32)],
        compiler_params=_cparams(("arbitrary",)),
        name="mixout_prompt",
    )(o_nsa, p_main, p_main, p_main, p_main, p_main, cw, gn, gc, w, x)


def _mixout_sample_kernel(on_ref, cb_ref, cc_ref, ch_ref, st_ref, cw_ref, gn_ref, gc_ref,
                          w_ref, x_ref, o_ref, nst_ref, *, dec_seq):
    ucat = [st_ref[0], st_ref[1]] + [cc_ref[t] * ch_ref[t] for t in range(dec_seq)]
    cw = cw_ref[...]
    for t in range(dec_seq):
        v = cw[0:1] * ucat[t] + cw[1:2] * ucat[t + 1] + cw[2:3] * ucat[t + 2]
        o_ref[t] = _mix_project(on_ref[t], cb_ref[t] * v, gn_ref[...], gc_ref[...],
                                w_ref[...], x_ref[t])
    nst_ref[0] = ucat[dec_seq]
    nst_ref[1] = ucat[dec_seq + 1]


def mixout_sample(o_nsa, p_s, state, cw, gn, gc, w, x, layer):
    dec_seq, nb, _ = x.shape
    cblk = lambda c: (lambda i: (0, 0, c // D_CONV))
    full3 = lambda i: (0, 0, 0)
    const = lambda i: (0, 0)
    return pl.pallas_call(
        functools.partial(_mixout_sample_kernel, dec_seq=dec_seq),
        grid=(1,),
        in_specs=[pl.BlockSpec((dec_seq, nb, D_NSA), full3),
                  pl.BlockSpec((dec_seq, nb, D_CONV), cblk(COL_CB)),
                  pl.BlockSpec((dec_seq, nb, D_CONV), cblk(COL_CC)),
                  pl.BlockSpec((dec_seq, nb, D_CONV), cblk(COL_CH)),
                  pl.BlockSpec((2, nb, D_CONV), full3),
                  pl.BlockSpec((8, D_CONV), const),
                  pl.BlockSpec((1, D_NSA), const),
                  pl.BlockSpec((1, D_CONV), const),
                  _wspec((D_MODEL, D_MODEL), const, layer),
                  pl.BlockSpec((dec_seq, nb, D_MODEL), full3)],
        out_specs=[pl.BlockSpec((dec_seq, nb, D_MODEL), full3),
                   pl.BlockSpec((2, nb, D_CONV), full3)],
        out_shape=[jax.ShapeDtypeStruct((dec_seq, nb, D_MODEL), F32),
                   jax.ShapeDtypeStruct((2, nb, D_CONV), F32)],
        compiler_params=_cparams(("arbitrary",)),
        name="mixout_sample",
    )(o_nsa, p_s, p_s, p_s, state, cw, gn, gc, w, x)


MEM_SCALE = MEM_HEAD_DIM ** -0.5


def _softmax(s):
    e = jnp.exp(s - jnp.max(s, axis=-1, keepdims=True))
    return e / jnp.sum(e, axis=-1, keepdims=True)


def _mem_prompt_kernel(x_ref, g_ref, wq_ref, kv_ref, wo_ref, o_ref):
    x = x_ref[...]
    qm = _dot(_rms(x, g_ref[...]).astype(BF16), wq_ref[...]).astype(BF16)
    kv = kv_ref[...].astype(BF16)
    outs = []
    for h in range(MEM_HEADS):
        c0, c1 = h * MEM_HEAD_DIM, (h + 1) * MEM_HEAD_DIM
        p = _softmax(_dot_nt(qm[:, c0:c1], kv[:, c0:c1]) * MEM_SCALE).astype(BF16)
        outs.append(_dot(p, kv[:, D_MEM + c0:D_MEM + c1]))
    om = jnp.concatenate(outs, axis=-1).astype(BF16)
    o_ref[...] = x + _dot(om, wo_ref[...])


def mem_prompt_attn(x, g, wq, mem_kv, wo, tm, seq, layer):
    m = x.shape[0]
    mem_len = mem_kv.shape[1]
    tps = seq // tm
    const = lambda i: (0, 0)
    return pl.pallas_call(
        _mem_prompt_kernel,
        grid=(m // tm,),
        in_specs=[pl.BlockSpec((tm, D_MODEL), lambda i: (i, 0)),
                  pl.BlockSpec((1, D_MODEL), const),
                  _wspec((D_MODEL, D_MEM), const, layer),
                  pl.BlockSpec((None, mem_len, 2 * D_MEM), lambda i: (i // tps, 0, 0)),
                  _wspec((D_MEM, D_MODEL), const, layer)],
        out_specs=pl.BlockSpec((tm, D_MODEL), lambda i: (i, 0)),
        out_shape=jax.ShapeDtypeStruct((m, D_MODEL), F32),
        compiler_params=_cparams(("arbitrary",)),
        name="mem_prompt_attn",
    )(x, g, wq, mem_kv, wo)


def _mem_sample_kernel(q_ref, kv_ref, o_ref, *, bt, mem_len):
    n_rows = q_ref.shape[1]
    stride = 2 * MEM_HEADS
    rowh = lax.broadcasted_iota(jnp.int32, (n_rows, 1), 0) // (n_rows // MEM_HEADS)
    for bb in range(bt):
        kmat = jnp.concatenate([kv_ref[bb, pl.ds(h, mem_len, stride=stride), :]
                                for h in range(MEM_HEADS)], axis=-1).astype(BF16)
        vmat = jnp.concatenate([kv_ref[bb, pl.ds(MEM_HEADS + h, mem_len, stride=stride), :]
                                for h in range(MEM_HEADS)], axis=-1).astype(BF16)
        p = _softmax(_dot_nt(q_ref[bb].astype(BF16), kmat) * MEM_SCALE).astype(BF16)
        o2 = _dot(p, vmat)
        out = jnp.zeros((n_rows, MEM_HEAD_DIM), F32)
        for h in range(MEM_HEADS):
            out = out + jnp.where(rowh == h, o2[:, h * MEM_HEAD_DIM:(h + 1) * MEM_HEAD_DIM], 0.0)
        o_ref[bb] = out


def mem_sample_attn(qbd, cache_mem, layer, bt):
    nb, n_rows, _ = qbd.shape
    rows = cache_mem.shape[2]
    return pl.pallas_call(
        functools.partial(_mem_sample_kernel, bt=bt, mem_len=rows // (2 * MEM_HEADS)),
        grid=(nb // bt,),
        in_specs=[pl.BlockSpec((bt, n_rows, D_MEM), lambda i: (i, 0, 0)),
                  pl.BlockSpec((None, bt, rows, MEM_HEAD_DIM), lambda i: (layer, i, 0, 0))],
        out_specs=pl.BlockSpec((bt, n_rows, MEM_HEAD_DIM), lambda i: (i, 0, 0)),
        out_shape=jax.ShapeDtypeStruct((nb, n_rows, MEM_HEAD_DIM), F32),
        compiler_params=_cparams(("arbitrary",)),
        name="mem_sample_attn",
    )(qbd, cache_mem)


def _silu(a):
    return a * _sigmoid(a)


def _ffn_prompt_kernel(x_ref, xh_ref, g_ref, wg_ref, wu_ref, cw_ref, wd_ref, gf_ref,
                       o_ref, gt_ref, h_s, hh_s, acc_s, *, tiles_per_seq, final):
    j = pl.program_id(1)
    tm = x_ref.shape[0]

    @pl.when(j == 0)
    def _():
        first = (pl.program_id(0) % tiles_per_seq) == 0
        h_s[...] = _rms(x_ref[...], g_ref[...]).astype(BF16)
        hh = jnp.where(first, 0.0, _rms(xh_ref[...], g_ref[...]))
        hh_s[...] = jnp.concatenate([hh, jnp.zeros_like(hh)], axis=0).astype(BF16)
        acc_s[...] = jnp.zeros_like(acc_s)

    h = h_s[...]
    gate = _dot(h, wg_ref[...])
    gate_h = _dot(hh_s[...], wg_ref[...])
    g1, g2 = _shift_rows(gate, gate_h[7:8], gate_h[6:7])
    cw = cw_ref[...]
    a = cw[0:1] * g2 + cw[1:2] * g1 + cw[2:3] * gate
    z = (_silu(a) * _dot(h, wu_ref[...])).astype(BF16)
    acc_s[...] += _dot(z, wd_ref[...])
    gt_ref[...] = gate[tm - 8:, :]

    @pl.when(j == pl.num_programs(1) - 1)
    def _():
        y = x_ref[...] + acc_s[...]
        o_ref[...] = _rms(y, gf_ref[...]) if final else y


def ffn_prompt(x, g, wg, wu, cw, wd, gf, tm, tn, seq, final, layer):
    m = x.shape[0]
    nt = m // tm
    nj = D_FF // tn
    const = lambda i, j: (0, 0)
    return pl.pallas_call(
        functools.partial(_ffn_prompt_kernel, tiles_per_seq=seq // tm, final=final),
        grid=(nt, nj),
        in_specs=[pl.BlockSpec((tm, D_MODEL), lambda i, j: (i, 0)),
                  pl.BlockSpec((8, D_MODEL), lambda i, j: (jnp.maximum(i * (tm // 8) - 1, 0), 0)),
                  pl.BlockSpec((1, D_MODEL), const),
                  _wspec((D_MODEL, tn), lambda i, j: (0, j), layer),
                  _wspec((D_MODEL, tn), lambda i, j: (0, j), layer),
                  pl.BlockSpec((8, tn), lambda i, j: (0, j)),
                  _wspec((tn, D_MODEL), lambda i, j: (j, 0), layer),
                  pl.BlockSpec((1, D_MODEL), const)],
        out_specs=[pl.BlockSpec((tm, D_MODEL), lambda i, j: (i, 0)),
                   pl.BlockSpec((None, 8, tn), lambda i, j: (i, 0, j))],
        out_shape=[jax.ShapeDtypeStruct((m, D_MODEL), F32),
                   jax.ShapeDtypeStruct((nt, 8, D_FF), F32)],
        scratch_shapes=[pltpu.VMEM((tm, D_MODEL), BF16),
                        pltpu.VMEM((16, D_MODEL), BF16),
                        pltpu.VMEM((tm, D_MODEL), F32)],
        compiler_params=_cparams(("arbitrary", "arbitrary")),
        name="ffn_prompt",
    )(x, x, g, wg, wu, cw, wd, gf)


def _ffn_sample_kernel(x_ref, st_ref, g_ref, wg_ref, wu_ref, cw_ref, wd_ref, gf_ref,
                       o_ref, nst_ref, h_s, acc_s, *, dec_seq, final):
    j = pl.program_id(0)

    @pl.when(j == 0)
    def _():
        for t in range(dec_seq):
            h_s[t] = _rms(x_ref[t], g_ref[...]).astype(BF16)
        acc_s[...] = jnp.zeros_like(acc_s)

    gcat = [st_ref[0], st_ref[1]] + [_dot(h_s[t], wg_ref[...]) for t in range(dec_seq)]
    cw = cw_ref[...]
    for t in range(dec_seq):
        a = cw[0:1] * gcat[t] + cw[1:2] * gcat[t + 1] + cw[2:3] * gcat[t + 2]
        z = (_silu(a) * _dot(h_s[t], wu_ref[...])).astype(BF16)
        acc_s[t] += _dot(z, wd_ref[...])
    nst_ref[0] = gcat[dec_seq]
    nst_ref[1] = gcat[dec_seq + 1]

    @pl.when(j == pl.num_programs(0) - 1)
    def _():
        for t in range(dec_seq):
            y = x_ref[t] + acc_s[t]
            o_ref[t] = _rms(y, gf_ref[...]) if final else y


def ffn_sample(x, state, g, wg, wu, cw, wd, gf, tn, final, layer):
    dec_seq, nb, _ = x.shape
    full3 = lambda j: (0, 0, 0)
    const = lambda j: (0, 0)
    return pl.pallas_call(
        functools.partial(_ffn_sample_kernel, dec_seq=dec_seq, final=final),
        grid=(D_FF // tn,),
        in_specs=[pl.BlockSpec((dec_seq, nb, D_MODEL), full3),
                  pl.BlockSpec((2, nb, tn), lambda j: (0, 0, j)),
                  pl.BlockSpec((1, D_MODEL), const),
                  _wspec((D_MODEL, tn), lambda j: (0, j), layer),
                  _wspec((D_MODEL, tn), lambda j: (0, j), layer),
                  pl.BlockSpec((8, tn), lambda j: (0, j)),
                  _wspec((tn, D_MODEL), lambda j: (j, 0), layer),
                  pl.BlockSpec((1, D_MODEL), const)],
        out_specs=[pl.BlockSpec((dec_seq, nb, D_MODEL), full3),
                   pl.BlockSpec((2, nb, tn), lambda j: (0, 0, j))],
        out_shape=[jax.ShapeDtypeStruct((dec_seq, nb, D_MODEL), F32),
                   jax.ShapeDtypeStruct((2, nb, D_FF), F32)],
        scratch_shapes=[pltpu.VMEM((dec_seq, nb, D_MODEL), BF16),
                        pltpu.VMEM((dec_seq, nb, D_MODEL), F32)],
        compiler_params=_cparams(("arbitrary",)),
        name="ffn_sample",
    )(x, state, g, wg, wu, cw, wd, gf)


def _overlap(n_cmp, n_sel, rows, cols):
    c0 = np.arange(n_cmp)[:, None] * CMP_STRIDE
    s0 = np.arange(n_sel)[None, :] * SEL_BLOCK
    ov = np.minimum(c0 + CMP_BLOCK, s0 + SEL_BLOCK) - np.maximum(c0, s0)
    out = np.zeros((rows, cols), np.float32)
    out[:n_cmp, :n_sel] = np.clip(ov, 0, None).astype(np.float32) / CMP_BLOCK
    return out


def _block_expand(n_keys):
    return jnp.asarray(np.arange(128)[:, None] == (np.arange(n_keys)[None, :] // SEL_BLOCK), BF16)


def _chunk_perm():
    row = np.arange(PAGE)
    return jnp.asarray((row[:, None] % RPC) * CMP_STRIDE + row[:, None] // RPC == row[None, :], BF16)


def _pad_rows(a, rows):
    return jnp.concatenate([a, jnp.zeros((rows - a.shape[0],) + a.shape[1:], a.dtype)], axis=0)


def _prep_w_in(w_in_l):
    offs = np.cumsum((D_NSA, KV_W, KV_W, KV_W, N_BRANCH * N_HEADS, D_CONV, D_CONV, D_CONV))
    q, kc, ks, kw, gl, cb, cc, ch = jnp.split(w_in_l, [int(o) for o in offs[:-1]], axis=-1)
    glp = jnp.concatenate([gl, jnp.zeros((D_MODEL, D_KVT - ROW_G - N_BRANCH * N_HEADS), F32)], axis=-1)
    w_main = jnp.concatenate([q * (HEAD_DIM ** -0.5), cb, cc, ch], axis=-1).astype(BF16)
    wt_kv = jnp.concatenate([kc, ks, kw, glp], axis=-1).T.astype(BF16)
    return w_main, wt_kv


def _prep_cmp(w_cmp_l, pe_cmp_l):
    eye = jnp.eye(2 * N_KV, dtype=F32).reshape(2, N_KV, 2, N_KV)
    wbd = jnp.einsum('csde,cgkh->scgdkhe', w_cmp_l, eye).reshape(CMP_BLOCK, KV_W, KV_W).astype(BF16)
    pe = pe_cmp_l.reshape(2, 2, CMP_STRIDE, HEAD_DIM).transpose(1, 0, 3, 2)
    pet = jnp.broadcast_to(pe[:, :, None, :, None, :], (2, 2, N_KV, HEAD_DIM, RPC, CMP_STRIDE))
    return wbd, pet.reshape(2, KV_W, PAGE)


TM_PROJ = 1024
TN_PROJ = 1024
TM_MIX = 256
TM_MEM = 256
TM_FFN = 512
TN_FFN = 512
BT_CMP = 4
BT_NSA = 4
BT_MEM = 8


def _kv_rows(kvt, row, start=0):
    slab = kvt[:, row:row + KV_W, start:]
    b, _, t = slab.shape
    return slab.reshape(b, 2, N_KV, HEAD_DIM, t).transpose(0, 4, 1, 2, 3)


def _layer_prompt(x, mem_kv, lw, consts, layer, batch, seq, final):
    p_main, kvt = proj_in(x, lw['g_mix'], lw['w_main'], lw['wt_kv'], TM_PROJ, TN_PROJ, seq)
    cmp = compress_prompt(kvt, lw['pet'], consts['perm'], lw['wbd'])
    n_cmp_pad = seq // CMP_STRIDE
    cmp_t = cmp.reshape(batch, n_cmp_pad, 2, N_KV, HEAD_DIM).transpose(0, 2, 3, 1, 4)
    o_nsa = nsa_prompt(p_main, kvt, cmp_t, consts['ovt_p'], consts['expand_p'])
    x1, u_tail = mixout_prompt(o_nsa, p_main, lw['conv_w'], lw['g_out_nsa'], lw['g_out_conv'],
                               lw['w_out'], x, TM_MIX, seq, layer)
    x2 = mem_prompt_attn(x1, lw['g_mem'], lw['w_mem_q'], mem_kv, lw['w_mem_o'], TM_MEM, seq, layer)
    x3, g_tail = ffn_prompt(x2, lw['g_ffn'], lw['w_ff_gate'], lw['w_ff_up'], lw['ffn_conv_w'],
                            lw['w_ff_down'], lw['g_final'], TM_FFN, TN_FFN, seq, final, layer)
    conv_state = u_tail.reshape(batch, seq // TM_MIX, 8, D_CONV)[:, -1, 6:8]
    ffn_state = g_tail.reshape(batch, seq // TM_FFN, 8, D_FF)[:, -1, 6:8]
    return (x3, _kv_rows(kvt, ROW_KC), _kv_rows(kvt, ROW_KS),
            _kv_rows(kvt, ROW_KW, seq - min(WINDOW, seq)), conv_state, ffn_state)


def _layer_sample(x, lw, consts, layer, cache_cmp, cache_sel, page_idx, cache_win, cache_mem,
                  st_conv, st_ffn, final):
    dec_seq, nb, _ = x.shape
    rows = dec_seq * nb
    p_s, kvt = proj_in(x.reshape(rows, D_MODEL), lw['g_mix'], lw['w_main'], lw['wt_kv'],
                       rows, TN_PROJ, rows)
    p3 = p_s.reshape(dec_seq, nb, D_MAIN)
    kvn = kvt.reshape(D_KVT, dec_seq, nb).transpose(2, 0, 1)
    kvn_pad = jnp.concatenate([kvn, jnp.zeros((nb, D_KVT, PAGE - dec_seq), F32)], axis=-1)
    cmp_s = compress_paged(cache_cmp, page_idx, lw['pet'], consts['perm'], lw['wbd'], BT_CMP)
    q = p3[:, :, COL_Q:COL_Q + D_NSA].reshape(dec_seq, nb, N_KV, HPG, HEAD_DIM).transpose(1, 3, 2, 0, 4)
    gsel = jnp.eye(N_KV, dtype=F32)
    qbd = (q[:, :, :, :, None, :] * gsel[None, None, :, None, :, None]).reshape(
        nb, HPG * N_KV * dec_seq, KV_HALF)
    gates = kvn[:, ROW_G:ROW_G + N_BRANCH * N_HEADS].reshape(
        nb, N_KV, HPG, N_BRANCH, dec_seq).transpose(0, 2, 1, 4, 3).reshape(nb, HPG * N_KV * dec_seq, N_BRANCH)
    o, win_new = nsa_sample(cache_sel, page_idx, qbd, gates, cmp_s, kvn_pad[:, ROW_KS:ROW_KS + KV_W],
                            cache_win, layer, kvn_pad[:, ROW_KW:ROW_KW + KV_W],
                            consts['ov_s'], consts['expand_s'], BT_NSA)
    o_nsa = o.reshape(nb, HPG, N_KV, dec_seq, HEAD_DIM).transpose(3, 0, 2, 1, 4).reshape(dec_seq, nb, D_NSA)
    x1, conv_state = mixout_sample(o_nsa, p3, st_conv, lw['conv_w'], lw['g_out_nsa'],
                                   lw['g_out_conv'], lw['w_out'], x, layer)
    qm = rms_matmul(x1.reshape(rows, D_MODEL), lw['g_mem'], lw['w_mem_q'], rows, D_MEM, layer)
    qh = qm.reshape(dec_seq, nb, MEM_HEADS, MEM_HEAD_DIM).transpose(1, 2, 0, 3)
    qmbd = (qh[:, :, :, None, :] * jnp.eye(MEM_HEADS, dtype=F32)[None, :, None, :, None]).reshape(
        nb, MEM_HEADS * dec_seq, D_MEM)
    om = mem_sample_attn(qmbd, cache_mem, layer, BT_MEM)
    om = om.reshape(nb, MEM_HEADS, dec_seq, MEM_HEAD_DIM).transpose(2, 0, 1, 3).reshape(rows, D_MEM)
    x2 = matmul_residual(x1.reshape(rows, D_MODEL), om, lw['w_mem_o'], rows, layer).reshape(
        dec_seq, nb, D_MODEL)
    x3, ffn_state = ffn_sample(x2, st_ffn, lw['g_ffn'], lw['w_ff_gate'], lw['w_ff_up'],
                               lw['ffn_conv_w'], lw['w_ff_down'], lw['g_final'], TN_FFN, final, layer)
    kv5 = kvn[:, 0:ROW_G].reshape(nb, N_BRANCH, 2, N_KV, HEAD_DIM, dec_seq).transpose(1, 0, 5, 2, 3, 4)
    return (x3, kv5[0], kv5[1], win_new.transpose(0, 4, 1, 2, 3),
            conv_state.transpose(1, 0, 2), ffn_state.transpose(1, 0, 2))


def kernel(x_prompt, x_sample, cache_cmp_kv, cache_sel_kv, cache_win_kv, cache_mem_kv,
           state_conv, state_ffn_conv, page_table, mem_prompt,
           g_mix, w_in, w_cmp, pe_cmp, conv_w, g_out_nsa, g_out_conv, w_out,
           g_mem_src, w_mem_kv, g_mem, w_mem_q, w_mem_o, g_ffn, w_ff_gate, w_ff_up,
           ffn_conv_w, w_ff_down, g_final):
    batch, seq, _ = x_prompt.shape
    nb, dec_seq, _ = x_sample.shape
    depth = w_in.shape[0]
    n_phys = cache_cmp_kv.shape[1]
    npages = page_table.shape[1]
    past = npages * PAGE
    win_buf = cache_win_kv.shape[2]
    mem_len = mem_prompt.shape[1]

    n_cmp_s = (past + dec_seq - CMP_BLOCK) // CMP_STRIDE + 1
    n_sel_s = -(-(past + dec_seq) // SEL_BLOCK)
    n_cmp_p = (seq - CMP_BLOCK) // CMP_STRIDE + 1
    n_sel_p = seq // SEL_BLOCK
    consts = {
        'ov_s': jnp.asarray(_overlap(n_cmp_s, n_sel_s, past // CMP_STRIDE, 128), BF16),
        'expand_s': _block_expand(past + PAGE),
        'ovt_p': jnp.asarray(_overlap(n_cmp_p, n_sel_p, seq // CMP_STRIDE, n_sel_p).T, BF16),
        'expand_p': _block_expand(seq),
        'perm': _chunk_perm(),
    }

    cache_cmp = cache_cmp_kv.transpose(0, 1, 3, 4, 5, 2).reshape(depth * n_phys, 2, N_KV, HEAD_DIM, PAGE)
    cache_sel = cache_sel_kv.transpose(0, 1, 3, 4, 5, 2).reshape(depth * n_phys, 2, N_KV, HEAD_DIM, PAGE)
    cache_win = cache_win_kv.transpose(0, 1, 3, 4, 5, 2)
    cache_mem = cache_mem_kv.reshape(depth, nb, mem_len * 2 * MEM_HEADS, MEM_HEAD_DIM)

    xp = x_prompt.reshape(batch * seq, D_MODEL)
    xs = x_sample.transpose(1, 0, 2)
    mem_rows = mem_prompt.reshape(batch * mem_len, D_MODEL)
    outs = [[] for _ in range(11)]
    stacked = {
        'w_out': w_out.astype(BF16), 'w_mem_q': w_mem_q.astype(BF16), 'w_mem_o': w_mem_o.astype(BF16),
        'w_ff_gate': w_ff_gate.astype(BF16), 'w_ff_up': w_ff_up.astype(BF16),
        'w_ff_down': w_ff_down.astype(BF16),
    }
    w_mem_kv_b = w_mem_kv.astype(BF16)
    for l in range(depth):
        wbd, pet = _prep_cmp(w_cmp[l], pe_cmp[l])
        w_main, wt_kv = _prep_w_in(w_in[l])
        lw = dict(stacked)
        lw.update({
            'g_mix': g_mix[l][None], 'w_main': w_main, 'wt_kv': wt_kv, 'wbd': wbd, 'pet': pet,
            'conv_w': _pad_rows(conv_w[l], 8), 'g_out_nsa': g_out_nsa[l][None],
            'g_out_conv': g_out_conv[l][None], 'g_mem': g_mem[l][None], 'g_ffn': g_ffn[l][None],
            'ffn_conv_w': _pad_rows(ffn_conv_w[l], 8), 'g_final': g_final[None],
        })
        final = l == depth - 1
        mem_kv_p = rms_matmul(mem_rows, g_mem_src[l][None], w_mem_kv_b,
                              batch * mem_len, 2 * D_MEM, l).reshape(batch, mem_len, 2 * D_MEM)
        xp, kc, ks, kw, cst, fst = _layer_prompt(xp, mem_kv_p, lw, consts, l, batch, seq, final)
        page_idx = page_table + l * n_phys
        xs, kc2, ks2, kw2, cst2, fst2 = _layer_sample(
            xs, lw, consts, l, cache_cmp, cache_sel, page_idx, cache_win, cache_mem,
            state_conv[l].transpose(1, 0, 2), state_ffn_conv[l].transpose(1, 0, 2), final)
        for lst, val in zip(outs, (kc, ks, kw, mem_kv_p, cst, fst, kc2, ks2, kw2, cst2, fst2)):
            lst.append(val)

    st = [jnp.stack(o) for o in outs]
    return (xp.reshape(batch, seq, D_MODEL), xs.transpose(1, 0, 2),
            st[0], st[1], st[2],
            st[3].reshape(depth, batch, mem_len, 2, MEM_HEADS, MEM_HEAD_DIM),
            st[4], st[5], st[6], st[7], st[8], st[9], st[10])
```

```python
import functools

import numpy as np
import jax
import jax.numpy as jnp
from jax import lax
from jax.experimental import pallas as pl
from jax.experimental.pallas import tpu as pltpu

F32 = jnp.float32
BF16 = jnp.bfloat16

D_MODEL = 2048
D_NSA = 1024
D_CONV = 1024
HEAD_DIM = 64
N_HEADS = 16
N_KV = 2
HPG = 8
N_BRANCH = 3
KV_W = 2 * N_KV * HEAD_DIM
KV_HALF = N_KV * HEAD_DIM
CMP_BLOCK = 32
CMP_STRIDE = 16
SEL_BLOCK = 64
TOPK = 16
WINDOW = 512
Q_BLOCK = 128
FORCE_BONUS = 1e4
D_FF = 5632
MEM_HEADS = 4
MEM_HEAD_DIM = 128
D_MEM = 512
PAGE = 128
EPS = 1e-6
NEG = -1e30

COL_Q = 0
COL_CB = 1024
COL_CC = 2048
COL_CH = 3072
D_MAIN = 4096
ROW_KC = 0
ROW_KS = 256
ROW_KW = 512
ROW_G = 768
D_KVT = 896

VMEM_LIMIT = 56 * 1024 * 1024


def _cparams(sem):
    return pltpu.CompilerParams(dimension_semantics=sem, vmem_limit_bytes=VMEM_LIMIT)


def _rms(x, g):
    return x * lax.rsqrt(jnp.mean(x * x, axis=-1, keepdims=True) + EPS) * g


def _dot(a, b):
    return jnp.dot(a, b, preferred_element_type=F32)


def _dot_nt(a, b):
    return lax.dot_general(a, b, (((1,), (1,)), ((), ())), preferred_element_type=F32)


def _softmax_bias(s, bias):
    s = s + bias
    e = jnp.exp(s - jnp.max(s, axis=-1, keepdims=True))
    return e, jnp.sum(e, axis=-1, keepdims=True)


def _sigmoid(x):
    return 1.0 / (1.0 + jnp.exp(-x))


def _rms_matmul_kernel(x_ref, g_ref, w_ref, o_ref, h_ref):
    @pl.when(pl.program_id(1) == 0)
    def _():
        h_ref[...] = _rms(x_ref[...], g_ref[...]).astype(BF16)

    o_ref[...] = _dot(h_ref[...], w_ref[...]).astype(o_ref.dtype)


def _wspec(block, imap, layer):
    return pl.BlockSpec((None,) + block, lambda *a: (layer,) + imap(*a))


def rms_matmul(x, g, w, tm, tn, layer):
    m, d = x.shape
    n = w.shape[2]
    return pl.pallas_call(
        _rms_matmul_kernel,
        grid=(m // tm, n // tn),
        in_specs=[pl.BlockSpec((tm, d), lambda i, j: (i, 0)),
                  pl.BlockSpec((1, d), lambda i, j: (0, 0)),
                  _wspec((d, tn), lambda i, j: (0, j), layer)],
        out_specs=pl.BlockSpec((tm, tn), lambda i, j: (i, j)),
        out_shape=jax.ShapeDtypeStruct((m, n), F32),
        scratch_shapes=[pltpu.VMEM((tm, d), BF16)],
        compiler_params=_cparams(("arbitrary", "arbitrary")),
        name="rms_matmul",
    )(x, g, w)


def _proj_in_kernel(x_ref, g_ref, w_ref, wt_ref, o_ref, ot_ref, h_ref):
    @pl.when(pl.program_id(1) == 0)
    def _():
        h = _rms(x_ref[...], g_ref[...]).astype(BF16)
        h_ref[...] = h
        ot_ref[...] = _dot_nt(wt_ref[...], h)

    o_ref[...] = _dot(h_ref[...], w_ref[...])


def proj_in(x, g, w_main, wt_kv, tm, tn, seq, layer):
    m, d = x.shape
    tps = seq // tm
    return pl.pallas_call(
        _proj_in_kernel,
        grid=(m // tm, D_MAIN // tn),
        in_specs=[pl.BlockSpec((tm, d), lambda i, j: (i, 0)),
                  pl.BlockSpec((1, d), lambda i, j: (0, 0)),
                  _wspec((d, tn), lambda i, j: (0, j), layer),
                  _wspec((D_KVT, d), lambda i, j: (0, 0), layer)],
        out_specs=[pl.BlockSpec((tm, tn), lambda i, j: (i, j)),
                   pl.BlockSpec((None, D_KVT, tm), lambda i, j: (i // tps, 0, i % tps))],
        out_shape=[jax.ShapeDtypeStruct((m, D_MAIN), F32),
                   jax.ShapeDtypeStruct((m // seq, D_KVT, seq), F32)],
        scratch_shapes=[pltpu.VMEM((tm, d), BF16)],
        compiler_params=_cparams(("arbitrary", "arbitrary")),
        name="proj_in",
    )(x, g, w_main, wt_kv)


def _matmul_res_kernel(x_ref, a_ref, w_ref, o_ref):
    o_ref[...] = x_ref[...] + _dot(a_ref[...].astype(BF16), w_ref[...])


def matmul_residual(x, a, w, tm, layer):
    m, n = x.shape
    k = a.shape[1]
    return pl.pallas_call(
        _matmul_res_kernel,
        grid=(m // tm,),
        in_specs=[pl.BlockSpec((tm, n), lambda i: (i, 0)),
                  pl.BlockSpec((tm, k), lambda i: (i, 0)),
                  _wspec((k, n), lambda i: (0, 0), layer)],
        out_specs=pl.BlockSpec((tm, n), lambda i: (i, 0)),
        out_shape=jax.ShapeDtypeStruct((m, n), F32),
        compiler_params=_cparams(("arbitrary",)),
        name="matmul_residual",
    )(x, a, w)


RPC = PAGE // CMP_STRIDE


def _compress_chunks(chunk_at, n_chunks, pet_ref, perm_ref, w_ref, xs_ref):
    perm = perm_ref[...]
    for ci in range(n_chunks):
        chunk = chunk_at(ci)
        for a in range(2):
            xs_ref[a, ci] = _dot_nt(perm, (chunk + pet_ref[a]).astype(BF16))
    rows = n_chunks * RPC
    acc0 = jnp.zeros((rows, KV_W), F32)
    acc1 = jnp.zeros((rows, KV_W), F32)
    for r in range(CMP_STRIDE):
        x0 = xs_ref[0, :, r * RPC:(r + 1) * RPC, :].reshape(rows, KV_W).astype(BF16)
        x1 = xs_ref[1, :, r * RPC:(r + 1) * RPC, :].reshape(rows, KV_W).astype(BF16)
        acc0 = acc0 + _dot(x0, w_ref[r])
        acc1 = acc1 + _dot(x1, w_ref[CMP_STRIDE + r])
    return acc0 + pltpu.roll(acc1, rows - 1, 0)


def _compress_prompt_kernel(kvt_ref, pet_ref, perm_ref, w_ref, o_ref, xs_ref, *, n_chunks):
    out = _compress_chunks(lambda ci: kvt_ref[:, ci * PAGE:(ci + 1) * PAGE], n_chunks,
                           pet_ref, perm_ref, w_ref, xs_ref)
    o_ref[...] = out.astype(BF16)


def compress_prompt(kvt, pet, perm, wbd):
    batch, _, seq = kvt.shape
    n_chunks = seq // PAGE
    nblk = seq // CMP_STRIDE
    return pl.pallas_call(
        functools.partial(_compress_prompt_kernel, n_chunks=n_chunks),
        grid=(batch,),
        in_specs=[pl.BlockSpec((None, KV_W, seq), lambda b: (b, ROW_KC // KV_W, 0)),
                  pl.BlockSpec((2, KV_W, PAGE), lambda b: (0, 0, 0)),
                  pl.BlockSpec((PAGE, PAGE), lambda b: (0, 0)),
                  pl.BlockSpec((CMP_BLOCK, KV_W, KV_W), lambda b: (0, 0, 0))],
        out_specs=pl.BlockSpec((None, nblk, KV_W), lambda b: (b, 0, 0)),
        out_shape=jax.ShapeDtypeStruct((batch, nblk, KV_W), BF16),
        scratch_shapes=[pltpu.VMEM((2, n_chunks, PAGE, KV_W), F32)],
        compiler_params=_cparams(("arbitrary",)),
        name="compress_prompt",
    )(kvt, pet, perm, wbd)


def _compress_paged_kernel(pt_ref, *refs, bt, npages):
    page_refs = refs[:bt * npages]
    pet_ref, perm_ref, w_ref, o_ref, xs_ref = refs[bt * npages:]
    out = _compress_chunks(lambda ci: page_refs[ci][...].reshape(KV_W, PAGE), bt * npages,
                           pet_ref, perm_ref, w_ref, xs_ref)
    o_ref[...] = out.astype(BF16).reshape(bt, npages * RPC, KV_W)


def compress_paged(cache, page_idx, pet, perm, wbd, bt):
    nb, npages = page_idx.shape
    nblk = npages * RPC
    page_specs = [
        pl.BlockSpec((None, 2, N_KV, HEAD_DIM, PAGE), functools.partial(
            lambda i, pt, bb, p: (pt[i * bt + bb, p], 0, 0, 0, 0), bb=bb, p=p))
        for bb in range(bt) for p in range(npages)]
    grid_spec = pltpu.PrefetchScalarGridSpec(
        num_scalar_prefetch=1,
        grid=(nb // bt,),
        in_specs=page_specs + [pl.BlockSpec((2, KV_W, PAGE), lambda i, pt: (0, 0, 0)),
                               pl.BlockSpec((PAGE, PAGE), lambda i, pt: (0, 0)),
                               pl.BlockSpec((CMP_BLOCK, KV_W, KV_W), lambda i, pt: (0, 0, 0))],
        out_specs=pl.BlockSpec((bt, nblk, KV_W), lambda i, pt: (i, 0, 0)),
        scratch_shapes=[pltpu.VMEM((2, bt * npages, PAGE, KV_W), F32)])
    return pl.pallas_call(
        functools.partial(_compress_paged_kernel, bt=bt, npages=npages),
        grid_spec=grid_spec,
        out_shape=jax.ShapeDtypeStruct((nb, nblk, KV_W), BF16),
        compiler_params=_cparams(("arbitrary",)),
        name="compress_paged",
    )(page_idx, *([cache] * (bt * npages)), pet, perm, wbd)


def _importance(imp, idx, tpos, n_sel):
    valid = (idx * SEL_BLOCK <= tpos) & (idx < n_sel)
    cur = tpos // SEL_BLOCK
    forced = (idx == 0) | (idx == cur) | (idx == cur - 1)
    return jnp.where(valid, imp + jnp.where(forced, FORCE_BONUS, 0.0), NEG)


def _topk_rows(imp, n_cand):
    groups = [imp[v * 8:(v + 1) * 8] for v in range(n_cand // 8)]
    sub = lax.broadcasted_iota(jnp.int32, groups[0].shape, 0)
    ranks = [jnp.zeros(g.shape, F32) for g in groups]
    for jp in range(n_cand):
        c = imp[jp:jp + 1, :]
        for v, g in enumerate(groups):
            ge = jnp.where(c >= g, 1.0, 0.0)
            gt = jnp.where(c > g, 1.0, 0.0)
            if v * 8 > jp:
                beats = ge
            elif v * 8 + 7 < jp:
                beats = gt
            else:
                beats = jnp.where(sub > jp - v * 8, ge, gt)
            ranks[v] = ranks[v] + beats
    return jnp.concatenate([jnp.where(r < TOPK, 1.0, 0.0) for r in ranks], axis=0)


def _topk_lanes(imp, idx, n_cand):
    rank = jnp.zeros(imp.shape, F32)
    for jp in range(n_cand):
        c = imp[:, jp:jp + 1]
        rank = rank + jnp.where(idx > jp, jnp.where(c >= imp, 1.0, 0.0), jnp.where(c > imp, 1.0, 0.0))
    return jnp.where(rank < TOPK, 1.0, 0.0)


KEY_TILE = 512
V_PAD = 16


def _tile_heads(x):
    return jnp.concatenate([x] * HPG, axis=0)


def _nsa_prompt_kernel(q_ref, gate_ref, cmp_ref, kst_ref, vst_ref, kwt_ref, vwt_ref, ovt_ref, exp_ref,
                       o_ref, ksb, vsb, kwb, vwb, *, seq):
    g = pl.program_id(1)
    i = pl.program_id(2)
    n_cmp_pad = seq // CMP_STRIDE
    n_sel = seq // SEL_BLOCK

    @pl.when(i == 0)
    def _():
        ones_row = jnp.where(lax.broadcasted_iota(jnp.int32, (V_PAD, seq), 0) == 0, 1.0, 0.0).astype(BF16)
        ksb[...] = kst_ref[...].astype(BF16)
        vsb[0:HEAD_DIM, :] = vst_ref[...].astype(BF16)
        vsb[HEAD_DIM:HEAD_DIM + V_PAD, :] = ones_row
        kwb[...] = kwt_ref[...].astype(BF16)
        vwb[0:HEAD_DIM, :] = vwt_ref[...].astype(BF16)
        vwb[HEAD_DIM:HEAD_DIM + V_PAD, :] = ones_row

    t0 = i * Q_BLOCK
    qb = q_ref[...]
    qs = jnp.concatenate([qb[:, h * HEAD_DIM:(h + 1) * HEAD_DIM] for h in range(HPG)],
                         axis=0).astype(BF16)
    tcol = lax.broadcasted_iota(jnp.int32, (Q_BLOCK, 1), 0) + t0

    ncol = lax.broadcasted_iota(jnp.int32, (Q_BLOCK, n_cmp_pad), 1)
    vis = jnp.where(ncol < n_cmp_pad - 1, ncol * CMP_STRIDE + (CMP_BLOCK - 1), seq) <= tcol
    e_c, l_c = _softmax_bias(_dot_nt(qs, cmp_ref[0]), _tile_heads(jnp.where(vis, 0.0, NEG)))
    any_vis = _tile_heads(jnp.where(tcol >= CMP_BLOCK - 1, 1.0, 0.0))
    p_c = (e_c * (any_vis / jnp.maximum(l_c, 1e-30))).astype(BF16)
    o_c = _dot(p_c, cmp_ref[1])

    po = _dot_nt(ovt_ref[...], p_c)
    imp_t = po[:, 0:Q_BLOCK]
    for h in range(1, HPG):
        imp_t = imp_t + po[:, h * Q_BLOCK:(h + 1) * Q_BLOCK]
    jj = lax.broadcasted_iota(jnp.int32, (n_sel, Q_BLOCK), 0)
    tt = lax.broadcasted_iota(jnp.int32, (n_sel, Q_BLOCK), 1) + t0
    sel_t = _topk_rows(_importance(imp_t, jj, tt, n_sel), n_sel)
    sel = jnp.concatenate([sel_t, jnp.zeros((128 - n_sel, Q_BLOCK), F32)], axis=0).T.astype(BF16)

    kk = lax.broadcasted_iota(jnp.int32, (Q_BLOCK, KEY_TILE), 1)

    def scores(c):
        k0 = pl.multiple_of(c * KEY_TILE, KEY_TILE)
        msel = _dot(sel, exp_ref[:, pl.ds(k0, KEY_TILE)])
        bias = jnp.where(msel > 0.5, jnp.where(kk + k0 <= tcol, 0.0, NEG), NEG)
        return _dot(qs, ksb[:, pl.ds(k0, KEY_TILE)]) + _tile_heads(bias)

    def accumulate(c, s, carry):
        m, acc = carry
        k0 = pl.multiple_of(c * KEY_TILE, KEY_TILE)
        m_new = jnp.maximum(m, jnp.max(s, axis=-1, keepdims=True))
        p = jnp.exp(s - m_new).astype(BF16)
        return m_new, jnp.exp(m - m_new) * acc + _dot_nt(p, vsb[:, pl.ds(k0, KEY_TILE)])

    def tile_pair(c2, carry):
        s_a = scores(2 * c2)
        s_b = scores(2 * c2 + 1)
        return accumulate(2 * c2 + 1, s_b, accumulate(2 * c2, s_a, carry))

    n_rows = HPG * Q_BLOCK
    m0 = jnp.full((n_rows, 1), NEG, F32)
    a0 = jnp.zeros((n_rows, HEAD_DIM + V_PAD), F32)
    n_tiles = (t0 + Q_BLOCK + KEY_TILE - 1) // KEY_TILE
    carry = lax.fori_loop(0, n_tiles // 2, tile_pair, (m0, a0))
    _, acc_s = lax.cond(n_tiles % 2 == 1,
                        lambda cr: accumulate(n_tiles - 1, scores(n_tiles - 1), cr),
                        lambda cr: cr, carry)
    o_s = acc_s[:, 0:HEAD_DIM] / acc_s[:, HEAD_DIM:HEAD_DIM + 1]

    span = WINDOW + Q_BLOCK
    ws = pl.multiple_of(jnp.maximum(t0 - WINDOW, 0), Q_BLOCK)
    dlt = tcol - (lax.broadcasted_iota(jnp.int32, (Q_BLOCK, span), 1) + ws)
    bias_w = jnp.where(dlt >= 0, jnp.where(dlt < WINDOW, 0.0, NEG), NEG)
    s_w = _dot(qs, kwb[:, pl.ds(ws, span)]) + _tile_heads(bias_w)
    e_w = jnp.exp(s_w - jnp.max(s_w, axis=-1, keepdims=True)).astype(BF16)
    acc_w = _dot_nt(e_w, vwb[:, pl.ds(ws, span)])
    o_w = acc_w[:, 0:HEAD_DIM] / acc_w[:, HEAD_DIM:HEAD_DIM + 1]

    gs = _sigmoid(gate_ref[...]).T
    gsel = jnp.where(g == 0, gs[:, 0:HPG * N_BRANCH], gs[:, HPG * N_BRANCH:2 * HPG * N_BRANCH])
    for h in range(HPG):
        r0, r1 = h * Q_BLOCK, (h + 1) * Q_BLOCK
        c0 = h * N_BRANCH
        o_ref[:, h * HEAD_DIM:(h + 1) * HEAD_DIM] = (
            o_c[r0:r1] * gsel[:, c0:c0 + 1] + o_s[r0:r1] * gsel[:, c0 + 1:c0 + 2]
            + o_w[r0:r1] * gsel[:, c0 + 2:c0 + 3])


def nsa_prompt(p_main, kvt, cmp_t, ovt, expand):
    batch, _, seq = kvt.shape
    nqb = seq // Q_BLOCK
    gw = HPG * HEAD_DIM
    n_cmp_pad = seq // CMP_STRIDE
    n_sel = seq // SEL_BLOCK
    kvt_spec = lambda row: pl.BlockSpec((None, HEAD_DIM, seq),
                                        lambda b, g, i: (b, row // HEAD_DIM + g, 0))
    return pl.pallas_call(
        functools.partial(_nsa_prompt_kernel, seq=seq),
        grid=(batch, N_KV, nqb),
        in_specs=[
            pl.BlockSpec((Q_BLOCK, gw), lambda b, g, i: (b * nqb + i, g)),
            pl.BlockSpec((None, 128, Q_BLOCK), lambda b, g, i: (b, ROW_G // 128, i)),
            pl.BlockSpec((None, 2, None, n_cmp_pad, HEAD_DIM), lambda b, g, i: (b, 0, g, 0, 0)),
            kvt_spec(ROW_KS), kvt_spec(ROW_KS + KV_HALF),
            kvt_spec(ROW_KW), kvt_spec(ROW_KW + KV_HALF),
            pl.BlockSpec((n_sel, n_cmp_pad), lambda b, g, i: (0, 0)),
            pl.BlockSpec((128, seq), lambda b, g, i: (0, 0)),
        ],
        out_specs=pl.BlockSpec((Q_BLOCK, gw), lambda b, g, i: (b * nqb + i, g)),
        out_shape=jax.ShapeDtypeStruct((batch * seq, D_NSA), F32),
        scratch_shapes=[pltpu.VMEM((HEAD_DIM, seq), BF16), pltpu.VMEM((HEAD_DIM + V_PAD, seq), BF16)] * 2,
        compiler_params=_cparams(("arbitrary", "arbitrary", "arbitrary")),
        name="nsa_prompt",
    )(p_main, kvt, cmp_t, kvt, kvt, kvt, kvt, ovt, expand)


def _pick_group(o2, rg):
    return jnp.where(rg == 0, o2[:, 0:HEAD_DIM], o2[:, HEAD_DIM:2 * HEAD_DIM])


def _masked_softmax(s, mask):
    s = jnp.where(mask, s, NEG)
    m = jnp.max(s, axis=-1, keepdims=True)
    e = jnp.where(mask, jnp.exp(s - m), 0.0)
    return e / jnp.maximum(jnp.sum(e, axis=-1, keepdims=True), 1e-30)


def _masked_softmax2(s_a, mask_a, s_b, mask_b):
    s_a = jnp.where(mask_a, s_a, NEG)
    s_b = jnp.where(mask_b, s_b, NEG)
    m = jnp.maximum(jnp.max(s_a, axis=-1, keepdims=True), jnp.max(s_b, axis=-1, keepdims=True))
    e_a = jnp.where(mask_a, jnp.exp(s_a - m), 0.0)
    e_b = jnp.where(mask_b, jnp.exp(s_b - m), 0.0)
    tot = jnp.sum(e_a, axis=-1, keepdims=True) + jnp.sum(e_b, axis=-1, keepdims=True)
    inv = 1.0 / jnp.maximum(tot, 1e-30)
    return (e_a * inv).astype(BF16), (e_b * inv).astype(BF16)


def _nsa_sample_kernel(pt_ref, *refs, bt, npages, dec_seq):
    page_refs = refs[:bt * npages]
    (q_ref, gate_ref, cmp_ref, new_ref, win_ref, ov_ref, exp_ref,
     o_ref, wout_ref, kts, vts, ktw, vtw) = refs[bt * npages:]
    past = npages * PAGE
    n_cmp = (past + dec_seq - CMP_BLOCK) // CMP_STRIDE + 1
    n_sel = -(-(past + dec_seq) // SEL_BLOCK)
    n_rows = HPG * N_KV * dec_seq
    win_buf = win_ref.shape[-1]
    n_cmp_pad = cmp_ref.shape[1]
    n_new = new_ref.shape[1]
    gt = N_KV * dec_seq

    rowi = lax.broadcasted_iota(jnp.int32, (n_rows, 1), 0)
    rg = (rowi // dec_seq) % N_KV
    tpos = past + rowi % dec_seq
    ncol = lax.broadcasted_iota(jnp.int32, (n_rows, n_cmp_pad), 1)
    m_c = (ncol * CMP_STRIDE + (CMP_BLOCK - 1) <= tpos) & (ncol < n_cmp)
    jj = lax.broadcasted_iota(jnp.int32, (gt, 128), 1)
    t8 = past + lax.broadcasted_iota(jnp.int32, (gt, 128), 0) % dec_seq
    d_new = tpos - (past + lax.broadcasted_iota(jnp.int32, (n_rows, n_new), 1))
    m_new = d_new >= 0
    m_wn = m_new & (d_new < WINDOW)
    wcol = lax.broadcasted_iota(jnp.int32, (n_rows, win_buf), 1)
    dlt = tpos - (wcol + (past - win_buf))
    m_w = (dlt >= 0) & (dlt < WINDOW)
    lane = lax.broadcasted_iota(jnp.int32, (KV_W, PAGE), 1)

    bbs = range(bt)

    for bb in bbs:
        for p in range(npages):
            page = page_refs[bb * npages + p]
            kts[bb, :, p * PAGE:(p + 1) * PAGE] = page[0].reshape(KV_HALF, PAGE).astype(BF16)
            vts[bb, :, p * PAGE:(p + 1) * PAGE] = page[1].reshape(KV_HALF, PAGE).astype(BF16)
        win = win_ref[bb].reshape(KV_W, win_buf)
        ktw[bb] = win[0:KV_HALF].astype(BF16)
        vtw[bb] = win[KV_HALF:KV_W].astype(BF16)
        kwn = new_ref[bb, :, ROW_KW:ROW_KW + KV_W]
        kwn_t = jnp.concatenate([kwn, jnp.zeros((PAGE - n_new, KV_W), F32)], axis=0).T
        shifted = pltpu.roll(win, win_buf - dec_seq, 1)
        tail = jnp.where(lane >= PAGE - dec_seq, pltpu.roll(kwn_t, PAGE - dec_seq, 1),
                         shifted[:, win_buf - PAGE:])
        wout_ref[bb] = jnp.concatenate([shifted[:, 0:win_buf - PAGE], tail], axis=1).reshape(
            2, N_KV, HEAD_DIM, win_buf)

    qs = [q_ref[bb].astype(BF16) for bb in bbs]
    new = [new_ref[bb].astype(BF16) for bb in bbs]
    kn_s = [x[:, ROW_KS:ROW_KS + KV_HALF] for x in new]
    vn_s = [x[:, ROW_KS + KV_HALF:ROW_KS + KV_W] for x in new]
    kn_w = [x[:, ROW_KW:ROW_KW + KV_HALF] for x in new]
    vn_w = [x[:, ROW_KW + KV_HALF:ROW_KW + KV_W] for x in new]

    s_c = [_dot_nt(qs[bb], cmp_ref[bb, :, 0:KV_HALF]) for bb in bbs]
    s_w = [_dot(qs[bb], ktw[bb]) for bb in bbs]
    s_wn = [_dot_nt(qs[bb], kn_w[bb]) for bb in bbs]
    p_c = [_masked_softmax(s, m_c).astype(BF16) for s in s_c]
    p_w = [_masked_softmax2(s_w[bb], m_w, s_wn[bb], m_wn) for bb in bbs]
    o_c = [_pick_group(_dot(p_c[bb], cmp_ref[bb, :, KV_HALF:KV_W]), rg) for bb in bbs]
    o_w = [_pick_group(_dot_nt(p_w[bb][0], vtw[bb]) + _dot(p_w[bb][1], vn_w[bb]), rg) for bb in bbs]

    po = [_dot(p, ov_ref[...]) for p in p_c]
    imp = [functools.reduce(lambda a, h: a + x[h * gt:(h + 1) * gt], range(1, HPG), x[0:gt]) for x in po]
    sel8 = [_topk_lanes(_importance(x, jj, t8, n_sel), jj, n_sel) for x in imp]
    msel = [_dot(jnp.concatenate([x] * HPG, axis=0).astype(BF16), exp_ref[...]) for x in sel8]

    s_s = [_dot(qs[bb], kts[bb]) for bb in bbs]
    s_sn = [_dot_nt(qs[bb], kn_s[bb]) for bb in bbs]
    p_s = [_masked_softmax2(s_s[bb], msel[bb][:, 0:past] > 0.5,
                            s_sn[bb], (msel[bb][:, past:past + n_new] > 0.5) & m_new) for bb in bbs]
    o_s = [_pick_group(_dot_nt(p_s[bb][0], vts[bb]) + _dot(p_s[bb][1], vn_s[bb]), rg) for bb in bbs]

    for bb in bbs:
        gs = _sigmoid(gate_ref[bb])
        o_ref[bb] = o_c[bb] * gs[:, 0:1] + o_s[bb] * gs[:, 1:2] + o_w[bb] * gs[:, 2:3]


def nsa_sample(cache_sel, page_idx, qbd, gates, cmp_s, kv_new, cache_win, layer, ov_s, expand, bt):
    nb, npages = page_idx.shape
    n_rows = qbd.shape[1]
    dec_seq = n_rows // (HPG * N_KV)
    win_buf = cache_win.shape[-1]
    n_cmp_pad = cmp_s.shape[1]
    n_new = kv_new.shape[1]
    past = npages * PAGE
    page_specs = [
        pl.BlockSpec((None, 2, N_KV, HEAD_DIM, PAGE), functools.partial(
            lambda i, pt, bb, p: (pt[i * bt + bb, p], 0, 0, 0, 0), bb=bb, p=p))
        for bb in range(bt) for p in range(npages)]
    per_b = lambda i, pt: (i, 0, 0)
    const2 = lambda i, pt: (0, 0)
    win_shape = (2, N_KV, HEAD_DIM, win_buf)
    grid_spec = pltpu.PrefetchScalarGridSpec(
        num_scalar_prefetch=1,
        grid=(nb // bt,),
        in_specs=page_specs + [
            pl.BlockSpec((bt, n_rows, KV_HALF), per_b),
            pl.BlockSpec((bt, n_rows, N_BRANCH), per_b),
            pl.BlockSpec((bt, n_cmp_pad, KV_W), per_b),
            pl.BlockSpec((bt, n_new, D_KVT), per_b),
            pl.BlockSpec((None, bt) + win_shape, lambda i, pt: (layer, i, 0, 0, 0, 0)),
            pl.BlockSpec((n_cmp_pad, 128), const2),
            pl.BlockSpec((128, past + PAGE), const2),
        ],
        out_specs=[pl.BlockSpec((bt, n_rows, HEAD_DIM), per_b),
                   pl.BlockSpec((bt,) + win_shape, lambda i, pt: (i, 0, 0, 0, 0))],
        scratch_shapes=[pltpu.VMEM((bt, KV_HALF, past), BF16)] * 2
        + [pltpu.VMEM((bt, KV_HALF, win_buf), BF16)] * 2)
    return pl.pallas_call(
        functools.partial(_nsa_sample_kernel, bt=bt, npages=npages, dec_seq=dec_seq),
        grid_spec=grid_spec,
        out_shape=[jax.ShapeDtypeStruct((nb, n_rows, HEAD_DIM), F32),
                   jax.ShapeDtypeStruct((nb,) + win_shape, F32)],
        compiler_params=_cparams(("arbitrary",)),
        name="nsa_sample",
    )(page_idx, *([cache_sel] * (bt * npages)), qbd, gates, cmp_s, kv_new, cache_win, ov_s, expand)


def _shift_rows(u, p1, p2):
    r = lax.broadcasted_iota(jnp.int32, (u.shape[0], 1), 0)
    u1 = jnp.where(r >= 1, pltpu.roll(u, 1, 0), p1)
    u2 = jnp.where(r >= 2, pltpu.roll(u, 2, 0), jnp.where(r == 1, p1, p2))
    return u1, u2


def _mix_project(o_nsa, o_conv, gn, gc, w, x):
    mix = jnp.concatenate([_rms(o_nsa, gn), _rms(o_conv, gc)], axis=-1).astype(BF16)
    return x + _dot(mix, w)


def _mixout_prompt_kernel(on_ref, cb_ref, cc_ref, ch_ref, cch_ref, chh_ref, cw_ref, gn_ref, gc_ref,
                          w_ref, x_ref, gm_ref, wq_ref, kv_ref, wo_ref, o_ref, ut_ref, *, tiles_per_seq):
    first = (pl.program_id(0) % tiles_per_seq) == 0
    u = cc_ref[...] * ch_ref[...]
    uh = jnp.where(first, 0.0, cch_ref[...] * chh_ref[...])
    u1, u2 = _shift_rows(u, uh[7:8], uh[6:7])
    cw = cw_ref[...]
    v = cw[0:1] * u2 + cw[1:2] * u1 + cw[2:3] * u
    x1 = _mix_project(on_ref[...], cb_ref[...] * v, gn_ref[...], gc_ref[...], w_ref[...], x_ref[...])
    ut_ref[...] = u[u.shape[0] - 8:, :]
    o_ref[...] = _mem_attend(x1, gm_ref[...], wq_ref[...], kv_ref[...].astype(BF16), wo_ref[...])


def mixout_prompt(o_nsa, p_main, cw, gn, gc, w, x, g_mem, wq, mem_kv, wo, tm, seq, layer):
    m = x.shape[0]
    nt = m // tm
    tps = seq // tm
    mem_len = mem_kv.shape[1]
    cblk = lambda c: (lambda i: (i, c // D_CONV))
    halo = lambda c: (lambda i: (jnp.maximum(i * (tm // 8) - 1, 0), c // D_CONV))
    const = lambda i: (0, 0)
    return pl.pallas_call(
        functools.partial(_mixout_prompt_kernel, tiles_per_seq=tps),
        grid=(nt,),
        in_specs=[pl.BlockSpec((tm, D_NSA), lambda i: (i, 0)),
                  pl.BlockSpec((tm, D_CONV), cblk(COL_CB)),
                  pl.BlockSpec((tm, D_CONV), cblk(COL_CC)),
                  pl.BlockSpec((tm, D_CONV), cblk(COL_CH)),
                  pl.BlockSpec((8, D_CONV), halo(COL_CC)),
                  pl.BlockSpec((8, D_CONV), halo(COL_CH)),
                  pl.BlockSpec((8, D_CONV), const),
                  pl.BlockSpec((1, D_NSA), const),
                  pl.BlockSpec((1, D_CONV), const),
                  _wspec((D_MODEL, D_MODEL), const, layer),
                  pl.BlockSpec((tm, D_MODEL), lambda i: (i, 0)),
                  pl.BlockSpec((1, D_MODEL), const),
                  _wspec((D_MODEL, D_MEM), const, layer),
                  pl.BlockSpec((None, mem_len, 2 * D_MEM), lambda i: (i // tps, 0, 0)),
                  _wspec((D_MEM, D_MODEL), const, layer)],
        out_specs=[pl.BlockSpec((tm, D_MODEL), lambda i: (i, 0)),
                   pl.BlockSpec((None, 8, D_CONV), lambda i: (i, 0, 0))],
        out_shape=[jax.ShapeDtypeStruct((m, D_MODEL), F32),
                   jax.ShapeDtypeStruct((nt, 8, D_CONV), F32)],
        compiler_params=_cparams(("arbitrary",)),
        name="mixout_prompt",
    )(o_nsa, p_main, p_main, p_main, p_main, p_main, cw, gn, gc, w, x, g_mem, wq, mem_kv, wo)


def _mixout_sample_kernel(on_ref, cb_ref, cc_ref, ch_ref, st_ref, cw_ref, gn_ref, gc_ref,
                          w_ref, x_ref, o_ref, nst_ref, *, dec_seq):
    ucat = [st_ref[0], st_ref[1]] + [cc_ref[t] * ch_ref[t] for t in range(dec_seq)]
    cw = cw_ref[...]
    for t in range(dec_seq):
        v = cw[0:1] * ucat[t] + cw[1:2] * ucat[t + 1] + cw[2:3] * ucat[t + 2]
        o_ref[t] = _mix_project(on_ref[t], cb_ref[t] * v, gn_ref[...], gc_ref[...],
                                w_ref[...], x_ref[t])
    nst_ref[0] = ucat[dec_seq]
    nst_ref[1] = ucat[dec_seq + 1]


def mixout_sample(o_nsa, p_s, state, cw, gn, gc, w, x, layer):
    dec_seq, nb, _ = x.shape
    cblk = lambda c: (lambda i: (0, 0, c // D_CONV))
    full3 = lambda i: (0, 0, 0)
    const = lambda i: (0, 0)
    return pl.pallas_call(
        functools.partial(_mixout_sample_kernel, dec_seq=dec_seq),
        grid=(1,),
        in_specs=[pl.BlockSpec((dec_seq, nb, D_NSA), full3),
                  pl.BlockSpec((dec_seq, nb, D_CONV), cblk(COL_CB)),
                  pl.BlockSpec((dec_seq, nb, D_CONV), cblk(COL_CC)),
                  pl.BlockSpec((dec_seq, nb, D_CONV), cblk(COL_CH)),
                  pl.BlockSpec((2, nb, D_CONV), full3),
                  pl.BlockSpec((8, D_CONV), const),
                  pl.BlockSpec((1, D_NSA), const),
                  pl.BlockSpec((1, D_CONV), const),
                  _wspec((D_MODEL, D_MODEL), const, layer),
                  pl.BlockSpec((dec_seq, nb, D_MODEL), full3)],
        out_specs=[pl.BlockSpec((dec_seq, nb, D_MODEL), full3),
                   pl.BlockSpec((2, nb, D_CONV), full3)],
        out_shape=[jax.ShapeDtypeStruct((dec_seq, nb, D_MODEL), F32),
                   jax.ShapeDtypeStruct((2, nb, D_CONV), F32)],
        compiler_params=_cparams(("arbitrary",)),
        name="mixout_sample",
    )(o_nsa, p_s, p_s, p_s, state, cw, gn, gc, w, x)


MEM_SCALE = MEM_HEAD_DIM ** -0.5


def _softmax(s):
    e = jnp.exp(s - jnp.max(s, axis=-1, keepdims=True))
    return e / jnp.sum(e, axis=-1, keepdims=True)


def _mem_attend(x, g, wq, kv, wo):
    qm = _dot(_rms(x, g).astype(BF16), wq).astype(BF16)
    outs = []
    for h in range(MEM_HEADS):
        c0, c1 = h * MEM_HEAD_DIM, (h + 1) * MEM_HEAD_DIM
        p = _softmax(_dot_nt(qm[:, c0:c1], kv[:, c0:c1]) * MEM_SCALE).astype(BF16)
        outs.append(_dot(p, kv[:, D_MEM + c0:D_MEM + c1]))
    return x + _dot(jnp.concatenate(outs, axis=-1).astype(BF16), wo)


def _mem_sample_kernel(q_ref, kv_ref, o_ref, *, bt, mem_len):
    n_rows = q_ref.shape[1]
    stride = 2 * MEM_HEADS
    rowh = lax.broadcasted_iota(jnp.int32, (n_rows, 1), 0) // (n_rows // MEM_HEADS)
    for bb in range(bt):
        kmat = jnp.concatenate([kv_ref[bb, pl.ds(h, mem_len, stride=stride), :]
                                for h in range(MEM_HEADS)], axis=-1).astype(BF16)
        vmat = jnp.concatenate([kv_ref[bb, pl.ds(MEM_HEADS + h, mem_len, stride=stride), :]
                                for h in range(MEM_HEADS)], axis=-1).astype(BF16)
        p = _softmax(_dot_nt(q_ref[bb].astype(BF16), kmat) * MEM_SCALE).astype(BF16)
        o2 = _dot(p, vmat)
        out = jnp.zeros((n_rows, MEM_HEAD_DIM), F32)
        for h in range(MEM_HEADS):
            out = out + jnp.where(rowh == h, o2[:, h * MEM_HEAD_DIM:(h + 1) * MEM_HEAD_DIM], 0.0)
        o_ref[bb] = out


def mem_sample_attn(qbd, cache_mem, layer, bt):
    nb, n_rows, _ = qbd.shape
    rows = cache_mem.shape[2]
    return pl.pallas_call(
        functools.partial(_mem_sample_kernel, bt=bt, mem_len=rows // (2 * MEM_HEADS)),
        grid=(nb // bt,),
        in_specs=[pl.BlockSpec((bt, n_rows, D_MEM), lambda i: (i, 0, 0)),
                  pl.BlockSpec((None, bt, rows, MEM_HEAD_DIM), lambda i: (layer, i, 0, 0))],
        out_specs=pl.BlockSpec((bt, n_rows, MEM_HEAD_DIM), lambda i: (i, 0, 0)),
        out_shape=jax.ShapeDtypeStruct((nb, n_rows, MEM_HEAD_DIM), F32),
        compiler_params=_cparams(("arbitrary",)),
        name="mem_sample_attn",
    )(qbd, cache_mem)


def _silu(a):
    return a * _sigmoid(a)


def _ffn_prompt_kernel(x_ref, xh_ref, g_ref, wg_ref, wu_ref, cw_ref, wd_ref, gf_ref,
                       o_ref, gt_ref, h_s, hh_s, acc_s, *, tiles_per_seq, final):
    j = pl.program_id(1)
    tm = x_ref.shape[0]

    @pl.when(j == 0)
    def _():
        first = (pl.program_id(0) % tiles_per_seq) == 0
        h_s[...] = _rms(x_ref[...], g_ref[...]).astype(BF16)
        hh = jnp.where(first, 0.0, _rms(xh_ref[...], g_ref[...]))
        hh_s[...] = jnp.concatenate([hh, jnp.zeros_like(hh)], axis=0).astype(BF16)
        acc_s[...] = jnp.zeros_like(acc_s)

    h = h_s[...]
    gate = _dot(h, wg_ref[...])
    gate_h = _dot(hh_s[...], wg_ref[...])
    g1, g2 = _shift_rows(gate, gate_h[7:8], gate_h[6:7])
    cw = cw_ref[...]
    a = cw[0:1] * g2 + cw[1:2] * g1 + cw[2:3] * gate
    z = (_silu(a) * _dot(h, wu_ref[...])).astype(BF16)
    acc_s[...] += _dot(z, wd_ref[...])
    gt_ref[...] = gate[tm - 8:, :]

    @pl.when(j == pl.num_programs(1) - 1)
    def _():
        y = x_ref[...] + acc_s[...]
        o_ref[...] = _rms(y, gf_ref[...]) if final else y


def ffn_prompt(x, g, wg, wu, cw, wd, gf, tm, tn, seq, final, layer):
    m = x.shape[0]
    nt = m // tm
    nj = D_FF // tn
    const = lambda i, j: (0, 0)
    return pl.pallas_call(
        functools.partial(_ffn_prompt_kernel, tiles_per_seq=seq // tm, final=final),
        grid=(nt, nj),
        in_specs=[pl.BlockSpec((tm, D_MODEL), lambda i, j: (i, 0)),
                  pl.BlockSpec((8, D_MODEL), lambda i, j: (jnp.maximum(i * (tm // 8) - 1, 0), 0)),
                  pl.BlockSpec((1, D_MODEL), const),
                  _wspec((D_MODEL, tn), lambda i, j: (0, j), layer),
                  _wspec((D_MODEL, tn), lambda i, j: (0, j), layer),
                  pl.BlockSpec((8, tn), lambda i, j: (0, j)),
                  _wspec((tn, D_MODEL), lambda i, j: (j, 0), layer),
                  pl.BlockSpec((1, D_MODEL), const)],
        out_specs=[pl.BlockSpec((tm, D_MODEL), lambda i, j: (i, 0)),
                   pl.BlockSpec((None, 8, tn), lambda i, j: (i, 0, j))],
        out_shape=[jax.ShapeDtypeStruct((m, D_MODEL), F32),
                   jax.ShapeDtypeStruct((nt, 8, D_FF), F32)],
        scratch_shapes=[pltpu.VMEM((tm, D_MODEL), BF16),
                        pltpu.VMEM((16, D_MODEL), BF16),
                        pltpu.VMEM((tm, D_MODEL), F32)],
        compiler_params=_cparams(("arbitrary", "arbitrary")),
        name="ffn_prompt",
    )(x, x, g, wg, wu, cw, wd, gf)


def _ffn_sample_kernel(x_ref, st_ref, g_ref, wg_ref, wu_ref, cw_ref, wd_ref, gf_ref,
                       o_ref, nst_ref, h_s, acc_s, *, dec_seq, final):
    j = pl.program_id(0)

    nb = x_ref.shape[1]

    @pl.when(j == 0)
    def _():
        for t in range(dec_seq):
            h_s[t * nb:(t + 1) * nb, :] = _rms(x_ref[t], g_ref[...]).astype(BF16)
        acc_s[...] = jnp.zeros_like(acc_s)

    h = h_s[...]
    gate = _dot(h, wg_ref[...])
    up = _dot(h, wu_ref[...])
    gcat = [st_ref[0], st_ref[1]] + [gate[t * nb:(t + 1) * nb] for t in range(dec_seq)]
    cw = cw_ref[...]
    a = jnp.concatenate([cw[0:1] * gcat[t] + cw[1:2] * gcat[t + 1] + cw[2:3] * gcat[t + 2]
                         for t in range(dec_seq)], axis=0)
    acc_s[...] += _dot((_silu(a) * up).astype(BF16), wd_ref[...])
    nst_ref[0] = gcat[dec_seq]
    nst_ref[1] = gcat[dec_seq + 1]

    @pl.when(j == pl.num_programs(0) - 1)
    def _():
        for t in range(dec_seq):
            y = x_ref[t] + acc_s[t * nb:(t + 1) * nb, :]
            o_ref[t] = _rms(y, gf_ref[...]) if final else y


def ffn_sample(x, state, g, wg, wu, cw, wd, gf, tn, final, layer):
    dec_seq, nb, _ = x.shape
    full3 = lambda j: (0, 0, 0)
    const = lambda j: (0, 0)
    return pl.pallas_call(
        functools.partial(_ffn_sample_kernel, dec_seq=dec_seq, final=final),
        grid=(D_FF // tn,),
        in_specs=[pl.BlockSpec((dec_seq, nb, D_MODEL), full3),
                  pl.BlockSpec((2, nb, tn), lambda j: (0, 0, j)),
                  pl.BlockSpec((1, D_MODEL), const),
                  _wspec((D_MODEL, tn), lambda j: (0, j), layer),
                  _wspec((D_MODEL, tn), lambda j: (0, j), layer),
                  pl.BlockSpec((8, tn), lambda j: (0, j)),
                  _wspec((tn, D_MODEL), lambda j: (j, 0), layer),
                  pl.BlockSpec((1, D_MODEL), const)],
        out_specs=[pl.BlockSpec((dec_seq, nb, D_MODEL), full3),
                   pl.BlockSpec((2, nb, tn), lambda j: (0, 0, j))],
        out_shape=[jax.ShapeDtypeStruct((dec_seq, nb, D_MODEL), F32),
                   jax.ShapeDtypeStruct((2, nb, D_FF), F32)],
        scratch_shapes=[pltpu.VMEM((dec_seq * nb, D_MODEL), BF16),
                        pltpu.VMEM((dec_seq * nb, D_MODEL), F32)],
        compiler_params=_cparams(("arbitrary",)),
        name="ffn_sample",
    )(x, state, g, wg, wu, cw, wd, gf)


def _overlap(n_cmp, n_sel, rows, cols):
    c0 = np.arange(n_cmp)[:, None] * CMP_STRIDE
    s0 = np.arange(n_sel)[None, :] * SEL_BLOCK
    ov = np.minimum(c0 + CMP_BLOCK, s0 + SEL_BLOCK) - np.maximum(c0, s0)
    out = np.zeros((rows, cols), np.float32)
    out[:n_cmp, :n_sel] = np.clip(ov, 0, None).astype(np.float32) / CMP_BLOCK
    return out


def _block_expand(n_keys):
    return jnp.asarray(np.arange(128)[:, None] == (np.arange(n_keys)[None, :] // SEL_BLOCK), BF16)


def _chunk_perm():
    row = np.arange(PAGE)
    return jnp.asarray((row[:, None] % RPC) * CMP_STRIDE + row[:, None] // RPC == row[None, :], BF16)


def _pad_rows(a, rows):
    return jnp.concatenate([a, jnp.zeros((rows - a.shape[0],) + a.shape[1:], a.dtype)], axis=0)


def _prep_w_in(w_in_l):
    offs = np.cumsum((D_NSA, KV_W, KV_W, KV_W, N_BRANCH * N_HEADS, D_CONV, D_CONV, D_CONV))
    q, kc, ks, kw, gl, cb, cc, ch = jnp.split(w_in_l, [int(o) for o in offs[:-1]], axis=-1)
    glp = jnp.concatenate([gl, jnp.zeros((D_MODEL, D_KVT - ROW_G - N_BRANCH * N_HEADS), F32)], axis=-1)
    w_main = jnp.concatenate([q * (HEAD_DIM ** -0.5), cb, cc, ch], axis=-1).astype(BF16)
    wt_kv = jnp.concatenate([kc, ks, kw, glp], axis=-1).T.astype(BF16)
    return w_main[None], wt_kv[None]


def _prep_cmp(w_cmp_l, pe_cmp_l):
    blocks = [jnp.pad(w_cmp_l[c], ((0, 0), (0, 0), ((c * N_KV + g) * HEAD_DIM,
                                                     KV_W - (c * N_KV + g + 1) * HEAD_DIM)))
              for c in range(2) for g in range(N_KV)]
    wbd = jnp.concatenate(blocks, axis=1).astype(BF16)
    pe = pe_cmp_l.reshape(2, 2, CMP_STRIDE, HEAD_DIM).transpose(1, 0, 3, 2)
    pet = jnp.broadcast_to(pe[:, :, None, :, None, :], (2, 2, N_KV, HEAD_DIM, RPC, CMP_STRIDE))
    return wbd, pet.reshape(2, KV_W, PAGE)


TM_PROJ = 1024
TN_PROJ = 1024
TM_MIX = 256
TM_FFN = 512
TN_FFN = 512
BT_CMP = 4
BT_NSA = 4
BT_MEM = 8
NEW_PAD = 16


def _kv_rows(kvt, row, start=0):
    slab = kvt[:, row:row + KV_W, start:]
    b, _, t = slab.shape
    return slab.reshape(b, 2, N_KV, HEAD_DIM, t).transpose(0, 4, 1, 2, 3)


def _layer_prompt(x, mem_kv, lw, consts, layer, batch, seq, final):
    p_main, kvt = proj_in(x, lw['g_mix'], lw['w_main'], lw['wt_kv'], TM_PROJ, TN_PROJ, seq, 0)
    cmp = compress_prompt(kvt, lw['pet'], consts['perm'], lw['wbd'])
    n_cmp_pad = seq // CMP_STRIDE
    cmp_t = cmp.reshape(batch, n_cmp_pad, 2, N_KV, HEAD_DIM).transpose(0, 2, 3, 1, 4)
    o_nsa = nsa_prompt(p_main, kvt, cmp_t, consts['ovt_p'], consts['expand_p'])
    x2, u_tail = mixout_prompt(o_nsa, p_main, lw['conv_w'], lw['g_out_nsa'], lw['g_out_conv'],
                               lw['w_out'], x, lw['g_mem'], lw['w_mem_q'], mem_kv, lw['w_mem_o'],
                               TM_MIX, seq, layer)
    x3, g_tail = ffn_prompt(x2, lw['g_ffn'], lw['w_ff_gate'], lw['w_ff_up'], lw['ffn_conv_w'],
                            lw['w_ff_down'], lw['g_final'], TM_FFN, TN_FFN, seq, final, layer)
    conv_state = u_tail.reshape(batch, seq // TM_MIX, 8, D_CONV)[:, -1, 6:8]
    ffn_state = g_tail.reshape(batch, seq // TM_FFN, 8, D_FF)[:, -1, 6:8]
    return (x3, _kv_rows(kvt, ROW_KC), _kv_rows(kvt, ROW_KS),
            _kv_rows(kvt, ROW_KW, seq - min(WINDOW, seq)), conv_state, ffn_state)


def _layer_sample(x, lw, consts, layer, cache_cmp, cache_sel, page_idx, cache_win, cache_mem,
                  st_conv, st_ffn, final):
    dec_seq, nb, _ = x.shape
    rows = dec_seq * nb
    p_s, kvt = proj_in(x.reshape(rows, D_MODEL), lw['g_mix'], lw['w_main'], lw['wt_kv'],
                       rows, TN_PROJ, rows, 0)
    p3 = p_s.reshape(dec_seq, nb, D_MAIN)
    kvn = kvt.reshape(D_KVT, rows).T.reshape(dec_seq, nb, D_KVT).transpose(1, 0, 2)
    kvn_pad = jnp.concatenate([kvn, jnp.zeros((nb, NEW_PAD - dec_seq, D_KVT), F32)], axis=1)
    cmp_s = compress_paged(cache_cmp, page_idx, lw['pet'], consts['perm'], lw['wbd'], BT_CMP)
    q = p3[:, :, COL_Q:COL_Q + D_NSA].reshape(dec_seq, nb, N_KV, HPG, HEAD_DIM).transpose(1, 3, 2, 0, 4)
    gsel = jnp.eye(N_KV, dtype=F32)
    qbd = (q[:, :, :, :, None, :] * gsel[None, None, :, None, :, None]).reshape(
        nb, HPG * N_KV * dec_seq, KV_HALF)
    gates = kvn[:, :, ROW_G:ROW_G + N_BRANCH * N_HEADS].reshape(
        nb, dec_seq, N_KV, HPG, N_BRANCH).transpose(0, 3, 2, 1, 4).reshape(nb, HPG * N_KV * dec_seq, N_BRANCH)
    o, win_new = nsa_sample(cache_sel, page_idx, qbd, gates, cmp_s, kvn_pad, cache_win, layer,
                            consts['ov_s'], consts['expand_s'], BT_NSA)
    o_nsa = o.reshape(nb, HPG, N_KV, dec_seq, HEAD_DIM).transpose(3, 0, 2, 1, 4).reshape(dec_seq, nb, D_NSA)
    x1, conv_state = mixout_sample(o_nsa, p3, st_conv, lw['conv_w'], lw['g_out_nsa'],
                                   lw['g_out_conv'], lw['w_out'], x, layer)
    qm = rms_matmul(x1.reshape(rows, D_MODEL), lw['g_mem'], lw['w_mem_q'], rows, D_MEM, layer)
    qh = qm.reshape(dec_seq, nb, MEM_HEADS, MEM_HEAD_DIM).transpose(1, 2, 0, 3)
    qmbd = (qh[:, :, :, None, :] * jnp.eye(MEM_HEADS, dtype=F32)[None, :, None, :, None]).reshape(
        nb, MEM_HEADS * dec_seq, D_MEM)
    om = mem_sample_attn(qmbd, cache_mem, layer, BT_MEM)
    om = om.reshape(nb, MEM_HEADS, dec_seq, MEM_HEAD_DIM).transpose(2, 0, 1, 3).reshape(rows, D_MEM)
    x2 = matmul_residual(x1.reshape(rows, D_MODEL), om, lw['w_mem_o'], rows, layer).reshape(
        dec_seq, nb, D_MODEL)
    x3, ffn_state = ffn_sample(x2, st_ffn, lw['g_ffn'], lw['w_ff_gate'], lw['w_ff_up'],
                               lw['ffn_conv_w'], lw['w_ff_down'], lw['g_final'], TN_FFN, final, layer)
    kv_c = kvn[:, :, ROW_KC:ROW_KC + KV_W].reshape(nb, dec_seq, 2, N_KV, HEAD_DIM)
    kv_s = kvn[:, :, ROW_KS:ROW_KS + KV_W].reshape(nb, dec_seq, 2, N_KV, HEAD_DIM)
    return (x3, kv_c, kv_s, win_new.transpose(0, 4, 1, 2, 3),
            conv_state.transpose(1, 0, 2), ffn_state.transpose(1, 0, 2))


def kernel(x_prompt, x_sample, cache_cmp_kv, cache_sel_kv, cache_win_kv, cache_mem_kv,
           state_conv, state_ffn_conv, page_table, mem_prompt,
           g_mix, w_in, w_cmp, pe_cmp, conv_w, g_out_nsa, g_out_conv, w_out,
           g_mem_src, w_mem_kv, g_mem, w_mem_q, w_mem_o, g_ffn, w_ff_gate, w_ff_up,
           ffn_conv_w, w_ff_down, g_final):
    batch, seq, _ = x_prompt.shape
    nb, dec_seq, _ = x_sample.shape
    depth = w_in.shape[0]
    n_phys = cache_cmp_kv.shape[1]
    npages = page_table.shape[1]
    past = npages * PAGE
    win_buf = cache_win_kv.shape[2]
    mem_len = mem_prompt.shape[1]

    n_cmp_s = (past + dec_seq - CMP_BLOCK) // CMP_STRIDE + 1
    n_sel_s = -(-(past + dec_seq) // SEL_BLOCK)
    n_cmp_p = (seq - CMP_BLOCK) // CMP_STRIDE + 1
    n_sel_p = seq // SEL_BLOCK
    consts = {
        'ov_s': jnp.asarray(_overlap(n_cmp_s, n_sel_s, past // CMP_STRIDE, 128), BF16),
        'expand_s': _block_expand(past + PAGE),
        'ovt_p': jnp.asarray(_overlap(n_cmp_p, n_sel_p, seq // CMP_STRIDE, n_sel_p).T, BF16),
        'expand_p': _block_expand(seq),
        'perm': _chunk_perm(),
    }

    cache_cmp = cache_cmp_kv.transpose(0, 1, 3, 4, 5, 2).reshape(depth * n_phys, 2, N_KV, HEAD_DIM, PAGE)
    cache_sel = cache_sel_kv.transpose(0, 1, 3, 4, 5, 2).reshape(depth * n_phys, 2, N_KV, HEAD_DIM, PAGE)
    cache_win = cache_win_kv.transpose(0, 1, 3, 4, 5, 2)
    cache_mem = cache_mem_kv.reshape(depth, nb, mem_len * 2 * MEM_HEADS, MEM_HEAD_DIM)

    xp = x_prompt.reshape(batch * seq, D_MODEL)
    xs = x_sample.transpose(1, 0, 2)
    mem_rows = mem_prompt.reshape(batch * mem_len, D_MODEL)
    outs = [[] for _ in range(11)]
    stacked = {
        'w_out': w_out.astype(BF16), 'w_mem_q': w_mem_q.astype(BF16), 'w_mem_o': w_mem_o.astype(BF16),
        'w_ff_gate': w_ff_gate.astype(BF16), 'w_ff_up': w_ff_up.astype(BF16),
        'w_ff_down': w_ff_down.astype(BF16),
    }
    w_mem_kv_b = w_mem_kv.astype(BF16)
    for l in range(depth):
        wbd, pet = _prep_cmp(w_cmp[l], pe_cmp[l])
        w_main, wt_kv = _prep_w_in(w_in[l])
        lw = dict(stacked)
        lw.update({
            'g_mix': g_mix[l][None], 'w_main': w_main, 'wt_kv': wt_kv, 'wbd': wbd, 'pet': pet,
            'conv_w': _pad_rows(conv_w[l], 8), 'g_out_nsa': g_out_nsa[l][None],
            'g_out_conv': g_out_conv[l][None], 'g_mem': g_mem[l][None], 'g_ffn': g_ffn[l][None],
            'ffn_conv_w': _pad_rows(ffn_conv_w[l], 8), 'g_final': g_final[None],
        })
        final = l == depth - 1
        mem_kv_p = rms_matmul(mem_rows, g_mem_src[l][None], w_mem_kv_b,
                              batch * mem_len, 2 * D_MEM, l).reshape(batch, mem_len, 2 * D_MEM)
        xp, kc, ks, kw, cst, fst = _layer_prompt(xp, mem_kv_p, lw, consts, l, batch, seq, final)
        page_idx = page_table + l * n_phys
        xs, kc2, ks2, kw2, cst2, fst2 = _layer_sample(
            xs, lw, consts, l, cache_cmp, cache_sel, page_idx, cache_win, cache_mem,
            state_conv[l].transpose(1, 0, 2), state_ffn_conv[l].transpose(1, 0, 2), final)
        for lst, val in zip(outs, (kc, ks, kw, mem_kv_p, cst, fst, kc2, ks2, kw2, cst2, fst2)):
            lst.append(val)

    st = [jnp.stack(o) for o in outs]
    return (xp.reshape(batch, seq, D_MODEL), xs.transpose(1, 0, 2),
            st[0], st[1], st[2],
            st[3].reshape(depth, batch, mem_len, 2, MEM_HEADS, MEM_HEAD_DIM),
            st[4], st[5], st[6], st[7], st[8], st[9], st[10])
```

```python
import functools

import numpy as np
import jax
import jax.numpy as jnp
from jax import lax
from jax.experimental import pallas as pl
from jax.experimental.pallas import tpu as pltpu

F32 = jnp.float32
BF16 = jnp.bfloat16

D_MODEL = 2048
D_NSA = 1024
D_CONV = 1024
HEAD_DIM = 64
N_HEADS = 16
N_KV = 2
HPG = 8
N_BRANCH = 3
KV_W = 2 * N_KV * HEAD_DIM
KV_HALF = N_KV * HEAD_DIM
CMP_BLOCK = 32
CMP_STRIDE = 16
SEL_BLOCK = 64
TOPK = 16
WINDOW = 512
Q_BLOCK = 128
FORCE_BONUS = 1e4
D_FF = 5632
MEM_HEADS = 4
MEM_HEAD_DIM = 128
D_MEM = 512
PAGE = 128
EPS = 1e-6
NEG = -1e30

COL_Q = 0
COL_CB = 1024
COL_CC = 2048
COL_CH = 3072
D_MAIN = 4096
ROW_KC = 0
ROW_KS = 256
ROW_KW = 512
ROW_G = 768
D_KVT = 896

VMEM_LIMIT = 56 * 1024 * 1024


def _cparams(sem):
    return pltpu.CompilerParams(dimension_semantics=sem, vmem_limit_bytes=VMEM_LIMIT)


def _rms(x, g):
    return x * lax.rsqrt(jnp.mean(x * x, axis=-1, keepdims=True) + EPS) * g


def _dot(a, b):
    return jnp.dot(a, b, preferred_element_type=F32)


def _dot_nt(a, b):
    return lax.dot_general(a, b, (((1,), (1,)), ((), ())), preferred_element_type=F32)


def _softmax_bias(s, bias):
    s = s + bias
    e = jnp.exp(s - jnp.max(s, axis=-1, keepdims=True))
    return e, jnp.sum(e, axis=-1, keepdims=True)


def _sigmoid(x):
    return 1.0 / (1.0 + jnp.exp(-x))


def _rms_matmul_kernel(x_ref, g_ref, w_ref, o_ref, h_ref):
    @pl.when(pl.program_id(1) == 0)
    def _():
        h_ref[...] = _rms(x_ref[...], g_ref[...]).astype(BF16)

    o_ref[...] = _dot(h_ref[...], w_ref[...]).astype(o_ref.dtype)


def _wspec(block, imap, layer):
    return pl.BlockSpec((None,) + block, lambda *a: (layer,) + imap(*a))


def rms_matmul(x, g, w, tm, tn, layer):
    m, d = x.shape
    n = w.shape[2]
    return pl.pallas_call(
        _rms_matmul_kernel,
        grid=(m // tm, n // tn),
        in_specs=[pl.BlockSpec((tm, d), lambda i, j: (i, 0)),
                  pl.BlockSpec((1, d), lambda i, j: (0, 0)),
                  _wspec((d, tn), lambda i, j: (0, j), layer)],
        out_specs=pl.BlockSpec((tm, tn), lambda i, j: (i, j)),
        out_shape=jax.ShapeDtypeStruct((m, n), F32),
        scratch_shapes=[pltpu.VMEM((tm, d), BF16)],
        compiler_params=_cparams(("arbitrary", "arbitrary")),
        name="rms_matmul",
    )(x, g, w)


def _proj_in_kernel(x_ref, g_ref, w_ref, wt_ref, o_ref, ot_ref, h_ref):
    @pl.when(pl.program_id(1) == 0)
    def _():
        h = _rms(x_ref[...], g_ref[...]).astype(BF16)
        h_ref[...] = h
        ot_ref[...] = _dot_nt(wt_ref[...], h)

    o_ref[...] = _dot(h_ref[...], w_ref[...])


def proj_in(x, g, w_main, wt_kv, tm, tn, seq, layer):
    m, d = x.shape
    tps = seq // tm
    return pl.pallas_call(
        _proj_in_kernel,
        grid=(m // tm, D_MAIN // tn),
        in_specs=[pl.BlockSpec((tm, d), lambda i, j: (i, 0)),
                  pl.BlockSpec((1, d), lambda i, j: (0, 0)),
                  _wspec((d, tn), lambda i, j: (0, j), layer),
                  _wspec((D_KVT, d), lambda i, j: (0, 0), layer)],
        out_specs=[pl.BlockSpec((tm, tn), lambda i, j: (i, j)),
                   pl.BlockSpec((None, D_KVT, tm), lambda i, j: (i // tps, 0, i % tps))],
        out_shape=[jax.ShapeDtypeStruct((m, D_MAIN), F32),
                   jax.ShapeDtypeStruct((m // seq, D_KVT, seq), F32)],
        scratch_shapes=[pltpu.VMEM((tm, d), BF16)],
        compiler_params=_cparams(("arbitrary", "arbitrary")),
        name="proj_in",
    )(x, g, w_main, wt_kv)


def _matmul_res_kernel(x_ref, a_ref, w_ref, o_ref):
    o_ref[...] = x_ref[...] + _dot(a_ref[...].astype(BF16), w_ref[...])


def matmul_residual(x, a, w, tm, layer):
    m, n = x.shape
    k = a.shape[1]
    return pl.pallas_call(
        _matmul_res_kernel,
        grid=(m // tm,),
        in_specs=[pl.BlockSpec((tm, n), lambda i: (i, 0)),
                  pl.BlockSpec((tm, k), lambda i: (i, 0)),
                  _wspec((k, n), lambda i: (0, 0), layer)],
        out_specs=pl.BlockSpec((tm, n), lambda i: (i, 0)),
        out_shape=jax.ShapeDtypeStruct((m, n), F32),
        compiler_params=_cparams(("arbitrary",)),
        name="matmul_residual",
    )(x, a, w)


RPC = PAGE // CMP_STRIDE


def _compress_chunks(chunk_at, n_chunks, pet_ref, perm_ref, w_ref, xs_ref):
    perm = perm_ref[...]
    for ci in range(n_chunks):
        chunk = chunk_at(ci)
        for a in range(2):
            xs_ref[a, ci] = _dot_nt(perm, (chunk + pet_ref[a]).astype(BF16))
    rows = n_chunks * RPC
    acc0 = jnp.zeros((rows, KV_W), F32)
    acc1 = jnp.zeros((rows, KV_W), F32)
    for r in range(CMP_STRIDE):
        x0 = xs_ref[0, :, r * RPC:(r + 1) * RPC, :].reshape(rows, KV_W).astype(BF16)
        x1 = xs_ref[1, :, r * RPC:(r + 1) * RPC, :].reshape(rows, KV_W).astype(BF16)
        acc0 = acc0 + _dot(x0, w_ref[r])
        acc1 = acc1 + _dot(x1, w_ref[CMP_STRIDE + r])
    return acc0 + pltpu.roll(acc1, rows - 1, 0)


def _compress_prompt_kernel(kvt_ref, pet_ref, perm_ref, w_ref, o_ref, xs_ref, *, n_chunks):
    out = _compress_chunks(lambda ci: kvt_ref[:, ci * PAGE:(ci + 1) * PAGE], n_chunks,
                           pet_ref, perm_ref, w_ref, xs_ref)
    o_ref[...] = out.astype(BF16)


def compress_prompt(kvt, pet, perm, wbd):
    batch, _, seq = kvt.shape
    n_chunks = seq // PAGE
    nblk = seq // CMP_STRIDE
    return pl.pallas_call(
        functools.partial(_compress_prompt_kernel, n_chunks=n_chunks),
        grid=(batch,),
        in_specs=[pl.BlockSpec((None, KV_W, seq), lambda b: (b, ROW_KC // KV_W, 0)),
                  pl.BlockSpec((2, KV_W, PAGE), lambda b: (0, 0, 0)),
                  pl.BlockSpec((PAGE, PAGE), lambda b: (0, 0)),
                  pl.BlockSpec((CMP_BLOCK, KV_W, KV_W), lambda b: (0, 0, 0))],
        out_specs=pl.BlockSpec((None, nblk, KV_W), lambda b: (b, 0, 0)),
        out_shape=jax.ShapeDtypeStruct((batch, nblk, KV_W), BF16),
        scratch_shapes=[pltpu.VMEM((2, n_chunks, PAGE, KV_W), F32)],
        compiler_params=_cparams(("arbitrary",)),
        name="compress_prompt",
    )(kvt, pet, perm, wbd)


def _compress_paged_kernel(pt_ref, *refs, bt, npages):
    page_refs = refs[:bt * npages]
    pet_ref, perm_ref, w_ref, o_ref, xs_ref = refs[bt * npages:]
    out = _compress_chunks(lambda ci: page_refs[ci][...].reshape(KV_W, PAGE), bt * npages,
                           pet_ref, perm_ref, w_ref, xs_ref)
    o_ref[...] = out.astype(BF16).reshape(bt, npages * RPC, KV_W)


def compress_paged(cache, page_idx, pet, perm, wbd, bt):
    nb, npages = page_idx.shape
    nblk = npages * RPC
    page_specs = [
        pl.BlockSpec((None, 2, N_KV, HEAD_DIM, PAGE), functools.partial(
            lambda i, pt, bb, p: (pt[i * bt + bb, p], 0, 0, 0, 0), bb=bb, p=p))
        for bb in range(bt) for p in range(npages)]
    grid_spec = pltpu.PrefetchScalarGridSpec(
        num_scalar_prefetch=1,
        grid=(nb // bt,),
        in_specs=page_specs + [pl.BlockSpec((2, KV_W, PAGE), lambda i, pt: (0, 0, 0)),
                               pl.BlockSpec((PAGE, PAGE), lambda i, pt: (0, 0)),
                               pl.BlockSpec((CMP_BLOCK, KV_W, KV_W), lambda i, pt: (0, 0, 0))],
        out_specs=pl.BlockSpec((bt, nblk, KV_W), lambda i, pt: (i, 0, 0)),
        scratch_shapes=[pltpu.VMEM((2, bt * npages, PAGE, KV_W), F32)])
    return pl.pallas_call(
        functools.partial(_compress_paged_kernel, bt=bt, npages=npages),
        grid_spec=grid_spec,
        out_shape=jax.ShapeDtypeStruct((nb, nblk, KV_W), BF16),
        compiler_params=_cparams(("arbitrary",)),
        name="compress_paged",
    )(page_idx, *([cache] * (bt * npages)), pet, perm, wbd)


def _importance(imp, idx, tpos, n_sel):
    valid = (idx * SEL_BLOCK <= tpos) & (idx < n_sel)
    cur = tpos // SEL_BLOCK
    forced = (idx == 0) | (idx == cur) | (idx == cur - 1)
    return jnp.where(valid, imp + jnp.where(forced, FORCE_BONUS, 0.0), NEG)


def _topk_rows(imp, n_cand):
    groups = [imp[v * 8:(v + 1) * 8] for v in range(n_cand // 8)]
    sub = lax.broadcasted_iota(jnp.int32, groups[0].shape, 0)
    ranks = [jnp.zeros(g.shape, F32) for g in groups]
    for jp in range(n_cand):
        c = imp[jp:jp + 1, :]
        for v, g in enumerate(groups):
            ge = jnp.where(c >= g, 1.0, 0.0)
            gt = jnp.where(c > g, 1.0, 0.0)
            if v * 8 > jp:
                beats = ge
            elif v * 8 + 7 < jp:
                beats = gt
            else:
                beats = jnp.where(sub > jp - v * 8, ge, gt)
            ranks[v] = ranks[v] + beats
    return jnp.concatenate([jnp.where(r < TOPK, 1.0, 0.0) for r in ranks], axis=0)


def _topk_lanes(imp, idx, n_cand):
    rank = jnp.zeros(imp.shape, F32)
    for jp in range(n_cand):
        c = imp[:, jp:jp + 1]
        rank = rank + jnp.where(idx > jp, jnp.where(c >= imp, 1.0, 0.0), jnp.where(c > imp, 1.0, 0.0))
    return jnp.where(rank < TOPK, 1.0, 0.0)


KEY_TILE = 512
V_PAD = 16


def _tile_heads(x):
    return jnp.concatenate([x] * HPG, axis=0)


def _nsa_prompt_kernel(q_ref, gate_ref, cmp_ref, kst_ref, vst_ref, kwt_ref, vwt_ref, ovt_ref, exp_ref,
                       o_ref, ksb, vsb, kwb, vwb, *, seq):
    g = pl.program_id(1)
    i = pl.program_id(2)
    n_cmp_pad = seq // CMP_STRIDE
    n_sel = seq // SEL_BLOCK

    @pl.when(i == 0)
    def _():
        ones_row = jnp.where(lax.broadcasted_iota(jnp.int32, (V_PAD, seq), 0) == 0, 1.0, 0.0).astype(BF16)
        ksb[...] = kst_ref[...].astype(BF16)
        vsb[0:HEAD_DIM, :] = vst_ref[...].astype(BF16)
        vsb[HEAD_DIM:HEAD_DIM + V_PAD, :] = ones_row
        kwb[...] = kwt_ref[...].astype(BF16)
        vwb[0:HEAD_DIM, :] = vwt_ref[...].astype(BF16)
        vwb[HEAD_DIM:HEAD_DIM + V_PAD, :] = ones_row

    t0 = i * Q_BLOCK
    qb = q_ref[...]
    qs = jnp.concatenate([qb[:, h * HEAD_DIM:(h + 1) * HEAD_DIM] for h in range(HPG)],
                         axis=0).astype(BF16)
    tcol = lax.broadcasted_iota(jnp.int32, (Q_BLOCK, 1), 0) + t0

    ncol = lax.broadcasted_iota(jnp.int32, (Q_BLOCK, n_cmp_pad), 1)
    vis = jnp.where(ncol < n_cmp_pad - 1, ncol * CMP_STRIDE + (CMP_BLOCK - 1), seq) <= tcol
    e_c, l_c = _softmax_bias(_dot_nt(qs, cmp_ref[0]), _tile_heads(jnp.where(vis, 0.0, NEG)))
    any_vis = _tile_heads(jnp.where(tcol >= CMP_BLOCK - 1, 1.0, 0.0))
    p_c = (e_c * (any_vis / jnp.maximum(l_c, 1e-30))).astype(BF16)
    o_c = _dot(p_c, cmp_ref[1])

    po = _dot_nt(ovt_ref[...], p_c)
    imp_t = po[:, 0:Q_BLOCK]
    for h in range(1, HPG):
        imp_t = imp_t + po[:, h * Q_BLOCK:(h + 1) * Q_BLOCK]
    jj = lax.broadcasted_iota(jnp.int32, (n_sel, Q_BLOCK), 0)
    tt = lax.broadcasted_iota(jnp.int32, (n_sel, Q_BLOCK), 1) + t0
    sel_t = _topk_rows(_importance(imp_t, jj, tt, n_sel), n_sel)
    sel = jnp.concatenate([sel_t, jnp.zeros((128 - n_sel, Q_BLOCK), F32)], axis=0).T.astype(BF16)

    kk = lax.broadcasted_iota(jnp.int32, (Q_BLOCK, KEY_TILE), 1)

    def scores(c):
        k0 = pl.multiple_of(c * KEY_TILE, KEY_TILE)
        msel = _dot(sel, exp_ref[:, pl.ds(k0, KEY_TILE)])
        bias = jnp.where(msel > 0.5, jnp.where(kk + k0 <= tcol, 0.0, NEG), NEG)
        return _dot(qs, ksb[:, pl.ds(k0, KEY_TILE)]) + _tile_heads(bias)

    def accumulate(c, s, carry):
        m, acc = carry
        k0 = pl.multiple_of(c * KEY_TILE, KEY_TILE)
        m_new = jnp.maximum(m, jnp.max(s, axis=-1, keepdims=True))
        p = jnp.exp(s - m_new).astype(BF16)
        return m_new, jnp.exp(m - m_new) * acc + _dot_nt(p, vsb[:, pl.ds(k0, KEY_TILE)])

    def tile_pair(c2, carry):
        s_a = scores(2 * c2)
        s_b = scores(2 * c2 + 1)
        return accumulate(2 * c2 + 1, s_b, accumulate(2 * c2, s_a, carry))

    n_rows = HPG * Q_BLOCK
    m0 = jnp.full((n_rows, 1), NEG, F32)
    a0 = jnp.zeros((n_rows, HEAD_DIM + V_PAD), F32)
    n_tiles = (t0 + Q_BLOCK + KEY_TILE - 1) // KEY_TILE
    def tile_quad(c4, carry):
        ss = [scores(4 * c4 + u) for u in range(4)]
        for u in range(4):
            carry = accumulate(4 * c4 + u, ss[u], carry)
        return carry

    carry = lax.fori_loop(0, n_tiles // 4, tile_quad, (m0, a0))
    carry = lax.cond(n_tiles % 4 >= 2, lambda cr: tile_pair((n_tiles // 4) * 2, cr), lambda cr: cr, carry)
    _, acc_s = lax.cond(n_tiles % 2 == 1,
                        lambda cr: accumulate(n_tiles - 1, scores(n_tiles - 1), cr),
                        lambda cr: cr, carry)
    o_s = acc_s[:, 0:HEAD_DIM] / acc_s[:, HEAD_DIM:HEAD_DIM + 1]

    span = WINDOW + Q_BLOCK
    ws = pl.multiple_of(jnp.maximum(t0 - WINDOW, 0), Q_BLOCK)
    dlt = tcol - (lax.broadcasted_iota(jnp.int32, (Q_BLOCK, span), 1) + ws)
    bias_w = jnp.where(dlt >= 0, jnp.where(dlt < WINDOW, 0.0, NEG), NEG)
    s_w = _dot(qs, kwb[:, pl.ds(ws, span)]) + _tile_heads(bias_w)
    e_w = jnp.exp(s_w - jnp.max(s_w, axis=-1, keepdims=True)).astype(BF16)
    acc_w = _dot_nt(e_w, vwb[:, pl.ds(ws, span)])
    o_w = acc_w[:, 0:HEAD_DIM] / acc_w[:, HEAD_DIM:HEAD_DIM + 1]

    gs = _sigmoid(gate_ref[...]).T
    gsel = jnp.where(g == 0, gs[:, 0:HPG * N_BRANCH], gs[:, HPG * N_BRANCH:2 * HPG * N_BRANCH])
    for h in range(HPG):
        r0, r1 = h * Q_BLOCK, (h + 1) * Q_BLOCK
        c0 = h * N_BRANCH
        o_ref[:, h * HEAD_DIM:(h + 1) * HEAD_DIM] = (
            o_c[r0:r1] * gsel[:, c0:c0 + 1] + o_s[r0:r1] * gsel[:, c0 + 1:c0 + 2]
            + o_w[r0:r1] * gsel[:, c0 + 2:c0 + 3])


def nsa_prompt(p_main, kvt, cmp_t, ovt, expand):
    batch, _, seq = kvt.shape
    nqb = seq // Q_BLOCK
    gw = HPG * HEAD_DIM
    n_cmp_pad = seq // CMP_STRIDE
    n_sel = seq // SEL_BLOCK
    kvt_spec = lambda row: pl.BlockSpec((None, HEAD_DIM, seq),
                                        lambda b, g, i: (b, row // HEAD_DIM + g, 0))
    return pl.pallas_call(
        functools.partial(_nsa_prompt_kernel, seq=seq),
        grid=(batch, N_KV, nqb),
        in_specs=[
            pl.BlockSpec((Q_BLOCK, gw), lambda b, g, i: (b * nqb + i, g)),
            pl.BlockSpec((None, 128, Q_BLOCK), lambda b, g, i: (b, ROW_G // 128, i)),
            pl.BlockSpec((None, 2, None, n_cmp_pad, HEAD_DIM), lambda b, g, i: (b, 0, g, 0, 0)),
            kvt_spec(ROW_KS), kvt_spec(ROW_KS + KV_HALF),
            kvt_spec(ROW_KW), kvt_spec(ROW_KW + KV_HALF),
            pl.BlockSpec((n_sel, n_cmp_pad), lambda b, g, i: (0, 0)),
            pl.BlockSpec((128, seq), lambda b, g, i: (0, 0)),
        ],
        out_specs=pl.BlockSpec((Q_BLOCK, gw), lambda b, g, i: (b * nqb + i, g)),
        out_shape=jax.ShapeDtypeStruct((batch * seq, D_NSA), F32),
        scratch_shapes=[pltpu.VMEM((HEAD_DIM, seq), BF16), pltpu.VMEM((HEAD_DIM + V_PAD, seq), BF16)] * 2,
        compiler_params=_cparams(("arbitrary", "arbitrary", "arbitrary")),
        name="nsa_prompt",
    )(p_main, kvt, cmp_t, kvt, kvt, kvt, kvt, ovt, expand)


def _pick_group(o2, rg):
    return jnp.where(rg == 0, o2[:, 0:HEAD_DIM], o2[:, HEAD_DIM:2 * HEAD_DIM])


def _masked_softmax(s, mask):
    s = jnp.where(mask, s, NEG)
    m = jnp.max(s, axis=-1, keepdims=True)
    e = jnp.where(mask, jnp.exp(s - m), 0.0)
    return e / jnp.maximum(jnp.sum(e, axis=-1, keepdims=True), 1e-30)


def _masked_softmax2(s_a, mask_a, s_b, mask_b):
    s_a = jnp.where(mask_a, s_a, NEG)
    s_b = jnp.where(mask_b, s_b, NEG)
    m = jnp.maximum(jnp.max(s_a, axis=-1, keepdims=True), jnp.max(s_b, axis=-1, keepdims=True))
    e_a = jnp.where(mask_a, jnp.exp(s_a - m), 0.0)
    e_b = jnp.where(mask_b, jnp.exp(s_b - m), 0.0)
    tot = jnp.sum(e_a, axis=-1, keepdims=True) + jnp.sum(e_b, axis=-1, keepdims=True)
    inv = 1.0 / jnp.maximum(tot, 1e-30)
    return (e_a * inv).astype(BF16), (e_b * inv).astype(BF16)


def _nsa_sample_kernel(pt_ref, *refs, bt, npages, dec_seq):
    page_refs = refs[:bt * npages]
    (q_ref, gate_ref, cmp_ref, new_ref, win_ref, ov_ref, exp_ref,
     o_ref, wout_ref, kts, vts, ktw, vtw) = refs[bt * npages:]
    past = npages * PAGE
    n_cmp = (past + dec_seq - CMP_BLOCK) // CMP_STRIDE + 1
    n_sel = -(-(past + dec_seq) // SEL_BLOCK)
    n_rows = HPG * N_KV * dec_seq
    win_buf = win_ref.shape[-1]
    n_cmp_pad = cmp_ref.shape[1]
    n_new = new_ref.shape[1]
    gt = N_KV * dec_seq

    rowi = lax.broadcasted_iota(jnp.int32, (n_rows, 1), 0)
    rg = (rowi // dec_seq) % N_KV
    tpos = past + rowi % dec_seq
    ncol = lax.broadcasted_iota(jnp.int32, (n_rows, n_cmp_pad), 1)
    m_c = (ncol * CMP_STRIDE + (CMP_BLOCK - 1) <= tpos) & (ncol < n_cmp)
    jj = lax.broadcasted_iota(jnp.int32, (gt, 128), 1)
    t8 = past + lax.broadcasted_iota(jnp.int32, (gt, 128), 0) % dec_seq
    d_new = tpos - (past + lax.broadcasted_iota(jnp.int32, (n_rows, n_new), 1))
    m_new = d_new >= 0
    m_wn = m_new & (d_new < WINDOW)
    wcol = lax.broadcasted_iota(jnp.int32, (n_rows, win_buf), 1)
    dlt = tpos - (wcol + (past - win_buf))
    m_w = (dlt >= 0) & (dlt < WINDOW)
    lane = lax.broadcasted_iota(jnp.int32, (KV_W, PAGE), 1)

    bbs = range(bt)

    for bb in bbs:
        for p in range(npages):
            page = page_refs[bb * npages + p]
            kts[bb, :, p * PAGE:(p + 1) * PAGE] = page[0].reshape(KV_HALF, PAGE).astype(BF16)
            vts[bb, :, p * PAGE:(p + 1) * PAGE] = page[1].reshape(KV_HALF, PAGE).astype(BF16)
        win = win_ref[bb].reshape(KV_W, win_buf)
        ktw[bb] = win[0:KV_HALF].astype(BF16)
        vtw[bb] = win[KV_HALF:KV_W].astype(BF16)
        kwn = new_ref[bb, :, ROW_KW:ROW_KW + KV_W]
        kwn_t = jnp.concatenate([kwn, jnp.zeros((PAGE - n_new, KV_W), F32)], axis=0).T
        shifted = pltpu.roll(win, win_buf - dec_seq, 1)
        tail = jnp.where(lane >= PAGE - dec_seq, pltpu.roll(kwn_t, PAGE - dec_seq, 1),
                         shifted[:, win_buf - PAGE:])
        wout_ref[bb] = jnp.concatenate([shifted[:, 0:win_buf - PAGE], tail], axis=1).reshape(
            2, N_KV, HEAD_DIM, win_buf)

    qs = [q_ref[bb].astype(BF16) for bb in bbs]
    new = [new_ref[bb].astype(BF16) for bb in bbs]
    kn_s = [x[:, ROW_KS:ROW_KS + KV_HALF] for x in new]
    vn_s = [x[:, ROW_KS + KV_HALF:ROW_KS + KV_W] for x in new]
    kn_w = [x[:, ROW_KW:ROW_KW + KV_HALF] for x in new]
    vn_w = [x[:, ROW_KW + KV_HALF:ROW_KW + KV_W] for x in new]

    s_c = [_dot_nt(qs[bb], cmp_ref[bb, :, 0:KV_HALF]) for bb in bbs]
    s_w = [_dot(qs[bb], ktw[bb]) for bb in bbs]
    s_wn = [_dot_nt(qs[bb], kn_w[bb]) for bb in bbs]
    p_c = [_masked_softmax(s, m_c).astype(BF16) for s in s_c]
    p_w = [_masked_softmax2(s_w[bb], m_w, s_wn[bb], m_wn) for bb in bbs]
    o_c = [_pick_group(_dot(p_c[bb], cmp_ref[bb, :, KV_HALF:KV_W]), rg) for bb in bbs]
    o_w = [_pick_group(_dot_nt(p_w[bb][0], vtw[bb]) + _dot(p_w[bb][1], vn_w[bb]), rg) for bb in bbs]

    po = [_dot(p, ov_ref[...]) for p in p_c]
    imp = [functools.reduce(lambda a, h: a + x[h * gt:(h + 1) * gt], range(1, HPG), x[0:gt]) for x in po]
    sel8 = [_topk_lanes(_importance(x, jj, t8, n_sel), jj, n_sel) for x in imp]
    msel = [_dot(jnp.concatenate([x] * HPG, axis=0).astype(BF16), exp_ref[...]) for x in sel8]

    s_s = [_dot(qs[bb], kts[bb]) for bb in bbs]
    s_sn = [_dot_nt(qs[bb], kn_s[bb]) for bb in bbs]
    p_s = [_masked_softmax2(s_s[bb], msel[bb][:, 0:past] > 0.5,
                            s_sn[bb], (msel[bb][:, past:past + n_new] > 0.5) & m_new) for bb in bbs]
    o_s = [_pick_group(_dot_nt(p_s[bb][0], vts[bb]) + _dot(p_s[bb][1], vn_s[bb]), rg) for bb in bbs]

    for bb in bbs:
        gs = _sigmoid(gate_ref[bb])
        o_ref[bb] = o_c[bb] * gs[:, 0:1] + o_s[bb] * gs[:, 1:2] + o_w[bb] * gs[:, 2:3]


def nsa_sample(cache_sel, page_idx, qbd, gates, cmp_s, kv_new, cache_win, layer, ov_s, expand, bt):
    nb, npages = page_idx.shape
    n_rows = qbd.shape[1]
    dec_seq = n_rows // (HPG * N_KV)
    win_buf = cache_win.shape[-1]
    n_cmp_pad = cmp_s.shape[1]
    n_new = kv_new.shape[1]
    past = npages * PAGE
    page_specs = [
        pl.BlockSpec((None, 2, N_KV, HEAD_DIM, PAGE), functools.partial(
            lambda i, pt, bb, p: (pt[i * bt + bb, p], 0, 0, 0, 0), bb=bb, p=p))
        for bb in range(bt) for p in range(npages)]
    per_b = lambda i, pt: (i, 0, 0)
    const2 = lambda i, pt: (0, 0)
    win_shape = (2, N_KV, HEAD_DIM, win_buf)
    grid_spec = pltpu.PrefetchScalarGridSpec(
        num_scalar_prefetch=1,
        grid=(nb // bt,),
        in_specs=page_specs + [
            pl.BlockSpec((bt, n_rows, KV_HALF), per_b),
            pl.BlockSpec((bt, n_rows, N_BRANCH), per_b),
            pl.BlockSpec((bt, n_cmp_pad, KV_W), per_b),
            pl.BlockSpec((bt, n_new, D_KVT), per_b),
            pl.BlockSpec((None, bt) + win_shape, lambda i, pt: (layer, i, 0, 0, 0, 0)),
            pl.BlockSpec((n_cmp_pad, 128), const2),
            pl.BlockSpec((128, past + PAGE), const2),
        ],
        out_specs=[pl.BlockSpec((bt, n_rows, HEAD_DIM), per_b),
                   pl.BlockSpec((bt,) + win_shape, lambda i, pt: (i, 0, 0, 0, 0))],
        scratch_shapes=[pltpu.VMEM((bt, KV_HALF, past), BF16)] * 2
        + [pltpu.VMEM((bt, KV_HALF, win_buf), BF16)] * 2)
    return pl.pallas_call(
        functools.partial(_nsa_sample_kernel, bt=bt, npages=npages, dec_seq=dec_seq),
        grid_spec=grid_spec,
        out_shape=[jax.ShapeDtypeStruct((nb, n_rows, HEAD_DIM), F32),
                   jax.ShapeDtypeStruct((nb,) + win_shape, F32)],
        compiler_params=_cparams(("arbitrary",)),
        name="nsa_sample",
    )(page_idx, *([cache_sel] * (bt * npages)), qbd, gates, cmp_s, kv_new, cache_win, ov_s, expand)


def _shift_rows(u, p1, p2):
    r = lax.broadcasted_iota(jnp.int32, (u.shape[0], 1), 0)
    u1 = jnp.where(r >= 1, pltpu.roll(u, 1, 0), p1)
    u2 = jnp.where(r >= 2, pltpu.roll(u, 2, 0), jnp.where(r == 1, p1, p2))
    return u1, u2


def _mix_project(o_nsa, o_conv, gn, gc, w, x):
    mix = jnp.concatenate([_rms(o_nsa, gn), _rms(o_conv, gc)], axis=-1).astype(BF16)
    return x + _dot(mix, w)


def _mixout_prompt_kernel(on_ref, cb_ref, cc_ref, ch_ref, cch_ref, chh_ref, cw_ref, gn_ref, gc_ref,
                          w_ref, x_ref, gm_ref, wq_ref, kv_ref, wo_ref, o_ref, ut_ref, *, tiles_per_seq):
    first = (pl.program_id(0) % tiles_per_seq) == 0
    u = cc_ref[...] * ch_ref[...]
    uh = jnp.where(first, 0.0, cch_ref[...] * chh_ref[...])
    u1, u2 = _shift_rows(u, uh[7:8], uh[6:7])
    cw = cw_ref[...]
    v = cw[0:1] * u2 + cw[1:2] * u1 + cw[2:3] * u
    x1 = _mix_project(on_ref[...], cb_ref[...] * v, gn_ref[...], gc_ref[...], w_ref[...], x_ref[...])
    ut_ref[...] = u[u.shape[0] - 8:, :]
    o_ref[...] = _mem_attend(x1, gm_ref[...], wq_ref[...], kv_ref[...].astype(BF16), wo_ref[...])


def mixout_prompt(o_nsa, p_main, cw, gn, gc, w, x, g_mem, wq, mem_kv, wo, tm, seq, layer):
    m = x.shape[0]
    nt = m // tm
    tps = seq // tm
    mem_len = mem_kv.shape[1]
    cblk = lambda c: (lambda i: (i, c // D_CONV))
    halo = lambda c: (lambda i: (jnp.maximum(i * (tm // 8) - 1, 0), c // D_CONV))
    const = lambda i: (0, 0)
    return pl.pallas_call(
        functools.partial(_mixout_prompt_kernel, tiles_per_seq=tps),
        grid=(nt,),
        in_specs=[pl.BlockSpec((tm, D_NSA), lambda i: (i, 0)),
                  pl.BlockSpec((tm, D_CONV), cblk(COL_CB)),
                  pl.BlockSpec((tm, D_CONV), cblk(COL_CC)),
                  pl.BlockSpec((tm, D_CONV), cblk(COL_CH)),
                  pl.BlockSpec((8, D_CONV), halo(COL_CC)),
                  pl.BlockSpec((8, D_CONV), halo(COL_CH)),
                  pl.BlockSpec((8, D_CONV), const),
                  pl.BlockSpec((1, D_NSA), const),
                  pl.BlockSpec((1, D_CONV), const),
                  _wspec((D_MODEL, D_MODEL), const, layer),
                  pl.BlockSpec((tm, D_MODEL), lambda i: (i, 0)),
                  pl.BlockSpec((1, D_MODEL), const),
                  _wspec((D_MODEL, D_MEM), const, layer),
                  pl.BlockSpec((None, mem_len, 2 * D_MEM), lambda i: (i // tps, 0, 0)),
                  _wspec((D_MEM, D_MODEL), const, layer)],
        out_specs=[pl.BlockSpec((tm, D_MODEL), lambda i: (i, 0)),
                   pl.BlockSpec((None, 8, D_CONV), lambda i: (i, 0, 0))],
        out_shape=[jax.ShapeDtypeStruct((m, D_MODEL), F32),
                   jax.ShapeDtypeStruct((nt, 8, D_CONV), F32)],
        compiler_params=_cparams(("arbitrary",)),
        name="mixout_prompt",
    )(o_nsa, p_main, p_main, p_main, p_main, p_main, cw, gn, gc, w, x, g_mem, wq, mem_kv, wo)


def _mixout_sample_kernel(on_ref, cb_ref, cc_ref, ch_ref, st_ref, cw_ref, gn_ref, gc_ref,
                          w_ref, x_ref, o_ref, nst_ref, *, dec_seq):
    ucat = [st_ref[0], st_ref[1]] + [cc_ref[t] * ch_ref[t] for t in range(dec_seq)]
    cw = cw_ref[...]
    for t in range(dec_seq):
        v = cw[0:1] * ucat[t] + cw[1:2] * ucat[t + 1] + cw[2:3] * ucat[t + 2]
        o_ref[t] = _mix_project(on_ref[t], cb_ref[t] * v, gn_ref[...], gc_ref[...],
                                w_ref[...], x_ref[t])
    nst_ref[0] = ucat[dec_seq]
    nst_ref[1] = ucat[dec_seq + 1]


def mixout_sample(o_nsa, p_s, state, cw, gn, gc, w, x, layer):
    dec_seq, nb, _ = x.shape
    cblk = lambda c: (lambda i: (0, 0, c // D_CONV))
    full3 = lambda i: (0, 0, 0)
    const = lambda i: (0, 0)
    return pl.pallas_call(
        functools.partial(_mixout_sample_kernel, dec_seq=dec_seq),
        grid=(1,),
        in_specs=[pl.BlockSpec((dec_seq, nb, D_NSA), full3),
                  pl.BlockSpec((dec_seq, nb, D_CONV), cblk(COL_CB)),
                  pl.BlockSpec((dec_seq, nb, D_CONV), cblk(COL_CC)),
                  pl.BlockSpec((dec_seq, nb, D_CONV), cblk(COL_CH)),
                  pl.BlockSpec((2, nb, D_CONV), full3),
                  pl.BlockSpec((8, D_CONV), const),
                  pl.BlockSpec((1, D_NSA), const),
                  pl.BlockSpec((1, D_CONV), const),
                  _wspec((D_MODEL, D_MODEL), const, layer),
                  pl.BlockSpec((dec_seq, nb, D_MODEL), full3)],
        out_specs=[pl.BlockSpec((dec_seq, nb, D_MODEL), full3),
                   pl.BlockSpec((2, nb, D_CONV), full3)],
        out_shape=[jax.ShapeDtypeStruct((dec_seq, nb, D_MODEL), F32),
                   jax.ShapeDtypeStruct((2, nb, D_CONV), F32)],
        compiler_params=_cparams(("arbitrary",)),
        name="mixout_sample",
    )(o_nsa, p_s, p_s, p_s, state, cw, gn, gc, w, x)


MEM_SCALE = MEM_HEAD_DIM ** -0.5


def _softmax(s):
    e = jnp.exp(s - jnp.max(s, axis=-1, keepdims=True))
    return e / jnp.sum(e, axis=-1, keepdims=True)


def _mem_attend(x, g, wq, kv, wo):
    qm = _dot(_rms(x, g).astype(BF16), wq).astype(BF16)
    outs = []
    for h in range(MEM_HEADS):
        c0, c1 = h * MEM_HEAD_DIM, (h + 1) * MEM_HEAD_DIM
        p = _softmax(_dot_nt(qm[:, c0:c1], kv[:, c0:c1]) * MEM_SCALE).astype(BF16)
        outs.append(_dot(p, kv[:, D_MEM + c0:D_MEM + c1]))
    return x + _dot(jnp.concatenate(outs, axis=-1).astype(BF16), wo)


def _mem_sample_kernel(q_ref, kv_ref, o_ref, *, bt, mem_len):
    n_rows = q_ref.shape[1]
    stride = 2 * MEM_HEADS
    rowh = lax.broadcasted_iota(jnp.int32, (n_rows, 1), 0) // (n_rows // MEM_HEADS)
    for bb in range(bt):
        kmat = jnp.concatenate([kv_ref[bb, pl.ds(h, mem_len, stride=stride), :]
                                for h in range(MEM_HEADS)], axis=-1).astype(BF16)
        vmat = jnp.concatenate([kv_ref[bb, pl.ds(MEM_HEADS + h, mem_len, stride=stride), :]
                                for h in range(MEM_HEADS)], axis=-1).astype(BF16)
        p = _softmax(_dot_nt(q_ref[bb].astype(BF16), kmat) * MEM_SCALE).astype(BF16)
        o2 = _dot(p, vmat)
        out = jnp.zeros((n_rows, MEM_HEAD_DIM), F32)
        for h in range(MEM_HEADS):
            out = out + jnp.where(rowh == h, o2[:, h * MEM_HEAD_DIM:(h + 1) * MEM_HEAD_DIM], 0.0)
        o_ref[bb] = out


def mem_sample_attn(qbd, cache_mem, layer, bt):
    nb, n_rows, _ = qbd.shape
    rows = cache_mem.shape[2]
    return pl.pallas_call(
        functools.partial(_mem_sample_kernel, bt=bt, mem_len=rows // (2 * MEM_HEADS)),
        grid=(nb // bt,),
        in_specs=[pl.BlockSpec((bt, n_rows, D_MEM), lambda i: (i, 0, 0)),
                  pl.BlockSpec((None, bt, rows, MEM_HEAD_DIM), lambda i: (layer, i, 0, 0))],
        out_specs=pl.BlockSpec((bt, n_rows, MEM_HEAD_DIM), lambda i: (i, 0, 0)),
        out_shape=jax.ShapeDtypeStruct((nb, n_rows, MEM_HEAD_DIM), F32),
        compiler_params=_cparams(("arbitrary",)),
        name="mem_sample_attn",
    )(qbd, cache_mem)


def _silu(a):
    return a * _sigmoid(a)


def _ffn_prompt_kernel(x_ref, xh_ref, g_ref, wg_ref, wu_ref, cw_ref, wd_ref, gf_ref,
                       o_ref, gt_ref, h_s, hh_s, *, tiles_per_seq, final):
    j = pl.program_id(1)
    tm = x_ref.shape[0]

    @pl.when(j == 0)
    def _():
        first = (pl.program_id(0) % tiles_per_seq) == 0
        x = x_ref[...]
        h_s[...] = _rms(x, g_ref[...]).astype(BF16)
        hh = jnp.where(first, 0.0, _rms(xh_ref[...], g_ref[...]))
        hh_s[...] = jnp.concatenate([hh, jnp.zeros_like(hh)], axis=0).astype(BF16)
        o_ref[...] = x

    h = h_s[...]
    gate = _dot(h, wg_ref[...])
    gate_h = _dot(hh_s[...], wg_ref[...])
    g1, g2 = _shift_rows(gate, gate_h[7:8], gate_h[6:7])
    cw = cw_ref[...]
    a = cw[0:1] * g2 + cw[1:2] * g1 + cw[2:3] * gate
    z = (_silu(a) * _dot(h, wu_ref[...])).astype(BF16)
    o_ref[...] += _dot(z, wd_ref[...])
    gt_ref[...] = gate[tm - 8:, :]

    if final:
        @pl.when(j == pl.num_programs(1) - 1)
        def _():
            o_ref[...] = _rms(o_ref[...], gf_ref[...])


def ffn_prompt(x, g, wg, wu, cw, wd, gf, tm, tn, seq, final, layer):
    m = x.shape[0]
    nt = m // tm
    nj = D_FF // tn
    const = lambda i, j: (0, 0)
    return pl.pallas_call(
        functools.partial(_ffn_prompt_kernel, tiles_per_seq=seq // tm, final=final),
        grid=(nt, nj),
        in_specs=[pl.BlockSpec((tm, D_MODEL), lambda i, j: (i, 0)),
                  pl.BlockSpec((8, D_MODEL), lambda i, j: (jnp.maximum(i * (tm // 8) - 1, 0), 0)),
                  pl.BlockSpec((1, D_MODEL), const),
                  _wspec((D_MODEL, tn), lambda i, j: (0, j), layer),
                  _wspec((D_MODEL, tn), lambda i, j: (0, j), layer),
                  pl.BlockSpec((8, tn), lambda i, j: (0, j)),
                  _wspec((tn, D_MODEL), lambda i, j: (j, 0), layer),
                  pl.BlockSpec((1, D_MODEL), const)],
        out_specs=[pl.BlockSpec((tm, D_MODEL), lambda i, j: (i, 0)),
                   pl.BlockSpec((None, 8, tn), lambda i, j: (i, 0, j))],
        out_shape=[jax.ShapeDtypeStruct((m, D_MODEL), F32),
                   jax.ShapeDtypeStruct((nt, 8, D_FF), F32)],
        scratch_shapes=[pltpu.VMEM((tm, D_MODEL), BF16),
                        pltpu.VMEM((16, D_MODEL), BF16)],
        compiler_params=_cparams(("arbitrary", "arbitrary")),
        name="ffn_prompt",
    )(x, x, g, wg, wu, cw, wd, gf)


def _ffn_sample_kernel(x_ref, st_ref, g_ref, wg_ref, wu_ref, cw_ref, wd_ref, gf_ref,
                       o_ref, nst_ref, h_s, acc_s, *, dec_seq, final):
    j = pl.program_id(0)

    nb = x_ref.shape[1]

    @pl.when(j == 0)
    def _():
        for t in range(dec_seq):
            h_s[t * nb:(t + 1) * nb, :] = _rms(x_ref[t], g_ref[...]).astype(BF16)
        acc_s[...] = jnp.zeros_like(acc_s)

    h = h_s[...]
    gate = _dot(h, wg_ref[...])
    up = _dot(h, wu_ref[...])
    gcat = [st_ref[0], st_ref[1]] + [gate[t * nb:(t + 1) * nb] for t in range(dec_seq)]
    cw = cw_ref[...]
    a = jnp.concatenate([cw[0:1] * gcat[t] + cw[1:2] * gcat[t + 1] + cw[2:3] * gcat[t + 2]
                         for t in range(dec_seq)], axis=0)
    acc_s[...] += _dot((_silu(a) * up).astype(BF16), wd_ref[...])
    nst_ref[0] = gcat[dec_seq]
    nst_ref[1] = gcat[dec_seq + 1]

    @pl.when(j == pl.num_programs(0) - 1)
    def _():
        for t in range(dec_seq):
            y = x_ref[t] + acc_s[t * nb:(t + 1) * nb, :]
            o_ref[t] = _rms(y, gf_ref[...]) if final else y


def ffn_sample(x, state, g, wg, wu, cw, wd, gf, tn, final, layer):
    dec_seq, nb, _ = x.shape
    full3 = lambda j: (0, 0, 0)
    const = lambda j: (0, 0)
    return pl.pallas_call(
        functools.partial(_ffn_sample_kernel, dec_seq=dec_seq, final=final),
        grid=(D_FF // tn,),
        in_specs=[pl.BlockSpec((dec_seq, nb, D_MODEL), full3),
                  pl.BlockSpec((2, nb, tn), lambda j: (0, 0, j)),
                  pl.BlockSpec((1, D_MODEL), const),
                  _wspec((D_MODEL, tn), lambda j: (0, j), layer),
                  _wspec((D_MODEL, tn), lambda j: (0, j), layer),
                  pl.BlockSpec((8, tn), lambda j: (0, j)),
                  _wspec((tn, D_MODEL), lambda j: (j, 0), layer),
                  pl.BlockSpec((1, D_MODEL), const)],
        out_specs=[pl.BlockSpec((dec_seq, nb, D_MODEL), full3),
                   pl.BlockSpec((2, nb, tn), lambda j: (0, 0, j))],
        out_shape=[jax.ShapeDtypeStruct((dec_seq, nb, D_MODEL), F32),
                   jax.ShapeDtypeStruct((2, nb, D_FF), F32)],
        scratch_shapes=[pltpu.VMEM((dec_seq * nb, D_MODEL), BF16),
                        pltpu.VMEM((dec_seq * nb, D_MODEL), F32)],
        compiler_params=_cparams(("arbitrary",)),
        name="ffn_sample",
    )(x, state, g, wg, wu, cw, wd, gf)


def _overlap(n_cmp, n_sel, rows, cols):
    c0 = np.arange(n_cmp)[:, None] * CMP_STRIDE
    s0 = np.arange(n_sel)[None, :] * SEL_BLOCK
    ov = np.minimum(c0 + CMP_BLOCK, s0 + SEL_BLOCK) - np.maximum(c0, s0)
    out = np.zeros((rows, cols), np.float32)
    out[:n_cmp, :n_sel] = np.clip(ov, 0, None).astype(np.float32) / CMP_BLOCK
    return out


def _block_expand(n_keys):
    return jnp.asarray(np.arange(128)[:, None] == (np.arange(n_keys)[None, :] // SEL_BLOCK), BF16)


def _chunk_perm():
    row = np.arange(PAGE)
    return jnp.asarray((row[:, None] % RPC) * CMP_STRIDE + row[:, None] // RPC == row[None, :], BF16)


def _pad_rows(a, rows):
    return jnp.concatenate([a, jnp.zeros((rows - a.shape[0],) + a.shape[1:], a.dtype)], axis=0)


def _prep_w_in(w_in_l):
    offs = np.cumsum((D_NSA, KV_W, KV_W, KV_W, N_BRANCH * N_HEADS, D_CONV, D_CONV, D_CONV))
    q, kc, ks, kw, gl, cb, cc, ch = jnp.split(w_in_l, [int(o) for o in offs[:-1]], axis=-1)
    glp = jnp.concatenate([gl, jnp.zeros((D_MODEL, D_KVT - ROW_G - N_BRANCH * N_HEADS), F32)], axis=-1)
    w_main = jnp.concatenate([q * (HEAD_DIM ** -0.5), cb, cc, ch], axis=-1).astype(BF16)
    wt_kv = jnp.concatenate([kc, ks, kw, glp], axis=-1).T.astype(BF16)
    return w_main[None], wt_kv[None]


def _prep_cmp(w_cmp_l, pe_cmp_l):
    blocks = [jnp.pad(w_cmp_l[c], ((0, 0), (0, 0), ((c * N_KV + g) * HEAD_DIM,
                                                     KV_W - (c * N_KV + g + 1) * HEAD_DIM)))
              for c in range(2) for g in range(N_KV)]
    wbd = jnp.concatenate(blocks, axis=1).astype(BF16)
    pe = pe_cmp_l.reshape(2, 2, CMP_STRIDE, HEAD_DIM).transpose(1, 0, 3, 2)
    pet = jnp.broadcast_to(pe[:, :, None, :, None, :], (2, 2, N_KV, HEAD_DIM, RPC, CMP_STRIDE))
    return wbd, pet.reshape(2, KV_W, PAGE)


TM_PROJ = 1024
TN_PROJ = 1024
TM_MIX = 256
TM_FFN = 1024
TN_FFN = 512
BT_CMP = 4
BT_NSA = 4
BT_MEM = 8
NEW_PAD = 16


def _kv_rows(kvt, row, start=0):
    slab = kvt[:, row:row + KV_W, start:]
    b, _, t = slab.shape
    return slab.reshape(b, 2, N_KV, HEAD_DIM, t).transpose(0, 4, 1, 2, 3)


def _layer_prompt(x, mem_kv, lw, consts, layer, batch, seq, final):
    p_main, kvt = proj_in(x, lw['g_mix'], lw['w_main'], lw['wt_kv'], TM_PROJ, TN_PROJ, seq, 0)
    cmp = compress_prompt(kvt, lw['pet'], consts['perm'], lw['wbd'])
    n_cmp_pad = seq // CMP_STRIDE
    cmp_t = cmp.reshape(batch, n_cmp_pad, 2, N_KV, HEAD_DIM).transpose(0, 2, 3, 1, 4)
    o_nsa = nsa_prompt(p_main, kvt, cmp_t, consts['ovt_p'], consts['expand_p'])
    x2, u_tail = mixout_prompt(o_nsa, p_main, lw['conv_w'], lw['g_out_nsa'], lw['g_out_conv'],
                               lw['w_out'], x, lw['g_mem'], lw['w_mem_q'], mem_kv, lw['w_mem_o'],
                               TM_MIX, seq, layer)
    x3, g_tail = ffn_prompt(x2, lw['g_ffn'], lw['w_ff_gate'], lw['w_ff_up'], lw['ffn_conv_w'],
                            lw['w_ff_down'], lw['g_final'], TM_FFN, TN_FFN, seq, final, layer)
    conv_state = u_tail.reshape(batch, seq // TM_MIX, 8, D_CONV)[:, -1, 6:8]
    ffn_state = g_tail.reshape(batch, seq // TM_FFN, 8, D_FF)[:, -1, 6:8]
    return (x3, _kv_rows(kvt, ROW_KC), _kv_rows(kvt, ROW_KS),
            _kv_rows(kvt, ROW_KW, seq - min(WINDOW, seq)), conv_state, ffn_state)


def _layer_sample(x, lw, consts, layer, cache_cmp, cache_sel, page_idx, cache_win, cache_mem,
                  st_conv, st_ffn, final):
    dec_seq, nb, _ = x.shape
    rows = dec_seq * nb
    p_s, kvt = proj_in(x.reshape(rows, D_MODEL), lw['g_mix'], lw['w_main'], lw['wt_kv'],
                       rows, TN_PROJ, rows, 0)
    p3 = p_s.reshape(dec_seq, nb, D_MAIN)
    kvn = kvt.reshape(D_KVT, rows).T.reshape(dec_seq, nb, D_KVT).transpose(1, 0, 2)
    kvn_pad = jnp.concatenate([kvn, jnp.zeros((nb, NEW_PAD - dec_seq, D_KVT), F32)], axis=1)
    cmp_s = compress_paged(cache_cmp, page_idx, lw['pet'], consts['perm'], lw['wbd'], BT_CMP)
    q = p3[:, :, COL_Q:COL_Q + D_NSA].reshape(dec_seq, nb, N_KV, HPG, HEAD_DIM).transpose(1, 3, 2, 0, 4)
    gsel = jnp.eye(N_KV, dtype=F32)
    qbd = (q[:, :, :, :, None, :] * gsel[None, None, :, None, :, None]).reshape(
        nb, HPG * N_KV * dec_seq, KV_HALF)
    gates = kvn[:, :, ROW_G:ROW_G + N_BRANCH * N_HEADS].reshape(
        nb, dec_seq, N_KV, HPG, N_BRANCH).transpose(0, 3, 2, 1, 4).reshape(nb, HPG * N_KV * dec_seq, N_BRANCH)
    o, win_new = nsa_sample(cache_sel, page_idx, qbd, gates, cmp_s, kvn_pad, cache_win, layer,
                            consts['ov_s'], consts['expand_s'], BT_NSA)
    o_nsa = o.reshape(nb, HPG, N_KV, dec_seq, HEAD_DIM).transpose(3, 0, 2, 1, 4).reshape(dec_seq, nb, D_NSA)
    x1, conv_state = mixout_sample(o_nsa, p3, st_conv, lw['conv_w'], lw['g_out_nsa'],
                                   lw['g_out_conv'], lw['w_out'], x, layer)
    qm = rms_matmul(x1.reshape(rows, D_MODEL), lw['g_mem'], lw['w_mem_q'], rows, D_MEM, layer)
    qh = qm.reshape(dec_seq, nb, MEM_HEADS, MEM_HEAD_DIM).transpose(1, 2, 0, 3)
    qmbd = (qh[:, :, :, None, :] * jnp.eye(MEM_HEADS, dtype=F32)[None, :, None, :, None]).reshape(
        nb, MEM_HEADS * dec_seq, D_MEM)
    om = mem_sample_attn(qmbd, cache_mem, layer, BT_MEM)
    om = om.reshape(nb, MEM_HEADS, dec_seq, MEM_HEAD_DIM).transpose(2, 0, 1, 3).reshape(rows, D_MEM)
    x2 = matmul_residual(x1.reshape(rows, D_MODEL), om, lw['w_mem_o'], rows, layer).reshape(
        dec_seq, nb, D_MODEL)
    x3, ffn_state = ffn_sample(x2, st_ffn, lw['g_ffn'], lw['w_ff_gate'], lw['w_ff_up'],
                               lw['ffn_conv_w'], lw['w_ff_down'], lw['g_final'], TN_FFN, final, layer)
    kv_c = kvn[:, :, ROW_KC:ROW_KC + KV_W].reshape(nb, dec_seq, 2, N_KV, HEAD_DIM)
    kv_s = kvn[:, :, ROW_KS:ROW_KS + KV_W].reshape(nb, dec_seq, 2, N_KV, HEAD_DIM)
    return (x3, kv_c, kv_s, win_new.transpose(0, 4, 1, 2, 3),
            conv_state.transpose(1, 0, 2), ffn_state.transpose(1, 0, 2))


def kernel(x_prompt, x_sample, cache_cmp_kv, cache_sel_kv, cache_win_kv, cache_mem_kv,
           state_conv, state_ffn_conv, page_table, mem_prompt,
           g_mix, w_in, w_cmp, pe_cmp, conv_w, g_out_nsa, g_out_conv, w_out,
           g_mem_src, w_mem_kv, g_mem, w_mem_q, w_mem_o, g_ffn, w_ff_gate, w_ff_up,
           ffn_conv_w, w_ff_down, g_final):
    batch, seq, _ = x_prompt.shape
    nb, dec_seq, _ = x_sample.shape
    depth = w_in.shape[0]
    n_phys = cache_cmp_kv.shape[1]
    npages = page_table.shape[1]
    past = npages * PAGE
    win_buf = cache_win_kv.shape[2]
    mem_len = mem_prompt.shape[1]

    n_cmp_s = (past + dec_seq - CMP_BLOCK) // CMP_STRIDE + 1
    n_sel_s = -(-(past + dec_seq) // SEL_BLOCK)
    n_cmp_p = (seq - CMP_BLOCK) // CMP_STRIDE + 1
    n_sel_p = seq // SEL_BLOCK
    consts = {
        'ov_s': jnp.asarray(_overlap(n_cmp_s, n_sel_s, past // CMP_STRIDE, 128), BF16),
        'expand_s': _block_expand(past + PAGE),
        'ovt_p': jnp.asarray(_overlap(n_cmp_p, n_sel_p, seq // CMP_STRIDE, n_sel_p).T, BF16),
        'expand_p': _block_expand(seq),
        'perm': _chunk_perm(),
    }

    cache_cmp = cache_cmp_kv.transpose(0, 1, 3, 4, 5, 2).reshape(depth * n_phys, 2, N_KV, HEAD_DIM, PAGE)
    cache_sel = cache_sel_kv.transpose(0, 1, 3, 4, 5, 2).reshape(depth * n_phys, 2, N_KV, HEAD_DIM, PAGE)
    cache_win = cache_win_kv.transpose(0, 1, 3, 4, 5, 2)
    cache_mem = cache_mem_kv.reshape(depth, nb, mem_len * 2 * MEM_HEADS, MEM_HEAD_DIM)

    xp = x_prompt.reshape(batch * seq, D_MODEL)
    xs = x_sample.transpose(1, 0, 2)
    mem_rows = mem_prompt.reshape(batch * mem_len, D_MODEL)
    outs = [[] for _ in range(11)]
    stacked = {
        'w_out': w_out.astype(BF16), 'w_mem_q': w_mem_q.astype(BF16), 'w_mem_o': w_mem_o.astype(BF16),
        'w_ff_gate': w_ff_gate.astype(BF16), 'w_ff_up': w_ff_up.astype(BF16),
        'w_ff_down': w_ff_down.astype(BF16),
    }
    w_mem_kv_b = w_mem_kv.astype(BF16)
    for l in range(depth):
        wbd, pet = _prep_cmp(w_cmp[l], pe_cmp[l])
        w_main, wt_kv = _prep_w_in(w_in[l])
        lw = dict(stacked)
        lw.update({
            'g_mix': g_mix[l][None], 'w_main': w_main, 'wt_kv': wt_kv, 'wbd': wbd, 'pet': pet,
            'conv_w': _pad_rows(conv_w[l], 8), 'g_out_nsa': g_out_nsa[l][None],
            'g_out_conv': g_out_conv[l][None], 'g_mem': g_mem[l][None], 'g_ffn': g_ffn[l][None],
            'ffn_conv_w': _pad_rows(ffn_conv_w[l], 8), 'g_final': g_final[None],
        })
        final = l == depth - 1
        mem_kv_p = rms_matmul(mem_rows, g_mem_src[l][None], w_mem_kv_b,
                              batch * mem_len, 2 * D_MEM, l).reshape(batch, mem_len, 2 * D_MEM)
        xp, kc, ks, kw, cst, fst = _layer_prompt(xp, mem_kv_p, lw, consts, l, batch, seq, final)
        page_idx = page_table + l * n_phys
        xs, kc2, ks2, kw2, cst2, fst2 = _layer_sample(
            xs, lw, consts, l, cache_cmp, cache_sel, page_idx, cache_win, cache_mem,
            state_conv[l].transpose(1, 0, 2), state_ffn_conv[l].transpose(1, 0, 2), final)
        for lst, val in zip(outs, (kc, ks, kw, mem_kv_p, cst, fst, kc2, ks2, kw2, cst2, fst2)):
            lst.append(val)

    st = [jnp.stack(o) for o in outs]
    return (xp.reshape(batch, seq, D_MODEL), xs.transpose(1, 0, 2),
            st[0], st[1], st[2],
            st[3].reshape(depth, batch, mem_len, 2, MEM_HEADS, MEM_HEAD_DIM),
            st[4], st[5], st[6], st[7], st[8], st[9], st[10])
```

```python
import functools

import numpy as np
import jax
import jax.numpy as jnp
from jax import lax
from jax.experimental import pallas as pl
from jax.experimental.pallas import tpu as pltpu

F32 = jnp.float32
BF16 = jnp.bfloat16

D_MODEL = 2048
D_NSA = 1024
D_CONV = 1024
HEAD_DIM = 64
N_HEADS = 16
N_KV = 2
HPG = 8
N_BRANCH = 3
KV_W = 2 * N_KV * HEAD_DIM
KV_HALF = N_KV * HEAD_DIM
CMP_BLOCK = 32
CMP_STRIDE = 16
SEL_BLOCK = 64
TOPK = 16
WINDOW = 512
Q_BLOCK = 128
FORCE_BONUS = 1e4
D_FF = 5632
MEM_HEADS = 4
MEM_HEAD_DIM = 128
D_MEM = 512
PAGE = 128
EPS = 1e-6
NEG = -1e30

COL_Q = 0
COL_CB = 1024
COL_CC = 2048
COL_CH = 3072
D_MAIN = 4096
ROW_KC = 0
ROW_KS = 256
ROW_KW = 512
ROW_G = 768
D_KVT = 896

VMEM_LIMIT = 56 * 1024 * 1024


def _cparams(sem):
    return pltpu.CompilerParams(dimension_semantics=sem, vmem_limit_bytes=VMEM_LIMIT)


def _rms(x, g):
    return x * lax.rsqrt(jnp.mean(x * x, axis=-1, keepdims=True) + EPS) * g


def _dot(a, b):
    return jnp.dot(a, b, preferred_element_type=F32)


def _dot_nt(a, b):
    return lax.dot_general(a, b, (((1,), (1,)), ((), ())), preferred_element_type=F32)


def _softmax_bias(s, bias):
    s = s + bias
    e = jnp.exp(s - jnp.max(s, axis=-1, keepdims=True))
    return e, jnp.sum(e, axis=-1, keepdims=True)


def _sigmoid(x):
    return 1.0 / (1.0 + jnp.exp(-x))


def _rms_matmul_kernel(x_ref, g_ref, w_ref, o_ref, h_ref):
    @pl.when(pl.program_id(1) == 0)
    def _():
        h_ref[...] = _rms(x_ref[...], g_ref[...]).astype(BF16)

    o_ref[...] = _dot(h_ref[...], w_ref[...]).astype(o_ref.dtype)


def _wspec(block, imap, layer):
    return pl.BlockSpec((None,) + block, lambda *a: (layer,) + imap(*a))


def rms_matmul(x, g, w, tm, tn, layer):
    m, d = x.shape
    n = w.shape[2]
    return pl.pallas_call(
        _rms_matmul_kernel,
        grid=(m // tm, n // tn),
        in_specs=[pl.BlockSpec((tm, d), lambda i, j: (i, 0)),
                  pl.BlockSpec((1, d), lambda i, j: (0, 0)),
                  _wspec((d, tn), lambda i, j: (0, j), layer)],
        out_specs=pl.BlockSpec((tm, tn), lambda i, j: (i, j)),
        out_shape=jax.ShapeDtypeStruct((m, n), F32),
        scratch_shapes=[pltpu.VMEM((tm, d), BF16)],
        compiler_params=_cparams(("arbitrary", "arbitrary")),
        name="rms_matmul",
    )(x, g, w)


def _proj_in_kernel(x_ref, g_ref, w_ref, wt_ref, o_ref, ot_ref, h_ref):
    @pl.when(pl.program_id(1) == 0)
    def _():
        h = _rms(x_ref[...], g_ref[...]).astype(BF16)
        h_ref[...] = h
        ot_ref[...] = _dot_nt(wt_ref[...], h)

    o_ref[...] = _dot_nt(h_ref[...], w_ref[...])


def proj_in(x, g, w_main, wt_kv, tm, tn, seq, layer):
    m, d = x.shape
    tps = seq // tm
    return pl.pallas_call(
        _proj_in_kernel,
        grid=(m // tm, D_MAIN // tn),
        in_specs=[pl.BlockSpec((tm, d), lambda i, j: (i, 0)),
                  pl.BlockSpec((1, d), lambda i, j: (0, 0)),
                  _wspec((tn, d), lambda i, j: (j, 0), layer),
                  _wspec((D_KVT, d), lambda i, j: (0, 0), layer)],
        out_specs=[pl.BlockSpec((tm, tn), lambda i, j: (i, j)),
                   pl.BlockSpec((None, D_KVT, tm), lambda i, j: (i // tps, 0, i % tps))],
        out_shape=[jax.ShapeDtypeStruct((m, D_MAIN), F32),
                   jax.ShapeDtypeStruct((m // seq, D_KVT, seq), F32)],
        scratch_shapes=[pltpu.VMEM((tm, d), BF16)],
        compiler_params=_cparams(("arbitrary", "arbitrary")),
        name="proj_in",
    )(x, g, w_main, wt_kv)


def _matmul_res_kernel(x_ref, a_ref, w_ref, o_ref):
    o_ref[...] = x_ref[...] + _dot(a_ref[...].astype(BF16), w_ref[...])


def matmul_residual(x, a, w, tm, layer):
    m, n = x.shape
    k = a.shape[1]
    return pl.pallas_call(
        _matmul_res_kernel,
        grid=(m // tm,),
        in_specs=[pl.BlockSpec((tm, n), lambda i: (i, 0)),
                  pl.BlockSpec((tm, k), lambda i: (i, 0)),
                  _wspec((k, n), lambda i: (0, 0), layer)],
        out_specs=pl.BlockSpec((tm, n), lambda i: (i, 0)),
        out_shape=jax.ShapeDtypeStruct((m, n), F32),
        compiler_params=_cparams(("arbitrary",)),
        name="matmul_residual",
    )(x, a, w)


RPC = PAGE // CMP_STRIDE


def _compress_chunks(chunk_at, n_chunks, pet_ref, perm_ref, w_ref, xs_ref):
    perm = perm_ref[...]
    for ci in range(n_chunks):
        chunk = chunk_at(ci)
        for a in range(2):
            xs_ref[a, ci] = _dot_nt(perm, (chunk + pet_ref[a]).astype(BF16))
    rows = n_chunks * RPC
    acc0 = jnp.zeros((rows, KV_W), F32)
    acc1 = jnp.zeros((rows, KV_W), F32)
    for r in range(CMP_STRIDE):
        x0 = xs_ref[0, :, r * RPC:(r + 1) * RPC, :].reshape(rows, KV_W).astype(BF16)
        x1 = xs_ref[1, :, r * RPC:(r + 1) * RPC, :].reshape(rows, KV_W).astype(BF16)
        acc0 = acc0 + _dot(x0, w_ref[r])
        acc1 = acc1 + _dot(x1, w_ref[CMP_STRIDE + r])
    return acc0 + pltpu.roll(acc1, rows - 1, 0)


def _compress_prompt_kernel(kvt_ref, pet_ref, perm_ref, w_ref, o_ref, xs_ref, *, n_chunks):
    out = _compress_chunks(lambda ci: kvt_ref[:, ci * PAGE:(ci + 1) * PAGE], n_chunks,
                           pet_ref, perm_ref, w_ref, xs_ref)
    o_ref[...] = out.astype(BF16)


def compress_prompt(kvt, pet, perm, wbd):
    batch, _, seq = kvt.shape
    n_chunks = seq // PAGE
    nblk = seq // CMP_STRIDE
    return pl.pallas_call(
        functools.partial(_compress_prompt_kernel, n_chunks=n_chunks),
        grid=(batch,),
        in_specs=[pl.BlockSpec((None, KV_W, seq), lambda b: (b, ROW_KC // KV_W, 0)),
                  pl.BlockSpec((2, KV_W, PAGE), lambda b: (0, 0, 0)),
                  pl.BlockSpec((PAGE, PAGE), lambda b: (0, 0)),
                  pl.BlockSpec((CMP_BLOCK, KV_W, KV_W), lambda b: (0, 0, 0))],
        out_specs=pl.BlockSpec((None, nblk, KV_W), lambda b: (b, 0, 0)),
        out_shape=jax.ShapeDtypeStruct((batch, nblk, KV_W), BF16),
        scratch_shapes=[pltpu.VMEM((2, n_chunks, PAGE, KV_W), F32)],
        compiler_params=_cparams(("arbitrary",)),
        name="compress_prompt",
    )(kvt, pet, perm, wbd)


def _compress_paged_kernel(pt_ref, *refs, bt, npages):
    page_refs = refs[:bt * npages]
    pet_ref, perm_ref, w_ref, o_ref, xs_ref = refs[bt * npages:]
    out = _compress_chunks(lambda ci: page_refs[ci][...].reshape(KV_W, PAGE), bt * npages,
                           pet_ref, perm_ref, w_ref, xs_ref)
    o_ref[...] = out.astype(BF16).reshape(bt, npages * RPC, KV_W)


def compress_paged(cache, page_idx, pet, perm, wbd, bt):
    nb, npages = page_idx.shape
    nblk = npages * RPC
    page_specs = [
        pl.BlockSpec((None, 2, N_KV, HEAD_DIM, PAGE), functools.partial(
            lambda i, pt, bb, p: (pt[i * bt + bb, p], 0, 0, 0, 0), bb=bb, p=p))
        for bb in range(bt) for p in range(npages)]
    grid_spec = pltpu.PrefetchScalarGridSpec(
        num_scalar_prefetch=1,
        grid=(nb // bt,),
        in_specs=page_specs + [pl.BlockSpec((2, KV_W, PAGE), lambda i, pt: (0, 0, 0)),
                               pl.BlockSpec((PAGE, PAGE), lambda i, pt: (0, 0)),
                               pl.BlockSpec((CMP_BLOCK, KV_W, KV_W), lambda i, pt: (0, 0, 0))],
        out_specs=pl.BlockSpec((bt, nblk, KV_W), lambda i, pt: (i, 0, 0)),
        scratch_shapes=[pltpu.VMEM((2, bt * npages, PAGE, KV_W), F32)])
    return pl.pallas_call(
        functools.partial(_compress_paged_kernel, bt=bt, npages=npages),
        grid_spec=grid_spec,
        out_shape=jax.ShapeDtypeStruct((nb, nblk, KV_W), BF16),
        compiler_params=_cparams(("arbitrary",)),
        name="compress_paged",
    )(page_idx, *([cache] * (bt * npages)), pet, perm, wbd)


def _importance(imp, idx, tpos, n_sel):
    valid = (idx * SEL_BLOCK <= tpos) & (idx < n_sel)
    cur = tpos // SEL_BLOCK
    forced = (idx == 0) | (idx == cur) | (idx == cur - 1)
    return jnp.where(valid, imp + jnp.where(forced, FORCE_BONUS, 0.0), NEG)


def _topk_rows(imp, n_cand):
    groups = [imp[v * 8:(v + 1) * 8] for v in range(n_cand // 8)]
    sub = lax.broadcasted_iota(jnp.int32, groups[0].shape, 0)
    ranks = [jnp.zeros(g.shape, F32) for g in groups]
    for jp in range(n_cand):
        c = imp[jp:jp + 1, :]
        for v, g in enumerate(groups):
            ge = jnp.where(c >= g, 1.0, 0.0)
            gt = jnp.where(c > g, 1.0, 0.0)
            if v * 8 > jp:
                beats = ge
            elif v * 8 + 7 < jp:
                beats = gt
            else:
                beats = jnp.where(sub > jp - v * 8, ge, gt)
            ranks[v] = ranks[v] + beats
    return jnp.concatenate([jnp.where(r < TOPK, 1.0, 0.0) for r in ranks], axis=0)


def _topk_lanes(imp, idx, n_cand):
    rank = jnp.zeros(imp.shape, F32)
    for jp in range(n_cand):
        c = imp[:, jp:jp + 1]
        rank = rank + jnp.where(idx > jp, jnp.where(c >= imp, 1.0, 0.0), jnp.where(c > imp, 1.0, 0.0))
    return jnp.where(rank < TOPK, 1.0, 0.0)


KEY_TILE = 512
V_PAD = 16


def _tile_heads(x):
    return jnp.concatenate([x] * HPG, axis=0)


def _nsa_prompt_kernel(q_ref, gate_ref, cmp_ref, kst_ref, vst_ref, kwt_ref, vwt_ref, ovt_ref, exp_ref,
                       o_ref, ksb, vsb, kwb, vwb, *, seq):
    g = pl.program_id(1)
    i = pl.program_id(2)
    n_cmp_pad = seq // CMP_STRIDE
    n_sel = seq // SEL_BLOCK

    @pl.when(i == 0)
    def _():
        ones_row = jnp.where(lax.broadcasted_iota(jnp.int32, (V_PAD, seq), 0) == 0, 1.0, 0.0).astype(BF16)
        ksb[...] = kst_ref[...].astype(BF16)
        vsb[0:HEAD_DIM, :] = vst_ref[...].astype(BF16)
        vsb[HEAD_DIM:HEAD_DIM + V_PAD, :] = ones_row
        kwb[...] = kwt_ref[...].astype(BF16)
        vwb[0:HEAD_DIM, :] = vwt_ref[...].astype(BF16)
        vwb[HEAD_DIM:HEAD_DIM + V_PAD, :] = ones_row

    t0 = i * Q_BLOCK
    qb = q_ref[...]
    qs = jnp.concatenate([qb[:, h * HEAD_DIM:(h + 1) * HEAD_DIM] for h in range(HPG)],
                         axis=0).astype(BF16)
    tcol = lax.broadcasted_iota(jnp.int32, (Q_BLOCK, 1), 0) + t0

    ncol = lax.broadcasted_iota(jnp.int32, (Q_BLOCK, n_cmp_pad), 1)
    vis = jnp.where(ncol < n_cmp_pad - 1, ncol * CMP_STRIDE + (CMP_BLOCK - 1), seq) <= tcol
    e_c, l_c = _softmax_bias(_dot_nt(qs, cmp_ref[0]), _tile_heads(jnp.where(vis, 0.0, NEG)))
    any_vis = _tile_heads(jnp.where(tcol >= CMP_BLOCK - 1, 1.0, 0.0))
    p_c = (e_c * (any_vis / jnp.maximum(l_c, 1e-30))).astype(BF16)
    o_c = _dot(p_c, cmp_ref[1])

    po = _dot_nt(ovt_ref[...], p_c)
    imp_t = po[:, 0:Q_BLOCK]
    for h in range(1, HPG):
        imp_t = imp_t + po[:, h * Q_BLOCK:(h + 1) * Q_BLOCK]
    jj = lax.broadcasted_iota(jnp.int32, (n_sel, Q_BLOCK), 0)
    tt = lax.broadcasted_iota(jnp.int32, (n_sel, Q_BLOCK), 1) + t0
    sel_t = _topk_rows(_importance(imp_t, jj, tt, n_sel), n_sel)
    sel = jnp.concatenate([sel_t, jnp.zeros((128 - n_sel, Q_BLOCK), F32)], axis=0).T.astype(BF16)

    kk = lax.broadcasted_iota(jnp.int32, (Q_BLOCK, KEY_TILE), 1)

    def scores(c):
        k0 = pl.multiple_of(c * KEY_TILE, KEY_TILE)
        msel = _dot(sel, exp_ref[:, pl.ds(k0, KEY_TILE)])
        bias = jnp.where(msel > 0.5, jnp.where(kk + k0 <= tcol, 0.0, NEG), NEG)
        return _dot(qs, ksb[:, pl.ds(k0, KEY_TILE)]) + _tile_heads(bias)

    def accumulate(c, s, carry):
        m, acc = carry
        k0 = pl.multiple_of(c * KEY_TILE, KEY_TILE)
        m_new = jnp.maximum(m, jnp.max(s, axis=-1, keepdims=True))
        p = jnp.exp(s - m_new).astype(BF16)
        return m_new, jnp.exp(m - m_new) * acc + _dot_nt(p, vsb[:, pl.ds(k0, KEY_TILE)])

    def tile_pair(c2, carry):
        s_a = scores(2 * c2)
        s_b = scores(2 * c2 + 1)
        return accumulate(2 * c2 + 1, s_b, accumulate(2 * c2, s_a, carry))

    n_rows = HPG * Q_BLOCK
    m0 = jnp.full((n_rows, 1), NEG, F32)
    a0 = jnp.zeros((n_rows, HEAD_DIM + V_PAD), F32)
    n_tiles = (t0 + Q_BLOCK + KEY_TILE - 1) // KEY_TILE
    def tile_quad(c4, carry):
        ss = [scores(4 * c4 + u) for u in range(4)]
        for u in range(4):
            carry = accumulate(4 * c4 + u, ss[u], carry)
        return carry

    carry = lax.fori_loop(0, n_tiles // 4, tile_quad, (m0, a0))
    carry = lax.cond(n_tiles % 4 >= 2, lambda cr: tile_pair((n_tiles // 4) * 2, cr), lambda cr: cr, carry)
    _, acc_s = lax.cond(n_tiles % 2 == 1,
                        lambda cr: accumulate(n_tiles - 1, scores(n_tiles - 1), cr),
                        lambda cr: cr, carry)
    o_s = acc_s[:, 0:HEAD_DIM] / acc_s[:, HEAD_DIM:HEAD_DIM + 1]

    span = WINDOW + Q_BLOCK
    ws = pl.multiple_of(jnp.maximum(t0 - WINDOW, 0), Q_BLOCK)
    dlt = tcol - (lax.broadcasted_iota(jnp.int32, (Q_BLOCK, span), 1) + ws)
    bias_w = jnp.where(dlt >= 0, jnp.where(dlt < WINDOW, 0.0, NEG), NEG)
    s_w = _dot(qs, kwb[:, pl.ds(ws, span)]) + _tile_heads(bias_w)
    e_w = jnp.exp(s_w - jnp.max(s_w, axis=-1, keepdims=True)).astype(BF16)
    acc_w = _dot_nt(e_w, vwb[:, pl.ds(ws, span)])
    o_w = acc_w[:, 0:HEAD_DIM] / acc_w[:, HEAD_DIM:HEAD_DIM + 1]

    gs = _sigmoid(gate_ref[...]).T
    gsel = jnp.where(g == 0, gs[:, 0:HPG * N_BRANCH], gs[:, HPG * N_BRANCH:2 * HPG * N_BRANCH])
    for h in range(HPG):
        r0, r1 = h * Q_BLOCK, (h + 1) * Q_BLOCK
        c0 = h * N_BRANCH
        o_ref[:, h * HEAD_DIM:(h + 1) * HEAD_DIM] = (
            o_c[r0:r1] * gsel[:, c0:c0 + 1] + o_s[r0:r1] * gsel[:, c0 + 1:c0 + 2]
            + o_w[r0:r1] * gsel[:, c0 + 2:c0 + 3])


def nsa_prompt(p_main, kvt, cmp_t, ovt, expand):
    batch, _, seq = kvt.shape
    nqb = seq // Q_BLOCK
    gw = HPG * HEAD_DIM
    n_cmp_pad = seq // CMP_STRIDE
    n_sel = seq // SEL_BLOCK
    kvt_spec = lambda row: pl.BlockSpec((None, HEAD_DIM, seq),
                                        lambda b, g, i: (b, row // HEAD_DIM + g, 0))
    return pl.pallas_call(
        functools.partial(_nsa_prompt_kernel, seq=seq),
        grid=(batch, N_KV, nqb),
        in_specs=[
            pl.BlockSpec((Q_BLOCK, gw), lambda b, g, i: (b * nqb + i, g)),
            pl.BlockSpec((None, 128, Q_BLOCK), lambda b, g, i: (b, ROW_G // 128, i)),
            pl.BlockSpec((None, 2, None, n_cmp_pad, HEAD_DIM), lambda b, g, i: (b, 0, g, 0, 0)),
            kvt_spec(ROW_KS), kvt_spec(ROW_KS + KV_HALF),
            kvt_spec(ROW_KW), kvt_spec(ROW_KW + KV_HALF),
            pl.BlockSpec((n_sel, n_cmp_pad), lambda b, g, i: (0, 0)),
            pl.BlockSpec((128, seq), lambda b, g, i: (0, 0)),
        ],
        out_specs=pl.BlockSpec((Q_BLOCK, gw), lambda b, g, i: (b * nqb + i, g)),
        out_shape=jax.ShapeDtypeStruct((batch * seq, D_NSA), F32),
        scratch_shapes=[pltpu.VMEM((HEAD_DIM, seq), BF16), pltpu.VMEM((HEAD_DIM + V_PAD, seq), BF16)] * 2,
        compiler_params=_cparams(("arbitrary", "arbitrary", "arbitrary")),
        name="nsa_prompt",
    )(p_main, kvt, cmp_t, kvt, kvt, kvt, kvt, ovt, expand)


def _pick_group(o2, rg):
    return jnp.where(rg == 0, o2[:, 0:HEAD_DIM], o2[:, HEAD_DIM:2 * HEAD_DIM])


def _masked_softmax(s, mask):
    s = jnp.where(mask, s, NEG)
    m = jnp.max(s, axis=-1, keepdims=True)
    e = jnp.where(mask, jnp.exp(s - m), 0.0)
    return e / jnp.maximum(jnp.sum(e, axis=-1, keepdims=True), 1e-30)


def _masked_softmax2(s_a, mask_a, s_b, mask_b):
    s_a = jnp.where(mask_a, s_a, NEG)
    s_b = jnp.where(mask_b, s_b, NEG)
    m = jnp.maximum(jnp.max(s_a, axis=-1, keepdims=True), jnp.max(s_b, axis=-1, keepdims=True))
    e_a = jnp.where(mask_a, jnp.exp(s_a - m), 0.0)
    e_b = jnp.where(mask_b, jnp.exp(s_b - m), 0.0)
    tot = jnp.sum(e_a, axis=-1, keepdims=True) + jnp.sum(e_b, axis=-1, keepdims=True)
    inv = 1.0 / jnp.maximum(tot, 1e-30)
    return (e_a * inv).astype(BF16), (e_b * inv).astype(BF16)


def _nsa_sample_kernel(pt_ref, *refs, bt, npages, dec_seq):
    page_refs = refs[:bt * npages]
    rest = list(refs[bt * npages:])
    n_prev = rest[-5].shape[0] - 1
    prev_ref = rest.pop(5) if n_prev else None
    (q_ref, gate_ref, cmp_ref, new_ref, win_ref, ov_ref, exp_ref,
     o_ref, wout_ref, kts, vts, ktw, vtw) = rest
    past = npages * PAGE
    n_cmp = (past + dec_seq - CMP_BLOCK) // CMP_STRIDE + 1
    n_sel = -(-(past + dec_seq) // SEL_BLOCK)
    n_rows = HPG * N_KV * dec_seq
    win_buf = win_ref.shape[-1]
    n_cmp_pad = cmp_ref.shape[1]
    n_new = new_ref.shape[1]
    gt = N_KV * dec_seq

    rowi = lax.broadcasted_iota(jnp.int32, (n_rows, 1), 0)
    rg = (rowi // dec_seq) % N_KV
    tpos = past + rowi % dec_seq
    ncol = lax.broadcasted_iota(jnp.int32, (n_rows, n_cmp_pad), 1)
    m_c = (ncol * CMP_STRIDE + (CMP_BLOCK - 1) <= tpos) & (ncol < n_cmp)
    jj = lax.broadcasted_iota(jnp.int32, (gt, 128), 1)
    t8 = past + lax.broadcasted_iota(jnp.int32, (gt, 128), 0) % dec_seq
    d_new = tpos - (past + lax.broadcasted_iota(jnp.int32, (n_rows, n_new), 1))
    m_new = d_new >= 0
    m_wn = m_new & (d_new < WINDOW)
    wcol = lax.broadcasted_iota(jnp.int32, (n_rows, win_buf), 1)
    dlt = tpos - (wcol + (past - win_buf))
    m_w = (dlt >= 0) & (dlt < WINDOW)
    lane = lax.broadcasted_iota(jnp.int32, (KV_W, PAGE), 1)

    bbs = range(bt)

    for bb in bbs:
        for p in range(npages):
            page = page_refs[bb * npages + p]
            kts[bb, :, p * PAGE:(p + 1) * PAGE] = page[0].reshape(KV_HALF, PAGE).astype(BF16)
            vts[bb, :, p * PAGE:(p + 1) * PAGE] = page[1].reshape(KV_HALF, PAGE).astype(BF16)
        win = win_ref[bb].reshape(KV_W, win_buf)
        ktw[bb] = win[0:KV_HALF].astype(BF16)
        vtw[bb] = win[KV_HALF:KV_W].astype(BF16)
        kwn = new_ref[bb, :, ROW_KW:ROW_KW + KV_W]
        kwn_t = jnp.concatenate([kwn, jnp.zeros((PAGE - n_new, KV_W), F32)], axis=0).T
        shifted = pltpu.roll(win, win_buf - dec_seq, 1)
        tail = jnp.where(lane >= PAGE - dec_seq, pltpu.roll(kwn_t, PAGE - dec_seq, 1),
                         shifted[:, win_buf - PAGE:])
        wout_ref[n_prev, bb] = jnp.concatenate([shifted[:, 0:win_buf - PAGE], tail], axis=1).reshape(
            2, N_KV, HEAD_DIM, win_buf)
        for lp in range(n_prev):
            wout_ref[lp, bb] = prev_ref[lp, bb]

    qs = [q_ref[bb].astype(BF16) for bb in bbs]
    new = [new_ref[bb].astype(BF16) for bb in bbs]
    kn_s = [x[:, ROW_KS:ROW_KS + KV_HALF] for x in new]
    vn_s = [x[:, ROW_KS + KV_HALF:ROW_KS + KV_W] for x in new]
    kn_w = [x[:, ROW_KW:ROW_KW + KV_HALF] for x in new]
    vn_w = [x[:, ROW_KW + KV_HALF:ROW_KW + KV_W] for x in new]

    s_c = [_dot_nt(qs[bb], cmp_ref[bb, :, 0:KV_HALF]) for bb in bbs]
    s_w = [_dot(qs[bb], ktw[bb]) for bb in bbs]
    s_wn = [_dot_nt(qs[bb], kn_w[bb]) for bb in bbs]
    p_c = [_masked_softmax(s, m_c).astype(BF16) for s in s_c]
    p_w = [_masked_softmax2(s_w[bb], m_w, s_wn[bb], m_wn) for bb in bbs]
    o_c = [_pick_group(_dot(p_c[bb], cmp_ref[bb, :, KV_HALF:KV_W]), rg) for bb in bbs]
    o_w = [_pick_group(_dot_nt(p_w[bb][0], vtw[bb]) + _dot(p_w[bb][1], vn_w[bb]), rg) for bb in bbs]

    po = [_dot(p, ov_ref[...]) for p in p_c]
    imp = [functools.reduce(lambda a, h: a + x[h * gt:(h + 1) * gt], range(1, HPG), x[0:gt]) for x in po]
    sel8 = [_topk_lanes(_importance(x, jj, t8, n_sel), jj, n_sel) for x in imp]
    msel = [_dot(jnp.concatenate([x] * HPG, axis=0).astype(BF16), exp_ref[...]) for x in sel8]

    s_s = [_dot(qs[bb], kts[bb]) for bb in bbs]
    s_sn = [_dot_nt(qs[bb], kn_s[bb]) for bb in bbs]
    p_s = [_masked_softmax2(s_s[bb], msel[bb][:, 0:past] > 0.5,
                            s_sn[bb], (msel[bb][:, past:past + n_new] > 0.5) & m_new) for bb in bbs]
    o_s = [_pick_group(_dot_nt(p_s[bb][0], vts[bb]) + _dot(p_s[bb][1], vn_s[bb]), rg) for bb in bbs]

    for bb in bbs:
        gs = _sigmoid(gate_ref[bb])
        o_ref[bb] = o_c[bb] * gs[:, 0:1] + o_s[bb] * gs[:, 1:2] + o_w[bb] * gs[:, 2:3]


def nsa_sample(cache_sel, page_idx, qbd, gates, cmp_s, kv_new, cache_win, layer, prev_win, ov_s, expand, bt):
    nb, npages = page_idx.shape
    n_prev = 0 if prev_win is None else prev_win.shape[0]
    n_rows = qbd.shape[1]
    dec_seq = n_rows // (HPG * N_KV)
    win_buf = cache_win.shape[-1]
    n_cmp_pad = cmp_s.shape[1]
    n_new = kv_new.shape[1]
    past = npages * PAGE
    page_specs = [
        pl.BlockSpec((None, 2, N_KV, HEAD_DIM, PAGE), functools.partial(
            lambda i, pt, bb, p: (pt[i * bt + bb, p], 0, 0, 0, 0), bb=bb, p=p))
        for bb in range(bt) for p in range(npages)]
    per_b = lambda i, pt: (i, 0, 0)
    const2 = lambda i, pt: (0, 0)
    win_shape = (2, N_KV, HEAD_DIM, win_buf)
    grid_spec = pltpu.PrefetchScalarGridSpec(
        num_scalar_prefetch=1,
        grid=(nb // bt,),
        in_specs=page_specs + [
            pl.BlockSpec((bt, n_rows, KV_HALF), per_b),
            pl.BlockSpec((bt, n_rows, N_BRANCH), per_b),
            pl.BlockSpec((bt, n_cmp_pad, KV_W), per_b),
            pl.BlockSpec((bt, n_new, D_KVT), per_b),
            pl.BlockSpec((None, bt) + win_shape, lambda i, pt: (layer, i, 0, 0, 0, 0)),
        ] + ([pl.BlockSpec((n_prev, bt) + win_shape, lambda i, pt: (0, i, 0, 0, 0, 0))] if n_prev else []) + [
            pl.BlockSpec((n_cmp_pad, 128), const2),
            pl.BlockSpec((128, past + PAGE), const2),
        ],
        out_specs=[pl.BlockSpec((bt, n_rows, HEAD_DIM), per_b),
                   pl.BlockSpec((n_prev + 1, bt) + win_shape, lambda i, pt: (0, i, 0, 0, 0, 0))],
        scratch_shapes=[pltpu.VMEM((bt, KV_HALF, past), BF16)] * 2
        + [pltpu.VMEM((bt, KV_HALF, win_buf), BF16)] * 2)
    return pl.pallas_call(
        functools.partial(_nsa_sample_kernel, bt=bt, npages=npages, dec_seq=dec_seq),
        grid_spec=grid_spec,
        out_shape=[jax.ShapeDtypeStruct((nb, n_rows, HEAD_DIM), F32),
                   jax.ShapeDtypeStruct((n_prev + 1, nb) + win_shape, F32)],
        compiler_params=_cparams(("arbitrary",)),
        name="nsa_sample",
    )(page_idx, *([cache_sel] * (bt * npages)), qbd, gates, cmp_s, kv_new, cache_win,
      *([prev_win] if n_prev else []), ov_s, expand)


def _shift_rows(u, p1, p2):
    r = lax.broadcasted_iota(jnp.int32, (u.shape[0], 1), 0)
    u1 = jnp.where(r >= 1, pltpu.roll(u, 1, 0), p1)
    u2 = jnp.where(r >= 2, pltpu.roll(u, 2, 0), jnp.where(r == 1, p1, p2))
    return u1, u2


def _mix_project(o_nsa, o_conv, gn, gc, w, x):
    mix = jnp.concatenate([_rms(o_nsa, gn), _rms(o_conv, gc)], axis=-1).astype(BF16)
    return x + _dot(mix, w)


def _mixout_prompt_kernel(on_ref, cb_ref, cc_ref, ch_ref, cch_ref, chh_ref, cw_ref, gn_ref, gc_ref,
                          w_ref, x_ref, gm_ref, wq_ref, kv_ref, wo_ref, o_ref, ut_ref, *, tiles_per_seq):
    first = (pl.program_id(0) % tiles_per_seq) == 0
    u = cc_ref[...] * ch_ref[...]
    uh = jnp.where(first, 0.0, cch_ref[...] * chh_ref[...])
    u1, u2 = _shift_rows(u, uh[7:8], uh[6:7])
    cw = cw_ref[...]
    v = cw[0:1] * u2 + cw[1:2] * u1 + cw[2:3] * u
    x1 = _mix_project(on_ref[...], cb_ref[...] * v, gn_ref[...], gc_ref[...], w_ref[...], x_ref[...])
    ut_ref[...] = u[u.shape[0] - 8:, :]
    o_ref[...] = _mem_attend(x1, gm_ref[...], wq_ref[...], kv_ref[...].astype(BF16), wo_ref[...])


def mixout_prompt(o_nsa, p_main, cw, gn, gc, w, x, g_mem, wq, mem_kv, wo, tm, seq, layer):
    m = x.shape[0]
    nt = m // tm
    tps = seq // tm
    mem_len = mem_kv.shape[1]
    cblk = lambda c: (lambda i: (i, c // D_CONV))
    halo = lambda c: (lambda i: (jnp.maximum(i * (tm // 8) - 1, 0), c // D_CONV))
    const = lambda i: (0, 0)
    return pl.pallas_call(
        functools.partial(_mixout_prompt_kernel, tiles_per_seq=tps),
        grid=(nt,),
        in_specs=[pl.BlockSpec((tm, D_NSA), lambda i: (i, 0)),
                  pl.BlockSpec((tm, D_CONV), cblk(COL_CB)),
                  pl.BlockSpec((tm, D_CONV), cblk(COL_CC)),
                  pl.BlockSpec((tm, D_CONV), cblk(COL_CH)),
                  pl.BlockSpec((8, D_CONV), halo(COL_CC)),
                  pl.BlockSpec((8, D_CONV), halo(COL_CH)),
                  pl.BlockSpec((8, D_CONV), const),
                  pl.BlockSpec((1, D_NSA), const),
                  pl.BlockSpec((1, D_CONV), const),
                  _wspec((D_MODEL, D_MODEL), const, layer),
                  pl.BlockSpec((tm, D_MODEL), lambda i: (i, 0)),
                  pl.BlockSpec((1, D_MODEL), const),
                  _wspec((D_MODEL, D_MEM), const, layer),
                  pl.BlockSpec((None, mem_len, 2 * D_MEM), lambda i: (i // tps, 0, 0)),
                  _wspec((D_MEM, D_MODEL), const, layer)],
        out_specs=[pl.BlockSpec((tm, D_MODEL), lambda i: (i, 0)),
                   pl.BlockSpec((None, 8, D_CONV), lambda i: (i, 0, 0))],
        out_shape=[jax.ShapeDtypeStruct((m, D_MODEL), F32),
                   jax.ShapeDtypeStruct((nt, 8, D_CONV), F32)],
        compiler_params=_cparams(("arbitrary",)),
        name="mixout_prompt",
    )(o_nsa, p_main, p_main, p_main, p_main, p_main, cw, gn, gc, w, x, g_mem, wq, mem_kv, wo)


def _mixout_sample_kernel(on_ref, cb_ref, cc_ref, ch_ref, st_ref, cw_ref, gn_ref, gc_ref,
                          w_ref, x_ref, o_ref, nst_ref, *, dec_seq):
    ucat = [st_ref[0], st_ref[1]] + [cc_ref[t] * ch_ref[t] for t in range(dec_seq)]
    cw = cw_ref[...]
    for t in range(dec_seq):
        v = cw[0:1] * ucat[t] + cw[1:2] * ucat[t + 1] + cw[2:3] * ucat[t + 2]
        o_ref[t] = _mix_project(on_ref[t], cb_ref[t] * v, gn_ref[...], gc_ref[...],
                                w_ref[...], x_ref[t])
    nst_ref[0] = ucat[dec_seq]
    nst_ref[1] = ucat[dec_seq + 1]


def mixout_sample(o_nsa, p_s, state, cw, gn, gc, w, x, layer):
    dec_seq, nb, _ = x.shape
    cblk = lambda c: (lambda i: (0, 0, c // D_CONV))
    full3 = lambda i: (0, 0, 0)
    const = lambda i: (0, 0)
    return pl.pallas_call(
        functools.partial(_mixout_sample_kernel, dec_seq=dec_seq),
        grid=(1,),
        in_specs=[pl.BlockSpec((dec_seq, nb, D_NSA), full3),
                  pl.BlockSpec((dec_seq, nb, D_CONV), cblk(COL_CB)),
                  pl.BlockSpec((dec_seq, nb, D_CONV), cblk(COL_CC)),
                  pl.BlockSpec((dec_seq, nb, D_CONV), cblk(COL_CH)),
                  pl.BlockSpec((2, nb, D_CONV), full3),
                  pl.BlockSpec((8, D_CONV), const),
                  pl.BlockSpec((1, D_NSA), const),
                  pl.BlockSpec((1, D_CONV), const),
                  _wspec((D_MODEL, D_MODEL), const, layer),
                  pl.BlockSpec((dec_seq, nb, D_MODEL), full3)],
        out_specs=[pl.BlockSpec((dec_seq, nb, D_MODEL), full3),
                   pl.BlockSpec((2, nb, D_CONV), full3)],
        out_shape=[jax.ShapeDtypeStruct((dec_seq, nb, D_MODEL), F32),
                   jax.ShapeDtypeStruct((2, nb, D_CONV), F32)],
        compiler_params=_cparams(("arbitrary",)),
        name="mixout_sample",
    )(o_nsa, p_s, p_s, p_s, state, cw, gn, gc, w, x)


MEM_SCALE = MEM_HEAD_DIM ** -0.5


def _softmax(s):
    e = jnp.exp(s - jnp.max(s, axis=-1, keepdims=True))
    return e / jnp.sum(e, axis=-1, keepdims=True)


def _mem_attend(x, g, wq, kv, wo):
    qm = _dot(_rms(x, g).astype(BF16), wq).astype(BF16)
    outs = []
    for h in range(MEM_HEADS):
        c0, c1 = h * MEM_HEAD_DIM, (h + 1) * MEM_HEAD_DIM
        p = _softmax(_dot_nt(qm[:, c0:c1], kv[:, c0:c1]) * MEM_SCALE).astype(BF16)
        outs.append(_dot(p, kv[:, D_MEM + c0:D_MEM + c1]))
    return x + _dot(jnp.concatenate(outs, axis=-1).astype(BF16), wo)


def _mem_sample_kernel(q_ref, kv_ref, o_ref, *, bt, mem_len):
    n_rows = q_ref.shape[1]
    stride = 2 * MEM_HEADS
    rowh = lax.broadcasted_iota(jnp.int32, (n_rows, 1), 0) // (n_rows // MEM_HEADS)
    for bb in range(bt):
        kmat = jnp.concatenate([kv_ref[bb, pl.ds(h, mem_len, stride=stride), :]
                                for h in range(MEM_HEADS)], axis=-1).astype(BF16)
        vmat = jnp.concatenate([kv_ref[bb, pl.ds(MEM_HEADS + h, mem_len, stride=stride), :]
                                for h in range(MEM_HEADS)], axis=-1).astype(BF16)
        p = _softmax(_dot_nt(q_ref[bb].astype(BF16), kmat) * MEM_SCALE).astype(BF16)
        o2 = _dot(p, vmat)
        out = jnp.zeros((n_rows, MEM_HEAD_DIM), F32)
        for h in range(MEM_HEADS):
            out = out + jnp.where(rowh == h, o2[:, h * MEM_HEAD_DIM:(h + 1) * MEM_HEAD_DIM], 0.0)
        o_ref[bb] = out


def mem_sample_attn(qbd, cache_mem, layer, bt):
    nb, n_rows, _ = qbd.shape
    rows = cache_mem.shape[2]
    return pl.pallas_call(
        functools.partial(_mem_sample_kernel, bt=bt, mem_len=rows // (2 * MEM_HEADS)),
        grid=(nb // bt,),
        in_specs=[pl.BlockSpec((bt, n_rows, D_MEM), lambda i: (i, 0, 0)),
                  pl.BlockSpec((None, bt, rows, MEM_HEAD_DIM), lambda i: (layer, i, 0, 0))],
        out_specs=pl.BlockSpec((bt, n_rows, MEM_HEAD_DIM), lambda i: (i, 0, 0)),
        out_shape=jax.ShapeDtypeStruct((nb, n_rows, MEM_HEAD_DIM), F32),
        compiler_params=_cparams(("arbitrary",)),
        name="mem_sample_attn",
    )(qbd, cache_mem)


def _silu(a):
    return a * _sigmoid(a)


def _ffn_prompt_kernel(x_ref, xh_ref, g_ref, wg_ref, wu_ref, cw_ref, wd_ref, gf_ref,
                       o_ref, gt_ref, h_s, hh_s, *, tiles_per_seq, final):
    j = pl.program_id(1)
    tm = x_ref.shape[0]

    @pl.when(j == 0)
    def _():
        first = (pl.program_id(0) % tiles_per_seq) == 0
        x = x_ref[...]
        h_s[...] = _rms(x, g_ref[...]).astype(BF16)
        hh = jnp.where(first, 0.0, _rms(xh_ref[...], g_ref[...]))
        hh_s[...] = jnp.concatenate([hh, jnp.zeros_like(hh)], axis=0).astype(BF16)
        o_ref[...] = x

    h = h_s[...]
    gate = _dot(h, wg_ref[...])
    gate_h = _dot(hh_s[...], wg_ref[...])
    g1, g2 = _shift_rows(gate, gate_h[7:8], gate_h[6:7])
    cw = cw_ref[...]
    a = cw[0:1] * g2 + cw[1:2] * g1 + cw[2:3] * gate
    z = (_silu(a) * _dot(h, wu_ref[...])).astype(BF16)
    o_ref[...] += _dot(z, wd_ref[...])
    gt_ref[...] = gate[tm - 8:, :]

    if final:
        @pl.when(j == pl.num_programs(1) - 1)
        def _():
            o_ref[...] = _rms(o_ref[...], gf_ref[...])


def ffn_prompt(x, g, wg, wu, cw, wd, gf, tm, tn, seq, final, layer):
    m = x.shape[0]
    nt = m // tm
    nj = D_FF // tn
    const = lambda i, j: (0, 0)
    return pl.pallas_call(
        functools.partial(_ffn_prompt_kernel, tiles_per_seq=seq // tm, final=final),
        grid=(nt, nj),
        in_specs=[pl.BlockSpec((tm, D_MODEL), lambda i, j: (i, 0)),
                  pl.BlockSpec((8, D_MODEL), lambda i, j: (jnp.maximum(i * (tm // 8) - 1, 0), 0)),
                  pl.BlockSpec((1, D_MODEL), const),
                  _wspec((D_MODEL, tn), lambda i, j: (0, j), layer),
                  _wspec((D_MODEL, tn), lambda i, j: (0, j), layer),
                  pl.BlockSpec((8, tn), lambda i, j: (0, j)),
                  _wspec((tn, D_MODEL), lambda i, j: (j, 0), layer),
                  pl.BlockSpec((1, D_MODEL), const)],
        out_specs=[pl.BlockSpec((tm, D_MODEL), lambda i, j: (i, 0)),
                   pl.BlockSpec((None, 8, tn), lambda i, j: (i, 0, j))],
        out_shape=[jax.ShapeDtypeStruct((m, D_MODEL), F32),
                   jax.ShapeDtypeStruct((nt, 8, D_FF), F32)],
        scratch_shapes=[pltpu.VMEM((tm, D_MODEL), BF16),
                        pltpu.VMEM((16, D_MODEL), BF16)],
        compiler_params=_cparams(("arbitrary", "arbitrary")),
        name="ffn_prompt",
    )(x, x, g, wg, wu, cw, wd, gf)


def _ffn_sample_kernel(x_ref, st_ref, g_ref, wg_ref, wu_ref, cw_ref, wd_ref, gf_ref,
                       o_ref, nst_ref, h_s, acc_s, *, dec_seq, final):
    j = pl.program_id(0)

    nb = x_ref.shape[1]

    @pl.when(j == 0)
    def _():
        for t in range(dec_seq):
            h_s[t * nb:(t + 1) * nb, :] = _rms(x_ref[t], g_ref[...]).astype(BF16)
        acc_s[...] = jnp.zeros_like(acc_s)

    h = h_s[...]
    gate = _dot(h, wg_ref[...])
    up = _dot(h, wu_ref[...])
    gcat = [st_ref[0], st_ref[1]] + [gate[t * nb:(t + 1) * nb] for t in range(dec_seq)]
    cw = cw_ref[...]
    a = jnp.concatenate([cw[0:1] * gcat[t] + cw[1:2] * gcat[t + 1] + cw[2:3] * gcat[t + 2]
                         for t in range(dec_seq)], axis=0)
    acc_s[...] += _dot((_silu(a) * up).astype(BF16), wd_ref[...])
    nst_ref[0] = gcat[dec_seq]
    nst_ref[1] = gcat[dec_seq + 1]

    @pl.when(j == pl.num_programs(0) - 1)
    def _():
        for t in range(dec_seq):
            y = x_ref[t] + acc_s[t * nb:(t + 1) * nb, :]
            o_ref[t] = _rms(y, gf_ref[...]) if final else y


def ffn_sample(x, state, g, wg, wu, cw, wd, gf, tn, final, layer):
    dec_seq, nb, _ = x.shape
    full3 = lambda j: (0, 0, 0)
    const = lambda j: (0, 0)
    return pl.pallas_call(
        functools.partial(_ffn_sample_kernel, dec_seq=dec_seq, final=final),
        grid=(D_FF // tn,),
        in_specs=[pl.BlockSpec((dec_seq, nb, D_MODEL), full3),
                  pl.BlockSpec((2, nb, tn), lambda j: (0, 0, j)),
                  pl.BlockSpec((1, D_MODEL), const),
                  _wspec((D_MODEL, tn), lambda j: (0, j), layer),
                  _wspec((D_MODEL, tn), lambda j: (0, j), layer),
                  pl.BlockSpec((8, tn), lambda j: (0, j)),
                  _wspec((tn, D_MODEL), lambda j: (j, 0), layer),
                  pl.BlockSpec((1, D_MODEL), const)],
        out_specs=[pl.BlockSpec((dec_seq, nb, D_MODEL), full3),
                   pl.BlockSpec((2, nb, tn), lambda j: (0, 0, j))],
        out_shape=[jax.ShapeDtypeStruct((dec_seq, nb, D_MODEL), F32),
                   jax.ShapeDtypeStruct((2, nb, D_FF), F32)],
        scratch_shapes=[pltpu.VMEM((dec_seq * nb, D_MODEL), BF16),
                        pltpu.VMEM((dec_seq * nb, D_MODEL), F32)],
        compiler_params=_cparams(("arbitrary",)),
        name="ffn_sample",
    )(x, state, g, wg, wu, cw, wd, gf)


def _overlap(n_cmp, n_sel, rows, cols):
    c0 = np.arange(n_cmp)[:, None] * CMP_STRIDE
    s0 = np.arange(n_sel)[None, :] * SEL_BLOCK
    ov = np.minimum(c0 + CMP_BLOCK, s0 + SEL_BLOCK) - np.maximum(c0, s0)
    out = np.zeros((rows, cols), np.float32)
    out[:n_cmp, :n_sel] = np.clip(ov, 0, None).astype(np.float32) / CMP_BLOCK
    return out


def _block_expand(n_keys):
    return jnp.asarray(np.arange(128)[:, None] == (np.arange(n_keys)[None, :] // SEL_BLOCK), BF16)


def _chunk_perm():
    row = np.arange(PAGE)
    return jnp.asarray((row[:, None] % RPC) * CMP_STRIDE + row[:, None] // RPC == row[None, :], BF16)


def _pad_rows(a, rows):
    return jnp.concatenate([a, jnp.zeros((rows - a.shape[0],) + a.shape[1:], a.dtype)], axis=0)


def _prep_w_in(w_in_l):
    offs = np.cumsum((D_NSA, KV_W, KV_W, KV_W, N_BRANCH * N_HEADS, D_CONV, D_CONV, D_CONV))
    q, kc, ks, kw, gl, cb, cc, ch = jnp.split(w_in_l.T, [int(o) for o in offs[:-1]], axis=0)
    pad = jnp.zeros((D_KVT - ROW_G - N_BRANCH * N_HEADS, D_MODEL), F32)
    w_main = jnp.concatenate([q * (HEAD_DIM ** -0.5), cb, cc, ch], axis=0).astype(BF16)
    wt_kv = jnp.concatenate([kc, ks, kw, gl, pad], axis=0).astype(BF16)
    return w_main[None], wt_kv[None]


def _prep_cmp(w_cmp_l, pe_cmp_l):
    blocks = [jnp.pad(w_cmp_l[c], ((0, 0), (0, 0), ((c * N_KV + g) * HEAD_DIM,
                                                     KV_W - (c * N_KV + g + 1) * HEAD_DIM)))
              for c in range(2) for g in range(N_KV)]
    wbd = jnp.concatenate(blocks, axis=1).astype(BF16)
    pe = pe_cmp_l.reshape(2, 2, CMP_STRIDE, HEAD_DIM).transpose(1, 0, 3, 2)
    pet = jnp.broadcast_to(pe[:, :, None, :, None, :], (2, 2, N_KV, HEAD_DIM, RPC, CMP_STRIDE))
    return wbd, pet.reshape(2, KV_W, PAGE)


TM_PROJ = 1024
TN_PROJ = 1024
TM_MIX = 256
TM_FFN = 1024
TN_FFN = 512
BT_CMP = 4
BT_NSA = 4
BT_MEM = 8
NEW_PAD = 16


def _kv_rows(kvt, row, start=0):
    slab = kvt[:, row:row + KV_W, start:]
    b, _, t = slab.shape
    return slab.reshape(b, 2, N_KV, HEAD_DIM, t).transpose(0, 4, 1, 2, 3)


def _layer_prompt(x, mem_kv, lw, consts, layer, batch, seq, final):
    p_main, kvt = proj_in(x, lw['g_mix'], lw['w_main'], lw['wt_kv'], TM_PROJ, TN_PROJ, seq, 0)
    cmp = compress_prompt(kvt, lw['pet'], consts['perm'], lw['wbd'])
    n_cmp_pad = seq // CMP_STRIDE
    cmp_t = cmp.reshape(batch, n_cmp_pad, 2, N_KV, HEAD_DIM).transpose(0, 2, 3, 1, 4)
    o_nsa = nsa_prompt(p_main, kvt, cmp_t, consts['ovt_p'], consts['expand_p'])
    x2, u_tail = mixout_prompt(o_nsa, p_main, lw['conv_w'], lw['g_out_nsa'], lw['g_out_conv'],
                               lw['w_out'], x, lw['g_mem'], lw['w_mem_q'], mem_kv, lw['w_mem_o'],
                               TM_MIX, seq, layer)
    x3, g_tail = ffn_prompt(x2, lw['g_ffn'], lw['w_ff_gate'], lw['w_ff_up'], lw['ffn_conv_w'],
                            lw['w_ff_down'], lw['g_final'], TM_FFN, TN_FFN, seq, final, layer)
    conv_state = u_tail.reshape(batch, seq // TM_MIX, 8, D_CONV)[:, -1, 6:8]
    ffn_state = g_tail.reshape(batch, seq // TM_FFN, 8, D_FF)[:, -1, 6:8]
    return (x3, _kv_rows(kvt, ROW_KC), _kv_rows(kvt, ROW_KS),
            _kv_rows(kvt, ROW_KW, seq - min(WINDOW, seq)), conv_state, ffn_state)


def _layer_sample(x, lw, consts, layer, cache_cmp, cache_sel, page_idx, cache_win, prev_win, cache_mem,
                  st_conv, st_ffn, final):
    dec_seq, nb, _ = x.shape
    rows = dec_seq * nb
    p_s, kvt = proj_in(x.reshape(rows, D_MODEL), lw['g_mix'], lw['w_main'], lw['wt_kv'],
                       rows, TN_PROJ, rows, 0)
    p3 = p_s.reshape(dec_seq, nb, D_MAIN)
    kvn = kvt.reshape(D_KVT, rows).T.reshape(dec_seq, nb, D_KVT).transpose(1, 0, 2)
    kvn_pad = jnp.concatenate([kvn, jnp.zeros((nb, NEW_PAD - dec_seq, D_KVT), F32)], axis=1)
    cmp_s = compress_paged(cache_cmp, page_idx, lw['pet'], consts['perm'], lw['wbd'], BT_CMP)
    q = p3[:, :, COL_Q:COL_Q + D_NSA].reshape(dec_seq, nb, N_KV, HPG, HEAD_DIM).transpose(1, 3, 2, 0, 4)
    gsel = jnp.eye(N_KV, dtype=F32)
    qbd = (q[:, :, :, :, None, :] * gsel[None, None, :, None, :, None]).reshape(
        nb, HPG * N_KV * dec_seq, KV_HALF)
    gates = kvn[:, :, ROW_G:ROW_G + N_BRANCH * N_HEADS].reshape(
        nb, dec_seq, N_KV, HPG, N_BRANCH).transpose(0, 3, 2, 1, 4).reshape(nb, HPG * N_KV * dec_seq, N_BRANCH)
    o, win_stack = nsa_sample(cache_sel, page_idx, qbd, gates, cmp_s, kvn_pad, cache_win, layer, prev_win,
                              consts['ov_s'], consts['expand_s'], BT_NSA)
    o_nsa = o.reshape(nb, HPG, N_KV, dec_seq, HEAD_DIM).transpose(3, 0, 2, 1, 4).reshape(dec_seq, nb, D_NSA)
    x1, conv_state = mixout_sample(o_nsa, p3, st_conv, lw['conv_w'], lw['g_out_nsa'],
                                   lw['g_out_conv'], lw['w_out'], x, layer)
    qm = rms_matmul(x1.reshape(rows, D_MODEL), lw['g_mem'], lw['w_mem_q'], rows, D_MEM, layer)
    qh = qm.reshape(dec_seq, nb, MEM_HEADS, MEM_HEAD_DIM).transpose(1, 2, 0, 3)
    qmbd = (qh[:, :, :, None, :] * jnp.eye(MEM_HEADS, dtype=F32)[None, :, None, :, None]).reshape(
        nb, MEM_HEADS * dec_seq, D_MEM)
    om = mem_sample_attn(qmbd, cache_mem, layer, BT_MEM)
    om = om.reshape(nb, MEM_HEADS, dec_seq, MEM_HEAD_DIM).transpose(2, 0, 1, 3).reshape(rows, D_MEM)
    x2 = matmul_residual(x1.reshape(rows, D_MODEL), om, lw['w_mem_o'], rows, layer).reshape(
        dec_seq, nb, D_MODEL)
    x3, ffn_state = ffn_sample(x2, st_ffn, lw['g_ffn'], lw['w_ff_gate'], lw['w_ff_up'],
                               lw['ffn_conv_w'], lw['w_ff_down'], lw['g_final'], TN_FFN, final, layer)
    kv_c = kvn[:, :, ROW_KC:ROW_KC + KV_W].reshape(nb, dec_seq, 2, N_KV, HEAD_DIM)
    kv_s = kvn[:, :, ROW_KS:ROW_KS + KV_W].reshape(nb, dec_seq, 2, N_KV, HEAD_DIM)
    return (x3, kv_c, kv_s, win_stack, conv_state.transpose(1, 0, 2), ffn_state.transpose(1, 0, 2))


def kernel(x_prompt, x_sample, cache_cmp_kv, cache_sel_kv, cache_win_kv, cache_mem_kv,
           state_conv, state_ffn_conv, page_table, mem_prompt,
           g_mix, w_in, w_cmp, pe_cmp, conv_w, g_out_nsa, g_out_conv, w_out,
           g_mem_src, w_mem_kv, g_mem, w_mem_q, w_mem_o, g_ffn, w_ff_gate, w_ff_up,
           ffn_conv_w, w_ff_down, g_final):
    batch, seq, _ = x_prompt.shape
    nb, dec_seq, _ = x_sample.shape
    depth = w_in.shape[0]
    n_phys = cache_cmp_kv.shape[1]
    npages = page_table.shape[1]
    past = npages * PAGE
    win_buf = cache_win_kv.shape[2]
    mem_len = mem_prompt.shape[1]

    n_cmp_s = (past + dec_seq - CMP_BLOCK) // CMP_STRIDE + 1
    n_sel_s = -(-(past + dec_seq) // SEL_BLOCK)
    n_cmp_p = (seq - CMP_BLOCK) // CMP_STRIDE + 1
    n_sel_p = seq // SEL_BLOCK
    consts = {
        'ov_s': jnp.asarray(_overlap(n_cmp_s, n_sel_s, past // CMP_STRIDE, 128), BF16),
        'expand_s': _block_expand(past + PAGE),
        'ovt_p': jnp.asarray(_overlap(n_cmp_p, n_sel_p, seq // CMP_STRIDE, n_sel_p).T, BF16),
        'expand_p': _block_expand(seq),
        'perm': _chunk_perm(),
    }

    cache_cmp = cache_cmp_kv.transpose(0, 1, 3, 4, 5, 2).reshape(depth * n_phys, 2, N_KV, HEAD_DIM, PAGE)
    cache_sel = cache_sel_kv.transpose(0, 1, 3, 4, 5, 2).reshape(depth * n_phys, 2, N_KV, HEAD_DIM, PAGE)
    cache_win = cache_win_kv.transpose(0, 1, 3, 4, 5, 2)
    cache_mem = cache_mem_kv.reshape(depth, nb, mem_len * 2 * MEM_HEADS, MEM_HEAD_DIM)

    xp = x_prompt.reshape(batch * seq, D_MODEL)
    xs = x_sample.transpose(1, 0, 2)
    mem_rows = mem_prompt.reshape(batch * mem_len, D_MODEL)
    outs = [[] for _ in range(10)]
    win_stack = None
    stacked = {
        'w_out': w_out.astype(BF16), 'w_mem_q': w_mem_q.astype(BF16), 'w_mem_o': w_mem_o.astype(BF16),
        'w_ff_gate': w_ff_gate.astype(BF16), 'w_ff_up': w_ff_up.astype(BF16),
        'w_ff_down': w_ff_down.astype(BF16),
    }
    w_mem_kv_b = w_mem_kv.astype(BF16)
    for l in range(depth):
        wbd, pet = _prep_cmp(w_cmp[l], pe_cmp[l])
        w_main, wt_kv = _prep_w_in(w_in[l])
        lw = dict(stacked)
        lw.update({
            'g_mix': g_mix[l][None], 'w_main': w_main, 'wt_kv': wt_kv, 'wbd': wbd, 'pet': pet,
            'conv_w': _pad_rows(conv_w[l], 8), 'g_out_nsa': g_out_nsa[l][None],
            'g_out_conv': g_out_conv[l][None], 'g_mem': g_mem[l][None], 'g_ffn': g_ffn[l][None],
            'ffn_conv_w': _pad_rows(ffn_conv_w[l], 8), 'g_final': g_final[None],
        })
        final = l == depth - 1
        mem_kv_p = rms_matmul(mem_rows, g_mem_src[l][None], w_mem_kv_b,
                              batch * mem_len, 2 * D_MEM, l).reshape(batch, mem_len, 2 * D_MEM)
        xp, kc, ks, kw, cst, fst = _layer_prompt(xp, mem_kv_p, lw, consts, l, batch, seq, final)
        page_idx = page_table + l * n_phys
        xs, kc2, ks2, win_stack, cst2, fst2 = _layer_sample(
            xs, lw, consts, l, cache_cmp, cache_sel, page_idx, cache_win, win_stack, cache_mem,
            state_conv[l].transpose(1, 0, 2), state_ffn_conv[l].transpose(1, 0, 2), final)
        for lst, val in zip(outs, (kc, ks, kw, mem_kv_p, cst, fst, kc2, ks2, cst2, fst2)):
            lst.append(val)

    st = [jnp.stack(o) for o in outs]
    return (xp.reshape(batch, seq, D_MODEL), xs.transpose(1, 0, 2),
            st[0], st[1], st[2],
            st[3].reshape(depth, batch, mem_len, 2, MEM_HEADS, MEM_HEAD_DIM),
            st[4], st[5], st[6], st[7],
            win_stack.transpose(0, 1, 5, 2, 3, 4),
            st[8], st[9])
```

```python
import functools

import numpy as np
import jax
import jax.numpy as jnp
from jax import lax
from jax.experimental import pallas as pl
from jax.experimental.pallas import tpu as pltpu

F32 = jnp.float32
BF16 = jnp.bfloat16

D_MODEL = 2048
D_NSA = 1024
D_CONV = 1024
HEAD_DIM = 64
N_HEADS = 16
N_KV = 2
HPG = 8
N_BRANCH = 3
KV_W = 2 * N_KV * HEAD_DIM
KV_HALF = N_KV * HEAD_DIM
CMP_BLOCK = 32
CMP_STRIDE = 16
SEL_BLOCK = 64
TOPK = 16
WINDOW = 512
Q_BLOCK = 128
FORCE_BONUS = 1e4
D_FF = 5632
MEM_HEADS = 4
MEM_HEAD_DIM = 128
D_MEM = 512
PAGE = 128
EPS = 1e-6
NEG = -1e30

COL_Q = 0
COL_CB = 1024
COL_CC = 2048
COL_CH = 3072
D_MAIN = 4096
ROW_KC = 0
ROW_KS = 256
ROW_KW = 512
ROW_G = 768
D_KVT = 896

VMEM_LIMIT = 56 * 1024 * 1024


def _cparams(sem):
    return pltpu.CompilerParams(dimension_semantics=sem, vmem_limit_bytes=VMEM_LIMIT)


def _rms(x, g):
    return x * lax.rsqrt(jnp.mean(x * x, axis=-1, keepdims=True) + EPS) * g


def _dot(a, b):
    return jnp.dot(a, b, preferred_element_type=F32)


def _dot_nt(a, b):
    return lax.dot_general(a, b, (((1,), (1,)), ((), ())), preferred_element_type=F32)


def _softmax_bias(s, bias):
    s = s + bias
    e = jnp.exp(s - jnp.max(s, axis=-1, keepdims=True))
    return e, jnp.sum(e, axis=-1, keepdims=True)


def _sigmoid(x):
    return 1.0 / (1.0 + jnp.exp(-x))


def _rms_matmul_kernel(x_ref, g_ref, w_ref, o_ref, h_ref):
    @pl.when(pl.program_id(1) == 0)
    def _():
        h_ref[...] = _rms(x_ref[...], g_ref[...]).astype(BF16)

    o_ref[...] = _dot(h_ref[...], w_ref[...]).astype(o_ref.dtype)


def _wspec(block, imap, layer, resident=False):
    kwargs = {'pipeline_mode': pl.Buffered(1)} if resident else {}
    return pl.BlockSpec((None,) + block, lambda *a: (layer,) + imap(*a), **kwargs)


def rms_matmul(x, g, w, tm, tn, layer):
    m, d = x.shape
    n = w.shape[2]
    return pl.pallas_call(
        _rms_matmul_kernel,
        grid=(m // tm, n // tn),
        in_specs=[pl.BlockSpec((tm, d), lambda i, j: (i, 0)),
                  pl.BlockSpec((1, d), lambda i, j: (0, 0)),
                  _wspec((d, tn), lambda i, j: (0, j), layer)],
        out_specs=pl.BlockSpec((tm, tn), lambda i, j: (i, j)),
        out_shape=jax.ShapeDtypeStruct((m, n), F32),
        scratch_shapes=[pltpu.VMEM((tm, d), BF16)],
        compiler_params=_cparams(("arbitrary", "arbitrary")),
        name="rms_matmul",
    )(x, g, w)


def _proj_in_kernel(x_ref, g_ref, w_ref, wt_ref, o_ref, ot_ref, h_ref):
    @pl.when(pl.program_id(1) == 0)
    def _():
        h = _rms(x_ref[...], g_ref[...]).astype(BF16)
        h_ref[...] = h
        ot_ref[...] = _dot_nt(wt_ref[...], h)

    o_ref[...] = _dot_nt(h_ref[...], w_ref[...])


def proj_in(x, g, w_main, wt_kv, tm, tn, seq, layer):
    m, d = x.shape
    tps = seq // tm
    return pl.pallas_call(
        _proj_in_kernel,
        grid=(m // tm, D_MAIN // tn),
        in_specs=[pl.BlockSpec((tm, d), lambda i, j: (i, 0)),
                  pl.BlockSpec((1, d), lambda i, j: (0, 0)),
                  _wspec((tn, d), lambda i, j: (j, 0), layer),
                  _wspec((D_KVT, d), lambda i, j: (0, 0), layer)],
        out_specs=[pl.BlockSpec((tm, tn), lambda i, j: (i, j)),
                   pl.BlockSpec((None, D_KVT, tm), lambda i, j: (i // tps, 0, i % tps))],
        out_shape=[jax.ShapeDtypeStruct((m, D_MAIN), F32),
                   jax.ShapeDtypeStruct((m // seq, D_KVT, seq), F32)],
        scratch_shapes=[pltpu.VMEM((tm, d), BF16)],
        compiler_params=_cparams(("arbitrary", "arbitrary")),
        name="proj_in",
    )(x, g, w_main, wt_kv)


def _matmul_res_kernel(x_ref, a_ref, w_ref, o_ref):
    o_ref[...] = x_ref[...] + _dot(a_ref[...].astype(BF16), w_ref[...])


def matmul_residual(x, a, w, tm, layer):
    m, n = x.shape
    k = a.shape[1]
    return pl.pallas_call(
        _matmul_res_kernel,
        grid=(m // tm,),
        in_specs=[pl.BlockSpec((tm, n), lambda i: (i, 0)),
                  pl.BlockSpec((tm, k), lambda i: (i, 0)),
                  _wspec((k, n), lambda i: (0, 0), layer)],
        out_specs=pl.BlockSpec((tm, n), lambda i: (i, 0)),
        out_shape=jax.ShapeDtypeStruct((m, n), F32),
        compiler_params=_cparams(("arbitrary",)),
        name="matmul_residual",
    )(x, a, w)


RPC = PAGE // CMP_STRIDE


def _compress_chunks(chunk_at, n_chunks, pet_ref, perm_ref, w_ref, xs_ref):
    perm = perm_ref[...]
    for ci in range(n_chunks):
        chunk = chunk_at(ci)
        for a in range(2):
            xs_ref[a, ci] = _dot_nt(perm, (chunk + pet_ref[a]).astype(BF16))
    rows = n_chunks * RPC
    acc0 = jnp.zeros((rows, KV_W), F32)
    acc1 = jnp.zeros((rows, KV_W), F32)
    for r in range(CMP_STRIDE):
        x0 = xs_ref[0, :, r * RPC:(r + 1) * RPC, :].reshape(rows, KV_W).astype(BF16)
        x1 = xs_ref[1, :, r * RPC:(r + 1) * RPC, :].reshape(rows, KV_W).astype(BF16)
        acc0 = acc0 + _dot(x0, w_ref[r])
        acc1 = acc1 + _dot(x1, w_ref[CMP_STRIDE + r])
    return acc0 + pltpu.roll(acc1, rows - 1, 0)


def _compress_prompt_kernel(kvt_ref, pet_ref, perm_ref, w_ref, o_ref, xs_ref, *, n_chunks):
    out = _compress_chunks(lambda ci: kvt_ref[:, ci * PAGE:(ci + 1) * PAGE], n_chunks,
                           pet_ref, perm_ref, w_ref, xs_ref)
    o_ref[...] = out.astype(BF16)


def compress_prompt(kvt, pet, perm, wbd):
    batch, _, seq = kvt.shape
    n_chunks = seq // PAGE
    nblk = seq // CMP_STRIDE
    return pl.pallas_call(
        functools.partial(_compress_prompt_kernel, n_chunks=n_chunks),
        grid=(batch,),
        in_specs=[pl.BlockSpec((None, KV_W, seq), lambda b: (b, ROW_KC // KV_W, 0)),
                  pl.BlockSpec((2, KV_W, PAGE), lambda b: (0, 0, 0)),
                  pl.BlockSpec((PAGE, PAGE), lambda b: (0, 0)),
                  pl.BlockSpec((CMP_BLOCK, KV_W, KV_W), lambda b: (0, 0, 0))],
        out_specs=pl.BlockSpec((None, nblk, KV_W), lambda b: (b, 0, 0)),
        out_shape=jax.ShapeDtypeStruct((batch, nblk, KV_W), BF16),
        scratch_shapes=[pltpu.VMEM((2, n_chunks, PAGE, KV_W), F32)],
        compiler_params=_cparams(("arbitrary",)),
        name="compress_prompt",
    )(kvt, pet, perm, wbd)


def _compress_paged_kernel(pt_ref, *refs, bt, npages):
    page_refs = refs[:bt * npages]
    pet_ref, perm_ref, w_ref, o_ref, xs_ref = refs[bt * npages:]
    out = _compress_chunks(lambda ci: page_refs[ci][...].reshape(KV_W, PAGE), bt * npages,
                           pet_ref, perm_ref, w_ref, xs_ref)
    o_ref[...] = out.astype(BF16).reshape(bt, npages * RPC, KV_W)


def compress_paged(cache, page_idx, pet, perm, wbd, bt):
    nb, npages = page_idx.shape
    nblk = npages * RPC
    page_specs = [
        pl.BlockSpec((None, 2, N_KV, HEAD_DIM, PAGE), functools.partial(
            lambda i, pt, bb, p: (pt[i * bt + bb, p], 0, 0, 0, 0), bb=bb, p=p))
        for bb in range(bt) for p in range(npages)]
    grid_spec = pltpu.PrefetchScalarGridSpec(
        num_scalar_prefetch=1,
        grid=(nb // bt,),
        in_specs=page_specs + [pl.BlockSpec((2, KV_W, PAGE), lambda i, pt: (0, 0, 0)),
                               pl.BlockSpec((PAGE, PAGE), lambda i, pt: (0, 0)),
                               pl.BlockSpec((CMP_BLOCK, KV_W, KV_W), lambda i, pt: (0, 0, 0))],
        out_specs=pl.BlockSpec((bt, nblk, KV_W), lambda i, pt: (i, 0, 0)),
        scratch_shapes=[pltpu.VMEM((2, bt * npages, PAGE, KV_W), F32)])
    return pl.pallas_call(
        functools.partial(_compress_paged_kernel, bt=bt, npages=npages),
        grid_spec=grid_spec,
        out_shape=jax.ShapeDtypeStruct((nb, nblk, KV_W), BF16),
        compiler_params=_cparams(("arbitrary",)),
        name="compress_paged",
    )(page_idx, *([cache] * (bt * npages)), pet, perm, wbd)


def _importance(imp, idx, tpos, n_sel):
    valid = (idx * SEL_BLOCK <= tpos) & (idx < n_sel)
    cur = tpos // SEL_BLOCK
    forced = (idx == 0) | (idx == cur) | (idx == cur - 1)
    return jnp.where(valid, imp + jnp.where(forced, FORCE_BONUS, 0.0), NEG)


def _topk_rows(imp, n_cand):
    groups = [imp[v * 8:(v + 1) * 8] for v in range(n_cand // 8)]
    sub = lax.broadcasted_iota(jnp.int32, groups[0].shape, 0)
    ranks = [jnp.zeros(g.shape, F32) for g in groups]
    for jp in range(n_cand):
        c = imp[jp:jp + 1, :]
        for v, g in enumerate(groups):
            ge = jnp.where(c >= g, 1.0, 0.0)
            gt = jnp.where(c > g, 1.0, 0.0)
            if v * 8 > jp:
                beats = ge
            elif v * 8 + 7 < jp:
                beats = gt
            else:
                beats = jnp.where(sub > jp - v * 8, ge, gt)
            ranks[v] = ranks[v] + beats
    return jnp.concatenate([jnp.where(r < TOPK, 1.0, 0.0) for r in ranks], axis=0)


def _topk_lanes(imp, idx, n_cand):
    rank = jnp.zeros(imp.shape, F32)
    for jp in range(n_cand):
        c = imp[:, jp:jp + 1]
        rank = rank + jnp.where(idx > jp, jnp.where(c >= imp, 1.0, 0.0), jnp.where(c > imp, 1.0, 0.0))
    return jnp.where(rank < TOPK, 1.0, 0.0)


KEY_TILE = 512
V_PAD = 16


def _tile_heads(x):
    return jnp.concatenate([x] * HPG, axis=0)


def _nsa_prompt_kernel(q_ref, gate_ref, cmp_ref, kst_ref, vst_ref, kwt_ref, vwt_ref, ovt_ref, exp_ref,
                       o_ref, ksb, vsb, kwb, vwb, *, seq):
    g = pl.program_id(1)
    i = pl.program_id(2)
    n_cmp_pad = seq // CMP_STRIDE
    n_sel = seq // SEL_BLOCK

    @pl.when(i == 0)
    def _():
        ones_row = jnp.where(lax.broadcasted_iota(jnp.int32, (V_PAD, seq), 0) == 0, 1.0, 0.0).astype(BF16)
        ksb[...] = kst_ref[...].astype(BF16)
        vsb[0:HEAD_DIM, :] = vst_ref[...].astype(BF16)
        vsb[HEAD_DIM:HEAD_DIM + V_PAD, :] = ones_row
        kwb[...] = kwt_ref[...].astype(BF16)
        vwb[0:HEAD_DIM, :] = vwt_ref[...].astype(BF16)
        vwb[HEAD_DIM:HEAD_DIM + V_PAD, :] = ones_row

    t0 = i * Q_BLOCK
    qb = q_ref[...]
    qs = jnp.concatenate([qb[:, h * HEAD_DIM:(h + 1) * HEAD_DIM] for h in range(HPG)],
                         axis=0).astype(BF16)
    tcol = lax.broadcasted_iota(jnp.int32, (Q_BLOCK, 1), 0) + t0

    ncol = lax.broadcasted_iota(jnp.int32, (Q_BLOCK, n_cmp_pad), 1)
    vis = jnp.where(ncol < n_cmp_pad - 1, ncol * CMP_STRIDE + (CMP_BLOCK - 1), seq) <= tcol
    e_c, l_c = _softmax_bias(_dot_nt(qs, cmp_ref[0]), _tile_heads(jnp.where(vis, 0.0, NEG)))
    any_vis = _tile_heads(jnp.where(tcol >= CMP_BLOCK - 1, 1.0, 0.0))
    p_c = (e_c * (any_vis / jnp.maximum(l_c, 1e-30))).astype(BF16)
    o_c = _dot(p_c, cmp_ref[1])

    po = _dot_nt(ovt_ref[...], p_c)
    imp_t = po[:, 0:Q_BLOCK]
    for h in range(1, HPG):
        imp_t = imp_t + po[:, h * Q_BLOCK:(h + 1) * Q_BLOCK]
    jj = lax.broadcasted_iota(jnp.int32, (n_sel, Q_BLOCK), 0)
    tt = lax.broadcasted_iota(jnp.int32, (n_sel, Q_BLOCK), 1) + t0
    sel_t = _topk_rows(_importance(imp_t, jj, tt, n_sel), n_sel)
    sel = jnp.concatenate([sel_t, jnp.zeros((128 - n_sel, Q_BLOCK), F32)], axis=0).T.astype(BF16)

    kk = lax.broadcasted_iota(jnp.int32, (Q_BLOCK, KEY_TILE), 1)

    def scores(c):
        k0 = pl.multiple_of(c * KEY_TILE, KEY_TILE)
        msel = _dot(sel, exp_ref[:, pl.ds(k0, KEY_TILE)])
        bias = jnp.where(msel > 0.5, jnp.where(kk + k0 <= tcol, 0.0, NEG), NEG)
        return _dot(qs, ksb[:, pl.ds(k0, KEY_TILE)]) + _tile_heads(bias)

    def accumulate(c, s, carry):
        m, acc = carry
        k0 = pl.multiple_of(c * KEY_TILE, KEY_TILE)
        m_new = jnp.maximum(m, jnp.max(s, axis=-1, keepdims=True))
        p = jnp.exp(s - m_new).astype(BF16)
        return m_new, jnp.exp(m - m_new) * acc + _dot_nt(p, vsb[:, pl.ds(k0, KEY_TILE)])

    def tile_pair(c2, carry):
        s_a = scores(2 * c2)
        s_b = scores(2 * c2 + 1)
        return accumulate(2 * c2 + 1, s_b, accumulate(2 * c2, s_a, carry))

    n_rows = HPG * Q_BLOCK
    m0 = jnp.full((n_rows, 1), NEG, F32)
    a0 = jnp.zeros((n_rows, HEAD_DIM + V_PAD), F32)
    n_tiles = (t0 + Q_BLOCK + KEY_TILE - 1) // KEY_TILE
    def tile_quad(c4, carry):
        ss = [scores(4 * c4 + u) for u in range(4)]
        for u in range(4):
            carry = accumulate(4 * c4 + u, ss[u], carry)
        return carry

    carry = lax.fori_loop(0, n_tiles // 4, tile_quad, (m0, a0))
    carry = lax.cond(n_tiles % 4 >= 2, lambda cr: tile_pair((n_tiles // 4) * 2, cr), lambda cr: cr, carry)
    _, acc_s = lax.cond(n_tiles % 2 == 1,
                        lambda cr: accumulate(n_tiles - 1, scores(n_tiles - 1), cr),
                        lambda cr: cr, carry)
    o_s = acc_s[:, 0:HEAD_DIM] / acc_s[:, HEAD_DIM:HEAD_DIM + 1]

    span = WINDOW + Q_BLOCK
    ws = pl.multiple_of(jnp.maximum(t0 - WINDOW, 0), Q_BLOCK)
    dlt = tcol - (lax.broadcasted_iota(jnp.int32, (Q_BLOCK, span), 1) + ws)
    bias_w = jnp.where(dlt >= 0, jnp.where(dlt < WINDOW, 0.0, NEG), NEG)
    s_w = _dot(qs, kwb[:, pl.ds(ws, span)]) + _tile_heads(bias_w)
    e_w = jnp.exp(s_w - jnp.max(s_w, axis=-1, keepdims=True)).astype(BF16)
    acc_w = _dot_nt(e_w, vwb[:, pl.ds(ws, span)])
    o_w = acc_w[:, 0:HEAD_DIM] / acc_w[:, HEAD_DIM:HEAD_DIM + 1]

    gs = _sigmoid(gate_ref[...]).T
    gsel = jnp.where(g == 0, gs[:, 0:HPG * N_BRANCH], gs[:, HPG * N_BRANCH:2 * HPG * N_BRANCH])
    for h in range(HPG):
        r0, r1 = h * Q_BLOCK, (h + 1) * Q_BLOCK
        c0 = h * N_BRANCH
        o_ref[:, h * HEAD_DIM:(h + 1) * HEAD_DIM] = (
            o_c[r0:r1] * gsel[:, c0:c0 + 1] + o_s[r0:r1] * gsel[:, c0 + 1:c0 + 2]
            + o_w[r0:r1] * gsel[:, c0 + 2:c0 + 3])


def nsa_prompt(p_main, kvt, cmp_t, ovt, expand):
    batch, _, seq = kvt.shape
    nqb = seq // Q_BLOCK
    gw = HPG * HEAD_DIM
    n_cmp_pad = seq // CMP_STRIDE
    n_sel = seq // SEL_BLOCK
    kvt_spec = lambda row: pl.BlockSpec((None, HEAD_DIM, seq),
                                        lambda b, g, i: (b, row // HEAD_DIM + g, 0))
    return pl.pallas_call(
        functools.partial(_nsa_prompt_kernel, seq=seq),
        grid=(batch, N_KV, nqb),
        in_specs=[
            pl.BlockSpec((Q_BLOCK, gw), lambda b, g, i: (b * nqb + i, g)),
            pl.BlockSpec((None, 128, Q_BLOCK), lambda b, g, i: (b, ROW_G // 128, i)),
            pl.BlockSpec((None, 2, None, n_cmp_pad, HEAD_DIM), lambda b, g, i: (b, 0, g, 0, 0)),
            kvt_spec(ROW_KS), kvt_spec(ROW_KS + KV_HALF),
            kvt_spec(ROW_KW), kvt_spec(ROW_KW + KV_HALF),
            pl.BlockSpec((n_sel, n_cmp_pad), lambda b, g, i: (0, 0)),
            pl.BlockSpec((128, seq), lambda b, g, i: (0, 0)),
        ],
        out_specs=pl.BlockSpec((Q_BLOCK, gw), lambda b, g, i: (b * nqb + i, g)),
        out_shape=jax.ShapeDtypeStruct((batch * seq, D_NSA), F32),
        scratch_shapes=[pltpu.VMEM((HEAD_DIM, seq), BF16), pltpu.VMEM((HEAD_DIM + V_PAD, seq), BF16)] * 2,
        compiler_params=_cparams(("arbitrary", "arbitrary", "arbitrary")),
        name="nsa_prompt",
    )(p_main, kvt, cmp_t, kvt, kvt, kvt, kvt, ovt, expand)


def _pick_group(o2, rg):
    return jnp.where(rg == 0, o2[:, 0:HEAD_DIM], o2[:, HEAD_DIM:2 * HEAD_DIM])


def _masked_softmax(s, mask):
    s = jnp.where(mask, s, NEG)
    m = jnp.max(s, axis=-1, keepdims=True)
    e = jnp.where(mask, jnp.exp(s - m), 0.0)
    return e / jnp.maximum(jnp.sum(e, axis=-1, keepdims=True), 1e-30)


def _masked_softmax2(s_a, mask_a, s_b, mask_b):
    s_a = jnp.where(mask_a, s_a, NEG)
    s_b = jnp.where(mask_b, s_b, NEG)
    m = jnp.maximum(jnp.max(s_a, axis=-1, keepdims=True), jnp.max(s_b, axis=-1, keepdims=True))
    e_a = jnp.where(mask_a, jnp.exp(s_a - m), 0.0)
    e_b = jnp.where(mask_b, jnp.exp(s_b - m), 0.0)
    tot = jnp.sum(e_a, axis=-1, keepdims=True) + jnp.sum(e_b, axis=-1, keepdims=True)
    inv = 1.0 / jnp.maximum(tot, 1e-30)
    return (e_a * inv).astype(BF16), (e_b * inv).astype(BF16)


def _nsa_sample_kernel(pt_ref, *refs, bt, npages, dec_seq):
    page_refs = refs[:bt * npages]
    rest = list(refs[bt * npages:])
    n_prev = rest[-5].shape[0] - 1
    prev_ref = rest.pop(5) if n_prev else None
    (q_ref, gate_ref, cmp_ref, new_ref, win_ref, ov_ref, exp_ref,
     o_ref, wout_ref, kts, vts, ktw, vtw) = rest
    past = npages * PAGE
    n_cmp = (past + dec_seq - CMP_BLOCK) // CMP_STRIDE + 1
    n_sel = -(-(past + dec_seq) // SEL_BLOCK)
    n_rows = HPG * N_KV * dec_seq
    win_buf = win_ref.shape[-1]
    n_cmp_pad = cmp_ref.shape[1]
    n_new = new_ref.shape[1]
    gt = N_KV * dec_seq

    rowi = lax.broadcasted_iota(jnp.int32, (n_rows, 1), 0)
    rg = (rowi // dec_seq) % N_KV
    tpos = past + rowi % dec_seq
    ncol = lax.broadcasted_iota(jnp.int32, (n_rows, n_cmp_pad), 1)
    m_c = (ncol * CMP_STRIDE + (CMP_BLOCK - 1) <= tpos) & (ncol < n_cmp)
    jj = lax.broadcasted_iota(jnp.int32, (gt, 128), 1)
    t8 = past + lax.broadcasted_iota(jnp.int32, (gt, 128), 0) % dec_seq
    d_new = tpos - (past + lax.broadcasted_iota(jnp.int32, (n_rows, n_new), 1))
    m_new = d_new >= 0
    m_wn = m_new & (d_new < WINDOW)
    wcol = lax.broadcasted_iota(jnp.int32, (n_rows, win_buf), 1)
    dlt = tpos - (wcol + (past - win_buf))
    m_w = (dlt >= 0) & (dlt < WINDOW)
    lane = lax.broadcasted_iota(jnp.int32, (KV_W, PAGE), 1)

    bbs = range(bt)

    for bb in bbs:
        for p in range(npages):
            page = page_refs[bb * npages + p]
            kts[bb, :, p * PAGE:(p + 1) * PAGE] = page[0].reshape(KV_HALF, PAGE).astype(BF16)
            vts[bb, :, p * PAGE:(p + 1) * PAGE] = page[1].reshape(KV_HALF, PAGE).astype(BF16)
        win = win_ref[bb].reshape(KV_W, win_buf)
        ktw[bb] = win[0:KV_HALF].astype(BF16)
        vtw[bb] = win[KV_HALF:KV_W].astype(BF16)
        kwn = new_ref[bb, :, ROW_KW:ROW_KW + KV_W]
        kwn_t = jnp.concatenate([kwn, jnp.zeros((PAGE - n_new, KV_W), F32)], axis=0).T
        shifted = pltpu.roll(win, win_buf - dec_seq, 1)
        tail = jnp.where(lane >= PAGE - dec_seq, pltpu.roll(kwn_t, PAGE - dec_seq, 1),
                         shifted[:, win_buf - PAGE:])
        wout_ref[n_prev, bb] = jnp.concatenate([shifted[:, 0:win_buf - PAGE], tail], axis=1).reshape(
            2, N_KV, HEAD_DIM, win_buf)
        for lp in range(n_prev):
            wout_ref[lp, bb] = prev_ref[lp, bb]

    qs = [q_ref[bb].astype(BF16) for bb in bbs]
    new = [new_ref[bb].astype(BF16) for bb in bbs]
    kn_s = [x[:, ROW_KS:ROW_KS + KV_HALF] for x in new]
    vn_s = [x[:, ROW_KS + KV_HALF:ROW_KS + KV_W] for x in new]
    kn_w = [x[:, ROW_KW:ROW_KW + KV_HALF] for x in new]
    vn_w = [x[:, ROW_KW + KV_HALF:ROW_KW + KV_W] for x in new]

    s_c = [_dot_nt(qs[bb], cmp_ref[bb, :, 0:KV_HALF]) for bb in bbs]
    s_w = [_dot(qs[bb], ktw[bb]) for bb in bbs]
    s_wn = [_dot_nt(qs[bb], kn_w[bb]) for bb in bbs]
    p_c = [_masked_softmax(s, m_c).astype(BF16) for s in s_c]
    p_w = [_masked_softmax2(s_w[bb], m_w, s_wn[bb], m_wn) for bb in bbs]
    o_c = [_pick_group(_dot(p_c[bb], cmp_ref[bb, :, KV_HALF:KV_W]), rg) for bb in bbs]
    o_w = [_pick_group(_dot_nt(p_w[bb][0], vtw[bb]) + _dot(p_w[bb][1], vn_w[bb]), rg) for bb in bbs]

    po = [_dot(p, ov_ref[...]) for p in p_c]
    imp = [functools.reduce(lambda a, h: a + x[h * gt:(h + 1) * gt], range(1, HPG), x[0:gt]) for x in po]
    sel8 = [_topk_lanes(_importance(x, jj, t8, n_sel), jj, n_sel) for x in imp]
    msel = [_dot(jnp.concatenate([x] * HPG, axis=0).astype(BF16), exp_ref[...]) for x in sel8]

    s_s = [_dot(qs[bb], kts[bb]) for bb in bbs]
    s_sn = [_dot_nt(qs[bb], kn_s[bb]) for bb in bbs]
    p_s = [_masked_softmax2(s_s[bb], msel[bb][:, 0:past] > 0.5,
                            s_sn[bb], (msel[bb][:, past:past + n_new] > 0.5) & m_new) for bb in bbs]
    o_s = [_pick_group(_dot_nt(p_s[bb][0], vts[bb]) + _dot(p_s[bb][1], vn_s[bb]), rg) for bb in bbs]

    for bb in bbs:
        gs = _sigmoid(gate_ref[bb])
        o_ref[bb] = o_c[bb] * gs[:, 0:1] + o_s[bb] * gs[:, 1:2] + o_w[bb] * gs[:, 2:3]


def nsa_sample(cache_sel, page_idx, qbd, gates, cmp_s, kv_new, cache_win, layer, prev_win, ov_s, expand, bt):
    nb, npages = page_idx.shape
    n_prev = 0 if prev_win is None else prev_win.shape[0]
    n_rows = qbd.shape[1]
    dec_seq = n_rows // (HPG * N_KV)
    win_buf = cache_win.shape[-1]
    n_cmp_pad = cmp_s.shape[1]
    n_new = kv_new.shape[1]
    past = npages * PAGE
    page_specs = [
        pl.BlockSpec((None, 2, N_KV, HEAD_DIM, PAGE), functools.partial(
            lambda i, pt, bb, p: (pt[i * bt + bb, p], 0, 0, 0, 0), bb=bb, p=p))
        for bb in range(bt) for p in range(npages)]
    per_b = lambda i, pt: (i, 0, 0)
    const2 = lambda i, pt: (0, 0)
    win_shape = (2, N_KV, HEAD_DIM, win_buf)
    grid_spec = pltpu.PrefetchScalarGridSpec(
        num_scalar_prefetch=1,
        grid=(nb // bt,),
        in_specs=page_specs + [
            pl.BlockSpec((bt, n_rows, KV_HALF), per_b),
            pl.BlockSpec((bt, n_rows, N_BRANCH), per_b),
            pl.BlockSpec((bt, n_cmp_pad, KV_W), per_b),
            pl.BlockSpec((bt, n_new, D_KVT), per_b),
            pl.BlockSpec((None, bt) + win_shape, lambda i, pt: (layer, i, 0, 0, 0, 0)),
        ] + ([pl.BlockSpec((n_prev, bt) + win_shape, lambda i, pt: (0, i, 0, 0, 0, 0))] if n_prev else []) + [
            pl.BlockSpec((n_cmp_pad, 128), const2),
            pl.BlockSpec((128, past + PAGE), const2),
        ],
        out_specs=[pl.BlockSpec((bt, n_rows, HEAD_DIM), per_b),
                   pl.BlockSpec((n_prev + 1, bt) + win_shape, lambda i, pt: (0, i, 0, 0, 0, 0))],
        scratch_shapes=[pltpu.VMEM((bt, KV_HALF, past), BF16)] * 2
        + [pltpu.VMEM((bt, KV_HALF, win_buf), BF16)] * 2)
    return pl.pallas_call(
        functools.partial(_nsa_sample_kernel, bt=bt, npages=npages, dec_seq=dec_seq),
        grid_spec=grid_spec,
        out_shape=[jax.ShapeDtypeStruct((nb, n_rows, HEAD_DIM), F32),
                   jax.ShapeDtypeStruct((n_prev + 1, nb) + win_shape, F32)],
        compiler_params=_cparams(("arbitrary",)),
        name="nsa_sample",
    )(page_idx, *([cache_sel] * (bt * npages)), qbd, gates, cmp_s, kv_new, cache_win,
      *([prev_win] if n_prev else []), ov_s, expand)


def _shift_rows(u, p1, p2):
    r = lax.broadcasted_iota(jnp.int32, (u.shape[0], 1), 0)
    u1 = jnp.where(r >= 1, pltpu.roll(u, 1, 0), p1)
    u2 = jnp.where(r >= 2, pltpu.roll(u, 2, 0), jnp.where(r == 1, p1, p2))
    return u1, u2


def _mix_project(o_nsa, o_conv, gn, gc, w, x):
    mix = jnp.concatenate([_rms(o_nsa, gn), _rms(o_conv, gc)], axis=-1).astype(BF16)
    return x + _dot(mix, w)


def _mixout_prompt_kernel(on_ref, cb_ref, cc_ref, ch_ref, cch_ref, chh_ref, cw_ref, gn_ref, gc_ref,
                          w_ref, x_ref, gm_ref, wq_ref, kv_ref, wo_ref, o_ref, ut_ref, *, tiles_per_seq):
    first = (pl.program_id(0) % tiles_per_seq) == 0
    u = cc_ref[...] * ch_ref[...]
    uh = jnp.where(first, 0.0, cch_ref[...] * chh_ref[...])
    u1, u2 = _shift_rows(u, uh[7:8], uh[6:7])
    cw = cw_ref[...]
    v = cw[0:1] * u2 + cw[1:2] * u1 + cw[2:3] * u
    x1 = _mix_project(on_ref[...], cb_ref[...] * v, gn_ref[...], gc_ref[...], w_ref[...], x_ref[...])
    ut_ref[...] = u[u.shape[0] - 8:, :]
    o_ref[...] = _mem_attend(x1, gm_ref[...], wq_ref[...], kv_ref[...].astype(BF16), wo_ref[...])


def mixout_prompt(o_nsa, p_main, cw, gn, gc, w, x, g_mem, wq, mem_kv, wo, tm, seq, layer):
    m = x.shape[0]
    nt = m // tm
    tps = seq // tm
    mem_len = mem_kv.shape[1]
    cblk = lambda c: (lambda i: (i, c // D_CONV))
    halo = lambda c: (lambda i: (jnp.maximum(i * (tm // 8) - 1, 0), c // D_CONV))
    const = lambda i: (0, 0)
    return pl.pallas_call(
        functools.partial(_mixout_prompt_kernel, tiles_per_seq=tps),
        grid=(nt,),
        in_specs=[pl.BlockSpec((tm, D_NSA), lambda i: (i, 0)),
                  pl.BlockSpec((tm, D_CONV), cblk(COL_CB)),
                  pl.BlockSpec((tm, D_CONV), cblk(COL_CC)),
                  pl.BlockSpec((tm, D_CONV), cblk(COL_CH)),
                  pl.BlockSpec((8, D_CONV), halo(COL_CC)),
                  pl.BlockSpec((8, D_CONV), halo(COL_CH)),
                  pl.BlockSpec((8, D_CONV), const),
                  pl.BlockSpec((1, D_NSA), const),
                  pl.BlockSpec((1, D_CONV), const),
                  _wspec((D_MODEL, D_MODEL), const, layer, resident=True),
                  pl.BlockSpec((tm, D_MODEL), lambda i: (i, 0)),
                  pl.BlockSpec((1, D_MODEL), const),
                  _wspec((D_MODEL, D_MEM), const, layer, resident=True),
                  pl.BlockSpec((None, mem_len, 2 * D_MEM), lambda i: (i // tps, 0, 0)),
                  _wspec((D_MEM, D_MODEL), const, layer, resident=True)],
        out_specs=[pl.BlockSpec((tm, D_MODEL), lambda i: (i, 0)),
                   pl.BlockSpec((None, 8, D_CONV), lambda i: (i, 0, 0))],
        out_shape=[jax.ShapeDtypeStruct((m, D_MODEL), F32),
                   jax.ShapeDtypeStruct((nt, 8, D_CONV), F32)],
        compiler_params=_cparams(("arbitrary",)),
        name="mixout_prompt",
    )(o_nsa, p_main, p_main, p_main, p_main, p_main, cw, gn, gc, w, x, g_mem, wq, mem_kv, wo)


def _mixout_sample_kernel(on_ref, cb_ref, cc_ref, ch_ref, st_ref, cw_ref, gn_ref, gc_ref,
                          w_ref, x_ref, o_ref, nst_ref, *, dec_seq):
    nb = x_ref.shape[1]
    ucat = [st_ref[0], st_ref[1]] + [cc_ref[t] * ch_ref[t] for t in range(dec_seq)]
    cw = cw_ref[...]
    mix = []
    for t in range(dec_seq):
        v = cw[0:1] * ucat[t] + cw[1:2] * ucat[t + 1] + cw[2:3] * ucat[t + 2]
        mix.append(jnp.concatenate([_rms(on_ref[t], gn_ref[...]), _rms(cb_ref[t] * v, gc_ref[...])],
                                   axis=-1).astype(BF16))
    proj = _dot(jnp.concatenate(mix, axis=0), w_ref[...])
    for t in range(dec_seq):
        o_ref[t] = x_ref[t] + proj[t * nb:(t + 1) * nb]
    nst_ref[0] = ucat[dec_seq]
    nst_ref[1] = ucat[dec_seq + 1]


def mixout_sample(o_nsa, p_s, state, cw, gn, gc, w, x, layer):
    dec_seq, nb, _ = x.shape
    cblk = lambda c: (lambda i: (0, 0, c // D_CONV))
    full3 = lambda i: (0, 0, 0)
    const = lambda i: (0, 0)
    return pl.pallas_call(
        functools.partial(_mixout_sample_kernel, dec_seq=dec_seq),
        grid=(1,),
        in_specs=[pl.BlockSpec((dec_seq, nb, D_NSA), full3),
                  pl.BlockSpec((dec_seq, nb, D_CONV), cblk(COL_CB)),
                  pl.BlockSpec((dec_seq, nb, D_CONV), cblk(COL_CC)),
                  pl.BlockSpec((dec_seq, nb, D_CONV), cblk(COL_CH)),
                  pl.BlockSpec((2, nb, D_CONV), full3),
                  pl.BlockSpec((8, D_CONV), const),
                  pl.BlockSpec((1, D_NSA), const),
                  pl.BlockSpec((1, D_CONV), const),
                  _wspec((D_MODEL, D_MODEL), const, layer),
                  pl.BlockSpec((dec_seq, nb, D_MODEL), full3)],
        out_specs=[pl.BlockSpec((dec_seq, nb, D_MODEL), full3),
                   pl.BlockSpec((2, nb, D_CONV), full3)],
        out_shape=[jax.ShapeDtypeStruct((dec_seq, nb, D_MODEL), F32),
                   jax.ShapeDtypeStruct((2, nb, D_CONV), F32)],
        compiler_params=_cparams(("arbitrary",)),
        name="mixout_sample",
    )(o_nsa, p_s, p_s, p_s, state, cw, gn, gc, w, x)


MEM_SCALE = MEM_HEAD_DIM ** -0.5


def _softmax(s):
    e = jnp.exp(s - jnp.max(s, axis=-1, keepdims=True))
    return e / jnp.sum(e, axis=-1, keepdims=True)


def _mem_attend(x, g, wq, kv, wo):
    qm = _dot(_rms(x, g).astype(BF16), wq).astype(BF16)
    outs = []
    for h in range(MEM_HEADS):
        c0, c1 = h * MEM_HEAD_DIM, (h + 1) * MEM_HEAD_DIM
        p = _softmax(_dot_nt(qm[:, c0:c1], kv[:, c0:c1]) * MEM_SCALE).astype(BF16)
        outs.append(_dot(p, kv[:, D_MEM + c0:D_MEM + c1]))
    return x + _dot(jnp.concatenate(outs, axis=-1).astype(BF16), wo)


def _mem_sample_kernel(q_ref, kv_ref, o_ref, *, bt, mem_len):
    n_rows = q_ref.shape[1]
    stride = 2 * MEM_HEADS
    rowh = lax.broadcasted_iota(jnp.int32, (n_rows, 1), 0) // (n_rows // MEM_HEADS)
    for bb in range(bt):
        kmat = jnp.concatenate([kv_ref[bb, pl.ds(h, mem_len, stride=stride), :]
                                for h in range(MEM_HEADS)], axis=-1).astype(BF16)
        vmat = jnp.concatenate([kv_ref[bb, pl.ds(MEM_HEADS + h, mem_len, stride=stride), :]
                                for h in range(MEM_HEADS)], axis=-1).astype(BF16)
        p = _softmax(_dot_nt(q_ref[bb].astype(BF16), kmat) * MEM_SCALE).astype(BF16)
        o2 = _dot(p, vmat)
        out = jnp.zeros((n_rows, MEM_HEAD_DIM), F32)
        for h in range(MEM_HEADS):
            out = out + jnp.where(rowh == h, o2[:, h * MEM_HEAD_DIM:(h + 1) * MEM_HEAD_DIM], 0.0)
        o_ref[bb] = out


def mem_sample_attn(qbd, cache_mem, layer, bt):
    nb, n_rows, _ = qbd.shape
    rows = cache_mem.shape[2]
    return pl.pallas_call(
        functools.partial(_mem_sample_kernel, bt=bt, mem_len=rows // (2 * MEM_HEADS)),
        grid=(nb // bt,),
        in_specs=[pl.BlockSpec((bt, n_rows, D_MEM), lambda i: (i, 0, 0)),
                  pl.BlockSpec((None, bt, rows, MEM_HEAD_DIM), lambda i: (layer, i, 0, 0))],
        out_specs=pl.BlockSpec((bt, n_rows, MEM_HEAD_DIM), lambda i: (i, 0, 0)),
        out_shape=jax.ShapeDtypeStruct((nb, n_rows, MEM_HEAD_DIM), F32),
        compiler_params=_cparams(("arbitrary",)),
        name="mem_sample_attn",
    )(qbd, cache_mem)


def _silu(a):
    return a * _sigmoid(a)


def _ffn_prompt_kernel(x_ref, xh_ref, g_ref, wg_ref, wu_ref, cw_ref, wd_ref, gf_ref,
                       o_ref, gt_ref, h_s, hh_s, *, tiles_per_seq, final):
    j = pl.program_id(1)
    tm = x_ref.shape[0]

    @pl.when(j == 0)
    def _():
        first = (pl.program_id(0) % tiles_per_seq) == 0
        x = x_ref[...]
        h_s[...] = _rms(x, g_ref[...]).astype(BF16)
        hh = jnp.where(first, 0.0, _rms(xh_ref[...], g_ref[...]))
        hh_s[...] = jnp.concatenate([hh, jnp.zeros_like(hh)], axis=0).astype(BF16)
        o_ref[...] = x

    h = h_s[...]
    gate = _dot(h, wg_ref[...])
    gate_h = _dot(hh_s[...], wg_ref[...])
    g1, g2 = _shift_rows(gate, gate_h[7:8], gate_h[6:7])
    cw = cw_ref[...]
    a = cw[0:1] * g2 + cw[1:2] * g1 + cw[2:3] * gate
    z = (_silu(a) * _dot(h, wu_ref[...])).astype(BF16)
    o_ref[...] += _dot(z, wd_ref[...])
    gt_ref[...] = gate[tm - 8:, :]

    if final:
        @pl.when(j == pl.num_programs(1) - 1)
        def _():
            o_ref[...] = _rms(o_ref[...], gf_ref[...])


def ffn_prompt(x, g, wg, wu, cw, wd, gf, tm, tn, seq, final, layer):
    m = x.shape[0]
    nt = m // tm
    nj = D_FF // tn
    const = lambda i, j: (0, 0)
    return pl.pallas_call(
        functools.partial(_ffn_prompt_kernel, tiles_per_seq=seq // tm, final=final),
        grid=(nt, nj),
        in_specs=[pl.BlockSpec((tm, D_MODEL), lambda i, j: (i, 0)),
                  pl.BlockSpec((8, D_MODEL), lambda i, j: (jnp.maximum(i * (tm // 8) - 1, 0), 0)),
                  pl.BlockSpec((1, D_MODEL), const),
                  _wspec((D_MODEL, tn), lambda i, j: (0, j), layer),
                  _wspec((D_MODEL, tn), lambda i, j: (0, j), layer),
                  pl.BlockSpec((8, tn), lambda i, j: (0, j)),
                  _wspec((tn, D_MODEL), lambda i, j: (j, 0), layer),
                  pl.BlockSpec((1, D_MODEL), const)],
        out_specs=[pl.BlockSpec((tm, D_MODEL), lambda i, j: (i, 0)),
                   pl.BlockSpec((None, 8, tn), lambda i, j: (i, 0, j))],
        out_shape=[jax.ShapeDtypeStruct((m, D_MODEL), F32),
                   jax.ShapeDtypeStruct((nt, 8, D_FF), F32)],
        scratch_shapes=[pltpu.VMEM((tm, D_MODEL), BF16),
                        pltpu.VMEM((16, D_MODEL), BF16)],
        compiler_params=_cparams(("arbitrary", "arbitrary")),
        name="ffn_prompt",
    )(x, x, g, wg, wu, cw, wd, gf)


def _ffn_sample_kernel(x_ref, st_ref, g_ref, wg_ref, wu_ref, cw_ref, wd_ref, gf_ref,
                       o_ref, nst_ref, h_s, acc_s, *, dec_seq, final):
    j = pl.program_id(0)

    nb = x_ref.shape[1]

    @pl.when(j == 0)
    def _():
        for t in range(dec_seq):
            h_s[t * nb:(t + 1) * nb, :] = _rms(x_ref[t], g_ref[...]).astype(BF16)
        acc_s[...] = jnp.zeros_like(acc_s)

    h = h_s[...]
    gate = _dot(h, wg_ref[...])
    up = _dot(h, wu_ref[...])
    gcat = [st_ref[0], st_ref[1]] + [gate[t * nb:(t + 1) * nb] for t in range(dec_seq)]
    cw = cw_ref[...]
    a = jnp.concatenate([cw[0:1] * gcat[t] + cw[1:2] * gcat[t + 1] + cw[2:3] * gcat[t + 2]
                         for t in range(dec_seq)], axis=0)
    acc_s[...] += _dot((_silu(a) * up).astype(BF16), wd_ref[...])
    nst_ref[0] = gcat[dec_seq]
    nst_ref[1] = gcat[dec_seq + 1]

    @pl.when(j == pl.num_programs(0) - 1)
    def _():
        for t in range(dec_seq):
            y = x_ref[t] + acc_s[t * nb:(t + 1) * nb, :]
            o_ref[t] = _rms(y, gf_ref[...]) if final else y


def ffn_sample(x, state, g, wg, wu, cw, wd, gf, tn, final, layer):
    dec_seq, nb, _ = x.shape
    full3 = lambda j: (0, 0, 0)
    const = lambda j: (0, 0)
    return pl.pallas_call(
        functools.partial(_ffn_sample_kernel, dec_seq=dec_seq, final=final),
        grid=(D_FF // tn,),
        in_specs=[pl.BlockSpec((dec_seq, nb, D_MODEL), full3),
                  pl.BlockSpec((2, nb, tn), lambda j: (0, 0, j)),
                  pl.BlockSpec((1, D_MODEL), const),
                  _wspec((D_MODEL, tn), lambda j: (0, j), layer),
                  _wspec((D_MODEL, tn), lambda j: (0, j), layer),
                  pl.BlockSpec((8, tn), lambda j: (0, j)),
                  _wspec((tn, D_MODEL), lambda j: (j, 0), layer),
                  pl.BlockSpec((1, D_MODEL), const)],
        out_specs=[pl.BlockSpec((dec_seq, nb, D_MODEL), full3),
                   pl.BlockSpec((2, nb, tn), lambda j: (0, 0, j))],
        out_shape=[jax.ShapeDtypeStruct((dec_seq, nb, D_MODEL), F32),
                   jax.ShapeDtypeStruct((2, nb, D_FF), F32)],
        scratch_shapes=[pltpu.VMEM((dec_seq * nb, D_MODEL), BF16),
                        pltpu.VMEM((dec_seq * nb, D_MODEL), F32)],
        compiler_params=_cparams(("arbitrary",)),
        name="ffn_sample",
    )(x, state, g, wg, wu, cw, wd, gf)


def _overlap(n_cmp, n_sel, rows, cols):
    c0 = np.arange(n_cmp)[:, None] * CMP_STRIDE
    s0 = np.arange(n_sel)[None, :] * SEL_BLOCK
    ov = np.minimum(c0 + CMP_BLOCK, s0 + SEL_BLOCK) - np.maximum(c0, s0)
    out = np.zeros((rows, cols), np.float32)
    out[:n_cmp, :n_sel] = np.clip(ov, 0, None).astype(np.float32) / CMP_BLOCK
    return out


def _block_expand(n_keys):
    return jnp.asarray(np.arange(128)[:, None] == (np.arange(n_keys)[None, :] // SEL_BLOCK), BF16)


def _chunk_perm():
    row = np.arange(PAGE)
    return jnp.asarray((row[:, None] % RPC) * CMP_STRIDE + row[:, None] // RPC == row[None, :], BF16)


def _pad_rows(a, rows):
    return jnp.concatenate([a, jnp.zeros((rows - a.shape[0],) + a.shape[1:], a.dtype)], axis=0)


def _prep_w_in(w_in_l):
    offs = np.cumsum((D_NSA, KV_W, KV_W, KV_W, N_BRANCH * N_HEADS, D_CONV, D_CONV, D_CONV))
    q, kc, ks, kw, gl, cb, cc, ch = jnp.split(w_in_l.T, [int(o) for o in offs[:-1]], axis=0)
    pad = jnp.zeros((D_KVT - ROW_G - N_BRANCH * N_HEADS, D_MODEL), F32)
    w_main = jnp.concatenate([q * (HEAD_DIM ** -0.5), cb, cc, ch], axis=0).astype(BF16)
    wt_kv = jnp.concatenate([kc, ks, kw, gl, pad], axis=0).astype(BF16)
    return w_main[None], wt_kv[None]


def _prep_cmp(w_cmp_l, pe_cmp_l):
    blocks = [jnp.pad(w_cmp_l[c], ((0, 0), (0, 0), ((c * N_KV + g) * HEAD_DIM,
                                                     KV_W - (c * N_KV + g + 1) * HEAD_DIM)))
              for c in range(2) for g in range(N_KV)]
    wbd = jnp.concatenate(blocks, axis=1).astype(BF16)
    pe = pe_cmp_l.reshape(2, 2, CMP_STRIDE, HEAD_DIM).transpose(1, 0, 3, 2)
    pet = jnp.broadcast_to(pe[:, :, None, :, None, :], (2, 2, N_KV, HEAD_DIM, RPC, CMP_STRIDE))
    return wbd, pet.reshape(2, KV_W, PAGE)


TM_PROJ = 1024
TN_PROJ = 1024
TM_MIX = 512
TM_FFN = 1024
TN_FFN = 512
BT_CMP = 4
BT_NSA = 4
BT_MEM = 8
NEW_PAD = 16


def _kv_rows(kvt, row, start=0):
    slab = kvt[:, row:row + KV_W, start:]
    b, _, t = slab.shape
    return slab.reshape(b, 2, N_KV, HEAD_DIM, t).transpose(0, 4, 1, 2, 3)


def _layer_prompt(x, mem_kv, lw, consts, layer, batch, seq, final):
    p_main, kvt = proj_in(x, lw['g_mix'], lw['w_main'], lw['wt_kv'], TM_PROJ, TN_PROJ, seq, 0)
    cmp = compress_prompt(kvt, lw['pet'], consts['perm'], lw['wbd'])
    n_cmp_pad = seq // CMP_STRIDE
    cmp_t = cmp.reshape(batch, n_cmp_pad, 2, N_KV, HEAD_DIM).transpose(0, 2, 3, 1, 4)
    o_nsa = nsa_prompt(p_main, kvt, cmp_t, consts['ovt_p'], consts['expand_p'])
    x2, u_tail = mixout_prompt(o_nsa, p_main, lw['conv_w'], lw['g_out_nsa'], lw['g_out_conv'],
                               lw['w_out'], x, lw['g_mem'], lw['w_mem_q'], mem_kv, lw['w_mem_o'],
                               TM_MIX, seq, layer)
    x3, g_tail = ffn_prompt(x2, lw['g_ffn'], lw['w_ff_gate'], lw['w_ff_up'], lw['ffn_conv_w'],
                            lw['w_ff_down'], lw['g_final'], TM_FFN, TN_FFN, seq, final, layer)
    conv_state = u_tail.reshape(batch, seq // TM_MIX, 8, D_CONV)[:, -1, 6:8]
    ffn_state = g_tail.reshape(batch, seq // TM_FFN, 8, D_FF)[:, -1, 6:8]
    return (x3, _kv_rows(kvt, ROW_KC), _kv_rows(kvt, ROW_KS),
            _kv_rows(kvt, ROW_KW, seq - min(WINDOW, seq)), conv_state, ffn_state)


def _layer_sample(x, lw, consts, layer, cache_cmp, cache_sel, page_idx, cache_win, prev_win, cache_mem,
                  st_conv, st_ffn, final):
    dec_seq, nb, _ = x.shape
    rows = dec_seq * nb
    p_s, kvt = proj_in(x.reshape(rows, D_MODEL), lw['g_mix'], lw['w_main'], lw['wt_kv'],
                       rows, TN_PROJ, rows, 0)
    p3 = p_s.reshape(dec_seq, nb, D_MAIN)
    kvn = kvt.reshape(D_KVT, rows).T.reshape(dec_seq, nb, D_KVT).transpose(1, 0, 2)
    kvn_pad = jnp.concatenate([kvn, jnp.zeros((nb, NEW_PAD - dec_seq, D_KVT), F32)], axis=1)
    cmp_s = compress_paged(cache_cmp, page_idx, lw['pet'], consts['perm'], lw['wbd'], BT_CMP)
    q = p3[:, :, COL_Q:COL_Q + D_NSA].reshape(dec_seq, nb, N_KV, HPG, HEAD_DIM).transpose(1, 3, 2, 0, 4)
    gsel = jnp.eye(N_KV, dtype=F32)
    qbd = (q[:, :, :, :, None, :] * gsel[None, None, :, None, :, None]).reshape(
        nb, HPG * N_KV * dec_seq, KV_HALF)
    gates = kvn[:, :, ROW_G:ROW_G + N_BRANCH * N_HEADS].reshape(
        nb, dec_seq, N_KV, HPG, N_BRANCH).transpose(0, 3, 2, 1, 4).reshape(nb, HPG * N_KV * dec_seq, N_BRANCH)
    o, win_stack = nsa_sample(cache_sel, page_idx, qbd, gates, cmp_s, kvn_pad, cache_win, layer, prev_win,
                              consts['ov_s'], consts['expand_s'], BT_NSA)
    o_nsa = o.reshape(nb, HPG, N_KV, dec_seq, HEAD_DIM).transpose(3, 0, 2, 1, 4).reshape(dec_seq, nb, D_NSA)
    x1, conv_state = mixout_sample(o_nsa, p3, st_conv, lw['conv_w'], lw['g_out_nsa'],
                                   lw['g_out_conv'], lw['w_out'], x, layer)
    qm = rms_matmul(x1.reshape(rows, D_MODEL), lw['g_mem'], lw['w_mem_q'], rows, D_MEM, layer)
    qh = qm.reshape(dec_seq, nb, MEM_HEADS, MEM_HEAD_DIM).transpose(1, 2, 0, 3)
    qmbd = (qh[:, :, :, None, :] * jnp.eye(MEM_HEADS, dtype=F32)[None, :, None, :, None]).reshape(
        nb, MEM_HEADS * dec_seq, D_MEM)
    om = mem_sample_attn(qmbd, cache_mem, layer, BT_MEM)
    om = om.reshape(nb, MEM_HEADS, dec_seq, MEM_HEAD_DIM).transpose(2, 0, 1, 3).reshape(rows, D_MEM)
    x2 = matmul_residual(x1.reshape(rows, D_MODEL), om, lw['w_mem_o'], rows, layer).reshape(
        dec_seq, nb, D_MODEL)
    x3, ffn_state = ffn_sample(x2, st_ffn, lw['g_ffn'], lw['w_ff_gate'], lw['w_ff_up'],
                               lw['ffn_conv_w'], lw['w_ff_down'], lw['g_final'], TN_FFN, final, layer)
    kv_c = kvn[:, :, ROW_KC:ROW_KC + KV_W].reshape(nb, dec_seq, 2, N_KV, HEAD_DIM)
    kv_s = kvn[:, :, ROW_KS:ROW_KS + KV_W].reshape(nb, dec_seq, 2, N_KV, HEAD_DIM)
    return (x3, kv_c, kv_s, win_stack, conv_state.transpose(1, 0, 2), ffn_state.transpose(1, 0, 2))


def kernel(x_prompt, x_sample, cache_cmp_kv, cache_sel_kv, cache_win_kv, cache_mem_kv,
           state_conv, state_ffn_conv, page_table, mem_prompt,
           g_mix, w_in, w_cmp, pe_cmp, conv_w, g_out_nsa, g_out_conv, w_out,
           g_mem_src, w_mem_kv, g_mem, w_mem_q, w_mem_o, g_ffn, w_ff_gate, w_ff_up,
           ffn_conv_w, w_ff_down, g_final):
    batch, seq, _ = x_prompt.shape
    nb, dec_seq, _ = x_sample.shape
    depth = w_in.shape[0]
    n_phys = cache_cmp_kv.shape[1]
    npages = page_table.shape[1]
    past = npages * PAGE
    win_buf = cache_win_kv.shape[2]
    mem_len = mem_prompt.shape[1]

    n_cmp_s = (past + dec_seq - CMP_BLOCK) // CMP_STRIDE + 1
    n_sel_s = -(-(past + dec_seq) // SEL_BLOCK)
    n_cmp_p = (seq - CMP_BLOCK) // CMP_STRIDE + 1
    n_sel_p = seq // SEL_BLOCK
    consts = {
        'ov_s': jnp.asarray(_overlap(n_cmp_s, n_sel_s, past // CMP_STRIDE, 128), BF16),
        'expand_s': _block_expand(past + PAGE),
        'ovt_p': jnp.asarray(_overlap(n_cmp_p, n_sel_p, seq // CMP_STRIDE, n_sel_p).T, BF16),
        'expand_p': _block_expand(seq),
        'perm': _chunk_perm(),
    }

    cache_cmp = cache_cmp_kv.transpose(0, 1, 3, 4, 5, 2).reshape(depth * n_phys, 2, N_KV, HEAD_DIM, PAGE)
    cache_sel = cache_sel_kv.transpose(0, 1, 3, 4, 5, 2).reshape(depth * n_phys, 2, N_KV, HEAD_DIM, PAGE)
    cache_win = cache_win_kv.transpose(0, 1, 3, 4, 5, 2)
    cache_mem = cache_mem_kv.reshape(depth, nb, mem_len * 2 * MEM_HEADS, MEM_HEAD_DIM)

    xp = x_prompt.reshape(batch * seq, D_MODEL)
    xs = x_sample.transpose(1, 0, 2)
    mem_rows = mem_prompt.reshape(batch * mem_len, D_MODEL)
    outs = [[] for _ in range(10)]
    win_stack = None
    stacked = {
        'w_out': w_out.astype(BF16), 'w_mem_q': w_mem_q.astype(BF16), 'w_mem_o': w_mem_o.astype(BF16),
        'w_ff_gate': w_ff_gate.astype(BF16), 'w_ff_up': w_ff_up.astype(BF16),
        'w_ff_down': w_ff_down.astype(BF16),
    }
    w_mem_kv_b = w_mem_kv.astype(BF16)
    for l in range(depth):
        wbd, pet = _prep_cmp(w_cmp[l], pe_cmp[l])
        w_main, wt_kv = _prep_w_in(w_in[l])
        lw = dict(stacked)
        lw.update({
            'g_mix': g_mix[l][None], 'w_main': w_main, 'wt_kv': wt_kv, 'wbd': wbd, 'pet': pet,
            'conv_w': _pad_rows(conv_w[l], 8), 'g_out_nsa': g_out_nsa[l][None],
            'g_out_conv': g_out_conv[l][None], 'g_mem': g_mem[l][None], 'g_ffn': g_ffn[l][None],
            'ffn_conv_w': _pad_rows(ffn_conv_w[l], 8), 'g_final': g_final[None],
        })
        final = l == depth - 1
        mem_kv_p = rms_matmul(mem_rows, g_mem_src[l][None], w_mem_kv_b,
                              batch * mem_len, 2 * D_MEM, l).reshape(batch, mem_len, 2 * D_MEM)
        xp, kc, ks, kw, cst, fst = _layer_prompt(xp, mem_kv_p, lw, consts, l, batch, seq, final)
        page_idx = page_table + l * n_phys
        xs, kc2, ks2, win_stack, cst2, fst2 = _layer_sample(
            xs, lw, consts, l, cache_cmp, cache_sel, page_idx, cache_win, win_stack, cache_mem,
            state_conv[l].transpose(1, 0, 2), state_ffn_conv[l].transpose(1, 0, 2), final)
        for lst, val in zip(outs, (kc, ks, kw, mem_kv_p, cst, fst, kc2, ks2, cst2, fst2)):
            lst.append(val)

    st = [jnp.stack(o) for o in outs]
    return (xp.reshape(batch, seq, D_MODEL), xs.transpose(1, 0, 2),
            st[0], st[1], st[2],
            st[3].reshape(depth, batch, mem_len, 2, MEM_HEADS, MEM_HEAD_DIM),
            st[4], st[5], st[6], st[7],
            win_stack.transpose(0, 1, 5, 2, 3, 4),
            st[8], st[9])
```

```python
import functools

import numpy as np
import jax
import jax.numpy as jnp
from jax import lax
from jax.experimental import pallas as pl
from jax.experimental.pallas import tpu as pltpu

F32 = jnp.float32
BF16 = jnp.bfloat16

D_MODEL = 2048
D_NSA = 1024
D_CONV = 1024
HEAD_DIM = 64
N_HEADS = 16
N_KV = 2
HPG = 8
N_BRANCH = 3
KV_W = 2 * N_KV * HEAD_DIM
KV_HALF = N_KV * HEAD_DIM
CMP_BLOCK = 32
CMP_STRIDE = 16
SEL_BLOCK = 64
TOPK = 16
WINDOW = 512
Q_BLOCK = 128
FORCE_BONUS = 1e4
D_FF = 5632
MEM_HEADS = 4
MEM_HEAD_DIM = 128
D_MEM = 512
PAGE = 128
EPS = 1e-6
NEG = -1e30
SCORE_SCALE = HEAD_DIM ** -0.5

COL_Q = 0
COL_CB = 1024
COL_CC = 2048
COL_CH = 3072
D_MAIN = 4096
ROW_KC = 0
ROW_KS = 256
ROW_KW = 512
ROW_G = 768
D_KVT = 896

VMEM_LIMIT = 56 * 1024 * 1024


def _cparams(sem):
    return pltpu.CompilerParams(dimension_semantics=sem, vmem_limit_bytes=VMEM_LIMIT)


def _rms(x, g):
    return x * lax.rsqrt(jnp.mean(x * x, axis=-1, keepdims=True) + EPS) * g


def _dot(a, b):
    return jnp.dot(a, b, preferred_element_type=F32)


def _dot_nt(a, b):
    return lax.dot_general(a, b, (((1,), (1,)), ((), ())), preferred_element_type=F32)


def _softmax_bias(s, bias):
    s = s + bias
    e = jnp.exp(s - jnp.max(s, axis=-1, keepdims=True))
    return e, jnp.sum(e, axis=-1, keepdims=True)


def _sigmoid(x):
    return 1.0 / (1.0 + jnp.exp(-x))


def _rms_matmul_kernel(x_ref, g_ref, w_ref, o_ref, h_ref):
    @pl.when(pl.program_id(1) == 0)
    def _():
        h_ref[...] = _rms(x_ref[...], g_ref[...]).astype(BF16)

    o_ref[...] = _dot(h_ref[...], w_ref[...]).astype(o_ref.dtype)


def _wspec(block, imap, layer, resident=False):
    kwargs = {'pipeline_mode': pl.Buffered(1)} if resident else {}
    return pl.BlockSpec((None,) + block, lambda *a: (layer,) + imap(*a), **kwargs)


def rms_matmul(x, g, w, tm, tn, layer):
    m, d = x.shape
    n = w.shape[2]
    return pl.pallas_call(
        _rms_matmul_kernel,
        grid=(m // tm, n // tn),
        in_specs=[pl.BlockSpec((tm, d), lambda i, j: (i, 0)),
                  pl.BlockSpec((1, d), lambda i, j: (0, 0)),
                  _wspec((d, tn), lambda i, j: (0, j), layer)],
        out_specs=pl.BlockSpec((tm, tn), lambda i, j: (i, j)),
        out_shape=jax.ShapeDtypeStruct((m, n), F32),
        scratch_shapes=[pltpu.VMEM((tm, d), BF16)],
        compiler_params=_cparams(("arbitrary", "arbitrary")),
        name="rms_matmul",
    )(x, g, w)


def _proj_in_kernel(x_ref, g_ref, w_ref, wt_ref, o_ref, ot_ref, h_ref):
    @pl.when(pl.program_id(1) == 0)
    def _():
        h = _rms(x_ref[...], g_ref[...]).astype(BF16)
        h_ref[...] = h
        ot_ref[...] = _dot_nt(wt_ref[...], h)

    o_ref[...] = _dot_nt(h_ref[...], w_ref[...])


def proj_in(x, g, w_main, wt_kv, tm, tn, seq, layer):
    m, d = x.shape
    tps = seq // tm
    return pl.pallas_call(
        _proj_in_kernel,
        grid=(m // tm, D_MAIN // tn),
        in_specs=[pl.BlockSpec((tm, d), lambda i, j: (i, 0)),
                  pl.BlockSpec((1, d), lambda i, j: (0, 0)),
                  _wspec((tn, d), lambda i, j: (j, 0), layer),
                  _wspec((D_KVT, d), lambda i, j: (0, 0), layer)],
        out_specs=[pl.BlockSpec((tm, tn), lambda i, j: (i, j)),
                   pl.BlockSpec((None, D_KVT, tm), lambda i, j: (i // tps, 0, i % tps))],
        out_shape=[jax.ShapeDtypeStruct((m, D_MAIN), F32),
                   jax.ShapeDtypeStruct((m // seq, D_KVT, seq), F32)],
        scratch_shapes=[pltpu.VMEM((tm, d), BF16)],
        compiler_params=_cparams(("arbitrary", "arbitrary")),
        name="proj_in",
    )(x, g, w_main, wt_kv)


def _matmul_res_kernel(x_ref, a_ref, w_ref, o_ref):
    o_ref[...] = x_ref[...] + _dot(a_ref[...].astype(BF16), w_ref[...])


def matmul_residual(x, a, w, tm, layer):
    m, n = x.shape
    k = a.shape[1]
    return pl.pallas_call(
        _matmul_res_kernel,
        grid=(m // tm,),
        in_specs=[pl.BlockSpec((tm, n), lambda i: (i, 0)),
                  pl.BlockSpec((tm, k), lambda i: (i, 0)),
                  _wspec((k, n), lambda i: (0, 0), layer)],
        out_specs=pl.BlockSpec((tm, n), lambda i: (i, 0)),
        out_shape=jax.ShapeDtypeStruct((m, n), F32),
        compiler_params=_cparams(("arbitrary",)),
        name="matmul_residual",
    )(x, a, w)


RPC = PAGE // CMP_STRIDE


def _compress_chunks(chunk_at, n_chunks, pet_ref, perm_ref, w_ref, xs_ref):
    perm = perm_ref[...]
    for ci in range(n_chunks):
        chunk = chunk_at(ci)
        for a in range(2):
            xs_ref[a, ci] = _dot_nt(perm, (chunk + pet_ref[a]).astype(BF16))
    rows = n_chunks * RPC
    acc0 = jnp.zeros((rows, KV_W), F32)
    acc1 = jnp.zeros((rows, KV_W), F32)
    for r in range(CMP_STRIDE):
        x0 = xs_ref[0, :, r * RPC:(r + 1) * RPC, :].reshape(rows, KV_W).astype(BF16)
        x1 = xs_ref[1, :, r * RPC:(r + 1) * RPC, :].reshape(rows, KV_W).astype(BF16)
        acc0 = acc0 + _dot(x0, w_ref[r])
        acc1 = acc1 + _dot(x1, w_ref[CMP_STRIDE + r])
    return acc0 + pltpu.roll(acc1, rows - 1, 0)


def _compress_prompt_kernel(kvt_ref, pet_ref, perm_ref, w_ref, o_ref, xs_ref, *, n_chunks):
    out = _compress_chunks(lambda ci: kvt_ref[:, ci * PAGE:(ci + 1) * PAGE], n_chunks,
                           pet_ref, perm_ref, w_ref, xs_ref)
    o_ref[...] = out.astype(BF16)


def compress_prompt(kvt, pet, perm, wbd):
    batch, _, seq = kvt.shape
    n_chunks = seq // PAGE
    nblk = seq // CMP_STRIDE
    return pl.pallas_call(
        functools.partial(_compress_prompt_kernel, n_chunks=n_chunks),
        grid=(batch,),
        in_specs=[pl.BlockSpec((None, KV_W, seq), lambda b: (b, ROW_KC // KV_W, 0)),
                  pl.BlockSpec((2, KV_W, PAGE), lambda b: (0, 0, 0)),
                  pl.BlockSpec((PAGE, PAGE), lambda b: (0, 0)),
                  pl.BlockSpec((CMP_BLOCK, KV_W, KV_W), lambda b: (0, 0, 0))],
        out_specs=pl.BlockSpec((None, nblk, KV_W), lambda b: (b, 0, 0)),
        out_shape=jax.ShapeDtypeStruct((batch, nblk, KV_W), BF16),
        scratch_shapes=[pltpu.VMEM((2, n_chunks, PAGE, KV_W), F32)],
        compiler_params=_cparams(("arbitrary",)),
        name="compress_prompt",
    )(kvt, pet, perm, wbd)


def _compress_paged_kernel(pt_ref, *refs, bt, npages):
    page_refs = refs[:bt * npages]
    pet_ref, perm_ref, w_ref, o_ref, xs_ref = refs[bt * npages:]
    out = _compress_chunks(lambda ci: page_refs[ci][...].reshape(KV_W, PAGE), bt * npages,
                           pet_ref, perm_ref, w_ref, xs_ref)
    o_ref[...] = out.astype(BF16).reshape(bt, npages * RPC, KV_W)


def compress_paged(cache, page_idx, pet, perm, wbd, bt):
    nb, npages = page_idx.shape
    nblk = npages * RPC
    page_specs = [
        pl.BlockSpec((None, 2, N_KV, HEAD_DIM, PAGE), functools.partial(
            lambda i, pt, bb, p: (pt[i * bt + bb, p], 0, 0, 0, 0), bb=bb, p=p))
        for bb in range(bt) for p in range(npages)]
    grid_spec = pltpu.PrefetchScalarGridSpec(
        num_scalar_prefetch=1,
        grid=(nb // bt,),
        in_specs=page_specs + [pl.BlockSpec((2, KV_W, PAGE), lambda i, pt: (0, 0, 0)),
                               pl.BlockSpec((PAGE, PAGE), lambda i, pt: (0, 0)),
                               pl.BlockSpec((CMP_BLOCK, KV_W, KV_W), lambda i, pt: (0, 0, 0))],
        out_specs=pl.BlockSpec((bt, nblk, KV_W), lambda i, pt: (i, 0, 0)),
        scratch_shapes=[pltpu.VMEM((2, bt * npages, PAGE, KV_W), F32)])
    return pl.pallas_call(
        functools.partial(_compress_paged_kernel, bt=bt, npages=npages),
        grid_spec=grid_spec,
        out_shape=jax.ShapeDtypeStruct((nb, nblk, KV_W), BF16),
        compiler_params=_cparams(("arbitrary",)),
        name="compress_paged",
    )(page_idx, *([cache] * (bt * npages)), pet, perm, wbd)


def _importance(imp, idx, tpos, n_sel):
    valid = (idx * SEL_BLOCK <= tpos) & (idx < n_sel)
    cur = tpos // SEL_BLOCK
    forced = (idx == 0) | (idx == cur) | (idx == cur - 1)
    return jnp.where(valid, imp + jnp.where(forced, FORCE_BONUS, 0.0), NEG)


def _topk_rows(imp, n_cand):
    groups = [imp[v * 8:(v + 1) * 8] for v in range(n_cand // 8)]
    sub = lax.broadcasted_iota(jnp.int32, groups[0].shape, 0)
    ranks = [jnp.zeros(g.shape, F32) for g in groups]
    for jp in range(n_cand):
        c = imp[jp:jp + 1, :]
        for v, g in enumerate(groups):
            ge = jnp.where(c >= g, 1.0, 0.0)
            gt = jnp.where(c > g, 1.0, 0.0)
            if v * 8 > jp:
                beats = ge
            elif v * 8 + 7 < jp:
                beats = gt
            else:
                beats = jnp.where(sub > jp - v * 8, ge, gt)
            ranks[v] = ranks[v] + beats
    return jnp.concatenate([jnp.where(r < TOPK, 1.0, 0.0) for r in ranks], axis=0)


def _topk_lanes(imp, idx, n_cand):
    rank = jnp.zeros(imp.shape, F32)
    for jp in range(n_cand):
        c = imp[:, jp:jp + 1]
        rank = rank + jnp.where(idx > jp, jnp.where(c >= imp, 1.0, 0.0), jnp.where(c > imp, 1.0, 0.0))
    return jnp.where(rank < TOPK, 1.0, 0.0)


KEY_TILE = 512
V_PAD = 16


def _tile_heads(x):
    return jnp.concatenate([x] * HPG, axis=0)


def _nsa_prompt_kernel(q_ref, gate_ref, cmp_ref, kst_ref, vst_ref, kwt_ref, vwt_ref, ovt_ref, exp_ref,
                       o_ref, ksb, vsb, kwb, vwb, *, seq):
    g = pl.program_id(1)
    i = pl.program_id(2)
    n_cmp_pad = seq // CMP_STRIDE
    n_sel = seq // SEL_BLOCK

    @pl.when(i == 0)
    def _():
        ones_row = jnp.where(lax.broadcasted_iota(jnp.int32, (V_PAD, seq), 0) == 0, 1.0, 0.0).astype(BF16)
        ksb[...] = kst_ref[...].astype(BF16)
        vsb[0:HEAD_DIM, :] = vst_ref[...].astype(BF16)
        vsb[HEAD_DIM:HEAD_DIM + V_PAD, :] = ones_row
        kwb[...] = kwt_ref[...].astype(BF16)
        vwb[0:HEAD_DIM, :] = vwt_ref[...].astype(BF16)
        vwb[HEAD_DIM:HEAD_DIM + V_PAD, :] = ones_row

    t0 = i * Q_BLOCK
    qb = q_ref[...] * SCORE_SCALE
    qs = jnp.concatenate([qb[:, h * HEAD_DIM:(h + 1) * HEAD_DIM] for h in range(HPG)],
                         axis=0).astype(BF16)
    tcol = lax.broadcasted_iota(jnp.int32, (Q_BLOCK, 1), 0) + t0

    ncol = lax.broadcasted_iota(jnp.int32, (Q_BLOCK, n_cmp_pad), 1)
    vis = jnp.where(ncol < n_cmp_pad - 1, ncol * CMP_STRIDE + (CMP_BLOCK - 1), seq) <= tcol
    e_c, l_c = _softmax_bias(_dot_nt(qs, cmp_ref[0]), _tile_heads(jnp.where(vis, 0.0, NEG)))
    any_vis = _tile_heads(jnp.where(tcol >= CMP_BLOCK - 1, 1.0, 0.0))
    p_c = (e_c * (any_vis / jnp.maximum(l_c, 1e-30))).astype(BF16)
    o_c = _dot(p_c, cmp_ref[1])

    po = _dot_nt(ovt_ref[...], p_c)
    imp_t = po[:, 0:Q_BLOCK]
    for h in range(1, HPG):
        imp_t = imp_t + po[:, h * Q_BLOCK:(h + 1) * Q_BLOCK]
    jj = lax.broadcasted_iota(jnp.int32, (n_sel, Q_BLOCK), 0)
    tt = lax.broadcasted_iota(jnp.int32, (n_sel, Q_BLOCK), 1) + t0
    sel_t = _topk_rows(_importance(imp_t, jj, tt, n_sel), n_sel)
    sel = jnp.concatenate([sel_t, jnp.zeros((128 - n_sel, Q_BLOCK), F32)], axis=0).T.astype(BF16)

    kk = lax.broadcasted_iota(jnp.int32, (Q_BLOCK, KEY_TILE), 1)

    def scores(c):
        k0 = pl.multiple_of(c * KEY_TILE, KEY_TILE)
        msel = _dot(sel, exp_ref[:, pl.ds(k0, KEY_TILE)])
        bias = jnp.where(msel > 0.5, jnp.where(kk + k0 <= tcol, 0.0, NEG), NEG)
        return _dot(qs, ksb[:, pl.ds(k0, KEY_TILE)]) + _tile_heads(bias)

    def accumulate(c, s, carry):
        m, acc = carry
        k0 = pl.multiple_of(c * KEY_TILE, KEY_TILE)
        m_new = jnp.maximum(m, jnp.max(s, axis=-1, keepdims=True))
        p = jnp.exp(s - m_new).astype(BF16)
        return m_new, jnp.exp(m - m_new) * acc + _dot_nt(p, vsb[:, pl.ds(k0, KEY_TILE)])

    def tile_pair(c2, carry):
        s_a = scores(2 * c2)
        s_b = scores(2 * c2 + 1)
        return accumulate(2 * c2 + 1, s_b, accumulate(2 * c2, s_a, carry))

    n_rows = HPG * Q_BLOCK
    m0 = jnp.full((n_rows, 1), NEG, F32)
    a0 = jnp.zeros((n_rows, HEAD_DIM + V_PAD), F32)
    n_tiles = (t0 + Q_BLOCK + KEY_TILE - 1) // KEY_TILE
    def tile_quad(c4, carry):
        ss = [scores(4 * c4 + u) for u in range(4)]
        for u in range(4):
            carry = accumulate(4 * c4 + u, ss[u], carry)
        return carry

    carry = lax.fori_loop(0, n_tiles // 4, tile_quad, (m0, a0))
    carry = lax.cond(n_tiles % 4 >= 2, lambda cr: tile_pair((n_tiles // 4) * 2, cr), lambda cr: cr, carry)
    _, acc_s = lax.cond(n_tiles % 2 == 1,
                        lambda cr: accumulate(n_tiles - 1, scores(n_tiles - 1), cr),
                        lambda cr: cr, carry)
    o_s = acc_s[:, 0:HEAD_DIM] / acc_s[:, HEAD_DIM:HEAD_DIM + 1]

    span = WINDOW + Q_BLOCK
    ws = pl.multiple_of(jnp.maximum(t0 - WINDOW, 0), Q_BLOCK)
    dlt = tcol - (lax.broadcasted_iota(jnp.int32, (Q_BLOCK, span), 1) + ws)
    bias_w = jnp.where(dlt >= 0, jnp.where(dlt < WINDOW, 0.0, NEG), NEG)
    s_w = _dot(qs, kwb[:, pl.ds(ws, span)]) + _tile_heads(bias_w)
    e_w = jnp.exp(s_w - jnp.max(s_w, axis=-1, keepdims=True)).astype(BF16)
    acc_w = _dot_nt(e_w, vwb[:, pl.ds(ws, span)])
    o_w = acc_w[:, 0:HEAD_DIM] / acc_w[:, HEAD_DIM:HEAD_DIM + 1]

    gs = _sigmoid(gate_ref[...]).T
    gsel = jnp.where(g == 0, gs[:, 0:HPG * N_BRANCH], gs[:, HPG * N_BRANCH:2 * HPG * N_BRANCH])
    for h in range(HPG):
        r0, r1 = h * Q_BLOCK, (h + 1) * Q_BLOCK
        c0 = h * N_BRANCH
        o_ref[:, h * HEAD_DIM:(h + 1) * HEAD_DIM] = (
            o_c[r0:r1] * gsel[:, c0:c0 + 1] + o_s[r0:r1] * gsel[:, c0 + 1:c0 + 2]
            + o_w[r0:r1] * gsel[:, c0 + 2:c0 + 3])


def nsa_prompt(p_main, kvt, cmp_t, ovt, expand):
    batch, _, seq = kvt.shape
    nqb = seq // Q_BLOCK
    gw = HPG * HEAD_DIM
    n_cmp_pad = seq // CMP_STRIDE
    n_sel = seq // SEL_BLOCK
    kvt_spec = lambda row: pl.BlockSpec((None, HEAD_DIM, seq),
                                        lambda b, g, i: (b, row // HEAD_DIM + g, 0))
    return pl.pallas_call(
        functools.partial(_nsa_prompt_kernel, seq=seq),
        grid=(batch, N_KV, nqb),
        in_specs=[
            pl.BlockSpec((Q_BLOCK, gw), lambda b, g, i: (b * nqb + i, g)),
            pl.BlockSpec((None, 128, Q_BLOCK), lambda b, g, i: (b, ROW_G // 128, i)),
            pl.BlockSpec((None, 2, None, n_cmp_pad, HEAD_DIM), lambda b, g, i: (b, 0, g, 0, 0)),
            kvt_spec(ROW_KS), kvt_spec(ROW_KS + KV_HALF),
            kvt_spec(ROW_KW), kvt_spec(ROW_KW + KV_HALF),
            pl.BlockSpec((n_sel, n_cmp_pad), lambda b, g, i: (0, 0)),
            pl.BlockSpec((128, seq), lambda b, g, i: (0, 0)),
        ],
        out_specs=pl.BlockSpec((Q_BLOCK, gw), lambda b, g, i: (b * nqb + i, g)),
        out_shape=jax.ShapeDtypeStruct((batch * seq, D_NSA), F32),
        scratch_shapes=[pltpu.VMEM((HEAD_DIM, seq), BF16), pltpu.VMEM((HEAD_DIM + V_PAD, seq), BF16)] * 2,
        compiler_params=_cparams(("arbitrary", "arbitrary", "arbitrary")),
        name="nsa_prompt",
    )(p_main, kvt, cmp_t, kvt, kvt, kvt, kvt, ovt, expand)


def _pick_group(o2, rg):
    return jnp.where(rg == 0, o2[:, 0:HEAD_DIM], o2[:, HEAD_DIM:2 * HEAD_DIM])


def _masked_softmax(s, mask):
    s = jnp.where(mask, s, NEG)
    m = jnp.max(s, axis=-1, keepdims=True)
    e = jnp.where(mask, jnp.exp(s - m), 0.0)
    return e / jnp.maximum(jnp.sum(e, axis=-1, keepdims=True), 1e-30)


def _masked_softmax2(s_a, mask_a, s_b, mask_b):
    s_a = jnp.where(mask_a, s_a, NEG)
    s_b = jnp.where(mask_b, s_b, NEG)
    m = jnp.maximum(jnp.max(s_a, axis=-1, keepdims=True), jnp.max(s_b, axis=-1, keepdims=True))
    e_a = jnp.where(mask_a, jnp.exp(s_a - m), 0.0)
    e_b = jnp.where(mask_b, jnp.exp(s_b - m), 0.0)
    tot = jnp.sum(e_a, axis=-1, keepdims=True) + jnp.sum(e_b, axis=-1, keepdims=True)
    inv = 1.0 / jnp.maximum(tot, 1e-30)
    return (e_a * inv).astype(BF16), (e_b * inv).astype(BF16)


def _nsa_sample_kernel(pt_ref, *refs, bt, npages, dec_seq):
    page_refs = refs[:bt * npages]
    rest = list(refs[bt * npages:])
    n_prev = rest[-5].shape[0] - 1
    prev_ref = rest.pop(5) if n_prev else None
    (q_ref, gate_ref, cmp_ref, new_ref, win_ref, ov_ref, exp_ref,
     o_ref, wout_ref, kts, vts, ktw, vtw) = rest
    past = npages * PAGE
    n_cmp = (past + dec_seq - CMP_BLOCK) // CMP_STRIDE + 1
    n_sel = -(-(past + dec_seq) // SEL_BLOCK)
    n_rows = HPG * N_KV * dec_seq
    win_buf = win_ref.shape[-1]
    n_cmp_pad = cmp_ref.shape[1]
    n_new = new_ref.shape[1]
    gt = N_KV * dec_seq

    rowi = lax.broadcasted_iota(jnp.int32, (n_rows, 1), 0)
    rg = (rowi // dec_seq) % N_KV
    tpos = past + rowi % dec_seq
    ncol = lax.broadcasted_iota(jnp.int32, (n_rows, n_cmp_pad), 1)
    m_c = (ncol * CMP_STRIDE + (CMP_BLOCK - 1) <= tpos) & (ncol < n_cmp)
    jj = lax.broadcasted_iota(jnp.int32, (gt, 128), 1)
    t8 = past + lax.broadcasted_iota(jnp.int32, (gt, 128), 0) % dec_seq
    d_new = tpos - (past + lax.broadcasted_iota(jnp.int32, (n_rows, n_new), 1))
    m_new = d_new >= 0
    m_wn = m_new & (d_new < WINDOW)
    wcol = lax.broadcasted_iota(jnp.int32, (n_rows, win_buf), 1)
    dlt = tpos - (wcol + (past - win_buf))
    m_w = (dlt >= 0) & (dlt < WINDOW)
    lane = lax.broadcasted_iota(jnp.int32, (KV_W, PAGE), 1)

    bbs = range(bt)

    for bb in bbs:
        for p in range(npages):
            page = page_refs[bb * npages + p]
            kts[bb, :, p * PAGE:(p + 1) * PAGE] = page[0].reshape(KV_HALF, PAGE).astype(BF16)
            vts[bb, :, p * PAGE:(p + 1) * PAGE] = page[1].reshape(KV_HALF, PAGE).astype(BF16)
        win = win_ref[bb].reshape(KV_W, win_buf)
        ktw[bb] = win[0:KV_HALF].astype(BF16)
        vtw[bb] = win[KV_HALF:KV_W].astype(BF16)
        kwn = new_ref[bb, :, ROW_KW:ROW_KW + KV_W]
        kwn_t = jnp.concatenate([kwn, jnp.zeros((PAGE - n_new, KV_W), F32)], axis=0).T
        shifted = pltpu.roll(win, win_buf - dec_seq, 1)
        tail = jnp.where(lane >= PAGE - dec_seq, pltpu.roll(kwn_t, PAGE - dec_seq, 1),
                         shifted[:, win_buf - PAGE:])
        wout_ref[n_prev, bb] = jnp.concatenate([shifted[:, 0:win_buf - PAGE], tail], axis=1).reshape(
            2, N_KV, HEAD_DIM, win_buf)
        for lp in range(n_prev):
            wout_ref[lp, bb] = prev_ref[lp, bb]

    qs = [q_ref[bb].astype(BF16) for bb in bbs]
    new = [new_ref[bb].astype(BF16) for bb in bbs]
    kn_s = [x[:, ROW_KS:ROW_KS + KV_HALF] for x in new]
    vn_s = [x[:, ROW_KS + KV_HALF:ROW_KS + KV_W] for x in new]
    kn_w = [x[:, ROW_KW:ROW_KW + KV_HALF] for x in new]
    vn_w = [x[:, ROW_KW + KV_HALF:ROW_KW + KV_W] for x in new]

    s_c = [_dot_nt(qs[bb], cmp_ref[bb, :, 0:KV_HALF]) for bb in bbs]
    s_w = [_dot(qs[bb], ktw[bb]) for bb in bbs]
    s_wn = [_dot_nt(qs[bb], kn_w[bb]) for bb in bbs]
    p_c = [_masked_softmax(s, m_c).astype(BF16) for s in s_c]
    p_w = [_masked_softmax2(s_w[bb], m_w, s_wn[bb], m_wn) for bb in bbs]
    o_c = [_pick_group(_dot(p_c[bb], cmp_ref[bb, :, KV_HALF:KV_W]), rg) for bb in bbs]
    o_w = [_pick_group(_dot_nt(p_w[bb][0], vtw[bb]) + _dot(p_w[bb][1], vn_w[bb]), rg) for bb in bbs]

    po = [_dot(p, ov_ref[...]) for p in p_c]
    imp = [functools.reduce(lambda a, h: a + x[h * gt:(h + 1) * gt], range(1, HPG), x[0:gt]) for x in po]
    sel8 = [_topk_lanes(_importance(x, jj, t8, n_sel), jj, n_sel) for x in imp]
    msel = [_dot(jnp.concatenate([x] * HPG, axis=0).astype(BF16), exp_ref[...]) for x in sel8]

    s_s = [_dot(qs[bb], kts[bb]) for bb in bbs]
    s_sn = [_dot_nt(qs[bb], kn_s[bb]) for bb in bbs]
    p_s = [_masked_softmax2(s_s[bb], msel[bb][:, 0:past] > 0.5,
                            s_sn[bb], (msel[bb][:, past:past + n_new] > 0.5) & m_new) for bb in bbs]
    o_s = [_pick_group(_dot_nt(p_s[bb][0], vts[bb]) + _dot(p_s[bb][1], vn_s[bb]), rg) for bb in bbs]

    for bb in bbs:
        gs = _sigmoid(gate_ref[bb])
        o_ref[bb] = o_c[bb] * gs[:, 0:1] + o_s[bb] * gs[:, 1:2] + o_w[bb] * gs[:, 2:3]


def nsa_sample(cache_sel, page_idx, qbd, gates, cmp_s, kv_new, cache_win, layer, prev_win, ov_s, expand, bt):
    nb, npages = page_idx.shape
    n_prev = 0 if prev_win is None else prev_win.shape[0]
    n_rows = qbd.shape[1]
    dec_seq = n_rows // (HPG * N_KV)
    win_buf = cache_win.shape[-1]
    n_cmp_pad = cmp_s.shape[1]
    n_new = kv_new.shape[1]
    past = npages * PAGE
    page_specs = [
        pl.BlockSpec((None, 2, N_KV, HEAD_DIM, PAGE), functools.partial(
            lambda i, pt, bb, p: (pt[i * bt + bb, p], 0, 0, 0, 0), bb=bb, p=p))
        for bb in range(bt) for p in range(npages)]
    per_b = lambda i, pt: (i, 0, 0)
    const2 = lambda i, pt: (0, 0)
    win_shape = (2, N_KV, HEAD_DIM, win_buf)
    grid_spec = pltpu.PrefetchScalarGridSpec(
        num_scalar_prefetch=1,
        grid=(nb // bt,),
        in_specs=page_specs + [
            pl.BlockSpec((bt, n_rows, KV_HALF), per_b),
            pl.BlockSpec((bt, n_rows, N_BRANCH), per_b),
            pl.BlockSpec((bt, n_cmp_pad, KV_W), per_b),
            pl.BlockSpec((bt, n_new, D_KVT), per_b),
            pl.BlockSpec((None, bt) + win_shape, lambda i, pt: (layer, i, 0, 0, 0, 0)),
        ] + ([pl.BlockSpec((n_prev, bt) + win_shape, lambda i, pt: (0, i, 0, 0, 0, 0))] if n_prev else []) + [
            pl.BlockSpec((n_cmp_pad, 128), const2),
            pl.BlockSpec((128, past + PAGE), const2),
        ],
        out_specs=[pl.BlockSpec((bt, n_rows, HEAD_DIM), per_b),
                   pl.BlockSpec((n_prev + 1, bt) + win_shape, lambda i, pt: (0, i, 0, 0, 0, 0))],
        scratch_shapes=[pltpu.VMEM((bt, KV_HALF, past), BF16)] * 2
        + [pltpu.VMEM((bt, KV_HALF, win_buf), BF16)] * 2)
    return pl.pallas_call(
        functools.partial(_nsa_sample_kernel, bt=bt, npages=npages, dec_seq=dec_seq),
        grid_spec=grid_spec,
        out_shape=[jax.ShapeDtypeStruct((nb, n_rows, HEAD_DIM), F32),
                   jax.ShapeDtypeStruct((n_prev + 1, nb) + win_shape, F32)],
        compiler_params=_cparams(("arbitrary",)),
        name="nsa_sample",
    )(page_idx, *([cache_sel] * (bt * npages)), qbd, gates, cmp_s, kv_new, cache_win,
      *([prev_win] if n_prev else []), ov_s, expand)


def _shift_rows(u, p1, p2):
    r = lax.broadcasted_iota(jnp.int32, (u.shape[0], 1), 0)
    u1 = jnp.where(r >= 1, pltpu.roll(u, 1, 0), p1)
    u2 = jnp.where(r >= 2, pltpu.roll(u, 2, 0), jnp.where(r == 1, p1, p2))
    return u1, u2


def _mix_project(o_nsa, o_conv, gn, gc, w, x):
    mix = jnp.concatenate([_rms(o_nsa, gn), _rms(o_conv, gc)], axis=-1).astype(BF16)
    return x + _dot(mix, w)


def _mixout_prompt_kernel(on_ref, cb_ref, cc_ref, ch_ref, cch_ref, chh_ref, cw_ref, gn_ref, gc_ref,
                          w_ref, x_ref, gm_ref, wq_ref, kv_ref, wo_ref, o_ref, ut_ref, *, tiles_per_seq):
    first = (pl.program_id(0) % tiles_per_seq) == 0
    u = cc_ref[...] * ch_ref[...]
    uh = jnp.where(first, 0.0, cch_ref[...] * chh_ref[...])
    u1, u2 = _shift_rows(u, uh[7:8], uh[6:7])
    cw = cw_ref[...]
    v = cw[0:1] * u2 + cw[1:2] * u1 + cw[2:3] * u
    x1 = _mix_project(on_ref[...], cb_ref[...] * v, gn_ref[...], gc_ref[...], w_ref[...], x_ref[...])
    ut_ref[...] = u[u.shape[0] - 8:, :]
    o_ref[...] = _mem_attend(x1, gm_ref[...], wq_ref[...], kv_ref[...].astype(BF16), wo_ref[...])


def mixout_prompt(o_nsa, p_main, cw, gn, gc, w, x, g_mem, wq, mem_kv, wo, tm, seq, layer):
    m = x.shape[0]
    nt = m // tm
    tps = seq // tm
    mem_len = mem_kv.shape[1]
    cblk = lambda c: (lambda i: (i, c // D_CONV))
    halo = lambda c: (lambda i: (jnp.maximum(i * (tm // 8) - 1, 0), c // D_CONV))
    const = lambda i: (0, 0)
    return pl.pallas_call(
        functools.partial(_mixout_prompt_kernel, tiles_per_seq=tps),
        grid=(nt,),
        in_specs=[pl.BlockSpec((tm, D_NSA), lambda i: (i, 0)),
                  pl.BlockSpec((tm, D_CONV), cblk(COL_CB)),
                  pl.BlockSpec((tm, D_CONV), cblk(COL_CC)),
                  pl.BlockSpec((tm, D_CONV), cblk(COL_CH)),
                  pl.BlockSpec((8, D_CONV), halo(COL_CC)),
                  pl.BlockSpec((8, D_CONV), halo(COL_CH)),
                  pl.BlockSpec((8, D_CONV), const),
                  pl.BlockSpec((1, D_NSA), const),
                  pl.BlockSpec((1, D_CONV), const),
                  _wspec((D_MODEL, D_MODEL), const, layer, resident=True),
                  pl.BlockSpec((tm, D_MODEL), lambda i: (i, 0)),
                  pl.BlockSpec((1, D_MODEL), const),
                  _wspec((D_MODEL, D_MEM), const, layer, resident=True),
                  pl.BlockSpec((None, mem_len, 2 * D_MEM), lambda i: (i // tps, 0, 0)),
                  _wspec((D_MEM, D_MODEL), const, layer, resident=True)],
        out_specs=[pl.BlockSpec((tm, D_MODEL), lambda i: (i, 0)),
                   pl.BlockSpec((None, 8, D_CONV), lambda i: (i, 0, 0))],
        out_shape=[jax.ShapeDtypeStruct((m, D_MODEL), F32),
                   jax.ShapeDtypeStruct((nt, 8, D_CONV), F32)],
        compiler_params=_cparams(("arbitrary",)),
        name="mixout_prompt",
    )(o_nsa, p_main, p_main, p_main, p_main, p_main, cw, gn, gc, w, x, g_mem, wq, mem_kv, wo)


def _mixout_sample_kernel(on_ref, cb_ref, cc_ref, ch_ref, st_ref, cw_ref, gn_ref, gc_ref,
                          w_ref, x_ref, o_ref, nst_ref, *, dec_seq):
    nb = x_ref.shape[1]
    ucat = [st_ref[0], st_ref[1]] + [cc_ref[t] * ch_ref[t] for t in range(dec_seq)]
    cw = cw_ref[...]
    mix = []
    for t in range(dec_seq):
        v = cw[0:1] * ucat[t] + cw[1:2] * ucat[t + 1] + cw[2:3] * ucat[t + 2]
        mix.append(jnp.concatenate([_rms(on_ref[t], gn_ref[...]), _rms(cb_ref[t] * v, gc_ref[...])],
                                   axis=-1).astype(BF16))
    proj = _dot(jnp.concatenate(mix, axis=0), w_ref[...])
    for t in range(dec_seq):
        o_ref[t] = x_ref[t] + proj[t * nb:(t + 1) * nb]
    nst_ref[0] = ucat[dec_seq]
    nst_ref[1] = ucat[dec_seq + 1]


def mixout_sample(o_nsa, p_s, state, cw, gn, gc, w, x, layer):
    dec_seq, nb, _ = x.shape
    cblk = lambda c: (lambda i: (0, 0, c // D_CONV))
    full3 = lambda i: (0, 0, 0)
    const = lambda i: (0, 0)
    return pl.pallas_call(
        functools.partial(_mixout_sample_kernel, dec_seq=dec_seq),
        grid=(1,),
        in_specs=[pl.BlockSpec((dec_seq, nb, D_NSA), full3),
                  pl.BlockSpec((dec_seq, nb, D_CONV), cblk(COL_CB)),
                  pl.BlockSpec((dec_seq, nb, D_CONV), cblk(COL_CC)),
                  pl.BlockSpec((dec_seq, nb, D_CONV), cblk(COL_CH)),
                  pl.BlockSpec((2, nb, D_CONV), full3),
                  pl.BlockSpec((8, D_CONV), const),
                  pl.BlockSpec((1, D_NSA), const),
                  pl.BlockSpec((1, D_CONV), const),
                  _wspec((D_MODEL, D_MODEL), const, layer),
                  pl.BlockSpec((dec_seq, nb, D_MODEL), full3)],
        out_specs=[pl.BlockSpec((dec_seq, nb, D_MODEL), full3),
                   pl.BlockSpec((2, nb, D_CONV), full3)],
        out_shape=[jax.ShapeDtypeStruct((dec_seq, nb, D_MODEL), F32),
                   jax.ShapeDtypeStruct((2, nb, D_CONV), F32)],
        compiler_params=_cparams(("arbitrary",)),
        name="mixout_sample",
    )(o_nsa, p_s, p_s, p_s, state, cw, gn, gc, w, x)


MEM_SCALE = MEM_HEAD_DIM ** -0.5


def _softmax(s):
    e = jnp.exp(s - jnp.max(s, axis=-1, keepdims=True))
    return e / jnp.sum(e, axis=-1, keepdims=True)


def _mem_attend(x, g, wq, kv, wo):
    qm = _dot(_rms(x, g).astype(BF16), wq).astype(BF16)
    outs = []
    for h in range(MEM_HEADS):
        c0, c1 = h * MEM_HEAD_DIM, (h + 1) * MEM_HEAD_DIM
        p = _softmax(_dot_nt(qm[:, c0:c1], kv[:, c0:c1]) * MEM_SCALE).astype(BF16)
        outs.append(_dot(p, kv[:, D_MEM + c0:D_MEM + c1]))
    return x + _dot(jnp.concatenate(outs, axis=-1).astype(BF16), wo)


def _mem_sample_kernel(q_ref, kv_ref, o_ref, *, bt, mem_len):
    n_rows = q_ref.shape[1]
    stride = 2 * MEM_HEADS
    rowh = lax.broadcasted_iota(jnp.int32, (n_rows, 1), 0) // (n_rows // MEM_HEADS)
    for bb in range(bt):
        kmat = jnp.concatenate([kv_ref[bb, pl.ds(h, mem_len, stride=stride), :]
                                for h in range(MEM_HEADS)], axis=-1).astype(BF16)
        vmat = jnp.concatenate([kv_ref[bb, pl.ds(MEM_HEADS + h, mem_len, stride=stride), :]
                                for h in range(MEM_HEADS)], axis=-1).astype(BF16)
        p = _softmax(_dot_nt(q_ref[bb].astype(BF16), kmat) * MEM_SCALE).astype(BF16)
        o2 = _dot(p, vmat)
        out = jnp.zeros((n_rows, MEM_HEAD_DIM), F32)
        for h in range(MEM_HEADS):
            out = out + jnp.where(rowh == h, o2[:, h * MEM_HEAD_DIM:(h + 1) * MEM_HEAD_DIM], 0.0)
        o_ref[bb] = out


def mem_sample_attn(qbd, cache_mem, layer, bt):
    nb, n_rows, _ = qbd.shape
    rows = cache_mem.shape[2]
    return pl.pallas_call(
        functools.partial(_mem_sample_kernel, bt=bt, mem_len=rows // (2 * MEM_HEADS)),
        grid=(nb // bt,),
        in_specs=[pl.BlockSpec((bt, n_rows, D_MEM), lambda i: (i, 0, 0)),
                  pl.BlockSpec((None, bt, rows, MEM_HEAD_DIM), lambda i: (layer, i, 0, 0))],
        out_specs=pl.BlockSpec((bt, n_rows, MEM_HEAD_DIM), lambda i: (i, 0, 0)),
        out_shape=jax.ShapeDtypeStruct((nb, n_rows, MEM_HEAD_DIM), F32),
        compiler_params=_cparams(("arbitrary",)),
        name="mem_sample_attn",
    )(qbd, cache_mem)


def _silu(a):
    return a * _sigmoid(a)


def _ffn_prompt_kernel(x_ref, xh_ref, g_ref, wg_ref, wu_ref, cw_ref, wd_ref, gf_ref,
                       o_ref, gt_ref, h_s, hh_s, *, tiles_per_seq, final):
    j = pl.program_id(1)
    tm = x_ref.shape[0]

    @pl.when(j == 0)
    def _():
        first = (pl.program_id(0) % tiles_per_seq) == 0
        x = x_ref[...]
        h_s[...] = _rms(x, g_ref[...]).astype(BF16)
        hh = jnp.where(first, 0.0, _rms(xh_ref[...], g_ref[...]))
        hh_s[...] = jnp.concatenate([hh, jnp.zeros_like(hh)], axis=0).astype(BF16)
        o_ref[...] = x

    h = h_s[...]
    gate = _dot(h, wg_ref[...])
    gate_h = _dot(hh_s[...], wg_ref[...])
    g1, g2 = _shift_rows(gate, gate_h[7:8], gate_h[6:7])
    cw = cw_ref[...]
    a = cw[0:1] * g2 + cw[1:2] * g1 + cw[2:3] * gate
    z = (_silu(a) * _dot(h, wu_ref[...])).astype(BF16)
    o_ref[...] += _dot(z, wd_ref[...])
    gt_ref[...] = gate[tm - 8:, :]

    if final:
        @pl.when(j == pl.num_programs(1) - 1)
        def _():
            o_ref[...] = _rms(o_ref[...], gf_ref[...])


def ffn_prompt(x, g, wg, wu, cw, wd, gf, tm, tn, seq, final, layer):
    m = x.shape[0]
    nt = m // tm
    nj = D_FF // tn
    const = lambda i, j: (0, 0)
    return pl.pallas_call(
        functools.partial(_ffn_prompt_kernel, tiles_per_seq=seq // tm, final=final),
        grid=(nt, nj),
        in_specs=[pl.BlockSpec((tm, D_MODEL), lambda i, j: (i, 0)),
                  pl.BlockSpec((8, D_MODEL), lambda i, j: (jnp.maximum(i * (tm // 8) - 1, 0), 0)),
                  pl.BlockSpec((1, D_MODEL), const),
                  _wspec((D_MODEL, tn), lambda i, j: (0, j), layer),
                  _wspec((D_MODEL, tn), lambda i, j: (0, j), layer),
                  pl.BlockSpec((8, tn), lambda i, j: (0, j)),
                  _wspec((tn, D_MODEL), lambda i, j: (j, 0), layer),
                  pl.BlockSpec((1, D_MODEL), const)],
        out_specs=[pl.BlockSpec((tm, D_MODEL), lambda i, j: (i, 0)),
                   pl.BlockSpec((None, 8, tn), lambda i, j: (i, 0, j))],
        out_shape=[jax.ShapeDtypeStruct((m, D_MODEL), F32),
                   jax.ShapeDtypeStruct((nt, 8, D_FF), F32)],
        scratch_shapes=[pltpu.VMEM((tm, D_MODEL), BF16),
                        pltpu.VMEM((16, D_MODEL), BF16)],
        compiler_params=_cparams(("arbitrary", "arbitrary")),
        name="ffn_prompt",
    )(x, x, g, wg, wu, cw, wd, gf)


def _ffn_sample_kernel(x_ref, st_ref, g_ref, wg_ref, wu_ref, cw_ref, wd_ref, gf_ref,
                       o_ref, nst_ref, h_s, acc_s, *, dec_seq, final):
    j = pl.program_id(0)

    nb = x_ref.shape[1]

    @pl.when(j == 0)
    def _():
        for t in range(dec_seq):
            h_s[t * nb:(t + 1) * nb, :] = _rms(x_ref[t], g_ref[...]).astype(BF16)
        acc_s[...] = jnp.zeros_like(acc_s)

    h = h_s[...]
    gate = _dot(h, wg_ref[...])
    up = _dot(h, wu_ref[...])
    gcat = [st_ref[0], st_ref[1]] + [gate[t * nb:(t + 1) * nb] for t in range(dec_seq)]
    cw = cw_ref[...]
    a = jnp.concatenate([cw[0:1] * gcat[t] + cw[1:2] * gcat[t + 1] + cw[2:3] * gcat[t + 2]
                         for t in range(dec_seq)], axis=0)
    acc_s[...] += _dot((_silu(a) * up).astype(BF16), wd_ref[...])
    nst_ref[0] = gcat[dec_seq]
    nst_ref[1] = gcat[dec_seq + 1]

    @pl.when(j == pl.num_programs(0) - 1)
    def _():
        for t in range(dec_seq):
            y = x_ref[t] + acc_s[t * nb:(t + 1) * nb, :]
            o_ref[t] = _rms(y, gf_ref[...]) if final else y


def ffn_sample(x, state, g, wg, wu, cw, wd, gf, tn, final, layer):
    dec_seq, nb, _ = x.shape
    full3 = lambda j: (0, 0, 0)
    const = lambda j: (0, 0)
    return pl.pallas_call(
        functools.partial(_ffn_sample_kernel, dec_seq=dec_seq, final=final),
        grid=(D_FF // tn,),
        in_specs=[pl.BlockSpec((dec_seq, nb, D_MODEL), full3),
                  pl.BlockSpec((2, nb, tn), lambda j: (0, 0, j)),
                  pl.BlockSpec((1, D_MODEL), const),
                  _wspec((D_MODEL, tn), lambda j: (0, j), layer),
                  _wspec((D_MODEL, tn), lambda j: (0, j), layer),
                  pl.BlockSpec((8, tn), lambda j: (0, j)),
                  _wspec((tn, D_MODEL), lambda j: (j, 0), layer),
                  pl.BlockSpec((1, D_MODEL), const)],
        out_specs=[pl.BlockSpec((dec_seq, nb, D_MODEL), full3),
                   pl.BlockSpec((2, nb, tn), lambda j: (0, 0, j))],
        out_shape=[jax.ShapeDtypeStruct((dec_seq, nb, D_MODEL), F32),
                   jax.ShapeDtypeStruct((2, nb, D_FF), F32)],
        scratch_shapes=[pltpu.VMEM((dec_seq * nb, D_MODEL), BF16),
                        pltpu.VMEM((dec_seq * nb, D_MODEL), F32)],
        compiler_params=_cparams(("arbitrary",)),
        name="ffn_sample",
    )(x, state, g, wg, wu, cw, wd, gf)


def _overlap(n_cmp, n_sel, rows, cols):
    c0 = np.arange(n_cmp)[:, None] * CMP_STRIDE
    s0 = np.arange(n_sel)[None, :] * SEL_BLOCK
    ov = np.minimum(c0 + CMP_BLOCK, s0 + SEL_BLOCK) - np.maximum(c0, s0)
    out = np.zeros((rows, cols), np.float32)
    out[:n_cmp, :n_sel] = np.clip(ov, 0, None).astype(np.float32) / CMP_BLOCK
    return out


def _block_expand(n_keys):
    return jnp.asarray(np.arange(128)[:, None] == (np.arange(n_keys)[None, :] // SEL_BLOCK), BF16)


def _chunk_perm():
    row = np.arange(PAGE)
    return jnp.asarray((row[:, None] % RPC) * CMP_STRIDE + row[:, None] // RPC == row[None, :], BF16)


def _pad_rows(a, rows):
    return jnp.concatenate([a, jnp.zeros((rows - a.shape[0],) + a.shape[1:], a.dtype)], axis=0)


def _prep_w_in(w_in_l):
    offs = np.cumsum((D_NSA, KV_W, KV_W, KV_W, N_BRANCH * N_HEADS, D_CONV, D_CONV, D_CONV))
    q, kc, ks, kw, gl, cb, cc, ch = jnp.split(w_in_l.T, [int(o) for o in offs[:-1]], axis=0)
    pad = jnp.zeros((D_KVT - ROW_G - N_BRANCH * N_HEADS, D_MODEL), F32)
    w_main = jnp.concatenate([q, cb, cc, ch], axis=0).astype(BF16)
    wt_kv = jnp.concatenate([kc, ks, kw, gl, pad], axis=0).astype(BF16)
    return w_main[None], wt_kv[None]


def _prep_cmp(w_cmp_l, pe_cmp_l):
    blocks = [jnp.pad(w_cmp_l[c], ((0, 0), (0, 0), ((c * N_KV + g) * HEAD_DIM,
                                                     KV_W - (c * N_KV + g + 1) * HEAD_DIM)))
              for c in range(2) for g in range(N_KV)]
    wbd = jnp.concatenate(blocks, axis=1).astype(BF16)
    pe = pe_cmp_l.reshape(2, 2, CMP_STRIDE, HEAD_DIM).transpose(1, 0, 3, 2)
    pet = jnp.broadcast_to(pe[:, :, None, :, None, :], (2, 2, N_KV, HEAD_DIM, RPC, CMP_STRIDE))
    return wbd, pet.reshape(2, KV_W, PAGE)


TM_PROJ = 1024
TN_PROJ = 1024
TM_MIX = 512
TM_FFN = 1024
TN_FFN = 512
BT_CMP = 4
BT_NSA = 4
BT_MEM = 8
NEW_PAD = 16


def _kv_rows(kvt, row, start=0):
    slab = kvt[:, row:row + KV_W, start:]
    b, _, t = slab.shape
    return slab.reshape(b, 2, N_KV, HEAD_DIM, t).transpose(0, 4, 1, 2, 3)


def _layer_prompt(x, mem_kv, lw, consts, layer, batch, seq, final):
    p_main, kvt = proj_in(x, lw['g_mix'], lw['w_main'], lw['wt_kv'], TM_PROJ, TN_PROJ, seq, 0)
    cmp = compress_prompt(kvt, lw['pet'], consts['perm'], lw['wbd'])
    n_cmp_pad = seq // CMP_STRIDE
    cmp_t = cmp.reshape(batch, n_cmp_pad, 2, N_KV, HEAD_DIM).transpose(0, 2, 3, 1, 4)
    o_nsa = nsa_prompt(p_main, kvt, cmp_t, consts['ovt_p'], consts['expand_p'])
    x2, u_tail = mixout_prompt(o_nsa, p_main, lw['conv_w'], lw['g_out_nsa'], lw['g_out_conv'],
                               lw['w_out'], x, lw['g_mem'], lw['w_mem_q'], mem_kv, lw['w_mem_o'],
                               TM_MIX, seq, layer)
    x3, g_tail = ffn_prompt(x2, lw['g_ffn'], lw['w_ff_gate'], lw['w_ff_up'], lw['ffn_conv_w'],
                            lw['w_ff_down'], lw['g_final'], TM_FFN, TN_FFN, seq, final, layer)
    conv_state = u_tail.reshape(batch, seq // TM_MIX, 8, D_CONV)[:, -1, 6:8]
    ffn_state = g_tail.reshape(batch, seq // TM_FFN, 8, D_FF)[:, -1, 6:8]
    return (x3, _kv_rows(kvt, ROW_KC), _kv_rows(kvt, ROW_KS),
            _kv_rows(kvt, ROW_KW, seq - min(WINDOW, seq)), conv_state, ffn_state)


def _layer_sample(x, lw, consts, layer, cache_cmp, cache_sel, page_idx, cache_win, prev_win, cache_mem,
                  st_conv, st_ffn, final):
    dec_seq, nb, _ = x.shape
    rows = dec_seq * nb
    p_s, kvt = proj_in(x.reshape(rows, D_MODEL), lw['g_mix'], lw['w_main'], lw['wt_kv'],
                       rows, TN_PROJ, rows, 0)
    p3 = p_s.reshape(dec_seq, nb, D_MAIN)
    kvn = kvt.reshape(D_KVT, rows).T.reshape(dec_seq, nb, D_KVT).transpose(1, 0, 2)
    kvn_pad = jnp.concatenate([kvn, jnp.zeros((nb, NEW_PAD - dec_seq, D_KVT), F32)], axis=1)
    cmp_s = compress_paged(cache_cmp, page_idx, lw['pet'], consts['perm'], lw['wbd'], BT_CMP)
    q = p3[:, :, COL_Q:COL_Q + D_NSA].reshape(dec_seq, nb, N_KV, HPG, HEAD_DIM).transpose(1, 3, 2, 0, 4)
    gsel = jnp.eye(N_KV, dtype=F32) * SCORE_SCALE
    qbd = (q[:, :, :, :, None, :] * gsel[None, None, :, None, :, None]).reshape(
        nb, HPG * N_KV * dec_seq, KV_HALF)
    gates = kvn[:, :, ROW_G:ROW_G + N_BRANCH * N_HEADS].reshape(
        nb, dec_seq, N_KV, HPG, N_BRANCH).transpose(0, 3, 2, 1, 4).reshape(nb, HPG * N_KV * dec_seq, N_BRANCH)
    o, win_stack = nsa_sample(cache_sel, page_idx, qbd, gates, cmp_s, kvn_pad, cache_win, layer, prev_win,
                              consts['ov_s'], consts['expand_s'], BT_NSA)
    o_nsa = o.reshape(nb, HPG, N_KV, dec_seq, HEAD_DIM).transpose(3, 0, 2, 1, 4).reshape(dec_seq, nb, D_NSA)
    x1, conv_state = mixout_sample(o_nsa, p3, st_conv, lw['conv_w'], lw['g_out_nsa'],
                                   lw['g_out_conv'], lw['w_out'], x, layer)
    qm = rms_matmul(x1.reshape(rows, D_MODEL), lw['g_mem'], lw['w_mem_q'], rows, D_MEM, layer)
    qh = qm.reshape(dec_seq, nb, MEM_HEADS, MEM_HEAD_DIM).transpose(1, 2, 0, 3)
    qmbd = (qh[:, :, :, None, :] * jnp.eye(MEM_HEADS, dtype=F32)[None, :, None, :, None]).reshape(
        nb, MEM_HEADS * dec_seq, D_MEM)
    om = mem_sample_attn(qmbd, cache_mem, layer, BT_MEM)
    om = om.reshape(nb, MEM_HEADS, dec_seq, MEM_HEAD_DIM).transpose(2, 0, 1, 3).reshape(rows, D_MEM)
    x2 = matmul_residual(x1.reshape(rows, D_MODEL), om, lw['w_mem_o'], rows, layer).reshape(
        dec_seq, nb, D_MODEL)
    x3, ffn_state = ffn_sample(x2, st_ffn, lw['g_ffn'], lw['w_ff_gate'], lw['w_ff_up'],
                               lw['ffn_conv_w'], lw['w_ff_down'], lw['g_final'], TN_FFN, final, layer)
    kv_c = kvn[:, :, ROW_KC:ROW_KC + KV_W].reshape(nb, dec_seq, 2, N_KV, HEAD_DIM)
    kv_s = kvn[:, :, ROW_KS:ROW_KS + KV_W].reshape(nb, dec_seq, 2, N_KV, HEAD_DIM)
    return (x3, kv_c, kv_s, win_stack, conv_state.transpose(1, 0, 2), ffn_state.transpose(1, 0, 2))


def kernel(x_prompt, x_sample, cache_cmp_kv, cache_sel_kv, cache_win_kv, cache_mem_kv,
           state_conv, state_ffn_conv, page_table, mem_prompt,
           g_mix, w_in, w_cmp, pe_cmp, conv_w, g_out_nsa, g_out_conv, w_out,
           g_mem_src, w_mem_kv, g_mem, w_mem_q, w_mem_o, g_ffn, w_ff_gate, w_ff_up,
           ffn_conv_w, w_ff_down, g_final):
    batch, seq, _ = x_prompt.shape
    nb, dec_seq, _ = x_sample.shape
    depth = w_in.shape[0]
    n_phys = cache_cmp_kv.shape[1]
    npages = page_table.shape[1]
    past = npages * PAGE
    win_buf = cache_win_kv.shape[2]
    mem_len = mem_prompt.shape[1]

    n_cmp_s = (past + dec_seq - CMP_BLOCK) // CMP_STRIDE + 1
    n_sel_s = -(-(past + dec_seq) // SEL_BLOCK)
    n_cmp_p = (seq - CMP_BLOCK) // CMP_STRIDE + 1
    n_sel_p = seq // SEL_BLOCK
    consts = {
        'ov_s': jnp.asarray(_overlap(n_cmp_s, n_sel_s, past // CMP_STRIDE, 128), BF16),
        'expand_s': _block_expand(past + PAGE),
        'ovt_p': jnp.asarray(_overlap(n_cmp_p, n_sel_p, seq // CMP_STRIDE, n_sel_p).T, BF16),
        'expand_p': _block_expand(seq),
        'perm': _chunk_perm(),
    }

    cache_cmp = cache_cmp_kv.transpose(0, 1, 3, 4, 5, 2).reshape(depth * n_phys, 2, N_KV, HEAD_DIM, PAGE)
    cache_sel = cache_sel_kv.transpose(0, 1, 3, 4, 5, 2).reshape(depth * n_phys, 2, N_KV, HEAD_DIM, PAGE)
    cache_win = cache_win_kv.transpose(0, 1, 3, 4, 5, 2)
    cache_mem = cache_mem_kv.reshape(depth, nb, mem_len * 2 * MEM_HEADS, MEM_HEAD_DIM)

    xp = x_prompt.reshape(batch * seq, D_MODEL)
    xs = x_sample.transpose(1, 0, 2)
    mem_rows = mem_prompt.reshape(batch * mem_len, D_MODEL)
    outs = [[] for _ in range(10)]
    win_stack = None
    stacked = {
        'w_out': w_out.astype(BF16), 'w_mem_q': w_mem_q.astype(BF16), 'w_mem_o': w_mem_o.astype(BF16),
        'w_ff_gate': w_ff_gate.astype(BF16), 'w_ff_up': w_ff_up.astype(BF16),
        'w_ff_down': w_ff_down.astype(BF16),
    }
    w_mem_kv_b = w_mem_kv.astype(BF16)
    for l in range(depth):
        wbd, pet = _prep_cmp(w_cmp[l], pe_cmp[l])
        w_main, wt_kv = _prep_w_in(w_in[l])
        lw = dict(stacked)
        lw.update({
            'g_mix': g_mix[l][None], 'w_main': w_main, 'wt_kv': wt_kv, 'wbd': wbd, 'pet': pet,
            'conv_w': _pad_rows(conv_w[l], 8), 'g_out_nsa': g_out_nsa[l][None],
            'g_out_conv': g_out_conv[l][None], 'g_mem': g_mem[l][None], 'g_ffn': g_ffn[l][None],
            'ffn_conv_w': _pad_rows(ffn_conv_w[l], 8), 'g_final': g_final[None],
        })
        final = l == depth - 1
        mem_kv_p = rms_matmul(mem_rows, g_mem_src[l][None], w_mem_kv_b,
                              batch * mem_len, 2 * D_MEM, l).reshape(batch, mem_len, 2 * D_MEM)
        xp, kc, ks, kw, cst, fst = _layer_prompt(xp, mem_kv_p, lw, consts, l, batch, seq, final)
        page_idx = page_table + l * n_phys
        xs, kc2, ks2, win_stack, cst2, fst2 = _layer_sample(
            xs, lw, consts, l, cache_cmp, cache_sel, page_idx, cache_win, win_stack, cache_mem,
            state_conv[l].transpose(1, 0, 2), state_ffn_conv[l].transpose(1, 0, 2), final)
        for lst, val in zip(outs, (kc, ks, kw, mem_kv_p, cst, fst, kc2, ks2, cst2, fst2)):
            lst.append(val)

    st = [jnp.stack(o) for o in outs]
    return (xp.reshape(batch, seq, D_MODEL), xs.transpose(1, 0, 2),
            st[0], st[1], st[2],
            st[3].reshape(depth, batch, mem_len, 2, MEM_HEADS, MEM_HEAD_DIM),
            st[4], st[5], st[6], st[7],
            win_stack.transpose(0, 1, 5, 2, 3, 4),
            st[8], st[9])
```

```python
import functools

import numpy as np
import jax
import jax.numpy as jnp
from jax import lax
from jax.experimental import pallas as pl
from jax.experimental.pallas import tpu as pltpu

F32 = jnp.float32
BF16 = jnp.bfloat16

D_MODEL = 2048
D_NSA = 1024
D_CONV = 1024
HEAD_DIM = 64
N_HEADS = 16
N_KV = 2
HPG = 8
N_BRANCH = 3
KV_W = 2 * N_KV * HEAD_DIM
KV_HALF = N_KV * HEAD_DIM
CMP_BLOCK = 32
CMP_STRIDE = 16
SEL_BLOCK = 64
TOPK = 16
WINDOW = 512
Q_BLOCK = 128
FORCE_BONUS = 1e4
D_FF = 5632
MEM_HEADS = 4
MEM_HEAD_DIM = 128
D_MEM = 512
PAGE = 128
EPS = 1e-6
NEG = -1e30
SCORE_SCALE = HEAD_DIM ** -0.5

COL_Q = 0
COL_CB = 1024
COL_CC = 2048
COL_CH = 3072
D_MAIN = 4096
ROW_KC = 0
ROW_KS = 256
ROW_KW = 512
ROW_G = 768
D_KVT = 896

VMEM_LIMIT = 56 * 1024 * 1024


def _cparams(sem):
    return pltpu.CompilerParams(dimension_semantics=sem, vmem_limit_bytes=VMEM_LIMIT)


def _rms(x, g):
    return x * lax.rsqrt(jnp.mean(x * x, axis=-1, keepdims=True) + EPS) * g


def _dot(a, b):
    return jnp.dot(a, b, preferred_element_type=F32)


def _dot_nt(a, b):
    return lax.dot_general(a, b, (((1,), (1,)), ((), ())), preferred_element_type=F32)


def _softmax_bias(s, bias):
    s = s + bias
    e = jnp.exp(s - jnp.max(s, axis=-1, keepdims=True))
    return e, jnp.sum(e, axis=-1, keepdims=True)


def _sigmoid(x):
    return 1.0 / (1.0 + jnp.exp(-x))


def _rms_matmul_kernel(x_ref, g_ref, w_ref, o_ref, h_ref):
    @pl.when(pl.program_id(1) == 0)
    def _():
        h_ref[...] = _rms(x_ref[...], g_ref[...]).astype(BF16)

    o_ref[...] = _dot(h_ref[...], w_ref[...]).astype(o_ref.dtype)


def _wspec(block, imap, layer, resident=False):
    kwargs = {'pipeline_mode': pl.Buffered(1)} if resident else {}
    return pl.BlockSpec((None,) + block, lambda *a: (layer,) + imap(*a), **kwargs)


def rms_matmul(x, g, w, tm, tn, layer):
    m, d = x.shape
    n = w.shape[2]
    return pl.pallas_call(
        _rms_matmul_kernel,
        grid=(m // tm, n // tn),
        in_specs=[pl.BlockSpec((tm, d), lambda i, j: (i, 0)),
                  pl.BlockSpec((1, d), lambda i, j: (0, 0)),
                  _wspec((d, tn), lambda i, j: (0, j), layer)],
        out_specs=pl.BlockSpec((tm, tn), lambda i, j: (i, j)),
        out_shape=jax.ShapeDtypeStruct((m, n), F32),
        scratch_shapes=[pltpu.VMEM((tm, d), BF16)],
        compiler_params=_cparams(("arbitrary", "arbitrary")),
        name="rms_matmul",
    )(x, g, w)


def _proj_in_kernel(x_ref, g_ref, w_ref, wt_ref, o_ref, ot_ref, h_ref):
    @pl.when(pl.program_id(1) == 0)
    def _():
        h = _rms(x_ref[...], g_ref[...]).astype(BF16)
        h_ref[...] = h
        ot_ref[...] = _dot_nt(wt_ref[...], h)

    o_ref[...] = _dot_nt(h_ref[...], w_ref[...])


def proj_in(x, g, w_main, wt_kv, tm, tn, seq, layer):
    m, d = x.shape
    tps = seq // tm
    return pl.pallas_call(
        _proj_in_kernel,
        grid=(m // tm, D_MAIN // tn),
        in_specs=[pl.BlockSpec((tm, d), lambda i, j: (i, 0)),
                  pl.BlockSpec((1, d), lambda i, j: (0, 0)),
                  _wspec((tn, d), lambda i, j: (j, 0), layer),
                  _wspec((D_KVT, d), lambda i, j: (0, 0), layer)],
        out_specs=[pl.BlockSpec((tm, tn), lambda i, j: (i, j)),
                   pl.BlockSpec((None, D_KVT, tm), lambda i, j: (i // tps, 0, i % tps))],
        out_shape=[jax.ShapeDtypeStruct((m, D_MAIN), F32),
                   jax.ShapeDtypeStruct((m // seq, D_KVT, seq), F32)],
        scratch_shapes=[pltpu.VMEM((tm, d), BF16)],
        compiler_params=_cparams(("arbitrary", "arbitrary")),
        name="proj_in",
    )(x, g, w_main, wt_kv)


def _matmul_res_kernel(x_ref, a_ref, w_ref, o_ref):
    o_ref[...] = x_ref[...] + _dot(a_ref[...].astype(BF16), w_ref[...])


def matmul_residual(x, a, w, tm, layer):
    m, n = x.shape
    k = a.shape[1]
    return pl.pallas_call(
        _matmul_res_kernel,
        grid=(m // tm,),
        in_specs=[pl.BlockSpec((tm, n), lambda i: (i, 0)),
                  pl.BlockSpec((tm, k), lambda i: (i, 0)),
                  _wspec((k, n), lambda i: (0, 0), layer)],
        out_specs=pl.BlockSpec((tm, n), lambda i: (i, 0)),
        out_shape=jax.ShapeDtypeStruct((m, n), F32),
        compiler_params=_cparams(("arbitrary",)),
        name="matmul_residual",
    )(x, a, w)


RPC = PAGE // CMP_STRIDE


def _compress_chunks(chunk_at, n_chunks, pet_ref, perm_ref, w_ref, xs_ref):
    perm = perm_ref[...]
    for ci in range(n_chunks):
        chunk = chunk_at(ci)
        for a in range(2):
            xs_ref[a, ci] = _dot_nt(perm, (chunk + pet_ref[a]).astype(BF16))
    rows = n_chunks * RPC
    acc0 = jnp.zeros((rows, KV_W), F32)
    acc1 = jnp.zeros((rows, KV_W), F32)
    for r in range(CMP_STRIDE):
        x0 = xs_ref[0, :, r * RPC:(r + 1) * RPC, :].reshape(rows, KV_W).astype(BF16)
        x1 = xs_ref[1, :, r * RPC:(r + 1) * RPC, :].reshape(rows, KV_W).astype(BF16)
        acc0 = acc0 + _dot(x0, w_ref[r])
        acc1 = acc1 + _dot(x1, w_ref[CMP_STRIDE + r])
    return acc0 + pltpu.roll(acc1, rows - 1, 0)


def _compress_prompt_kernel(kvt_ref, pet_ref, perm_ref, w_ref, o_ref, xs_ref, *, n_chunks):
    out = _compress_chunks(lambda ci: kvt_ref[:, ci * PAGE:(ci + 1) * PAGE], n_chunks,
                           pet_ref, perm_ref, w_ref, xs_ref)
    o_ref[...] = out.astype(BF16)


def compress_prompt(kvt, pet, perm, wbd):
    batch, _, seq = kvt.shape
    n_chunks = seq // PAGE
    nblk = seq // CMP_STRIDE
    return pl.pallas_call(
        functools.partial(_compress_prompt_kernel, n_chunks=n_chunks),
        grid=(batch,),
        in_specs=[pl.BlockSpec((None, KV_W, seq), lambda b: (b, ROW_KC // KV_W, 0)),
                  pl.BlockSpec((2, KV_W, PAGE), lambda b: (0, 0, 0)),
                  pl.BlockSpec((PAGE, PAGE), lambda b: (0, 0)),
                  pl.BlockSpec((CMP_BLOCK, KV_W, KV_W), lambda b: (0, 0, 0))],
        out_specs=pl.BlockSpec((None, nblk, KV_W), lambda b: (b, 0, 0)),
        out_shape=jax.ShapeDtypeStruct((batch, nblk, KV_W), BF16),
        scratch_shapes=[pltpu.VMEM((2, n_chunks, PAGE, KV_W), F32)],
        compiler_params=_cparams(("arbitrary",)),
        name="compress_prompt",
    )(kvt, pet, perm, wbd)


def _compress_paged_kernel(pt_ref, *refs, bt, npages):
    page_refs = refs[:bt * npages]
    pet_ref, perm_ref, w_ref, o_ref, xs_ref = refs[bt * npages:]
    out = _compress_chunks(lambda ci: page_refs[ci][...].reshape(KV_W, PAGE), bt * npages,
                           pet_ref, perm_ref, w_ref, xs_ref)
    o_ref[...] = out.astype(BF16).reshape(bt, npages * RPC, KV_W)


def compress_paged(cache, page_idx, pet, perm, wbd, bt):
    nb, npages = page_idx.shape
    nblk = npages * RPC
    page_specs = [
        pl.BlockSpec((None, 2, N_KV, HEAD_DIM, PAGE), functools.partial(
            lambda i, pt, bb, p: (pt[i * bt + bb, p], 0, 0, 0, 0), bb=bb, p=p))
        for bb in range(bt) for p in range(npages)]
    grid_spec = pltpu.PrefetchScalarGridSpec(
        num_scalar_prefetch=1,
        grid=(nb // bt,),
        in_specs=page_specs + [pl.BlockSpec((2, KV_W, PAGE), lambda i, pt: (0, 0, 0)),
                               pl.BlockSpec((PAGE, PAGE), lambda i, pt: (0, 0)),
                               pl.BlockSpec((CMP_BLOCK, KV_W, KV_W), lambda i, pt: (0, 0, 0))],
        out_specs=pl.BlockSpec((bt, nblk, KV_W), lambda i, pt: (i, 0, 0)),
        scratch_shapes=[pltpu.VMEM((2, bt * npages, PAGE, KV_W), F32)])
    return pl.pallas_call(
        functools.partial(_compress_paged_kernel, bt=bt, npages=npages),
        grid_spec=grid_spec,
        out_shape=jax.ShapeDtypeStruct((nb, nblk, KV_W), BF16),
        compiler_params=_cparams(("arbitrary",)),
        name="compress_paged",
    )(page_idx, *([cache] * (bt * npages)), pet, perm, wbd)


def _importance(imp, idx, tpos, n_sel):
    valid = (idx * SEL_BLOCK <= tpos) & (idx < n_sel)
    cur = tpos // SEL_BLOCK
    forced = (idx == 0) | (idx == cur) | (idx == cur - 1)
    return jnp.where(valid, imp + jnp.where(forced, FORCE_BONUS, 0.0), NEG)


def _topk_rows(imp, n_cand):
    groups = [imp[v * 8:(v + 1) * 8] for v in range(n_cand // 8)]
    sub = lax.broadcasted_iota(jnp.int32, groups[0].shape, 0)
    ranks = [jnp.zeros(g.shape, F32) for g in groups]
    for jp in range(n_cand):
        c = imp[jp:jp + 1, :]
        for v, g in enumerate(groups):
            ge = jnp.where(c >= g, 1.0, 0.0)
            gt = jnp.where(c > g, 1.0, 0.0)
            if v * 8 > jp:
                beats = ge
            elif v * 8 + 7 < jp:
                beats = gt
            else:
                beats = jnp.where(sub > jp - v * 8, ge, gt)
            ranks[v] = ranks[v] + beats
    return jnp.concatenate([jnp.where(r < TOPK, 1.0, 0.0) for r in ranks], axis=0)


def _topk_lanes(imp, idx, n_cand):
    rank = jnp.zeros(imp.shape, F32)
    for jp in range(n_cand):
        c = imp[:, jp:jp + 1]
        rank = rank + jnp.where(idx > jp, jnp.where(c >= imp, 1.0, 0.0), jnp.where(c > imp, 1.0, 0.0))
    return jnp.where(rank < TOPK, 1.0, 0.0)


KEY_TILE = 512
V_PAD = 16


def _tile_heads(x):
    return jnp.concatenate([x] * HPG, axis=0)


def _nsa_prompt_kernel(q_ref, gate_ref, cmp_ref, kst_ref, vst_ref, kwt_ref, vwt_ref, ovt_ref, exp_ref,
                       o_ref, ksb, vsb, kwb, vwb, *, seq):
    g = pl.program_id(1)
    i = pl.program_id(2)
    n_cmp_pad = seq // CMP_STRIDE
    n_sel = seq // SEL_BLOCK

    @pl.when(i == 0)
    def _():
        ones_row = jnp.where(lax.broadcasted_iota(jnp.int32, (V_PAD, seq), 0) == 0, 1.0, 0.0).astype(BF16)
        ksb[...] = kst_ref[...].astype(BF16)
        vsb[0:HEAD_DIM, :] = vst_ref[...].astype(BF16)
        vsb[HEAD_DIM:HEAD_DIM + V_PAD, :] = ones_row
        kwb[...] = kwt_ref[...].astype(BF16)
        vwb[0:HEAD_DIM, :] = vwt_ref[...].astype(BF16)
        vwb[HEAD_DIM:HEAD_DIM + V_PAD, :] = ones_row

    t0 = i * Q_BLOCK
    qb = q_ref[...] * SCORE_SCALE
    qs = jnp.concatenate([qb[:, h * HEAD_DIM:(h + 1) * HEAD_DIM] for h in range(HPG)],
                         axis=0).astype(BF16)
    tcol = lax.broadcasted_iota(jnp.int32, (Q_BLOCK, 1), 0) + t0

    ncol = lax.broadcasted_iota(jnp.int32, (Q_BLOCK, n_cmp_pad), 1)
    vis = jnp.where(ncol < n_cmp_pad - 1, ncol * CMP_STRIDE + (CMP_BLOCK - 1), seq) <= tcol
    e_c, l_c = _softmax_bias(_dot_nt(qs, cmp_ref[0]), _tile_heads(jnp.where(vis, 0.0, NEG)))
    any_vis = _tile_heads(jnp.where(tcol >= CMP_BLOCK - 1, 1.0, 0.0))
    p_c = (e_c * (any_vis / jnp.maximum(l_c, 1e-30))).astype(BF16)
    o_c = _dot(p_c, cmp_ref[1])

    po = _dot_nt(ovt_ref[...], p_c)
    imp_t = po[:, 0:Q_BLOCK]
    for h in range(1, HPG):
        imp_t = imp_t + po[:, h * Q_BLOCK:(h + 1) * Q_BLOCK]
    jj = lax.broadcasted_iota(jnp.int32, (n_sel, Q_BLOCK), 0)
    tt = lax.broadcasted_iota(jnp.int32, (n_sel, Q_BLOCK), 1) + t0
    sel_t = _topk_rows(_importance(imp_t, jj, tt, n_sel), n_sel)
    sel = jnp.concatenate([sel_t, jnp.zeros((128 - n_sel, Q_BLOCK), F32)], axis=0).T.astype(BF16)

    kk = lax.broadcasted_iota(jnp.int32, (Q_BLOCK, KEY_TILE), 1)

    def scores(c):
        k0 = pl.multiple_of(c * KEY_TILE, KEY_TILE)
        msel = _dot(sel, exp_ref[:, pl.ds(k0, KEY_TILE)])
        bias = jnp.where(msel > 0.5, jnp.where(kk + k0 <= tcol, 0.0, NEG), NEG)
        return _dot(qs, ksb[:, pl.ds(k0, KEY_TILE)]) + _tile_heads(bias)

    def accumulate(c, s, carry):
        m, acc = carry
        k0 = pl.multiple_of(c * KEY_TILE, KEY_TILE)
        m_new = jnp.maximum(m, jnp.max(s, axis=-1, keepdims=True))
        p = jnp.exp(s - m_new).astype(BF16)
        return m_new, jnp.exp(m - m_new) * acc + _dot_nt(p, vsb[:, pl.ds(k0, KEY_TILE)])

    def tile_pair(c2, carry):
        s_a = scores(2 * c2)
        s_b = scores(2 * c2 + 1)
        return accumulate(2 * c2 + 1, s_b, accumulate(2 * c2, s_a, carry))

    n_rows = HPG * Q_BLOCK
    m0 = jnp.full((n_rows, 1), NEG, F32)
    a0 = jnp.zeros((n_rows, HEAD_DIM + V_PAD), F32)
    n_tiles = (t0 + Q_BLOCK + KEY_TILE - 1) // KEY_TILE
    def tile_quad(c4, carry):
        ss = [scores(4 * c4 + u) for u in range(4)]
        for u in range(4):
            carry = accumulate(4 * c4 + u, ss[u], carry)
        return carry

    carry = lax.fori_loop(0, n_tiles // 4, tile_quad, (m0, a0))

    def tile_rest(count):
        def run(cr):
            first = (n_tiles // 4) * 4
            ss = [scores(first + u) for u in range(count)]
            for u in range(count):
                cr = accumulate(first + u, ss[u], cr)
            return cr
        return run

    _, acc_s = lax.switch(n_tiles % 4, [tile_rest(r) for r in range(4)], carry)
    o_s = acc_s[:, 0:HEAD_DIM] / acc_s[:, HEAD_DIM:HEAD_DIM + 1]

    span = WINDOW + Q_BLOCK
    ws = pl.multiple_of(jnp.maximum(t0 - WINDOW, 0), Q_BLOCK)
    dlt = tcol - (lax.broadcasted_iota(jnp.int32, (Q_BLOCK, span), 1) + ws)
    bias_w = jnp.where(dlt >= 0, jnp.where(dlt < WINDOW, 0.0, NEG), NEG)
    s_w = _dot(qs, kwb[:, pl.ds(ws, span)]) + _tile_heads(bias_w)
    e_w = jnp.exp(s_w - jnp.max(s_w, axis=-1, keepdims=True)).astype(BF16)
    acc_w = _dot_nt(e_w, vwb[:, pl.ds(ws, span)])
    o_w = acc_w[:, 0:HEAD_DIM] / acc_w[:, HEAD_DIM:HEAD_DIM + 1]

    gs = _sigmoid(gate_ref[...]).T
    gsel = jnp.where(g == 0, gs[:, 0:HPG * N_BRANCH], gs[:, HPG * N_BRANCH:2 * HPG * N_BRANCH])
    for h in range(HPG):
        r0, r1 = h * Q_BLOCK, (h + 1) * Q_BLOCK
        c0 = h * N_BRANCH
        o_ref[:, h * HEAD_DIM:(h + 1) * HEAD_DIM] = (
            o_c[r0:r1] * gsel[:, c0:c0 + 1] + o_s[r0:r1] * gsel[:, c0 + 1:c0 + 2]
            + o_w[r0:r1] * gsel[:, c0 + 2:c0 + 3])


def nsa_prompt(p_main, kvt, cmp_t, ovt, expand):
    batch, _, seq = kvt.shape
    nqb = seq // Q_BLOCK
    gw = HPG * HEAD_DIM
    n_cmp_pad = seq // CMP_STRIDE
    n_sel = seq // SEL_BLOCK
    kvt_spec = lambda row: pl.BlockSpec((None, HEAD_DIM, seq),
                                        lambda b, g, i: (b, row // HEAD_DIM + g, 0))
    return pl.pallas_call(
        functools.partial(_nsa_prompt_kernel, seq=seq),
        grid=(batch, N_KV, nqb),
        in_specs=[
            pl.BlockSpec((Q_BLOCK, gw), lambda b, g, i: (b * nqb + i, g)),
            pl.BlockSpec((None, 128, Q_BLOCK), lambda b, g, i: (b, ROW_G // 128, i)),
            pl.BlockSpec((None, 2, None, n_cmp_pad, HEAD_DIM), lambda b, g, i: (b, 0, g, 0, 0)),
            kvt_spec(ROW_KS), kvt_spec(ROW_KS + KV_HALF),
            kvt_spec(ROW_KW), kvt_spec(ROW_KW + KV_HALF),
            pl.BlockSpec((n_sel, n_cmp_pad), lambda b, g, i: (0, 0)),
            pl.BlockSpec((128, seq), lambda b, g, i: (0, 0)),
        ],
        out_specs=pl.BlockSpec((Q_BLOCK, gw), lambda b, g, i: (b * nqb + i, g)),
        out_shape=jax.ShapeDtypeStruct((batch * seq, D_NSA), F32),
        scratch_shapes=[pltpu.VMEM((HEAD_DIM, seq), BF16), pltpu.VMEM((HEAD_DIM + V_PAD, seq), BF16)] * 2,
        compiler_params=_cparams(("arbitrary", "arbitrary", "arbitrary")),
        name="nsa_prompt",
    )(p_main, kvt, cmp_t, kvt, kvt, kvt, kvt, ovt, expand)


def _pick_group(o2, rg):
    return jnp.where(rg == 0, o2[:, 0:HEAD_DIM], o2[:, HEAD_DIM:2 * HEAD_DIM])


def _masked_softmax(s, mask):
    s = jnp.where(mask, s, NEG)
    m = jnp.max(s, axis=-1, keepdims=True)
    e = jnp.where(mask, jnp.exp(s - m), 0.0)
    return e / jnp.maximum(jnp.sum(e, axis=-1, keepdims=True), 1e-30)


def _masked_softmax2(s_a, mask_a, s_b, mask_b):
    s_a = jnp.where(mask_a, s_a, NEG)
    s_b = jnp.where(mask_b, s_b, NEG)
    m = jnp.maximum(jnp.max(s_a, axis=-1, keepdims=True), jnp.max(s_b, axis=-1, keepdims=True))
    e_a = jnp.where(mask_a, jnp.exp(s_a - m), 0.0)
    e_b = jnp.where(mask_b, jnp.exp(s_b - m), 0.0)
    tot = jnp.sum(e_a, axis=-1, keepdims=True) + jnp.sum(e_b, axis=-1, keepdims=True)
    inv = 1.0 / jnp.maximum(tot, 1e-30)
    return (e_a * inv).astype(BF16), (e_b * inv).astype(BF16)


def _nsa_sample_kernel(pt_ref, *refs, bt, npages, dec_seq):
    page_refs = refs[:bt * npages]
    rest = list(refs[bt * npages:])
    n_prev = rest[-5].shape[0] - 1
    prev_ref = rest.pop(5) if n_prev else None
    (q_ref, gate_ref, cmp_ref, new_ref, win_ref, ov_ref, exp_ref,
     o_ref, wout_ref, kts, vts, ktw, vtw) = rest
    past = npages * PAGE
    n_cmp = (past + dec_seq - CMP_BLOCK) // CMP_STRIDE + 1
    n_sel = -(-(past + dec_seq) // SEL_BLOCK)
    n_rows = HPG * N_KV * dec_seq
    win_buf = win_ref.shape[-1]
    n_cmp_pad = cmp_ref.shape[1]
    n_new = new_ref.shape[1]
    gt = N_KV * dec_seq

    rowi = lax.broadcasted_iota(jnp.int32, (n_rows, 1), 0)
    rg = (rowi // dec_seq) % N_KV
    tpos = past + rowi % dec_seq
    ncol = lax.broadcasted_iota(jnp.int32, (n_rows, n_cmp_pad), 1)
    m_c = (ncol * CMP_STRIDE + (CMP_BLOCK - 1) <= tpos) & (ncol < n_cmp)
    jj = lax.broadcasted_iota(jnp.int32, (gt, 128), 1)
    t8 = past + lax.broadcasted_iota(jnp.int32, (gt, 128), 0) % dec_seq
    d_new = tpos - (past + lax.broadcasted_iota(jnp.int32, (n_rows, n_new), 1))
    m_new = d_new >= 0
    m_wn = m_new & (d_new < WINDOW)
    wcol = lax.broadcasted_iota(jnp.int32, (n_rows, win_buf), 1)
    dlt = tpos - (wcol + (past - win_buf))
    m_w = (dlt >= 0) & (dlt < WINDOW)
    lane = lax.broadcasted_iota(jnp.int32, (KV_W, PAGE), 1)

    bbs = range(bt)

    for bb in bbs:
        for p in range(npages):
            page = page_refs[bb * npages + p]
            kts[bb, :, p * PAGE:(p + 1) * PAGE] = page[0].reshape(KV_HALF, PAGE).astype(BF16)
            vts[bb, :, p * PAGE:(p + 1) * PAGE] = page[1].reshape(KV_HALF, PAGE).astype(BF16)
        win = win_ref[bb].reshape(KV_W, win_buf)
        ktw[bb] = win[0:KV_HALF].astype(BF16)
        vtw[bb] = win[KV_HALF:KV_W].astype(BF16)
        kwn = new_ref[bb, :, ROW_KW:ROW_KW + KV_W]
        kwn_t = jnp.concatenate([kwn, jnp.zeros((PAGE - n_new, KV_W), F32)], axis=0).T
        shifted = pltpu.roll(win, win_buf - dec_seq, 1)
        tail = jnp.where(lane >= PAGE - dec_seq, pltpu.roll(kwn_t, PAGE - dec_seq, 1),
                         shifted[:, win_buf - PAGE:])
        wout_ref[n_prev, bb] = jnp.concatenate([shifted[:, 0:win_buf - PAGE], tail], axis=1).reshape(
            2, N_KV, HEAD_DIM, win_buf)
        for lp in range(n_prev):
            wout_ref[lp, bb] = prev_ref[lp, bb]

    qs = [q_ref[bb].astype(BF16) for bb in bbs]
    new = [new_ref[bb].astype(BF16) for bb in bbs]
    kn_s = [x[:, ROW_KS:ROW_KS + KV_HALF] for x in new]
    vn_s = [x[:, ROW_KS + KV_HALF:ROW_KS + KV_W] for x in new]
    kn_w = [x[:, ROW_KW:ROW_KW + KV_HALF] for x in new]
    vn_w = [x[:, ROW_KW + KV_HALF:ROW_KW + KV_W] for x in new]

    s_c = [_dot_nt(qs[bb], cmp_ref[bb, :, 0:KV_HALF]) for bb in bbs]
    s_w = [_dot(qs[bb], ktw[bb]) for bb in bbs]
    s_wn = [_dot_nt(qs[bb], kn_w[bb]) for bb in bbs]
    p_c = [_masked_softmax(s, m_c).astype(BF16) for s in s_c]
    p_w = [_masked_softmax2(s_w[bb], m_w, s_wn[bb], m_wn) for bb in bbs]
    o_c = [_pick_group(_dot(p_c[bb], cmp_ref[bb, :, KV_HALF:KV_W]), rg) for bb in bbs]
    o_w = [_pick_group(_dot_nt(p_w[bb][0], vtw[bb]) + _dot(p_w[bb][1], vn_w[bb]), rg) for bb in bbs]

    po = [_dot(p, ov_ref[...]) for p in p_c]
    imp = [functools.reduce(lambda a, h: a + x[h * gt:(h + 1) * gt], range(1, HPG), x[0:gt]) for x in po]
    sel8 = [_topk_lanes(_importance(x, jj, t8, n_sel), jj, n_sel) for x in imp]
    msel = [_dot(jnp.concatenate([x] * HPG, axis=0).astype(BF16), exp_ref[...]) for x in sel8]

    s_s = [_dot(qs[bb], kts[bb]) for bb in bbs]
    s_sn = [_dot_nt(qs[bb], kn_s[bb]) for bb in bbs]
    p_s = [_masked_softmax2(s_s[bb], msel[bb][:, 0:past] > 0.5,
                            s_sn[bb], (msel[bb][:, past:past + n_new] > 0.5) & m_new) for bb in bbs]
    o_s = [_pick_group(_dot_nt(p_s[bb][0], vts[bb]) + _dot(p_s[bb][1], vn_s[bb]), rg) for bb in bbs]

    for bb in bbs:
        gs = _sigmoid(gate_ref[bb])
        o_ref[bb] = o_c[bb] * gs[:, 0:1] + o_s[bb] * gs[:, 1:2] + o_w[bb] * gs[:, 2:3]


def nsa_sample(cache_sel, page_idx, qbd, gates, cmp_s, kv_new, cache_win, layer, prev_win, ov_s, expand, bt):
    nb, npages = page_idx.shape
    n_prev = 0 if prev_win is None else prev_win.shape[0]
    n_rows = qbd.shape[1]
    dec_seq = n_rows // (HPG * N_KV)
    win_buf = cache_win.shape[-1]
    n_cmp_pad = cmp_s.shape[1]
    n_new = kv_new.shape[1]
    past = npages * PAGE
    page_specs = [
        pl.BlockSpec((None, 2, N_KV, HEAD_DIM, PAGE), functools.partial(
            lambda i, pt, bb, p: (pt[i * bt + bb, p], 0, 0, 0, 0), bb=bb, p=p))
        for bb in range(bt) for p in range(npages)]
    per_b = lambda i, pt: (i, 0, 0)
    const2 = lambda i, pt: (0, 0)
    win_shape = (2, N_KV, HEAD_DIM, win_buf)
    grid_spec = pltpu.PrefetchScalarGridSpec(
        num_scalar_prefetch=1,
        grid=(nb // bt,),
        in_specs=page_specs + [
            pl.BlockSpec((bt, n_rows, KV_HALF), per_b),
            pl.BlockSpec((bt, n_rows, N_BRANCH), per_b),
            pl.BlockSpec((bt, n_cmp_pad, KV_W), per_b),
            pl.BlockSpec((bt, n_new, D_KVT), per_b),
            pl.BlockSpec((None, bt) + win_shape, lambda i, pt: (layer, i, 0, 0, 0, 0)),
        ] + ([pl.BlockSpec((n_prev, bt) + win_shape, lambda i, pt: (0, i, 0, 0, 0, 0))] if n_prev else []) + [
            pl.BlockSpec((n_cmp_pad, 128), const2),
            pl.BlockSpec((128, past + PAGE), const2),
        ],
        out_specs=[pl.BlockSpec((bt, n_rows, HEAD_DIM), per_b),
                   pl.BlockSpec((n_prev + 1, bt) + win_shape, lambda i, pt: (0, i, 0, 0, 0, 0))],
        scratch_shapes=[pltpu.VMEM((bt, KV_HALF, past), BF16)] * 2
        + [pltpu.VMEM((bt, KV_HALF, win_buf), BF16)] * 2)
    return pl.pallas_call(
        functools.partial(_nsa_sample_kernel, bt=bt, npages=npages, dec_seq=dec_seq),
        grid_spec=grid_spec,
        out_shape=[jax.ShapeDtypeStruct((nb, n_rows, HEAD_DIM), F32),
                   jax.ShapeDtypeStruct((n_prev + 1, nb) + win_shape, F32)],
        compiler_params=_cparams(("arbitrary",)),
        name="nsa_sample",
    )(page_idx, *([cache_sel] * (bt * npages)), qbd, gates, cmp_s, kv_new, cache_win,
      *([prev_win] if n_prev else []), ov_s, expand)


def _shift_rows(u, p1, p2):
    r = lax.broadcasted_iota(jnp.int32, (u.shape[0], 1), 0)
    u1 = jnp.where(r >= 1, pltpu.roll(u, 1, 0), p1)
    u2 = jnp.where(r >= 2, pltpu.roll(u, 2, 0), jnp.where(r == 1, p1, p2))
    return u1, u2


def _mix_project(o_nsa, o_conv, gn, gc, w, x):
    mix = jnp.concatenate([_rms(o_nsa, gn), _rms(o_conv, gc)], axis=-1).astype(BF16)
    return x + _dot(mix, w)


def _mixout_prompt_kernel(on_ref, cb_ref, cc_ref, ch_ref, cch_ref, chh_ref, cw_ref, gn_ref, gc_ref,
                          w_ref, x_ref, gm_ref, wq_ref, kv_ref, wo_ref, o_ref, ut_ref, *, tiles_per_seq):
    first = (pl.program_id(0) % tiles_per_seq) == 0
    u = cc_ref[...] * ch_ref[...]
    uh = jnp.where(first, 0.0, cch_ref[...] * chh_ref[...])
    u1, u2 = _shift_rows(u, uh[7:8], uh[6:7])
    cw = cw_ref[...]
    v = cw[0:1] * u2 + cw[1:2] * u1 + cw[2:3] * u
    x1 = _mix_project(on_ref[...], cb_ref[...] * v, gn_ref[...], gc_ref[...], w_ref[...], x_ref[...])
    ut_ref[...] = u[u.shape[0] - 8:, :]
    o_ref[...] = _mem_attend(x1, gm_ref[...], wq_ref[...], kv_ref[...].astype(BF16), wo_ref[...])


def mixout_prompt(o_nsa, p_main, cw, gn, gc, w, x, g_mem, wq, mem_kv, wo, tm, seq, layer):
    m = x.shape[0]
    nt = m // tm
    tps = seq // tm
    mem_len = mem_kv.shape[1]
    cblk = lambda c: (lambda i: (i, c // D_CONV))
    halo = lambda c: (lambda i: (jnp.maximum(i * (tm // 8) - 1, 0), c // D_CONV))
    const = lambda i: (0, 0)
    return pl.pallas_call(
        functools.partial(_mixout_prompt_kernel, tiles_per_seq=tps),
        grid=(nt,),
        in_specs=[pl.BlockSpec((tm, D_NSA), lambda i: (i, 0)),
                  pl.BlockSpec((tm, D_CONV), cblk(COL_CB)),
                  pl.BlockSpec((tm, D_CONV), cblk(COL_CC)),
                  pl.BlockSpec((tm, D_CONV), cblk(COL_CH)),
                  pl.BlockSpec((8, D_CONV), halo(COL_CC)),
                  pl.BlockSpec((8, D_CONV), halo(COL_CH)),
                  pl.BlockSpec((8, D_CONV), const),
                  pl.BlockSpec((1, D_NSA), const),
                  pl.BlockSpec((1, D_CONV), const),
                  _wspec((D_MODEL, D_MODEL), const, layer, resident=True),
                  pl.BlockSpec((tm, D_MODEL), lambda i: (i, 0)),
                  pl.BlockSpec((1, D_MODEL), const),
                  _wspec((D_MODEL, D_MEM), const, layer, resident=True),
                  pl.BlockSpec((None, mem_len, 2 * D_MEM), lambda i: (i // tps, 0, 0)),
                  _wspec((D_MEM, D_MODEL), const, layer, resident=True)],
        out_specs=[pl.BlockSpec((tm, D_MODEL), lambda i: (i, 0)),
                   pl.BlockSpec((None, 8, D_CONV), lambda i: (i, 0, 0))],
        out_shape=[jax.ShapeDtypeStruct((m, D_MODEL), F32),
                   jax.ShapeDtypeStruct((nt, 8, D_CONV), F32)],
        compiler_params=_cparams(("arbitrary",)),
        name="mixout_prompt",
    )(o_nsa, p_main, p_main, p_main, p_main, p_main, cw, gn, gc, w, x, g_mem, wq, mem_kv, wo)


def _mixout_sample_kernel(on_ref, cb_ref, cc_ref, ch_ref, st_ref, cw_ref, gn_ref, gc_ref,
                          w_ref, x_ref, o_ref, nst_ref, *, dec_seq):
    nb = x_ref.shape[1]
    ucat = [st_ref[0], st_ref[1]] + [cc_ref[t] * ch_ref[t] for t in range(dec_seq)]
    cw = cw_ref[...]
    mix = []
    for t in range(dec_seq):
        v = cw[0:1] * ucat[t] + cw[1:2] * ucat[t + 1] + cw[2:3] * ucat[t + 2]
        mix.append(jnp.concatenate([_rms(on_ref[t], gn_ref[...]), _rms(cb_ref[t] * v, gc_ref[...])],
                                   axis=-1).astype(BF16))
    proj = _dot(jnp.concatenate(mix, axis=0), w_ref[...])
    for t in range(dec_seq):
        o_ref[t] = x_ref[t] + proj[t * nb:(t + 1) * nb]
    nst_ref[0] = ucat[dec_seq]
    nst_ref[1] = ucat[dec_seq + 1]


def mixout_sample(o_nsa, p_s, state, cw, gn, gc, w, x, layer):
    dec_seq, nb, _ = x.shape
    cblk = lambda c: (lambda i: (0, 0, c // D_CONV))
    full3 = lambda i: (0, 0, 0)
    const = lambda i: (0, 0)
    return pl.pallas_call(
        functools.partial(_mixout_sample_kernel, dec_seq=dec_seq),
        grid=(1,),
        in_specs=[pl.BlockSpec((dec_seq, nb, D_NSA), full3),
                  pl.BlockSpec((dec_seq, nb, D_CONV), cblk(COL_CB)),
                  pl.BlockSpec((dec_seq, nb, D_CONV), cblk(COL_CC)),
                  pl.BlockSpec((dec_seq, nb, D_CONV), cblk(COL_CH)),
                  pl.BlockSpec((2, nb, D_CONV), full3),
                  pl.BlockSpec((8, D_CONV), const),
                  pl.BlockSpec((1, D_NSA), const),
                  pl.BlockSpec((1, D_CONV), const),
                  _wspec((D_MODEL, D_MODEL), const, layer),
                  pl.BlockSpec((dec_seq, nb, D_MODEL), full3)],
        out_specs=[pl.BlockSpec((dec_seq, nb, D_MODEL), full3),
                   pl.BlockSpec((2, nb, D_CONV), full3)],
        out_shape=[jax.ShapeDtypeStruct((dec_seq, nb, D_MODEL), F32),
                   jax.ShapeDtypeStruct((2, nb, D_CONV), F32)],
        compiler_params=_cparams(("arbitrary",)),
        name="mixout_sample",
    )(o_nsa, p_s, p_s, p_s, state, cw, gn, gc, w, x)


MEM_SCALE = MEM_HEAD_DIM ** -0.5


def _softmax(s):
    e = jnp.exp(s - jnp.max(s, axis=-1, keepdims=True))
    return e / jnp.sum(e, axis=-1, keepdims=True)


def _mem_attend(x, g, wq, kv, wo):
    qm = _dot(_rms(x, g).astype(BF16), wq).astype(BF16)
    outs = []
    for h in range(MEM_HEADS):
        c0, c1 = h * MEM_HEAD_DIM, (h + 1) * MEM_HEAD_DIM
        p = _softmax(_dot_nt(qm[:, c0:c1], kv[:, c0:c1]) * MEM_SCALE).astype(BF16)
        outs.append(_dot(p, kv[:, D_MEM + c0:D_MEM + c1]))
    return x + _dot(jnp.concatenate(outs, axis=-1).astype(BF16), wo)


def _mem_sample_kernel(q_ref, kv_ref, o_ref, *, bt, mem_len):
    n_rows = q_ref.shape[1]
    stride = 2 * MEM_HEADS
    rowh = lax.broadcasted_iota(jnp.int32, (n_rows, 1), 0) // (n_rows // MEM_HEADS)
    for bb in range(bt):
        kmat = jnp.concatenate([kv_ref[bb, pl.ds(h, mem_len, stride=stride), :]
                                for h in range(MEM_HEADS)], axis=-1).astype(BF16)
        vmat = jnp.concatenate([kv_ref[bb, pl.ds(MEM_HEADS + h, mem_len, stride=stride), :]
                                for h in range(MEM_HEADS)], axis=-1).astype(BF16)
        p = _softmax(_dot_nt(q_ref[bb].astype(BF16), kmat) * MEM_SCALE).astype(BF16)
        o2 = _dot(p, vmat)
        out = jnp.zeros((n_rows, MEM_HEAD_DIM), F32)
        for h in range(MEM_HEADS):
            out = out + jnp.where(rowh == h, o2[:, h * MEM_HEAD_DIM:(h + 1) * MEM_HEAD_DIM], 0.0)
        o_ref[bb] = out


def mem_sample_attn(qbd, cache_mem, layer, bt):
    nb, n_rows, _ = qbd.shape
    rows = cache_mem.shape[2]
    return pl.pallas_call(
        functools.partial(_mem_sample_kernel, bt=bt, mem_len=rows // (2 * MEM_HEADS)),
        grid=(nb // bt,),
        in_specs=[pl.BlockSpec((bt, n_rows, D_MEM), lambda i: (i, 0, 0)),
                  pl.BlockSpec((None, bt, rows, MEM_HEAD_DIM), lambda i: (layer, i, 0, 0))],
        out_specs=pl.BlockSpec((bt, n_rows, MEM_HEAD_DIM), lambda i: (i, 0, 0)),
        out_shape=jax.ShapeDtypeStruct((nb, n_rows, MEM_HEAD_DIM), F32),
        compiler_params=_cparams(("arbitrary",)),
        name="mem_sample_attn",
    )(qbd, cache_mem)


def _silu(a):
    return a * _sigmoid(a)


def _ffn_prompt_kernel(x_ref, xh_ref, g_ref, wg_ref, wu_ref, cw_ref, wd_ref, gf_ref,
                       o_ref, gt_ref, h_s, hh_s, *, tiles_per_seq, final):
    j = pl.program_id(1)
    tm = x_ref.shape[0]

    @pl.when(j == 0)
    def _():
        first = (pl.program_id(0) % tiles_per_seq) == 0
        x = x_ref[...]
        h_s[...] = _rms(x, g_ref[...]).astype(BF16)
        hh = jnp.where(first, 0.0, _rms(xh_ref[...], g_ref[...]))
        hh_s[...] = jnp.concatenate([hh, jnp.zeros_like(hh)], axis=0).astype(BF16)
        o_ref[...] = x

    h = h_s[...]
    gate = _dot(h, wg_ref[...])
    gate_h = _dot(hh_s[...], wg_ref[...])
    g1, g2 = _shift_rows(gate, gate_h[7:8], gate_h[6:7])
    cw = cw_ref[...]
    a = cw[0:1] * g2 + cw[1:2] * g1 + cw[2:3] * gate
    z = (_silu(a) * _dot(h, wu_ref[...])).astype(BF16)
    o_ref[...] += _dot(z, wd_ref[...])
    gt_ref[...] = gate[tm - 8:, :]

    if final:
        @pl.when(j == pl.num_programs(1) - 1)
        def _():
            o_ref[...] = _rms(o_ref[...], gf_ref[...])


def ffn_prompt(x, g, wg, wu, cw, wd, gf, tm, tn, seq, final, layer):
    m = x.shape[0]
    nt = m // tm
    nj = D_FF // tn
    const = lambda i, j: (0, 0)
    return pl.pallas_call(
        functools.partial(_ffn_prompt_kernel, tiles_per_seq=seq // tm, final=final),
        grid=(nt, nj),
        in_specs=[pl.BlockSpec((tm, D_MODEL), lambda i, j: (i, 0)),
                  pl.BlockSpec((8, D_MODEL), lambda i, j: (jnp.maximum(i * (tm // 8) - 1, 0), 0)),
                  pl.BlockSpec((1, D_MODEL), const),
                  _wspec((D_MODEL, tn), lambda i, j: (0, j), layer),
                  _wspec((D_MODEL, tn), lambda i, j: (0, j), layer),
                  pl.BlockSpec((8, tn), lambda i, j: (0, j)),
                  _wspec((tn, D_MODEL), lambda i, j: (j, 0), layer),
                  pl.BlockSpec((1, D_MODEL), const)],
        out_specs=[pl.BlockSpec((tm, D_MODEL), lambda i, j: (i, 0)),
                   pl.BlockSpec((None, 8, tn), lambda i, j: (i, 0, j))],
        out_shape=[jax.ShapeDtypeStruct((m, D_MODEL), F32),
                   jax.ShapeDtypeStruct((nt, 8, D_FF), F32)],
        scratch_shapes=[pltpu.VMEM((tm, D_MODEL), BF16),
                        pltpu.VMEM((16, D_MODEL), BF16)],
        compiler_params=_cparams(("arbitrary", "arbitrary")),
        name="ffn_prompt",
    )(x, x, g, wg, wu, cw, wd, gf)


def _ffn_sample_kernel(x_ref, st_ref, g_ref, wg_ref, wu_ref, cw_ref, wd_ref, gf_ref,
                       o_ref, nst_ref, h_s, acc_s, *, dec_seq, final):
    j = pl.program_id(0)

    nb = x_ref.shape[1]

    @pl.when(j == 0)
    def _():
        for t in range(dec_seq):
            h_s[t * nb:(t + 1) * nb, :] = _rms(x_ref[t], g_ref[...]).astype(BF16)
        acc_s[...] = jnp.zeros_like(acc_s)

    h = h_s[...]
    gate = _dot(h, wg_ref[...])
    up = _dot(h, wu_ref[...])
    gcat = [st_ref[0], st_ref[1]] + [gate[t * nb:(t + 1) * nb] for t in range(dec_seq)]
    cw = cw_ref[...]
    a = jnp.concatenate([cw[0:1] * gcat[t] + cw[1:2] * gcat[t + 1] + cw[2:3] * gcat[t + 2]
                         for t in range(dec_seq)], axis=0)
    acc_s[...] += _dot((_silu(a) * up).astype(BF16), wd_ref[...])
    nst_ref[0] = gcat[dec_seq]
    nst_ref[1] = gcat[dec_seq + 1]

    @pl.when(j == pl.num_programs(0) - 1)
    def _():
        for t in range(dec_seq):
            y = x_ref[t] + acc_s[t * nb:(t + 1) * nb, :]
            o_ref[t] = _rms(y, gf_ref[...]) if final else y


def ffn_sample(x, state, g, wg, wu, cw, wd, gf, tn, final, layer):
    dec_seq, nb, _ = x.shape
    full3 = lambda j: (0, 0, 0)
    const = lambda j: (0, 0)
    return pl.pallas_call(
        functools.partial(_ffn_sample_kernel, dec_seq=dec_seq, final=final),
        grid=(D_FF // tn,),
        in_specs=[pl.BlockSpec((dec_seq, nb, D_MODEL), full3),
                  pl.BlockSpec((2, nb, tn), lambda j: (0, 0, j)),
                  pl.BlockSpec((1, D_MODEL), const),
                  _wspec((D_MODEL, tn), lambda j: (0, j), layer),
                  _wspec((D_MODEL, tn), lambda j: (0, j), layer),
                  pl.BlockSpec((8, tn), lambda j: (0, j)),
                  _wspec((tn, D_MODEL), lambda j: (j, 0), layer),
                  pl.BlockSpec((1, D_MODEL), const)],
        out_specs=[pl.BlockSpec((dec_seq, nb, D_MODEL), full3),
                   pl.BlockSpec((2, nb, tn), lambda j: (0, 0, j))],
        out_shape=[jax.ShapeDtypeStruct((dec_seq, nb, D_MODEL), F32),
                   jax.ShapeDtypeStruct((2, nb, D_FF), F32)],
        scratch_shapes=[pltpu.VMEM((dec_seq * nb, D_MODEL), BF16),
                        pltpu.VMEM((dec_seq * nb, D_MODEL), F32)],
        compiler_params=_cparams(("arbitrary",)),
        name="ffn_sample",
    )(x, state, g, wg, wu, cw, wd, gf)


def _overlap(n_cmp, n_sel, rows, cols):
    c0 = np.arange(n_cmp)[:, None] * CMP_STRIDE
    s0 = np.arange(n_sel)[None, :] * SEL_BLOCK
    ov = np.minimum(c0 + CMP_BLOCK, s0 + SEL_BLOCK) - np.maximum(c0, s0)
    out = np.zeros((rows, cols), np.float32)
    out[:n_cmp, :n_sel] = np.clip(ov, 0, None).astype(np.float32) / CMP_BLOCK
    return out


def _block_expand(n_keys):
    return jnp.asarray(np.arange(128)[:, None] == (np.arange(n_keys)[None, :] // SEL_BLOCK), BF16)


def _chunk_perm():
    row = np.arange(PAGE)
    return jnp.asarray((row[:, None] % RPC) * CMP_STRIDE + row[:, None] // RPC == row[None, :], BF16)


def _pad_rows(a, rows):
    return jnp.concatenate([a, jnp.zeros((rows - a.shape[0],) + a.shape[1:], a.dtype)], axis=0)


def _prep_w_in(w_in_l):
    offs = np.cumsum((D_NSA, KV_W, KV_W, KV_W, N_BRANCH * N_HEADS, D_CONV, D_CONV, D_CONV))
    q, kc, ks, kw, gl, cb, cc, ch = jnp.split(w_in_l.T, [int(o) for o in offs[:-1]], axis=0)
    pad = jnp.zeros((D_KVT - ROW_G - N_BRANCH * N_HEADS, D_MODEL), F32)
    w_main = jnp.concatenate([q, cb, cc, ch], axis=0).astype(BF16)
    wt_kv = jnp.concatenate([kc, ks, kw, gl, pad], axis=0).astype(BF16)
    return w_main[None], wt_kv[None]


def _prep_cmp(w_cmp_l, pe_cmp_l):
    blocks = [jnp.pad(w_cmp_l[c], ((0, 0), (0, 0), ((c * N_KV + g) * HEAD_DIM,
                                                     KV_W - (c * N_KV + g + 1) * HEAD_DIM)))
              for c in range(2) for g in range(N_KV)]
    wbd = jnp.concatenate(blocks, axis=1).astype(BF16)
    pe = pe_cmp_l.reshape(2, 2, CMP_STRIDE, HEAD_DIM).transpose(1, 0, 3, 2)
    pet = jnp.broadcast_to(pe[:, :, None, :, None, :], (2, 2, N_KV, HEAD_DIM, RPC, CMP_STRIDE))
    return wbd, pet.reshape(2, KV_W, PAGE)


TM_PROJ = 1024
TN_PROJ = 1024
TM_MIX = 512
TM_FFN = 1024
TN_FFN = 512
BT_CMP = 4
BT_NSA = 4
BT_MEM = 8
NEW_PAD = 16


def _kv_rows(kvt, row, start=0):
    slab = kvt[:, row:row + KV_W, start:]
    b, _, t = slab.shape
    return slab.reshape(b, 2, N_KV, HEAD_DIM, t).transpose(0, 4, 1, 2, 3)


def _layer_prompt(x, mem_kv, lw, consts, layer, batch, seq, final):
    p_main, kvt = proj_in(x, lw['g_mix'], lw['w_main'], lw['wt_kv'], TM_PROJ, TN_PROJ, seq, 0)
    cmp = compress_prompt(kvt, lw['pet'], consts['perm'], lw['wbd'])
    n_cmp_pad = seq // CMP_STRIDE
    cmp_t = cmp.reshape(batch, n_cmp_pad, 2, N_KV, HEAD_DIM).transpose(0, 2, 3, 1, 4)
    o_nsa = nsa_prompt(p_main, kvt, cmp_t, consts['ovt_p'], consts['expand_p'])
    x2, u_tail = mixout_prompt(o_nsa, p_main, lw['conv_w'], lw['g_out_nsa'], lw['g_out_conv'],
                               lw['w_out'], x, lw['g_mem'], lw['w_mem_q'], mem_kv, lw['w_mem_o'],
                               TM_MIX, seq, layer)
    x3, g_tail = ffn_prompt(x2, lw['g_ffn'], lw['w_ff_gate'], lw['w_ff_up'], lw['ffn_conv_w'],
                            lw['w_ff_down'], lw['g_final'], TM_FFN, TN_FFN, seq, final, layer)
    conv_state = u_tail.reshape(batch, seq // TM_MIX, 8, D_CONV)[:, -1, 6:8]
    ffn_state = g_tail.reshape(batch, seq // TM_FFN, 8, D_FF)[:, -1, 6:8]
    return (x3, _kv_rows(kvt, ROW_KC), _kv_rows(kvt, ROW_KS),
            _kv_rows(kvt, ROW_KW, seq - min(WINDOW, seq)), conv_state, ffn_state)


def _layer_sample(x, lw, consts, layer, cache_cmp, cache_sel, page_idx, cache_win, prev_win, cache_mem,
                  st_conv, st_ffn, final):
    dec_seq, nb, _ = x.shape
    rows = dec_seq * nb
    p_s, kvt = proj_in(x.reshape(rows, D_MODEL), lw['g_mix'], lw['w_main'], lw['wt_kv'],
                       rows, TN_PROJ, rows, 0)
    p3 = p_s.reshape(dec_seq, nb, D_MAIN)
    kvn = kvt.reshape(D_KVT, rows).T.reshape(dec_seq, nb, D_KVT).transpose(1, 0, 2)
    kvn_pad = jnp.concatenate([kvn, jnp.zeros((nb, NEW_PAD - dec_seq, D_KVT), F32)], axis=1)
    cmp_s = compress_paged(cache_cmp, page_idx, lw['pet'], consts['perm'], lw['wbd'], BT_CMP)
    q = p3[:, :, COL_Q:COL_Q + D_NSA].reshape(dec_seq, nb, N_KV, HPG, HEAD_DIM).transpose(1, 3, 2, 0, 4)
    gsel = jnp.eye(N_KV, dtype=F32) * SCORE_SCALE
    qbd = (q[:, :, :, :, None, :] * gsel[None, None, :, None, :, None]).reshape(
        nb, HPG * N_KV * dec_seq, KV_HALF)
    gates = kvn[:, :, ROW_G:ROW_G + N_BRANCH * N_HEADS].reshape(
        nb, dec_seq, N_KV, HPG, N_BRANCH).transpose(0, 3, 2, 1, 4).reshape(nb, HPG * N_KV * dec_seq, N_BRANCH)
    o, win_stack = nsa_sample(cache_sel, page_idx, qbd, gates, cmp_s, kvn_pad, cache_win, layer, prev_win,
                              consts['ov_s'], consts['expand_s'], BT_NSA)
    o_nsa = o.reshape(nb, HPG, N_KV, dec_seq, HEAD_DIM).transpose(3, 0, 2, 1, 4).reshape(dec_seq, nb, D_NSA)
    x1, conv_state = mixout_sample(o_nsa, p3, st_conv, lw['conv_w'], lw['g_out_nsa'],
                                   lw['g_out_conv'], lw['w_out'], x, layer)
    qm = rms_matmul(x1.reshape(rows, D_MODEL), lw['g_mem'], lw['w_mem_q'], rows, D_MEM, layer)
    qh = qm.reshape(dec_seq, nb, MEM_HEADS, MEM_HEAD_DIM).transpose(1, 2, 0, 3)
    qmbd = (qh[:, :, :, None, :] * jnp.eye(MEM_HEADS, dtype=F32)[None, :, None, :, None]).reshape(
        nb, MEM_HEADS * dec_seq, D_MEM)
    om = mem_sample_attn(qmbd, cache_mem, layer, BT_MEM)
    om = om.reshape(nb, MEM_HEADS, dec_seq, MEM_HEAD_DIM).transpose(2, 0, 1, 3).reshape(rows, D_MEM)
    x2 = matmul_residual(x1.reshape(rows, D_MODEL), om, lw['w_mem_o'], rows, layer).reshape(
        dec_seq, nb, D_MODEL)
    x3, ffn_state = ffn_sample(x2, st_ffn, lw['g_ffn'], lw['w_ff_gate'], lw['w_ff_up'],
                               lw['ffn_conv_w'], lw['w_ff_down'], lw['g_final'], TN_FFN, final, layer)
    kv_c = kvn[:, :, ROW_KC:ROW_KC + KV_W].reshape(nb, dec_seq, 2, N_KV, HEAD_DIM)
    kv_s = kvn[:, :, ROW_KS:ROW_KS + KV_W].reshape(nb, dec_seq, 2, N_KV, HEAD_DIM)
    return (x3, kv_c, kv_s, win_stack, conv_state.transpose(1, 0, 2), ffn_state.transpose(1, 0, 2))


def kernel(x_prompt, x_sample, cache_cmp_kv, cache_sel_kv, cache_win_kv, cache_mem_kv,
           state_conv, state_ffn_conv, page_table, mem_prompt,
           g_mix, w_in, w_cmp, pe_cmp, conv_w, g_out_nsa, g_out_conv, w_out,
           g_mem_src, w_mem_kv, g_mem, w_mem_q, w_mem_o, g_ffn, w_ff_gate, w_ff_up,
           ffn_conv_w, w_ff_down, g_final):
    batch, seq, _ = x_prompt.shape
    nb, dec_seq, _ = x_sample.shape
    depth = w_in.shape[0]
    n_phys = cache_cmp_kv.shape[1]
    npages = page_table.shape[1]
    past = npages * PAGE
    win_buf = cache_win_kv.shape[2]
    mem_len = mem_prompt.shape[1]

    n_cmp_s = (past + dec_seq - CMP_BLOCK) // CMP_STRIDE + 1
    n_sel_s = -(-(past + dec_seq) // SEL_BLOCK)
    n_cmp_p = (seq - CMP_BLOCK) // CMP_STRIDE + 1
    n_sel_p = seq // SEL_BLOCK
    consts = {
        'ov_s': jnp.asarray(_overlap(n_cmp_s, n_sel_s, past // CMP_STRIDE, 128), BF16),
        'expand_s': _block_expand(past + PAGE),
        'ovt_p': jnp.asarray(_overlap(n_cmp_p, n_sel_p, seq // CMP_STRIDE, n_sel_p).T, BF16),
        'expand_p': _block_expand(seq),
        'perm': _chunk_perm(),
    }

    cache_cmp = cache_cmp_kv.transpose(0, 1, 3, 4, 5, 2).reshape(depth * n_phys, 2, N_KV, HEAD_DIM, PAGE)
    cache_sel = cache_sel_kv.transpose(0, 1, 3, 4, 5, 2).reshape(depth * n_phys, 2, N_KV, HEAD_DIM, PAGE)
    cache_win = cache_win_kv.transpose(0, 1, 3, 4, 5, 2)
    cache_mem = cache_mem_kv.reshape(depth, nb, mem_len * 2 * MEM_HEADS, MEM_HEAD_DIM)

    xp = x_prompt.reshape(batch * seq, D_MODEL)
    xs = x_sample.transpose(1, 0, 2)
    mem_rows = mem_prompt.reshape(batch * mem_len, D_MODEL)
    outs = [[] for _ in range(10)]
    win_stack = None
    stacked = {
        'w_out': w_out.astype(BF16), 'w_mem_q': w_mem_q.astype(BF16), 'w_mem_o': w_mem_o.astype(BF16),
        'w_ff_gate': w_ff_gate.astype(BF16), 'w_ff_up': w_ff_up.astype(BF16),
        'w_ff_down': w_ff_down.astype(BF16),
    }
    w_mem_kv_b = w_mem_kv.astype(BF16)
    for l in range(depth):
        wbd, pet = _prep_cmp(w_cmp[l], pe_cmp[l])
        w_main, wt_kv = _prep_w_in(w_in[l])
        lw = dict(stacked)
        lw.update({
            'g_mix': g_mix[l][None], 'w_main': w_main, 'wt_kv': wt_kv, 'wbd': wbd, 'pet': pet,
            'conv_w': _pad_rows(conv_w[l], 8), 'g_out_nsa': g_out_nsa[l][None],
            'g_out_conv': g_out_conv[l][None], 'g_mem': g_mem[l][None], 'g_ffn': g_ffn[l][None],
            'ffn_conv_w': _pad_rows(ffn_conv_w[l], 8), 'g_final': g_final[None],
        })
        final = l == depth - 1
        mem_kv_p = rms_matmul(mem_rows, g_mem_src[l][None], w_mem_kv_b,
                              batch * mem_len, 2 * D_MEM, l).reshape(batch, mem_len, 2 * D_MEM)
        xp, kc, ks, kw, cst, fst = _layer_prompt(xp, mem_kv_p, lw, consts, l, batch, seq, final)
        page_idx = page_table + l * n_phys
        xs, kc2, ks2, win_stack, cst2, fst2 = _layer_sample(
            xs, lw, consts, l, cache_cmp, cache_sel, page_idx, cache_win, win_stack, cache_mem,
            state_conv[l].transpose(1, 0, 2), state_ffn_conv[l].transpose(1, 0, 2), final)
        for lst, val in zip(outs, (kc, ks, kw, mem_kv_p, cst, fst, kc2, ks2, cst2, fst2)):
            lst.append(val)

    st = [jnp.stack(o) for o in outs]
    return (xp.reshape(batch, seq, D_MODEL), xs.transpose(1, 0, 2),
            st[0], st[1], st[2],
            st[3].reshape(depth, batch, mem_len, 2, MEM_HEADS, MEM_HEAD_DIM),
            st[4], st[5], st[6], st[7],
            win_stack.transpose(0, 1, 5, 2, 3, 4),
            st[8], st[9])
```
